```python
import math
import jax, jax.numpy as jnp
from jax import lax
import numpy as np

D_MODEL = 1024
BATCH = 8
SEQ = 8192
DEPTH = 4

CHUNK = 64
Q_BLOCK = 128
EPS = 1e-6

FOX_HEADS = 4
FOX_DIM = 128
GDN_HEADS = 4
GDN_DK = 128
GDN_DV = 128
GDN_CONV = 4
SB_HEADS = 4
SB_DIM = 128
MEM_TOKENS = 256
MEM_HEADS = 4
MEM_DIM = 128
D_FF = 2816
FFN_CONV = 3
N_BRANCH = 3

FOX_W = FOX_HEADS * FOX_DIM
GDN_KW = GDN_HEADS * GDN_DK
GDN_VW = GDN_HEADS * GDN_DV
SB_W = SB_HEADS * SB_DIM
MEM_W = MEM_HEADS * MEM_DIM

IN_SPLITS = (
    FOX_W, FOX_W, FOX_W, FOX_HEADS,
    GDN_KW, GDN_KW, GDN_VW, GDN_HEADS, GDN_HEADS, GDN_VW,
    SB_W, SB_W, SB_W,
    N_BRANCH * D_MODEL,
)
N_IN = sum(IN_SPLITS)

kernel_name = "hybrid_fox_gdn_stickbreak_encoder"


def rmsnorm(x, g):
    xf = x.astype(jnp.float32)
    y = xf * lax.rsqrt(jnp.mean(xf * xf, axis=-1, keepdims=True) + EPS)
    return (y * g.astype(jnp.float32)).astype(x.dtype)


def l2norm(x):
    xf = x.astype(jnp.float32)
    return xf * lax.rsqrt(jnp.sum(xf * xf, axis=-1, keepdims=True) + EPS)


def heads(x, n):
    return x.reshape(x.shape[:-1] + (n, -1))


def causal_dwconv(x, w):
    width, ch = w.shape
    return lax.conv_general_dilated(
        x, w[:, None, :].astype(x.dtype), window_strides=(1,), padding=[(width - 1, 0)],
        dimension_numbers=("NWC", "WIO", "NWC"), feature_group_count=ch)


def split_projection(p):
    points = [int(s) for s in np.cumsum(IN_SPLITS)[:-1]]
    return jnp.split(p, points, axis=-1)


def fox_attention(q, k, v, logf):
    B, S, H, dh = q.shape
    c = jnp.cumsum(logf, axis=1).transpose(0, 2, 1)
    qh = q.transpose(0, 2, 1, 3)
    kh = k.transpose(0, 2, 1, 3)
    vh = v.transpose(0, 2, 1, 3)
    scale = dh ** -0.5
    outs = []
    for i in range(S // Q_BLOCK):
        lo, hi = i * Q_BLOCK, (i + 1) * Q_BLOCK
        s = (jnp.einsum("bhqd,bhkd->bhqk", qh[:, :, lo:hi], kh[:, :, :hi]).astype(jnp.float32) * scale
             + (c[:, :, lo:hi, None] - c[:, :, None, :hi]))
        mask = jnp.arange(hi)[None, :] <= jnp.arange(lo, hi)[:, None]
        p = jax.nn.softmax(jnp.where(mask, s, -jnp.inf), axis=-1).astype(vh.dtype)
        outs.append(jnp.einsum("bhqk,bhkd->bhqd", p, vh[:, :, :hi]))
    o = jnp.concatenate(outs, axis=2)
    return o.transpose(0, 2, 1, 3).reshape(B, S, H * dh)


def stick_breaking_attention(q, k, v):
    B, S, H, dh = q.shape
    qh = q.transpose(0, 2, 1, 3)
    kh = k.transpose(0, 2, 1, 3)
    vh = v.transpose(0, 2, 1, 3)
    scale = dh ** -0.5
    idx = jnp.arange(Q_BLOCK)
    tri_in = (idx[:, None] >= idx[None, :]).astype(jnp.float32)
    outs = []
    for i in range(S // Q_BLOCK):
        lo, hi = i * Q_BLOCK, (i + 1) * Q_BLOCK
        nk = i + 1
        z = jnp.einsum("bhqd,bhkd->bhqk", qh[:, :, lo:hi], kh[:, :, :hi]).astype(jnp.float32) * scale
        mask = jnp.arange(hi)[None, :] < jnp.arange(lo, hi)[:, None]
        log_keep = jnp.where(mask, -jax.nn.softplus(z), 0.0)
        lk = log_keep.reshape(B, H, Q_BLOCK, nk, Q_BLOCK)
        within = jnp.einsum("bhqnj,jk->bhqnk", lk, tri_in)
        tot = jnp.sum(lk, axis=-1)
        blk = jnp.arange(nk)
        tri_blk = (blk[:, None] > blk[None, :]).astype(jnp.float32)
        after = jnp.einsum("bhqm,mn->bhqn", tot, tri_blk)
        rc = (within + after[..., None]).reshape(B, H, Q_BLOCK, hi)
        a = jnp.exp(jnp.where(mask, z + rc, -jnp.inf)).astype(vh.dtype)
        outs.append(jnp.einsum("bhqk,bhkd->bhqd", a, vh[:, :, :hi]))
    o = jnp.concatenate(outs, axis=2)
    return o.transpose(0, 2, 1, 3).reshape(B, S, H * dh)


def gated_delta_rule(q, k, v, g, beta):
    B, T, H, dk = q.shape
    dv = v.shape[-1]
    N = T // CHUNK
    f32 = jnp.float32

    def chunks(a):
        a = a.astype(f32).reshape((B, N, CHUNK, H) + a.shape[3:])
        return jnp.moveaxis(a, (1, 3), (0, 2))

    qc = chunks(q) * dk ** -0.5
    kc = chunks(k)
    vc = chunks(v)
    bc = chunks(beta)
    gc = jnp.cumsum(chunks(g), axis=-1)
    idx = jnp.arange(CHUNK)
    causal = idx[:, None] >= idx[None, :]
    strict = idx[:, None] > idx[None, :]
    decay = jnp.exp(jnp.where(causal, gc[..., :, None] - gc[..., None, :], -jnp.inf))
    kk = jnp.einsum("nbhcd,nbhed->nbhce", kc, kc)
    a_mat = jnp.where(strict, bc[..., :, None] * kk * decay, 0.0) + jnp.eye(CHUNK, dtype=f32)
    rhs = jnp.concatenate([vc * bc[..., None], kc * (bc * jnp.exp(gc))[..., None]], axis=-1)
    sol = lax.linalg.triangular_solve(a_mat, rhs, left_side=True, lower=True)
    u, w = sol[..., :dv], sol[..., dv:]
    attn = jnp.where(causal, jnp.einsum("nbhcd,nbhed->nbhce", qc, kc) * decay, 0.0)
    g_last = gc[..., -1]
    k_tail = kc * jnp.exp(g_last[..., None] - gc)[..., None]

    def step(state, xs):
        q_n, u_n, w_n, g_n, attn_n, kt_n, gl_n = xs
        v_new = u_n - jnp.einsum("bhck,bhkv->bhcv", w_n, state)
        o = (jnp.einsum("bhck,bhkv->bhcv", q_n * jnp.exp(g_n)[..., None], state)
             + jnp.einsum("bhce,bhev->bhcv", attn_n, v_new))
        state = state * jnp.exp(gl_n)[..., None, None] + jnp.einsum("bhck,bhcv->bhkv", kt_n, v_new)
        return state, o

    s0 = jnp.zeros((B, H, dk, dv), f32)
    _, o = lax.scan(step, s0, (qc, u, w, gc, attn, k_tail, g_last))
    return jnp.moveaxis(o, (0, 2), (1, 3)).reshape(B, T, H, dv)


def memory_cross_attention(h, m, w_q, w_kv, q_g, k_g, w_o):
    B, S, _ = h.shape
    q = rmsnorm(heads(h @ w_q, MEM_HEADS), q_g)
    k, v = jnp.split(m @ w_kv, 2, axis=-1)
    k = rmsnorm(heads(k, MEM_HEADS), k_g)
    v = heads(v, MEM_HEADS)
    s = jnp.einsum("bqhd,bkhd->bhqk", q, k).astype(jnp.float32) * MEM_DIM ** -0.5
    p = jax.nn.softmax(s, axis=-1).astype(v.dtype)
    o = jnp.einsum("bhqk,bkhd->bqhd", p, v).reshape(B, S, MEM_W)
    return o @ w_o


def conv_ffn(h, w_up, conv_w, conv_b, w_down):
    u = causal_dwconv(h @ w_up, conv_w) + conv_b
    a, b = jnp.split(u, 2, axis=-1)
    return (jax.nn.silu(a) * b) @ w_down


def _fwd_setup_inputs(seed: int = 0) -> dict:
    key = jax.random.key(seed)
    ks = iter(jax.random.split(key, 40))
    L, D = DEPTH, D_MODEL
    f32 = jnp.float32

    def nrm(shape, scale):
        return jax.random.normal(next(ks), shape, f32) * scale

    def gain(shape):
        return 1.0 + 0.02 * jax.random.normal(next(ks), shape, f32)

    x = nrm((BATCH, SEQ, D), 1.0)
    mem = nrm((BATCH, MEM_TOKENS, D), 1.0)
    norm_mix = gain((L, D))
    w_in = nrm((L, D, N_IN), D ** -0.5)
    fox_fbias = jax.random.uniform(next(ks), (L, FOX_HEADS), f32, minval=1.0, maxval=5.0)
    fox_qnorm = gain((L, FOX_DIM))
    fox_knorm = gain((L, FOX_DIM))
    gdn_conv = nrm((L, GDN_CONV, 2 * GDN_KW + GDN_VW), GDN_CONV ** -0.5)
    gdn_a_log = jnp.log(jax.random.uniform(next(ks), (L, GDN_HEADS), f32, minval=1.0, maxval=16.0))
    dt = jnp.exp(jax.random.uniform(next(ks), (L, GDN_HEADS), f32,
                                    minval=math.log(1e-3), maxval=math.log(1e-1)))
    gdn_dt_bias = dt + jnp.log(-jnp.expm1(-dt))
    gdn_onorm = gain((L, GDN_DV))
    gate_bias = nrm((L, N_BRANCH * D), 0.01)
    w_oa = nrm((L, FOX_W, D), FOX_W ** -0.5)
    w_ob = nrm((L, GDN_VW, D), GDN_VW ** -0.5)
    w_oc = nrm((L, SB_W, D), SB_W ** -0.5)
    w_out = nrm((L, D, D), D ** -0.5)
    norm_xq = gain((L, D))
    norm_mem = gain((L, D))
    w_mq = nrm((L, D, MEM_W), D ** -0.5)
    w_mkv = nrm((L, D, 2 * MEM_W), D ** -0.5)
    mq_norm = gain((L, MEM_DIM))
    mk_norm = gain((L, MEM_DIM))
    w_mo = nrm((L, MEM_W, D), MEM_W ** -0.5)
    norm_ffn = gain((L, D))
    w_up = nrm((L, D, 2 * D_FF), D ** -0.5)
    ffn_conv = nrm((L, FFN_CONV, 2 * D_FF), FFN_CONV ** -0.5)
    ffn_conv_b = nrm((L, 2 * D_FF), 0.01)
    w_down = nrm((L, D_FF, D), D_FF ** -0.5)
    return {
        "x": x, "mem": mem, "norm_mix": norm_mix, "w_in": w_in,
        "fox_fbias": fox_fbias, "fox_qnorm": fox_qnorm, "fox_knorm": fox_knorm,
        "gdn_conv": gdn_conv, "gdn_a_log": gdn_a_log, "gdn_dt_bias": gdn_dt_bias,
        "gdn_onorm": gdn_onorm, "gate_bias": gate_bias,
        "w_oa": w_oa, "w_ob": w_ob, "w_oc": w_oc, "w_out": w_out,
        "norm_xq": norm_xq, "norm_mem": norm_mem, "w_mq": w_mq, "w_mkv": w_mkv,
        "mq_norm": mq_norm, "mk_norm": mk_norm, "w_mo": w_mo,
        "norm_ffn": norm_ffn, "w_up": w_up, "ffn_conv": ffn_conv,
        "ffn_conv_b": ffn_conv_b, "w_down": w_down,
    }


def _fwd_reference(x, mem, norm_mix, w_in, fox_fbias, fox_qnorm, fox_knorm, gdn_conv, gdn_a_log,
              gdn_dt_bias, gdn_onorm, gate_bias, w_oa, w_ob, w_oc, w_out, norm_xq, norm_mem,
              w_mq, w_mkv, mq_norm, mk_norm, w_mo, norm_ffn, w_up, ffn_conv, ffn_conv_b, w_down):
    B, S, D = x.shape
    f32 = jnp.float32
    for l in range(DEPTH):
        h = rmsnorm(x, norm_mix[l])
        (fq, fk, fv, ff, gq, gk, gv, gb, ga, gz, sq, sk, sv, gates) = split_projection(h @ w_in[l])

        fq = rmsnorm(heads(fq, FOX_HEADS), fox_qnorm[l])
        fk = rmsnorm(heads(fk, FOX_HEADS), fox_knorm[l])
        logf = jax.nn.log_sigmoid((ff + fox_fbias[l]).astype(f32))
        ya = fox_attention(fq, fk, heads(fv, FOX_HEADS), logf)

        qkv = jax.nn.silu(causal_dwconv(jnp.concatenate([gq, gk, gv], axis=-1), gdn_conv[l]))
        cq, ck, cv = jnp.split(qkv, [GDN_KW, 2 * GDN_KW], axis=-1)
        beta = jax.nn.sigmoid(gb.astype(f32))
        g_log = -jnp.exp(gdn_a_log[l].astype(f32)) * jax.nn.softplus((ga + gdn_dt_bias[l]).astype(f32))
        o = gated_delta_rule(l2norm(heads(cq, GDN_HEADS)), l2norm(heads(ck, GDN_HEADS)),
                             heads(cv, GDN_HEADS), g_log, beta)
        yb = (rmsnorm(o, gdn_onorm[l]) * jax.nn.silu(heads(gz, GDN_HEADS).astype(f32)))
        yb = yb.astype(x.dtype).reshape(B, S, GDN_VW)

        yc = stick_breaking_attention(heads(sq, SB_HEADS), heads(sk, SB_HEADS), heads(sv, SB_HEADS))

        g = jax.nn.sigmoid((gates + gate_bias[l]).astype(f32)).astype(x.dtype).reshape(B, S, N_BRANCH, D)
        mixed = (g[..., 0, :] * (ya @ w_oa[l]) + g[..., 1, :] * (yb @ w_ob[l])
                 + g[..., 2, :] * (yc @ w_oc[l]))
        x = x + mixed @ w_out[l]

        x = x + memory_cross_attention(rmsnorm(x, norm_xq[l]), rmsnorm(mem, norm_mem[l]),
                                       w_mq[l], w_mkv[l], mq_norm[l], mk_norm[l], w_mo[l])

        x = x + conv_ffn(rmsnorm(x, norm_ffn[l]), w_up[l], ffn_conv[l], ffn_conv_b[l], w_down[l])
    return x


import jax as _jax
import jax.numpy as _jnp

TWIN_FORMAT = 'train_step'
FWD_PARAMS = ['x', 'mem', 'norm_mix', 'w_in', 'fox_fbias', 'fox_qnorm', 'fox_knorm', 'gdn_conv', 'gdn_a_log', 'gdn_dt_bias', 'gdn_onorm', 'gate_bias', 'w_oa', 'w_ob', 'w_oc', 'w_out', 'norm_xq', 'norm_mem', 'w_mq', 'w_mkv', 'mq_norm', 'mk_norm', 'w_mo', 'norm_ffn', 'w_up', 'ffn_conv', 'ffn_conv_b', 'w_down']
TWIN_WEIGHTS = ['norm_mix', 'w_in', 'fox_fbias', 'fox_qnorm', 'fox_knorm', 'gdn_conv', 'gdn_a_log', 'gdn_dt_bias', 'gdn_onorm', 'gate_bias', 'w_oa', 'w_ob', 'w_oc', 'w_out', 'norm_xq', 'norm_mem', 'w_mq', 'w_mkv', 'mq_norm', 'mk_norm', 'w_mo', 'norm_ffn', 'w_up', 'ffn_conv', 'ffn_conv_b', 'w_down']
TWIN_DIFF_INPUT = 'x'
TWIN_INPUTS = ['x', 'mem', 'norm_mix', 'w_in', 'fox_fbias', 'fox_qnorm', 'fox_knorm', 'gdn_conv', 'gdn_a_log', 'gdn_dt_bias', 'gdn_onorm', 'gate_bias', 'w_oa', 'w_ob', 'w_oc', 'w_out', 'norm_xq', 'norm_mem', 'w_mq', 'w_mkv', 'mq_norm', 'mk_norm', 'w_mo', 'norm_ffn', 'w_up', 'ffn_conv', 'ffn_conv_b', 'w_down', 'loss_target', 'm_norm_mix', 'm_w_in', 'm_fox_fbias', 'm_fox_qnorm', 'm_fox_knorm', 'm_gdn_conv', 'm_gdn_a_log', 'm_gdn_dt_bias', 'm_gdn_onorm', 'm_gate_bias', 'm_w_oa', 'm_w_ob', 'm_w_oc', 'm_w_out', 'm_norm_xq', 'm_norm_mem', 'm_w_mq', 'm_w_mkv', 'm_mq_norm', 'm_mk_norm', 'm_w_mo', 'm_norm_ffn', 'm_w_up', 'm_ffn_conv', 'm_ffn_conv_b', 'm_w_down', 'v_norm_mix', 'v_w_in', 'v_fox_fbias', 'v_fox_qnorm', 'v_fox_knorm', 'v_gdn_conv', 'v_gdn_a_log', 'v_gdn_dt_bias', 'v_gdn_onorm', 'v_gate_bias', 'v_w_oa', 'v_w_ob', 'v_w_oc', 'v_w_out', 'v_norm_xq', 'v_norm_mem', 'v_w_mq', 'v_w_mkv', 'v_mq_norm', 'v_mk_norm', 'v_w_mo', 'v_norm_ffn', 'v_w_up', 'v_ffn_conv', 'v_ffn_conv_b', 'v_w_down']
TWIN_OUTPUTS = ['loss', 'grad_x', 'grad_norm_mix', 'grad_w_in', 'grad_fox_fbias', 'grad_fox_qnorm', 'grad_fox_knorm', 'grad_gdn_conv', 'grad_gdn_a_log', 'grad_gdn_dt_bias', 'grad_gdn_onorm', 'grad_gate_bias', 'grad_w_oa', 'grad_w_ob', 'grad_w_oc', 'grad_w_out', 'grad_norm_xq', 'grad_norm_mem', 'grad_w_mq', 'grad_w_mkv', 'grad_mq_norm', 'grad_mk_norm', 'grad_w_mo', 'grad_norm_ffn', 'grad_w_up', 'grad_ffn_conv', 'grad_ffn_conv_b', 'grad_w_down', 'delta_norm_mix', 'delta_w_in', 'delta_fox_fbias', 'delta_fox_qnorm', 'delta_fox_knorm', 'delta_gdn_conv', 'delta_gdn_a_log', 'delta_gdn_dt_bias', 'delta_gdn_onorm', 'delta_gate_bias', 'delta_w_oa', 'delta_w_ob', 'delta_w_oc', 'delta_w_out', 'delta_norm_xq', 'delta_norm_mem', 'delta_w_mq', 'delta_w_mkv', 'delta_mq_norm', 'delta_mk_norm', 'delta_w_mo', 'delta_norm_ffn', 'delta_w_up', 'delta_ffn_conv', 'delta_ffn_conv_b', 'delta_w_down', 'new_m_norm_mix', 'new_m_w_in', 'new_m_fox_fbias', 'new_m_fox_qnorm', 'new_m_fox_knorm', 'new_m_gdn_conv', 'new_m_gdn_a_log', 'new_m_gdn_dt_bias', 'new_m_gdn_onorm', 'new_m_gate_bias', 'new_m_w_oa', 'new_m_w_ob', 'new_m_w_oc', 'new_m_w_out', 'new_m_norm_xq', 'new_m_norm_mem', 'new_m_w_mq', 'new_m_w_mkv', 'new_m_mq_norm', 'new_m_mk_norm', 'new_m_w_mo', 'new_m_norm_ffn', 'new_m_w_up', 'new_m_ffn_conv', 'new_m_ffn_conv_b', 'new_m_w_down', 'new_v_norm_mix', 'new_v_w_in', 'new_v_fox_fbias', 'new_v_fox_qnorm', 'new_v_fox_knorm', 'new_v_gdn_conv', 'new_v_gdn_a_log', 'new_v_gdn_dt_bias', 'new_v_gdn_onorm', 'new_v_gate_bias', 'new_v_w_oa', 'new_v_w_ob', 'new_v_w_oc', 'new_v_w_out', 'new_v_norm_xq', 'new_v_norm_mem', 'new_v_w_mq', 'new_v_w_mkv', 'new_v_mq_norm', 'new_v_mk_norm', 'new_v_w_mo', 'new_v_norm_ffn', 'new_v_w_up', 'new_v_ffn_conv', 'new_v_ffn_conv_b', 'new_v_w_down']
TWIN_LEAF_KINDS = {'loss': 'loss', 'grad_x': 'grad_x', 'grad_norm_mix': 'grad_w', 'grad_w_in': 'grad_w', 'grad_fox_fbias': 'grad_w', 'grad_fox_qnorm': 'grad_w', 'grad_fox_knorm': 'grad_w', 'grad_gdn_conv': 'grad_w', 'grad_gdn_a_log': 'grad_w', 'grad_gdn_dt_bias': 'grad_w', 'grad_gdn_onorm': 'grad_w', 'grad_gate_bias': 'grad_w', 'grad_w_oa': 'grad_w', 'grad_w_ob': 'grad_w', 'grad_w_oc': 'grad_w', 'grad_w_out': 'grad_w', 'grad_norm_xq': 'grad_w', 'grad_norm_mem': 'grad_w', 'grad_w_mq': 'grad_w', 'grad_w_mkv': 'grad_w', 'grad_mq_norm': 'grad_w', 'grad_mk_norm': 'grad_w', 'grad_w_mo': 'grad_w', 'grad_norm_ffn': 'grad_w', 'grad_w_up': 'grad_w', 'grad_ffn_conv': 'grad_w', 'grad_ffn_conv_b': 'grad_w', 'grad_w_down': 'grad_w', 'delta_norm_mix': 'delta_w', 'delta_w_in': 'delta_w', 'delta_fox_fbias': 'delta_w', 'delta_fox_qnorm': 'delta_w', 'delta_fox_knorm': 'delta_w', 'delta_gdn_conv': 'delta_w', 'delta_gdn_a_log': 'delta_w', 'delta_gdn_dt_bias': 'delta_w', 'delta_gdn_onorm': 'delta_w', 'delta_gate_bias': 'delta_w', 'delta_w_oa': 'delta_w', 'delta_w_ob': 'delta_w', 'delta_w_oc': 'delta_w', 'delta_w_out': 'delta_w', 'delta_norm_xq': 'delta_w', 'delta_norm_mem': 'delta_w', 'delta_w_mq': 'delta_w', 'delta_w_mkv': 'delta_w', 'delta_mq_norm': 'delta_w', 'delta_mk_norm': 'delta_w', 'delta_w_mo': 'delta_w', 'delta_norm_ffn': 'delta_w', 'delta_w_up': 'delta_w', 'delta_ffn_conv': 'delta_w', 'delta_ffn_conv_b': 'delta_w', 'delta_w_down': 'delta_w', 'new_m_norm_mix': 'new_m', 'new_m_w_in': 'new_m', 'new_m_fox_fbias': 'new_m', 'new_m_fox_qnorm': 'new_m', 'new_m_fox_knorm': 'new_m', 'new_m_gdn_conv': 'new_m', 'new_m_gdn_a_log': 'new_m', 'new_m_gdn_dt_bias': 'new_m', 'new_m_gdn_onorm': 'new_m', 'new_m_gate_bias': 'new_m', 'new_m_w_oa': 'new_m', 'new_m_w_ob': 'new_m', 'new_m_w_oc': 'new_m', 'new_m_w_out': 'new_m', 'new_m_norm_xq': 'new_m', 'new_m_norm_mem': 'new_m', 'new_m_w_mq': 'new_m', 'new_m_w_mkv': 'new_m', 'new_m_mq_norm': 'new_m', 'new_m_mk_norm': 'new_m', 'new_m_w_mo': 'new_m', 'new_m_norm_ffn': 'new_m', 'new_m_w_up': 'new_m', 'new_m_ffn_conv': 'new_m', 'new_m_ffn_conv_b': 'new_m', 'new_m_w_down': 'new_m', 'new_v_norm_mix': 'new_v', 'new_v_w_in': 'new_v', 'new_v_fox_fbias': 'new_v', 'new_v_fox_qnorm': 'new_v', 'new_v_fox_knorm': 'new_v', 'new_v_gdn_conv': 'new_v', 'new_v_gdn_a_log': 'new_v', 'new_v_gdn_dt_bias': 'new_v', 'new_v_gdn_onorm': 'new_v', 'new_v_gate_bias': 'new_v', 'new_v_w_oa': 'new_v', 'new_v_w_ob': 'new_v', 'new_v_w_oc': 'new_v', 'new_v_w_out': 'new_v', 'new_v_norm_xq': 'new_v', 'new_v_norm_mem': 'new_v', 'new_v_w_mq': 'new_v', 'new_v_w_mkv': 'new_v', 'new_v_mq_norm': 'new_v', 'new_v_mk_norm': 'new_v', 'new_v_w_mo': 'new_v', 'new_v_norm_ffn': 'new_v', 'new_v_w_up': 'new_v', 'new_v_ffn_conv': 'new_v', 'new_v_ffn_conv_b': 'new_v', 'new_v_w_down': 'new_v'}


def _forward(args):
    return _fwd_reference(*[args[k] for k in FWD_PARAMS])


def _output_shape():
    def fwd():
        inp = _fwd_setup_inputs(0)
        return _fwd_reference(*[inp[k] for k in FWD_PARAMS])
    out = _jax.eval_shape(fwd)
    return out.shape, out.dtype

N_MICROBATCH = 1
ADAM_LR = 0.001
ADAM_B1 = 0.9
ADAM_B2 = 0.999
ADAM_EPS = 1e-08
ADAM_WD = 0.01
ADAM_STEP = 10
PER_EXAMPLE_BATCH_AXIS = {'x': 0, 'mem': 0, 'loss_target': 0}
SHARED_INPUTS = []
_WEIGHT_DTYPES = {'norm_mix': _jnp.float32, 'w_in': _jnp.float32, 'fox_fbias': _jnp.float32, 'fox_qnorm': _jnp.float32, 'fox_knorm': _jnp.float32, 'gdn_conv': _jnp.float32, 'gdn_a_log': _jnp.float32, 'gdn_dt_bias': _jnp.float32, 'gdn_onorm': _jnp.float32, 'gate_bias': _jnp.float32, 'w_oa': _jnp.float32, 'w_ob': _jnp.float32, 'w_oc': _jnp.float32, 'w_out': _jnp.float32, 'norm_xq': _jnp.float32, 'norm_mem': _jnp.float32, 'w_mq': _jnp.float32, 'w_mkv': _jnp.float32, 'mq_norm': _jnp.float32, 'mk_norm': _jnp.float32, 'w_mo': _jnp.float32, 'norm_ffn': _jnp.float32, 'w_up': _jnp.float32, 'ffn_conv': _jnp.float32, 'ffn_conv_b': _jnp.float32, 'w_down': _jnp.float32}
MOMENT_SCALE = {'norm_mix': 2.319828e+01, 'w_in': 5.455289e-01, 'fox_fbias': 2.102735e+02, 'fox_qnorm': 6.146483e+00, 'fox_knorm': 6.153320e+00, 'gdn_conv': 1.063212e+00, 'gdn_a_log': 4.485862e+01, 'gdn_dt_bias': 4.208143e+01, 'gdn_onorm': 5.749609e+01, 'gate_bias': 2.221410e+00, 'w_oa': 5.819773e-01, 'w_ob': 1.802178e+00, 'w_oc': 7.976582e-01, 'w_out': 1.928772e+00, 'norm_xq': 1.711144e-01, 'norm_mem': 9.321881e-01, 'w_mq': 2.268337e-01, 'w_mkv': 6.434362e-01, 'mq_norm': 4.941174e+00, 'mk_norm': 4.938504e+00, 'w_mo': 6.550262e-01, 'norm_ffn': 5.165099e+01, 'w_up': 7.357481e-01, 'ffn_conv': 7.168902e+00, 'ffn_conv_b': 7.001575e+00, 'w_down': 9.198543e-01}


def _to_microbatches(a, axis):
    t = _jnp.moveaxis(a, axis, 0)
    t = t.reshape((N_MICROBATCH, t.shape[0] // N_MICROBATCH) + t.shape[1:])
    return _jnp.moveaxis(t, 1, axis + 1)


def setup_inputs(seed: int = 0) -> dict:
    inp = _fwd_setup_inputs(seed)
    key = _jax.random.fold_in(_jax.random.key(seed), 7919)
    shape, _ = _output_shape()
    out = dict(inp)
    out["loss_target"] = _jax.random.normal(_jax.random.fold_in(key, 0), shape, _jnp.float32)
    for i, name in enumerate(TWIN_WEIGHTS):
        w = inp[name].astype(_jnp.float32)
        if MOMENT_SCALE is None:
            s = _jnp.sqrt(_jnp.mean(_jnp.square(w)) + 1e-30)
        else:
            s = MOMENT_SCALE[name]
        km, kv = _jax.random.split(_jax.random.fold_in(key, i + 1))
        out[name] = w
        out["m_" + name] = s * _jax.random.normal(km, w.shape, _jnp.float32)
        out["v_" + name] = (s * s) * _jax.random.uniform(kv, w.shape, _jnp.float32, 0.5, 1.5)
    if N_MICROBATCH > 1:
        for name, axis in PER_EXAMPLE_BATCH_AXIS.items():
            out[name] = _to_microbatches(out[name], axis)
    return {'x': out['x'], 'mem': out['mem'], 'norm_mix': out['norm_mix'], 'w_in': out['w_in'], 'fox_fbias': out['fox_fbias'], 'fox_qnorm': out['fox_qnorm'], 'fox_knorm': out['fox_knorm'], 'gdn_conv': out['gdn_conv'], 'gdn_a_log': out['gdn_a_log'], 'gdn_dt_bias': out['gdn_dt_bias'], 'gdn_onorm': out['gdn_onorm'], 'gate_bias': out['gate_bias'], 'w_oa': out['w_oa'], 'w_ob': out['w_ob'], 'w_oc': out['w_oc'], 'w_out': out['w_out'], 'norm_xq': out['norm_xq'], 'norm_mem': out['norm_mem'], 'w_mq': out['w_mq'], 'w_mkv': out['w_mkv'], 'mq_norm': out['mq_norm'], 'mk_norm': out['mk_norm'], 'w_mo': out['w_mo'], 'norm_ffn': out['norm_ffn'], 'w_up': out['w_up'], 'ffn_conv': out['ffn_conv'], 'ffn_conv_b': out['ffn_conv_b'], 'w_down': out['w_down'], 'loss_target': out['loss_target'], 'm_norm_mix': out['m_norm_mix'], 'm_w_in': out['m_w_in'], 'm_fox_fbias': out['m_fox_fbias'], 'm_fox_qnorm': out['m_fox_qnorm'], 'm_fox_knorm': out['m_fox_knorm'], 'm_gdn_conv': out['m_gdn_conv'], 'm_gdn_a_log': out['m_gdn_a_log'], 'm_gdn_dt_bias': out['m_gdn_dt_bias'], 'm_gdn_onorm': out['m_gdn_onorm'], 'm_gate_bias': out['m_gate_bias'], 'm_w_oa': out['m_w_oa'], 'm_w_ob': out['m_w_ob'], 'm_w_oc': out['m_w_oc'], 'm_w_out': out['m_w_out'], 'm_norm_xq': out['m_norm_xq'], 'm_norm_mem': out['m_norm_mem'], 'm_w_mq': out['m_w_mq'], 'm_w_mkv': out['m_w_mkv'], 'm_mq_norm': out['m_mq_norm'], 'm_mk_norm': out['m_mk_norm'], 'm_w_mo': out['m_w_mo'], 'm_norm_ffn': out['m_norm_ffn'], 'm_w_up': out['m_w_up'], 'm_ffn_conv': out['m_ffn_conv'], 'm_ffn_conv_b': out['m_ffn_conv_b'], 'm_w_down': out['m_w_down'], 'v_norm_mix': out['v_norm_mix'], 'v_w_in': out['v_w_in'], 'v_fox_fbias': out['v_fox_fbias'], 'v_fox_qnorm': out['v_fox_qnorm'], 'v_fox_knorm': out['v_fox_knorm'], 'v_gdn_conv': out['v_gdn_conv'], 'v_gdn_a_log': out['v_gdn_a_log'], 'v_gdn_dt_bias': out['v_gdn_dt_bias'], 'v_gdn_onorm': out['v_gdn_onorm'], 'v_gate_bias': out['v_gate_bias'], 'v_w_oa': out['v_w_oa'], 'v_w_ob': out['v_w_ob'], 'v_w_oc': out['v_w_oc'], 'v_w_out': out['v_w_out'], 'v_norm_xq': out['v_norm_xq'], 'v_norm_mem': out['v_norm_mem'], 'v_w_mq': out['v_w_mq'], 'v_w_mkv': out['v_w_mkv'], 'v_mq_norm': out['v_mq_norm'], 'v_mk_norm': out['v_mk_norm'], 'v_w_mo': out['v_w_mo'], 'v_norm_ffn': out['v_norm_ffn'], 'v_w_up': out['v_w_up'], 'v_ffn_conv': out['v_ffn_conv'], 'v_ffn_conv_b': out['v_ffn_conv_b'], 'v_w_down': out['v_w_down']}


def _loss(weights, diff, rest, loss_target):
    with _jax.named_scope("forward"):
        args = {**rest, TWIN_DIFF_INPUT: diff, **{k: w.astype(_WEIGHT_DTYPES[k]) for k, w in weights.items()}}
        y = _forward(args)
    with _jax.named_scope("loss_head"):
        err = _jnp.square(y.astype(_jnp.float32) - loss_target)
        return 0.5 * _jnp.sum(_jnp.mean(err, axis=-1)) if err.ndim else 0.5 * err


def _adamw(w, g, m, v):
    m = ADAM_B1 * m + (1.0 - ADAM_B1) * g
    v = ADAM_B2 * v + (1.0 - ADAM_B2) * _jnp.square(g)
    m_hat = m / (1.0 - ADAM_B1 ** ADAM_STEP)
    v_hat = v / (1.0 - ADAM_B2 ** ADAM_STEP)
    delta = -ADAM_LR * (m_hat / (_jnp.sqrt(v_hat) + ADAM_EPS) + ADAM_WD * w)
    return delta, m, v


def reference(x, mem, norm_mix, w_in, fox_fbias, fox_qnorm, fox_knorm, gdn_conv, gdn_a_log, gdn_dt_bias, gdn_onorm, gate_bias, w_oa, w_ob, w_oc, w_out, norm_xq, norm_mem, w_mq, w_mkv, mq_norm, mk_norm, w_mo, norm_ffn, w_up, ffn_conv, ffn_conv_b, w_down, loss_target, m_norm_mix, m_w_in, m_fox_fbias, m_fox_qnorm, m_fox_knorm, m_gdn_conv, m_gdn_a_log, m_gdn_dt_bias, m_gdn_onorm, m_gate_bias, m_w_oa, m_w_ob, m_w_oc, m_w_out, m_norm_xq, m_norm_mem, m_w_mq, m_w_mkv, m_mq_norm, m_mk_norm, m_w_mo, m_norm_ffn, m_w_up, m_ffn_conv, m_ffn_conv_b, m_w_down, v_norm_mix, v_w_in, v_fox_fbias, v_fox_qnorm, v_fox_knorm, v_gdn_conv, v_gdn_a_log, v_gdn_dt_bias, v_gdn_onorm, v_gate_bias, v_w_oa, v_w_ob, v_w_oc, v_w_out, v_norm_xq, v_norm_mem, v_w_mq, v_w_mkv, v_mq_norm, v_mk_norm, v_w_mo, v_norm_ffn, v_w_up, v_ffn_conv, v_ffn_conv_b, v_w_down):
    given = dict(x=x, mem=mem, norm_mix=norm_mix, w_in=w_in, fox_fbias=fox_fbias, fox_qnorm=fox_qnorm, fox_knorm=fox_knorm, gdn_conv=gdn_conv, gdn_a_log=gdn_a_log, gdn_dt_bias=gdn_dt_bias, gdn_onorm=gdn_onorm, gate_bias=gate_bias, w_oa=w_oa, w_ob=w_ob, w_oc=w_oc, w_out=w_out, norm_xq=norm_xq, norm_mem=norm_mem, w_mq=w_mq, w_mkv=w_mkv, mq_norm=mq_norm, mk_norm=mk_norm, w_mo=w_mo, norm_ffn=norm_ffn, w_up=w_up, ffn_conv=ffn_conv, ffn_conv_b=ffn_conv_b, w_down=w_down, loss_target=loss_target, m_norm_mix=m_norm_mix, m_w_in=m_w_in, m_fox_fbias=m_fox_fbias, m_fox_qnorm=m_fox_qnorm, m_fox_knorm=m_fox_knorm, m_gdn_conv=m_gdn_conv, m_gdn_a_log=m_gdn_a_log, m_gdn_dt_bias=m_gdn_dt_bias, m_gdn_onorm=m_gdn_onorm, m_gate_bias=m_gate_bias, m_w_oa=m_w_oa, m_w_ob=m_w_ob, m_w_oc=m_w_oc, m_w_out=m_w_out, m_norm_xq=m_norm_xq, m_norm_mem=m_norm_mem, m_w_mq=m_w_mq, m_w_mkv=m_w_mkv, m_mq_norm=m_mq_norm, m_mk_norm=m_mk_norm, m_w_mo=m_w_mo, m_norm_ffn=m_norm_ffn, m_w_up=m_w_up, m_ffn_conv=m_ffn_conv, m_ffn_conv_b=m_ffn_conv_b, m_w_down=m_w_down, v_norm_mix=v_norm_mix, v_w_in=v_w_in, v_fox_fbias=v_fox_fbias, v_fox_qnorm=v_fox_qnorm, v_fox_knorm=v_fox_knorm, v_gdn_conv=v_gdn_conv, v_gdn_a_log=v_gdn_a_log, v_gdn_dt_bias=v_gdn_dt_bias, v_gdn_onorm=v_gdn_onorm, v_gate_bias=v_gate_bias, v_w_oa=v_w_oa, v_w_ob=v_w_ob, v_w_oc=v_w_oc, v_w_out=v_w_out, v_norm_xq=v_norm_xq, v_norm_mem=v_norm_mem, v_w_mq=v_w_mq, v_w_mkv=v_w_mkv, v_mq_norm=v_mq_norm, v_mk_norm=v_mk_norm, v_w_mo=v_w_mo, v_norm_ffn=v_norm_ffn, v_w_up=v_w_up, v_ffn_conv=v_ffn_conv, v_ffn_conv_b=v_ffn_conv_b, v_w_down=v_w_down)
    weights = {n: given[n] for n in TWIN_WEIGHTS}
    shared = {n: given[n] for n in SHARED_INPUTS}
    per_example = {n: given[n] for n in ['x', 'mem']}
    grad_fn = _jax.value_and_grad(_loss, argnums=(0, 1))

    def one_microbatch(ex, loss_target):
        ex = dict(ex)
        diff = ex.pop(TWIN_DIFF_INPUT)
        return grad_fn(weights, diff, {**shared, **ex}, loss_target)

    if N_MICROBATCH == 1:
        loss, (grad_w, grad_x) = one_microbatch(per_example, given["loss_target"])
    else:
        def body(carry, xs):
            loss_sum, grad_sum = carry
            l_k, (gw_k, gx_k) = one_microbatch(xs[0], xs[1])
            with _jax.named_scope("update"):
                return (loss_sum + l_k, _jax.tree.map(_jnp.add, grad_sum, gw_k)), gx_k

        init = (_jnp.zeros((), _jnp.float32), _jax.tree.map(_jnp.zeros_like, weights))
        (loss, grad_w), grad_x = _jax.lax.scan(body, init, (per_example, given["loss_target"]))
    with _jax.named_scope("update"):
        delta_w, new_m, new_v = {}, {}, {}
        for n in TWIN_WEIGHTS:
            delta_w[n], new_m[n], new_v[n] = _adamw(weights[n], grad_w[n], given["m_" + n], given["v_" + n])
    return (loss, grad_x, *[grad_w[n] for n in TWIN_WEIGHTS], *[delta_w[n] for n in TWIN_WEIGHTS],
            *[new_m[n] for n in TWIN_WEIGHTS], *[new_v[n] for n in TWIN_WEIGHTS])
```

```python
import functools
import math

import jax
import jax.numpy as jnp
from jax import lax
from jax.experimental import pallas as pl
from jax.experimental.pallas import tpu as pltpu

F32 = jnp.float32
BF16 = jnp.bfloat16

N_DEV = 8
D_MODEL = 1024
DEPTH = 4
CHUNK = 64
EPS = 1e-6
HEADS = 4
HEAD_DIM = 128
HW = HEADS * HEAD_DIM
D_FF = 2816
N_IN = 8204
LANES = 128
SUBLANES = 8
VMEM_LIMIT = 56 * 1024 * 1024

ADAM_LR = 0.001
ADAM_B1 = 0.9
ADAM_B2 = 0.999
ADAM_EPS = 1e-08
ADAM_WD = 0.01
ADAM_STEP = 10

NEG = -1e30
MESH = pl.DeviceIdType.MESH

WEIGHTS = ['norm_mix', 'w_in', 'fox_fbias', 'fox_qnorm', 'fox_knorm', 'gdn_conv', 'gdn_a_log', 'gdn_dt_bias',
           'gdn_onorm', 'gate_bias', 'w_oa', 'w_ob', 'w_oc', 'w_out', 'norm_xq', 'norm_mem', 'w_mq', 'w_mkv',
           'mq_norm', 'mk_norm', 'w_mo', 'norm_ffn', 'w_up', 'ffn_conv', 'ffn_conv_b', 'w_down']
SHARDED = {
    'w_in': ((D_MODEL, N_IN), 0), 'gdn_conv': ((4, 3 * HW), 1), 'w_oa': ((HW, D_MODEL), 1),
    'w_ob': ((HW, D_MODEL), 1), 'w_oc': ((HW, D_MODEL), 1), 'w_out': ((D_MODEL, D_MODEL), 0),
    'w_mq': ((D_MODEL, HW), 0), 'w_mkv': ((D_MODEL, 2 * HW), 0), 'w_mo': ((HW, D_MODEL), 1),
    'w_up': ((D_MODEL, 2 * D_FF), 1), 'ffn_conv': ((3, 2 * D_FF), 1), 'w_down': ((D_FF, D_MODEL), 0),
}
SHARDED_ORDER = [n for n in WEIGHTS if n in SHARDED]
SMALL_ORDER = [n for n in WEIGHTS if n not in SHARDED]
SMALL_WIDTH = {'norm_mix': D_MODEL, 'fox_fbias': HEADS, 'fox_qnorm': HEAD_DIM, 'fox_knorm': HEAD_DIM,
               'gdn_a_log': HEADS, 'gdn_dt_bias': HEADS, 'gdn_onorm': HEAD_DIM, 'gate_bias': 3 * D_MODEL,
               'norm_xq': D_MODEL, 'norm_mem': D_MODEL, 'mq_norm': HEAD_DIM, 'mk_norm': HEAD_DIM,
               'norm_ffn': D_MODEL, 'ffn_conv_b': 2 * D_FF}
PACK_COLS = 1024


def _round_up(n, m):
    return (n + m - 1) // m * m


def _shard_elems(name):
    shape, _ = SHARDED[name]
    return shape[0] * shape[1] // N_DEV


PACK_ROWS = _round_up(sum(_shard_elems(n) for n in SHARDED_ORDER), SUBLANES * PACK_COLS) // PACK_COLS
SMALL_ROWS = _round_up(DEPTH * sum(SMALL_WIDTH.values()), SUBLANES * PACK_COLS) // PACK_COLS

_IN_SRC = [(0, 512), (512, 1024), (1024, 1536),
           (1540, 2052), (2052, 2564), (2564, 3076),
           (3084, 3596),
           (3596, 4108), (4108, 4620), (4620, 5132),
           (5132, 8204),
           (1536, 1540), (3076, 3080), (3080, 3084)]
N_IN_PAD = 8320
LANE_FF, LANE_GB, LANE_GA = 0, 4, 8


def _pick(dim, pref, unit):
    best = None
    t = unit
    while t <= min(dim, pref):
        if dim % t == 0:
            best = t
        t += unit
    return dim if best is None else best


def _params(n_grid):
    return pltpu.CompilerParams(dimension_semantics=("arbitrary",) * n_grid, vmem_limit_bytes=VMEM_LIMIT)


def _pcall(body, *, name, out_shape, grid, in_specs, out_specs, scratch_shapes=()):
    return pl.pallas_call(body, name=name, out_shape=out_shape, grid=grid, in_specs=in_specs, out_specs=out_specs,
                          scratch_shapes=scratch_shapes, compiler_params=_params(len(grid)),
                          interpret=False)


NN = ((1,), (0,))
NT = ((1,), (1,))
TN = ((0,), (0,))


def _dot(a, b, dn):
    return lax.dot_general(a.astype(BF16), b.astype(BF16), (dn, ((), ())), preferred_element_type=F32)


def _dotf(a, b, dn):
    return lax.dot_general(a, b, (dn, ((), ())), precision=lax.Precision.HIGHEST, preferred_element_type=F32)


def _dot2(a, b01, dn):
    hi = a.astype(BF16)
    lo = (a - hi.astype(F32)).astype(BF16)
    b = b01.astype(BF16)
    return (lax.dot_general(hi, b, (dn, ((), ())), preferred_element_type=F32)
            + lax.dot_general(lo, b, (dn, ((), ())), preferred_element_type=F32))


@jax.custom_vjp
def mm(a, b):
    return _dot(a, b, NN)


mm.defvjp(lambda a, b: (_dot(a, b, NN), (a, b)), lambda r, g: (_dot(g, r[1], NT), _dot(r[0], g, TN)))


@jax.custom_vjp
def mm_nt(a, b):
    return _dot(a, b, NT)


mm_nt.defvjp(lambda a, b: (_dot(a, b, NT), (a, b)), lambda r, g: (_dot(g, r[1], NN), _dot(g, r[0], TN)))


@jax.custom_vjp
def mm_tn(a, b):
    return _dot(a, b, TN)


mm_tn.defvjp(lambda a, b: (_dot(a, b, TN), (a, b)), lambda r, g: (_dot(r[1], g, NT), _dot(r[0], g, NN)))


@jax.custom_vjp
def tri_apply(t, x):
    return _dotf(t, x, NN)


tri_apply.defvjp(lambda t, x: (_dotf(t, x, NN), t), lambda t, g: (jnp.zeros_like(t), _dotf(t, g, TN)))


def _sigmoid(x):
    return 1.0 / (1.0 + jnp.exp(-x))


@jax.custom_vjp
def _softplus(x):
    return jnp.maximum(x, 0.0) + jnp.log(1.0 + jnp.exp(-jnp.abs(x)))


_softplus.defvjp(lambda x: (_softplus(x), x), lambda x, g: (g * _sigmoid(x),))


def _silu(x):
    return x * _sigmoid(x)


def _rms(x, g):
    return x * lax.rsqrt(jnp.mean(x * x, axis=-1, keepdims=True) + EPS) * g


def _iota2(n, m, axis):
    return lax.broadcasted_iota(jnp.int32, (n, m), axis)


def _whole(width):
    return [(0, width)]


def _split(width, n):
    w = width // n
    return [(k * w, w) for k in range(n)]


class RowOp:
    def __init__(self, name, f, in_pieces, out_pieces, tm=256):
        self.name, self.f, self.in_pieces, self.out_pieces, self.tm = name, f, in_pieces, out_pieces, tm
        op = jax.custom_vjp(self._fwd_call)
        op.defvjp(lambda *a: (self._fwd_call(*a), a), lambda res, g: self._bwd_call(res, g))
        self.op = op

    def __call__(self, *args):
        return self.op(*args)

    def _width(self, pieces):
        return max(o + w for o, w in pieces)

    def _row_specs(self, pieces_list, tm):
        return [pl.BlockSpec((tm, self._width(p)), lambda i: (i, 0)) for p in pieces_list]

    def _fwd_call(self, *args):
        nr = len(self.in_pieces)
        rows, params = args[:nr], args[nr:]
        m = rows[0].shape[0]
        tm = _pick(m, self.tm, SUBLANES)
        f, in_pieces, out_pieces = self.f, self.in_pieces, self.out_pieces
        no = len(out_pieces)

        def body(*refs):
            rin, pr, ro = refs[:nr], refs[nr:nr + len(params)], refs[nr + len(params):]
            xs = [r[:, o:o + w] for r, ps in zip(rin, in_pieces) for (o, w) in ps]
            ys = f(*xs, *[p[...] for p in pr])
            k = 0
            for r, ps in zip(ro, out_pieces):
                for (o, w) in ps:
                    r[:, o:o + w] = ys[k]
                    k += 1

        outs = _pcall(
            body, name=self.name + "_fwd",
            out_shape=[jax.ShapeDtypeStruct((m, self._width(p)), F32) for p in out_pieces],
            grid=(m // tm,),
            in_specs=self._row_specs(in_pieces, tm) + [pl.BlockSpec(p.shape, lambda i: (0, 0)) for p in params],
            out_specs=self._row_specs(out_pieces, tm),
        )(*rows, *params)
        return tuple(outs) if no > 1 else outs[0]

    def _bwd_call(self, res, g):
        nr = len(self.in_pieces)
        rows, params = res[:nr], res[nr:]
        no = len(self.out_pieces)
        gs = tuple(g) if no > 1 else (g,)
        m = rows[0].shape[0]
        tm = _pick(m, self.tm, SUBLANES)
        f, in_pieces, out_pieces = self.f, self.in_pieces, self.out_pieces
        npar = len(params)

        def body(*refs):
            rin, pr, dro = refs[:nr], refs[nr:nr + npar], refs[nr + npar:nr + npar + no]
            drin, dpr = refs[nr + npar + no:nr + npar + no + nr], refs[nr + npar + no + nr:]
            xs = [r[:, o:o + w] for r, ps in zip(rin, in_pieces) for (o, w) in ps]
            dys = [r[:, o:o + w] for r, ps in zip(dro, out_pieces) for (o, w) in ps]
            _, vjp = jax.vjp(lambda *a: tuple(f(*a)), *xs, *[p[...] for p in pr])
            grads = vjp(tuple(dys))
            k = 0
            for r, ps in zip(drin, in_pieces):
                for (o, w) in ps:
                    r[:, o:o + w] = grads[k]
                    k += 1

            @pl.when(pl.program_id(0) == 0)
            def _():
                for r in dpr:
                    r[...] = jnp.zeros_like(r)

            for j, r in enumerate(dpr):
                r[...] += grads[k + j]

        outs = _pcall(
            body, name=self.name + "_bwd",
            out_shape=[jax.ShapeDtypeStruct(r.shape, F32) for r in rows]
            + [jax.ShapeDtypeStruct(p.shape, F32) for p in params],
            grid=(m // tm,),
            in_specs=self._row_specs(in_pieces, tm) + [pl.BlockSpec(p.shape, lambda i: (0, 0)) for p in params]
            + self._row_specs(out_pieces, tm),
            out_specs=self._row_specs(in_pieces, tm) + [pl.BlockSpec(p.shape, lambda i: (0, 0)) for p in params],
        )(*rows, *params, *gs)
        return tuple(outs)


def _f_rms(x, g):
    return (_rms(x, g),)


def _f_headnorm(x0, x1, x2, x3, g):
    return tuple(_rms(x, g) for x in (x0, x1, x2, x3))


def _f_small(sm, fb, al, db):
    tm = sm.shape[0]
    logf = -_softplus(-(sm + fb))
    beta = _sigmoid(sm)
    glog = -jnp.exp(al) * _softplus(sm + db)
    r, c = _iota2(tm, tm, 0), _iota2(tm, tm, 1)
    bd = jnp.where((r >= c) & (jnp.bitwise_xor(r, c) < CHUNK), 1.0, 0.0).astype(F32)
    return logf, beta, tri_apply(bd, glog)


def _f_gdnpost(o0, o1, o2, o3, z0, z1, z2, z3, g):
    return tuple(_rms(o, g) * _silu(z) for o, z in zip((o0, o1, o2, o3), (z0, z1, z2, z3)))


def _f_merge(t0, t1, t2, a, b, c, b0, b1, b2):
    return (_sigmoid(t0 + b0) * a + _sigmoid(t1 + b1) * b + _sigmoid(t2 + b2) * c,)


def _make_rowops():
    return dict(
        rms=RowOp("rms", _f_rms, [_whole(D_MODEL)], [_whole(D_MODEL)]),
        headnorm=RowOp("headnorm", _f_headnorm, [_split(HW, HEADS)], [_split(HW, HEADS)]),
        small=RowOp("smallprep", _f_small, [_whole(LANES)], [_whole(LANES)] * 3),
        gdnpost=RowOp("gdnpost", _f_gdnpost, [_split(HW, HEADS)] * 2, [_split(HW, HEADS)]),
        merge=RowOp("merge", _f_merge, [_split(3 * D_MODEL, 3)] + [_whole(D_MODEL)] * 3, [_whole(D_MODEL)]),
    )


def _mm_call(name, a, b, mode, c=None):
    if mode == "nn":
        (m, kc), n = a.shape, b.shape[1]
    elif mode == "nt":
        (m, kc), n = a.shape, b.shape[0]
    else:
        (kc, m), n = a.shape, b.shape[1]
    tm = _pick(m, 512, LANES if mode == "tn" else SUBLANES)
    tn = _pick(n, 512, LANES)
    tk = _pick(kc, 512 if mode == "tn" else 1536, LANES if mode != "tn" else SUBLANES)
    dn = {"nn": NN, "nt": NT, "tn": TN}[mode]
    a_spec = {"nn": pl.BlockSpec((tm, tk), lambda i, j, k: (i, k)),
              "nt": pl.BlockSpec((tm, tk), lambda i, j, k: (i, k)),
              "tn": pl.BlockSpec((tk, tm), lambda i, j, k: (k, i))}[mode]
    b_spec = {"nn": pl.BlockSpec((tk, tn), lambda i, j, k: (k, j)),
              "nt": pl.BlockSpec((tn, tk), lambda i, j, k: (j, k)),
              "tn": pl.BlockSpec((tk, tn), lambda i, j, k: (k, j))}[mode]
    o_spec = pl.BlockSpec((tm, tn), lambda i, j, k: (i, j))
    has_c = c is not None

    def body(*refs):
        a_ref, b_ref = refs[0], refs[1]
        o_ref = refs[-1]

        @pl.when(pl.program_id(2) == 0)
        def _():
            o_ref[...] = refs[2][...] if has_c else jnp.zeros_like(o_ref)

        o_ref[...] += _dot(a_ref[...], b_ref[...], dn)

    return _pcall(body, name=name, out_shape=jax.ShapeDtypeStruct((m, n), F32), grid=(m // tm, n // tn, kc // tk),
                  in_specs=[a_spec, b_spec] + ([o_spec] if has_c else []), out_specs=o_spec,
                  )(*((a, b, c) if has_c else (a, b)))


@jax.custom_vjp
def matmul(a, w):
    return _mm_call("mm_nn", a, w, "nn")


matmul.defvjp(lambda a, w: (_mm_call("mm_nn", a, w, "nn"), (a, w)),
              lambda r, g: (_mm_call("mm_nt", g, r[1], "nt"), _mm_call("mm_tn", r[0], g, "tn")))


@jax.custom_vjp
def matmul_add(c, a, w):
    return _mm_call("mm_nn_add", a, w, "nn", c)


matmul_add.defvjp(lambda c, a, w: (_mm_call("mm_nn_add", a, w, "nn", c), (a, w)),
                  lambda r, g: (g, _mm_call("mm_nt", g, r[1], "nt"), _mm_call("mm_tn", r[0], g, "tn")))

_PROJ_GROUPS = [(0, 512), (512, 512), (1024, 512), (1536, 1536), (3072, 512), (3584, 512), (4096, 512), (4608, 512),
                (5120, 3072), (8192, 128)]


def _proj_impl(h, w):
    return tuple(_mm_call("proj_nn", h, w[:, s:s + n], "nn") for s, n in _PROJ_GROUPS)


proj = jax.custom_vjp(_proj_impl)


def _proj_bwd(res, gs):
    h, w = res
    dh = None
    dws = []
    for (s, n), g in zip(_PROJ_GROUPS, gs):
        dh = _mm_call("proj_nt", g, w[:, s:s + n], "nt", dh)
        dws.append(_mm_call("proj_tn", h, g, "tn"))
    return dh, jnp.concatenate(dws, axis=1)


proj.defvjp(lambda h, w: (_proj_impl(h, w), (h, w)), _proj_bwd)


def _cumsum_call(x, reverse):
    s, w = x.shape
    tm = _pick(s, 256, SUBLANES)
    nb = s // tm

    def body(x_ref, o_ref, carry):
        @pl.when(pl.program_id(0) == 0)
        def _():
            carry[...] = jnp.zeros_like(carry)

        blk = x_ref[...]
        r, c = _iota2(tm, tm, 0), _iota2(tm, tm, 1)
        tri = jnp.where((r <= c) if reverse else (r >= c), 1.0, 0.0).astype(F32)
        o_ref[...] = _dotf(tri, blk, NN) + carry[...]
        carry[...] += jnp.sum(blk, axis=0, keepdims=True)

    idx = (lambda i: (nb - 1 - i, 0)) if reverse else (lambda i: (i, 0))
    return _pcall(body, name="cumsum_rev" if reverse else "cumsum", out_shape=jax.ShapeDtypeStruct((s, w), F32),
                  grid=(nb,), in_specs=[pl.BlockSpec((tm, w), idx)], out_specs=pl.BlockSpec((tm, w), idx),
                  scratch_shapes=[pltpu.VMEM((1, w), F32)])(x)


@jax.custom_vjp
def seq_cumsum(x):
    return _cumsum_call(x, False)


seq_cumsum.defvjp(lambda x: (_cumsum_call(x, False), None), lambda _, g: (_cumsum_call(g, True),))


HALO = SUBLANES


class ConvOp:
    def __init__(self, name, width, post, c_pieces, out_widths, has_bias, tm):
        self.name, self.width, self.post, self.c_pieces = name, width, post, c_pieces
        self.out_widths, self.has_bias, self.tm = out_widths, has_bias, tm
        op = jax.custom_vjp(self._fwd_call)
        op.defvjp(lambda *a: (self._fwd_call(*a), a), lambda res, g: self._bwd_call(res, g))
        self.op = op

    def __call__(self, *args):
        return self.op(*args)

    def _conv(self, i, x_ref, prev_ref, w_ref, b_ref, buf):
        tm = x_ref.shape[0]
        buf[0:HALO, :] = jnp.where(i > 0, prev_ref[...], 0.0)
        buf[HALO:HALO + tm, :] = x_ref[...]
        taps = [buf[pl.ds(HALO - (self.width - 1) + j, tm), :] for j in range(self.width)]
        c = taps[0] * w_ref[0:1, :]
        for j in range(1, self.width):
            c = c + taps[j] * w_ref[j:j + 1, :]
        if self.has_bias:
            c = c + b_ref[...]
        return c, taps

    def _fwd_call(self, x, w, *bias):
        s, ch = x.shape
        tm = _pick(s, self.tm, SUBLANES)
        r8 = tm // HALO
        has_bias, post, c_pieces = self.has_bias, self.post, self.c_pieces

        def body(*refs):
            x_ref, prev_ref, w_ref = refs[:3]
            b_ref = refs[3] if has_bias else None
            outs, buf = refs[3 + has_bias:-1], refs[-1]
            c, _ = self._conv(pl.program_id(0), x_ref, prev_ref, w_ref, b_ref, buf)
            ys = post(*[c[:, o:o + n] for o, n in c_pieces])
            for r, y in zip(outs, ys):
                r[...] = y

        outs = _pcall(
            body, name=self.name + "_fwd", out_shape=[jax.ShapeDtypeStruct((s, n), F32) for n in self.out_widths],
            grid=(s // tm,),
            in_specs=[pl.BlockSpec((tm, ch), lambda i: (i, 0)),
                      pl.BlockSpec((HALO, ch), lambda i: (jnp.maximum(i * r8 - 1, 0), 0)),
                      pl.BlockSpec(w.shape, lambda i: (0, 0))]
            + ([pl.BlockSpec((1, ch), lambda i: (0, 0))] if has_bias else []),
            out_specs=[pl.BlockSpec((tm, n), lambda i: (i, 0)) for n in self.out_widths],
            scratch_shapes=[pltpu.VMEM((tm + HALO, ch), F32)],
        )(x, x, w, *bias)
        return tuple(outs) if len(outs) > 1 else outs[0]

    def _bwd_call(self, res, g):
        x, w = res[0], res[1]
        bias = res[2:]
        gs = tuple(g) if len(self.out_widths) > 1 else (g,)
        s, ch = x.shape
        tm = _pick(s, self.tm, SUBLANES)
        r8 = tm // HALO
        nb = s // tm
        has_bias, post, c_pieces, width = self.has_bias, self.post, self.c_pieces, self.width
        ng = len(gs)

        def body1(*refs):
            x_ref, prev_ref, w_ref = refs[:3]
            b_ref = refs[3] if has_bias else None
            k = 3 + has_bias
            g_refs = refs[k:k + ng]
            dc_ref, dw_ref = refs[k + ng], refs[k + ng + 1]
            db_ref = refs[k + ng + 2] if has_bias else None
            buf = refs[-1]
            i = pl.program_id(0)
            c, taps = self._conv(i, x_ref, prev_ref, w_ref, b_ref, buf)
            _, vjp = jax.vjp(lambda *a: tuple(post(*a)), *[c[:, o:o + n] for o, n in c_pieces])
            dcs = vjp(tuple(r[...] for r in g_refs))
            for (o, n), d in zip(c_pieces, dcs):
                dc_ref[:, o:o + n] = d

            @pl.when(i == 0)
            def _():
                dw_ref[...] = jnp.zeros_like(dw_ref)
                if has_bias:
                    db_ref[...] = jnp.zeros_like(db_ref)

            dc = dc_ref[...]
            for j in range(width):
                dw_ref[j:j + 1, :] += jnp.sum(dc * taps[j], axis=0, keepdims=True)
            if has_bias:
                db_ref[...] += jnp.sum(dc, axis=0, keepdims=True)

        outs1 = _pcall(
            body1, name=self.name + "_bwd_act",
            out_shape=[jax.ShapeDtypeStruct((s, ch), F32), jax.ShapeDtypeStruct(w.shape, F32)]
            + ([jax.ShapeDtypeStruct((1, ch), F32)] if has_bias else []),
            grid=(nb,),
            in_specs=[pl.BlockSpec((tm, ch), lambda i: (i, 0)),
                      pl.BlockSpec((HALO, ch), lambda i: (jnp.maximum(i * r8 - 1, 0), 0)),
                      pl.BlockSpec(w.shape, lambda i: (0, 0))]
            + ([pl.BlockSpec((1, ch), lambda i: (0, 0))] if has_bias else [])
            + [pl.BlockSpec((tm, n), lambda i: (i, 0)) for n in self.out_widths],
            out_specs=[pl.BlockSpec((tm, ch), lambda i: (i, 0)), pl.BlockSpec(w.shape, lambda i: (0, 0))]
            + ([pl.BlockSpec((1, ch), lambda i: (0, 0))] if has_bias else []),
            scratch_shapes=[pltpu.VMEM((tm + HALO, ch), F32)],
        )(x, x, w, *bias, *gs)
        dc, dw = outs1[0], outs1[1]

        def body2(dc_ref, next_ref, w_ref, dx_ref, buf):
            i = pl.program_id(0)
            buf[0:tm, :] = dc_ref[...]
            buf[tm:tm + HALO, :] = jnp.where(i < nb - 1, next_ref[...], 0.0)
            dx = buf[pl.ds(width - 1, tm), :] * w_ref[0:1, :]
            for j in range(1, width):
                dx = dx + buf[pl.ds(width - 1 - j, tm), :] * w_ref[j:j + 1, :]
            dx_ref[...] = dx

        dx = _pcall(
            body2, name=self.name + "_bwd_in", out_shape=jax.ShapeDtypeStruct((s, ch), F32), grid=(nb,),
            in_specs=[pl.BlockSpec((tm, ch), lambda i: (i, 0)),
                      pl.BlockSpec((HALO, ch), lambda i: (jnp.minimum((i + 1) * r8, s // HALO - 1), 0)),
                      pl.BlockSpec(w.shape, lambda i: (0, 0))],
            out_specs=pl.BlockSpec((tm, ch), lambda i: (i, 0)),
            scratch_shapes=[pltpu.VMEM((tm + HALO, ch), F32)],
        )(dc, dc, w)
        return (dx, dw) + ((outs1[2],) if has_bias else ())


def _make_convops():
    return dict(
        gdn=ConvOp("gdnconv", 4, lambda q, k, v: (_silu(q), _silu(k), _silu(v)), _split(3 * HW, 3), [HW] * 3,
                   False, 256),
        ffn=ConvOp("ffnconv", 3, lambda a, b: (_silu(a) * b,), _split(2 * D_FF, 2), [D_FF], True, 128),
    )


ATT_BLOCK = 256
SCALE = HEAD_DIM ** -0.5


def _att_specs(s, t):
    qspec = pl.BlockSpec((t, HEAD_DIM), lambda h, i: (i, h))
    kspec = pl.BlockSpec((s, HEAD_DIM), lambda h, i: (0, h))
    colspec = pl.BlockSpec((None, t, 1), lambda h, i: (h, i, 0))
    rowspec = pl.BlockSpec((None, s // t, 1, t), lambda h, i: (h, 0, 0, 0))
    return qspec, kspec, colspec, rowspec


def _kslice(ref, kb, t):
    return ref[pl.ds(pl.multiple_of(kb * t, t), t), :]


def _fox_fwd_call(q, k, v, ccol, crow):
    s = q.shape[0]
    t = _pick(s, ATT_BLOCK, LANES)
    qspec, kspec, colspec, rowspec = _att_specs(s, t)

    def body(q_ref, k_ref, v_ref, cq_ref, ck_ref, o_ref, lse_ref):
        i = pl.program_id(1)
        qb = q_ref[...].astype(BF16)
        cq = cq_ref[...]
        causal = _iota2(t, t, 1) <= _iota2(t, t, 0)

        def blk(kb, carry, masked):
            m, l, acc = carry
            sc = _dot(qb, _kslice(k_ref, kb, t), NT) * SCALE + (cq - ck_ref[kb])
            if masked:
                sc = jnp.where(causal, sc, NEG)
            m_new = jnp.maximum(m, jnp.max(sc, axis=-1, keepdims=True))
            alpha = jnp.exp(m - m_new)
            p = jnp.exp(sc - m_new)
            return (m_new, alpha * l + jnp.sum(p, axis=-1, keepdims=True),
                    alpha * acc + _dot(p, _kslice(v_ref, kb, t), NN))

        carry = (jnp.full((t, 1), NEG, F32), jnp.zeros((t, 1), F32), jnp.zeros((t, HEAD_DIM), F32))
        carry = lax.fori_loop(0, i, lambda kb, c: blk(kb, c, False), carry)
        m, l, acc = blk(i, carry, True)
        o_ref[...] = acc / l
        lse_ref[...] = m + jnp.log(l)

    return _pcall(body, name="fox_fwd",
                  out_shape=[jax.ShapeDtypeStruct((s, HW), F32), jax.ShapeDtypeStruct((HEADS, s, 1), F32)],
                  grid=(HEADS, s // t), in_specs=[qspec, kspec, kspec, colspec, rowspec],
                  out_specs=[qspec, colspec])(q, k, v, ccol, crow)


def _fox_bwd_call(q, k, v, ccol, crow, o, lse, do):
    s = q.shape[0]
    t = _pick(s, ATT_BLOCK, LANES)
    qspec, kspec, colspec, rowspec = _att_specs(s, t)

    def body(q_ref, k_ref, v_ref, cq_ref, ck_ref, o_ref, lse_ref, do_ref, dq_ref, dk_ref, dv_ref, dcq_ref, dck_ref):
        i = pl.program_id(1)

        @pl.when(i == 0)
        def _():
            dk_ref[...] = jnp.zeros_like(dk_ref)
            dv_ref[...] = jnp.zeros_like(dv_ref)
            dck_ref[...] = jnp.zeros_like(dck_ref)

        qb = q_ref[...].astype(BF16)
        dob = do_ref[...].astype(BF16)
        cq, lse = cq_ref[...], lse_ref[...]
        dl = jnp.sum(do_ref[...] * o_ref[...], axis=-1, keepdims=True)
        causal = _iota2(t, t, 1) <= _iota2(t, t, 0)

        def blk(kb, carry, masked):
            dq, dcq = carry
            kk, vv = _kslice(k_ref, kb, t), _kslice(v_ref, kb, t)
            rows = pl.ds(pl.multiple_of(kb * t, t), t)
            sc = _dot(qb, kk, NT) * SCALE + (cq - ck_ref[kb])
            p = jnp.exp(sc - lse)
            if masked:
                p = jnp.where(causal, p, 0.0)
            dv_ref[rows, :] += _dot(p, dob, TN)
            ds = p * (_dot(dob, vv, NT) - dl)
            dk_ref[rows, :] += _dot(ds, qb, TN) * SCALE
            dck_ref[kb] += -jnp.sum(ds, axis=0, keepdims=True)
            return dq + _dot(ds, kk, NN) * SCALE, dcq + jnp.sum(ds, axis=-1, keepdims=True)

        carry = (jnp.zeros((t, HEAD_DIM), F32), jnp.zeros((t, 1), F32))
        carry = lax.fori_loop(0, i, lambda kb, c: blk(kb, c, False), carry)
        dq, dcq = blk(i, carry, True)
        dq_ref[...] = dq
        dcq_ref[...] = dcq

    return _pcall(body, name="fox_bwd",
                  out_shape=[jax.ShapeDtypeStruct((s, HW), F32)] * 3
                  + [jax.ShapeDtypeStruct((HEADS, s, 1), F32), jax.ShapeDtypeStruct((HEADS, s // t, 1, t), F32)],
                  grid=(HEADS, s // t),
                  in_specs=[qspec, kspec, kspec, colspec, rowspec, qspec, colspec, qspec],
                  out_specs=[qspec, kspec, kspec, colspec, rowspec])(q, k, v, ccol, crow, o, lse, do)


@jax.custom_vjp
def fox_attention(q, k, v, ccol, crow):
    return _fox_fwd_call(q, k, v, ccol, crow)[0]


def _fox_vjp_fwd(q, k, v, ccol, crow):
    o, lse = _fox_fwd_call(q, k, v, ccol, crow)
    return o, (q, k, v, ccol, crow, o, lse)


fox_attention.defvjp(_fox_vjp_fwd, lambda res, g: tuple(_fox_bwd_call(*res, g)))


def _sb_fwd_call(q, k, v):
    s = q.shape[0]
    t = _pick(s, ATT_BLOCK, LANES)
    qspec, kspec, colspec, _ = _att_specs(s, t)

    def body(q_ref, k_ref, v_ref, o_ref, tot_ref):
        i = pl.program_id(1)
        qb = q_ref[...].astype(BF16)
        strict = _iota2(t, t, 1) < _iota2(t, t, 0)
        suffix = jnp.where(_iota2(t, t, 0) >= _iota2(t, t, 1), 1.0, 0.0).astype(BF16)

        def blk(kb, carry, masked):
            run, acc = carry
            z = _dot(qb, _kslice(k_ref, kb, t), NT) * SCALE
            lk = -_softplus(z)
            if masked:
                lk = jnp.where(strict, lk, 0.0)
            a = jnp.exp(z + _dot2(lk, suffix, NN) + run)
            if masked:
                a = jnp.where(strict, a, 0.0)
            return run + jnp.sum(lk, axis=-1, keepdims=True), acc + _dot(a, _kslice(v_ref, kb, t), NN)

        carry = blk(i, (jnp.zeros((t, 1), F32), jnp.zeros((t, HEAD_DIM), F32)), True)
        run, acc = lax.fori_loop(0, i, lambda j, c: blk(i - 1 - j, c, False), carry)
        o_ref[...] = acc
        tot_ref[...] = run

    return _pcall(body, name="sb_fwd",
                  out_shape=[jax.ShapeDtypeStruct((s, HW), F32), jax.ShapeDtypeStruct((HEADS, s, 1), F32)],
                  grid=(HEADS, s // t), in_specs=[qspec, kspec, kspec], out_specs=[qspec, colspec])(q, k, v)


def _sb_bwd_call(q, k, v, tot, do):
    s = q.shape[0]
    t = _pick(s, ATT_BLOCK, LANES)
    qspec, kspec, colspec, _ = _att_specs(s, t)

    def body(q_ref, k_ref, v_ref, tot_ref, do_ref, dq_ref, dk_ref, dv_ref):
        i = pl.program_id(1)

        @pl.when(i == 0)
        def _():
            dk_ref[...] = jnp.zeros_like(dk_ref)
            dv_ref[...] = jnp.zeros_like(dv_ref)

        qb = q_ref[...].astype(BF16)
        dob = do_ref[...].astype(BF16)
        tot = tot_ref[...]
        strict = _iota2(t, t, 1) < _iota2(t, t, 0)
        prefix = jnp.where(_iota2(t, t, 0) <= _iota2(t, t, 1), 1.0, 0.0).astype(BF16)

        def blk(kb, carry, masked):
            left, esum, dq = carry
            kk, vv = _kslice(k_ref, kb, t), _kslice(v_ref, kb, t)
            rows = pl.ds(pl.multiple_of(kb * t, t), t)
            z = _dot(qb, kk, NT) * SCALE
            lk = -_softplus(z)
            if masked:
                lk = jnp.where(strict, lk, 0.0)
            rc = (tot - left) - (_dot2(lk, prefix, NN) - lk)
            a = jnp.exp(z + rc)
            if masked:
                a = jnp.where(strict, a, 0.0)
            e = a * _dot(dob, vv, NT)
            dv_ref[rows, :] += _dot(a, dob, TN)
            dz = e - _sigmoid(z) * (esum + _dot2(e, prefix, NN))
            if masked:
                dz = jnp.where(strict, dz, 0.0)
            dk_ref[rows, :] += _dot(dz, qb, TN) * SCALE
            return (left + jnp.sum(lk, axis=-1, keepdims=True), esum + jnp.sum(e, axis=-1, keepdims=True),
                    dq + _dot(dz, kk, NN) * SCALE)

        carry = (jnp.zeros((t, 1), F32), jnp.zeros((t, 1), F32), jnp.zeros((t, HEAD_DIM), F32))
        carry = lax.fori_loop(0, i, lambda kb, c: blk(kb, c, False), carry)
        dq_ref[...] = blk(i, carry, True)[2]

    return _pcall(body, name="sb_bwd", out_shape=[jax.ShapeDtypeStruct((s, HW), F32)] * 3, grid=(HEADS, s // t),
                  in_specs=[qspec, kspec, kspec, colspec, qspec], out_specs=[qspec, kspec, kspec])(q, k, v, tot, do)


@jax.custom_vjp
def sb_attention(q, k, v):
    return _sb_fwd_call(q, k, v)[0]


def _sb_vjp_fwd(q, k, v):
    o, tot = _sb_fwd_call(q, k, v)
    return o, (q, k, v, tot)


sb_attention.defvjp(_sb_vjp_fwd, lambda res, g: tuple(_sb_bwd_call(*res, g)))


def _mem_specs(s, nk, t):
    return (pl.BlockSpec((t, HEAD_DIM), lambda h, i: (i, h)), pl.BlockSpec((nk, HEAD_DIM), lambda h, i: (0, h)))


def _mem_probs(qb, kk):
    sc = _dot(qb, kk, NT) * SCALE
    p = jnp.exp(sc - jnp.max(sc, axis=-1, keepdims=True))
    return p / jnp.sum(p, axis=-1, keepdims=True)


def _mem_fwd_call(q, k, v):
    s, nk = q.shape[0], k.shape[0]
    t = _pick(s, 512, SUBLANES)
    qspec, kspec = _mem_specs(s, nk, t)

    def body(q_ref, k_ref, v_ref, o_ref):
        o_ref[...] = _dot(_mem_probs(q_ref[...].astype(BF16), k_ref[...]), v_ref[...], NN)

    return _pcall(body, name="mem_fwd", out_shape=jax.ShapeDtypeStruct((s, HW), F32), grid=(HEADS, s // t),
                  in_specs=[qspec, kspec, kspec], out_specs=qspec)(q, k, v)


def _mem_bwd_call(q, k, v, do):
    s, nk = q.shape[0], k.shape[0]
    t = _pick(s, 512, SUBLANES)
    qspec, kspec = _mem_specs(s, nk, t)

    def body(q_ref, k_ref, v_ref, do_ref, dq_ref, dk_ref, dv_ref):
        @pl.when(pl.program_id(1) == 0)
        def _():
            dk_ref[...] = jnp.zeros_like(dk_ref)
            dv_ref[...] = jnp.zeros_like(dv_ref)

        qb = q_ref[...].astype(BF16)
        dob = do_ref[...].astype(BF16)
        p = _mem_probs(qb, k_ref[...])
        dv_ref[...] += _dot(p, dob, TN)
        dp = _dot(dob, v_ref[...], NT)
        ds = p * (dp - jnp.sum(p * dp, axis=-1, keepdims=True))
        dq_ref[...] = _dot(ds, k_ref[...], NN) * SCALE
        dk_ref[...] += _dot(ds, qb, TN) * SCALE

    return _pcall(body, name="mem_bwd",
                  out_shape=[jax.ShapeDtypeStruct((s, HW), F32)] + [jax.ShapeDtypeStruct((nk, HW), F32)] * 2,
                  grid=(HEADS, s // t), in_specs=[qspec, kspec, kspec, qspec],
                  out_specs=[qspec, kspec, kspec])(q, k, v, do)


@jax.custom_vjp
def mem_attention(q, k, v):
    return _mem_fwd_call(q, k, v)


mem_attention.defvjp(lambda q, k, v: (_mem_fwd_call(q, k, v), (q, k, v)),
                     lambda res, g: tuple(_mem_bwd_call(*res, g)))


def _unit_lower_inverse(nm):
    eye = jnp.where(_iota2(CHUNK, CHUNK, 0) == _iota2(CHUNK, CHUNK, 1), 1.0, 0.0).astype(F32)
    p = eye - nm
    m = nm
    for _ in range(5):
        m = _dotf(m, m, NN)
        p = _dotf(p, eye + m, NN)
    return p


@jax.custom_vjp
def _solve2(nm, r1, r2):
    inv = _unit_lower_inverse(nm)
    return _dotf(inv, r1, NN), _dotf(inv, r2, NN)


def _solve2_fwd(nm, r1, r2):
    inv = _unit_lower_inverse(nm)
    u, w = _dotf(inv, r1, NN), _dotf(inv, r2, NN)
    return (u, w), (inv, u, w)


def _solve2_bwd(res, g):
    inv, u, w = res
    d1, d2 = _dotf(inv, g[0], TN), _dotf(inv, g[1], TN)
    return -(_dotf(d1, u, NT) + _dotf(d2, w, NT)), d1, d2


_solve2.defvjp(_solve2_fwd, _solve2_bwd)


def _gdn_chunk(q, k, v, gcc, gcr, b, gl, st):
    qn = q * lax.rsqrt(jnp.sum(q * q, axis=-1, keepdims=True) + EPS) * SCALE
    kn = k * lax.rsqrt(jnp.sum(k * k, axis=-1, keepdims=True) + EPS)
    r, c = _iota2(CHUNK, CHUNK, 0), _iota2(CHUNK, CHUNK, 1)
    decay = jnp.exp(jnp.where(r >= c, gcc - gcr, NEG))
    nm = jnp.where(r > c, b * mm_nt(kn, kn) * decay, 0.0)
    eg = jnp.exp(gcc)
    u, w = _solve2(nm, v * b, kn * (b * eg))
    attn = mm_nt(qn, kn) * decay
    v_new = u - mm(w, st)
    o = mm(qn * eg, st) + mm(attn, v_new)
    st_new = st * jnp.exp(gl) + mm_tn(kn * jnp.exp(gl - gcc), v_new)
    return o, st_new


GDN_ROWS = 512


def _gdn_specs(s, tg, rev):
    nb = s // tg
    cpb = tg // CHUNK
    j_of = (lambda j: nb - 1 - j) if rev else (lambda j: j)
    qspec = pl.BlockSpec((tg, HEAD_DIM), lambda h, j: (j_of(j), h))
    colspec = pl.BlockSpec((None, tg, 1), lambda h, j: (h, j_of(j), 0))
    rowspec = pl.BlockSpec((None, cpb, 1, CHUNK), lambda h, j: (h, j_of(j), 0, 0))
    onespec = pl.BlockSpec((None, cpb, 1, 1), lambda h, j: (h, j_of(j), 0, 0))
    stspec = pl.BlockSpec((None, cpb, HEAD_DIM, HEAD_DIM), lambda h, j: (h, j_of(j), 0, 0))
    return qspec, colspec, rowspec, onespec, stspec


def _gdn_fwd_call(q, k, v, gcc, gcr, bc, gl):
    s = q.shape[0]
    tg = _pick(s, GDN_ROWS, CHUNK)
    cpb = tg // CHUNK
    qspec, colspec, rowspec, onespec, stspec = _gdn_specs(s, tg, False)

    def body(q_ref, k_ref, v_ref, gcc_ref, gcr_ref, b_ref, gl_ref, o_ref, st_ref, st):
        @pl.when(pl.program_id(1) == 0)
        def _():
            st[...] = jnp.zeros_like(st)

        def chunk(ci, _):
            rows = pl.ds(pl.multiple_of(ci * CHUNK, CHUNK), CHUNK)
            s_in = st[...]
            st_ref[ci] = s_in
            o, s_new = _gdn_chunk(q_ref[rows, :], k_ref[rows, :], v_ref[rows, :], gcc_ref[rows, :], gcr_ref[ci],
                                  b_ref[rows, :], gl_ref[ci], s_in)
            o_ref[rows, :] = o
            st[...] = s_new
            return 0

        lax.fori_loop(0, cpb, chunk, 0)

    return _pcall(body, name="gdn_fwd",
                  out_shape=[jax.ShapeDtypeStruct((s, HW), F32),
                             jax.ShapeDtypeStruct((HEADS, s // CHUNK, HEAD_DIM, HEAD_DIM), F32)],
                  grid=(HEADS, s // tg), in_specs=[qspec, qspec, qspec, colspec, rowspec, colspec, onespec],
                  out_specs=[qspec, stspec], scratch_shapes=[pltpu.VMEM((HEAD_DIM, HEAD_DIM), F32)],
                  )(q, k, v, gcc, gcr, bc, gl)


def _gdn_bwd_call(q, k, v, gcc, gcr, bc, gl, states, do):
    s = q.shape[0]
    tg = _pick(s, GDN_ROWS, CHUNK)
    cpb = tg // CHUNK
    qspec, colspec, rowspec, onespec, stspec = _gdn_specs(s, tg, True)

    def body(q_ref, k_ref, v_ref, gcc_ref, gcr_ref, b_ref, gl_ref, st_ref, do_ref,
             dq_ref, dk_ref, dv_ref, dgcc_ref, dgcr_ref, db_ref, dgl_ref, dst):
        @pl.when(pl.program_id(1) == 0)
        def _():
            dst[...] = jnp.zeros_like(dst)

        def chunk(n, _):
            ci = cpb - 1 - n
            rows = pl.ds(pl.multiple_of(ci * CHUNK, CHUNK), CHUNK)
            _, vjp = jax.vjp(_gdn_chunk, q_ref[rows, :], k_ref[rows, :], v_ref[rows, :], gcc_ref[rows, :],
                             gcr_ref[ci], b_ref[rows, :], gl_ref[ci], st_ref[ci])
            dq, dk, dv, dgcc, dgcr, db, dgl, ds_in = vjp((do_ref[rows, :], dst[...]))
            dq_ref[rows, :] = dq
            dk_ref[rows, :] = dk
            dv_ref[rows, :] = dv
            dgcc_ref[rows, :] = dgcc
            dgcr_ref[ci] = dgcr
            db_ref[rows, :] = db
            dgl_ref[ci] = dgl
            dst[...] = ds_in
            return 0

        lax.fori_loop(0, cpb, chunk, 0)

    n = s // CHUNK
    return _pcall(body, name="gdn_bwd",
                  out_shape=[jax.ShapeDtypeStruct((s, HW), F32)] * 3
                  + [jax.ShapeDtypeStruct((HEADS, s, 1), F32), jax.ShapeDtypeStruct((HEADS, n, 1, CHUNK), F32),
                     jax.ShapeDtypeStruct((HEADS, s, 1), F32), jax.ShapeDtypeStruct((HEADS, n, 1, 1), F32)],
                  grid=(HEADS, s // tg),
                  in_specs=[qspec, qspec, qspec, colspec, rowspec, colspec, onespec, stspec, qspec],
                  out_specs=[qspec, qspec, qspec, colspec, rowspec, colspec, onespec],
                  scratch_shapes=[pltpu.VMEM((HEAD_DIM, HEAD_DIM), F32)],
                  )(q, k, v, gcc, gcr, bc, gl, states, do)


@jax.custom_vjp
def gated_delta(q, k, v, gcc, gcr, bc, gl):
    return _gdn_fwd_call(q, k, v, gcc, gcr, bc, gl)[0]


def _gdn_vjp_fwd(q, k, v, gcc, gcr, bc, gl):
    o, states = _gdn_fwd_call(q, k, v, gcc, gcr, bc, gl)
    return o, (q, k, v, gcc, gcr, bc, gl, states)


gated_delta.defvjp(_gdn_vjp_fwd, lambda res, g: tuple(_gdn_bwd_call(*res, g)))


def _loss_call(y, target):
    s, d = y.shape
    tm = _pick(s, 512, SUBLANES)

    def body(y_ref, t_ref, dy_ref, loss_ref):
        @pl.when(pl.program_id(0) == 0)
        def _():
            loss_ref[...] = jnp.zeros_like(loss_ref)

        err = y_ref[...] - t_ref[...]
        dy_ref[...] = err * (1.0 / d)
        loss_ref[...] += 0.5 * jnp.sum(jnp.mean(err * err, axis=-1, keepdims=True), axis=0, keepdims=True)

    dy, part = _pcall(body, name="loss_head",
                      out_shape=[jax.ShapeDtypeStruct((s, d), F32), jax.ShapeDtypeStruct((1, 1), F32)],
                      grid=(s // tm,), in_specs=[pl.BlockSpec((tm, d), lambda i: (i, 0))] * 2,
                      out_specs=[pl.BlockSpec((tm, d), lambda i: (i, 0)), pl.BlockSpec((1, 1), lambda i: (0, 0))],
                      )(y, target)
    return part[0, 0], dy


def _cols_and_rows(a, lane0, t):
    s = a.shape[0]
    at = a[:, lane0:lane0 + HEADS].T
    return at, at[:, :, None], at.reshape(HEADS, s // t, 1, t)


def _pad_lanes(v, lane0):
    return jnp.pad(v, (lane0, LANES - lane0 - v.shape[0])).reshape(1, LANES)


def _layer(x, mem, w, ops, convs):
    s = x.shape[0]
    row = lambda v: v.reshape(1, -1)
    h = ops["rms"](x, row(w["norm_mix"]))
    fq, fk, fv, gqkv, gz, sq, sk, sv, gt, gm = proj(h, w["w_in"])

    logf, beta, gc = ops["small"](gm, _pad_lanes(w["fox_fbias"], LANE_FF), _pad_lanes(w["gdn_a_log"], LANE_GA),
                                  _pad_lanes(w["gdn_dt_bias"], LANE_GA))
    t = _pick(s, ATT_BLOCK, LANES)
    _, ccol, crow = _cols_and_rows(seq_cumsum(logf), LANE_FF, t)
    ya = fox_attention(ops["headnorm"](fq, row(w["fox_qnorm"])), ops["headnorm"](fk, row(w["fox_knorm"])), fv,
                       ccol, crow)
    cq, ck, cv = convs["gdn"](gqkv, w["gdn_conv"])
    gct, gcc, gcr = _cols_and_rows(gc, LANE_GA, CHUNK)
    gl = gct.reshape(HEADS, s // CHUNK, CHUNK)[:, :, CHUNK - 1].reshape(HEADS, s // CHUNK, 1, 1)
    bc = beta[:, LANE_GB:LANE_GB + HEADS].T[:, :, None]
    yb = ops["gdnpost"](gated_delta(cq, ck, cv, gcc, gcr, bc, gl), gz, row(w["gdn_onorm"]))
    yc = sb_attention(sq, sk, sv)
    gb = w["gate_bias"]
    mixed = ops["merge"](gt, matmul(ya, w["w_oa"]), matmul(yb, w["w_ob"]), matmul(yc, w["w_oc"]),
                         row(gb[:D_MODEL]), row(gb[D_MODEL:2 * D_MODEL]), row(gb[2 * D_MODEL:]))
    x = matmul_add(x, mixed, w["w_out"])
    mq = ops["headnorm"](matmul(ops["rms"](x, row(w["norm_xq"])), w["w_mq"]), row(w["mq_norm"]))
    kv = matmul(ops["rms"](mem, row(w["norm_mem"])), w["w_mkv"])
    mk = ops["headnorm"](kv[:, :HW], row(w["mk_norm"]))
    x = matmul_add(x, mem_attention(mq, mk, kv[:, HW:]), w["w_mo"])
    u = matmul(ops["rms"](x, row(w["norm_ffn"])), w["w_up"])
    act = convs["ffn"](u, w["ffn_conv"], row(w["ffn_conv_b"]))
    return matmul_add(x, act, w["w_down"])


def _forward(x, mem, layers):
    ops, convs = _make_rowops(), _make_convops()
    for w in layers:
        x = _layer(x, mem, w, ops, convs)
    return x


ANY = pl.BlockSpec(memory_space=pl.ANY)


def _ccall(body, *, name, out_shape, n_in, scratch_shapes):
    return pl.pallas_call(body, name=name, out_shape=out_shape, in_specs=[ANY] * n_in, out_specs=ANY,
                          scratch_shapes=scratch_shapes,
                          interpret=False)


def _all_gather(name, x_shard):
    m_per, n = x_shard.shape

    def body(x_ref, out_ref, send_sems, recv_sems, local_sem):
        x, y, c = lax.axis_index("x"), lax.axis_index("y"), lax.axis_index("c")
        me, sibling = (x, y, c), (x, y, 1 - c)
        chips = [(1 - x, y), (x, 1 - y), (1 - x, 1 - y)]

        def rows(px, py, pc):
            return out_ref.at[pl.ds((4 * px + 2 * py + pc) * m_per, m_per), :]

        def copy(k, block, to, src=None):
            return pltpu.make_async_remote_copy(
                src_ref=rows(*block) if src is None else src, dst_ref=rows(*block),
                send_sem=send_sems.at[k], recv_sem=recv_sems.at[k], device_id=to, device_id_type=MESH)

        mine = pltpu.make_async_copy(x_ref, rows(*me), local_sem)
        mine.start()
        first = [copy(0, me, sibling, src=x_ref)]
        first += [copy(1 + j, me, (*chip, c), src=x_ref) for j, chip in enumerate(chips)]
        for cp in first:
            cp.start()
        passed = [copy(4 + j, (*chip, c), sibling) for j, chip in enumerate(chips)]
        for j, chip in enumerate(chips):
            copy(1 + j, (*chip, c), me).wait_recv()
            passed[j].start()
        copy(0, sibling, me).wait_recv()
        for j, chip in enumerate(chips):
            copy(4 + j, (*chip, 1 - c), me).wait_recv()
        for cp in first + passed:
            cp.wait_send()
        mine.wait()

    return _ccall(body, name=name, out_shape=jax.ShapeDtypeStruct((N_DEV * m_per, n), x_shard.dtype), n_in=1,
                  scratch_shapes=[pltpu.SemaphoreType.DMA((7,)), pltpu.SemaphoreType.DMA((7,)),
                                  pltpu.SemaphoreType.DMA])(x_shard)


def _exchange(name, parts):
    _, r, n = parts.shape

    def body(p_ref, out_ref, send_sems, recv_sems, local_sem):
        x, y, c = lax.axis_index("x"), lax.axis_index("y"), lax.axis_index("c")
        me = 4 * x + 2 * y + c

        def peer(k):
            return (x ^ ((k >> 2) & 1), y ^ ((k >> 1) & 1), c ^ (k & 1))

        def copy(k):
            px, py, pc = peer(k)
            return pltpu.make_async_remote_copy(
                src_ref=p_ref.at[4 * px + 2 * py + pc], dst_ref=out_ref.at[me],
                send_sem=send_sems.at[k - 1], recv_sem=recv_sems.at[k - 1], device_id=(px, py, pc),
                device_id_type=MESH)

        def arrival(k):
            px, py, pc = peer(k)
            slot = out_ref.at[4 * px + 2 * py + pc]
            return pltpu.make_async_remote_copy(
                src_ref=slot, dst_ref=slot, send_sem=send_sems.at[k - 1], recv_sem=recv_sems.at[k - 1],
                device_id=(px, py, pc), device_id_type=MESH)

        mine = pltpu.make_async_copy(p_ref.at[me], out_ref.at[me], local_sem)
        mine.start()
        sends = [copy(k) for k in range(1, N_DEV)]
        for cp in sends:
            cp.start()
        for k in range(1, N_DEV):
            arrival(k).wait_recv()
        for cp in sends:
            cp.wait_send()
        mine.wait()

    return _ccall(body, name=name, out_shape=jax.ShapeDtypeStruct(parts.shape, parts.dtype), n_in=1,
                  scratch_shapes=[pltpu.SemaphoreType.DMA((7,)), pltpu.SemaphoreType.DMA((7,)),
                                  pltpu.SemaphoreType.DMA])(parts)


def _adam_call(name, w, slots, m, v):
    r, n = w.shape
    tr = _pick(r, 256, SUBLANES)
    spec = pl.BlockSpec((tr, n), lambda i: (i, 0))

    def body(w_ref, s_ref, m_ref, v_ref, g_ref, d_ref, nm_ref, nv_ref):
        g = s_ref[0]
        for d in range(1, N_DEV):
            g = g + s_ref[d]
        nm = ADAM_B1 * m_ref[...] + (1.0 - ADAM_B1) * g
        nv = ADAM_B2 * v_ref[...] + (1.0 - ADAM_B2) * (g * g)
        m_hat = nm / (1.0 - ADAM_B1 ** ADAM_STEP)
        v_hat = nv / (1.0 - ADAM_B2 ** ADAM_STEP)
        g_ref[...] = g
        d_ref[...] = -ADAM_LR * (m_hat / (jnp.sqrt(v_hat) + ADAM_EPS) + ADAM_WD * w_ref[...])
        nm_ref[...] = nm
        nv_ref[...] = nv

    return _pcall(body, name=name, out_shape=[jax.ShapeDtypeStruct((r, n), F32)] * 4, grid=(r // tr,),
                  in_specs=[spec, pl.BlockSpec((N_DEV, tr, n), lambda i: (0, i, 0)), spec, spec],
                  out_specs=[spec] * 4)(w, slots, m, v)


def _pack_rows(flat, rows):
    return jnp.pad(flat, (0, rows * PACK_COLS - flat.shape[0])).reshape(rows, PACK_COLS)


def _pack_local(shards, l):
    return _pack_rows(jnp.concatenate([shards[n][l].reshape(-1) for n in SHARDED_ORDER]), PACK_ROWS)


def _unpack_local(packed, like):
    flat = packed.reshape(-1)
    out, off = {}, 0
    for n in SHARDED_ORDER:
        shape = like[n].shape[1:]
        size = math.prod(shape)
        out[n] = flat[off:off + size].reshape(shape)
        off += size
    return out


def _regroup_in(w_in):
    cols = [w_in[:, a:b] for a, b in _IN_SRC]
    return jnp.concatenate(cols + [jnp.zeros((w_in.shape[0], N_IN_PAD - N_IN), F32)], axis=1)


def _ungroup_in(d):
    starts = {}
    off = 0
    for a, b in _IN_SRC:
        starts[a] = (off, b - a)
        off += b - a
    return jnp.concatenate([d[:, starts[a][0]:starts[a][0] + starts[a][1]] for a in sorted(starts)], axis=1)


def _unpack_gathered(gathered):
    blocks = gathered.reshape(N_DEV, -1)
    out, off = {}, 0
    for n in SHARDED_ORDER:
        (r, c), axis = SHARDED[n]
        size = r * c // N_DEV
        part = blocks[:, off:off + size]
        if axis == 0:
            out[n] = part.reshape(r, c)
        else:
            out[n] = part.reshape(N_DEV, r, c // N_DEV).transpose(1, 0, 2).reshape(r, c)
        off += size
    out["w_in"] = _regroup_in(out["w_in"])
    return out


def _pack_parts(grads):
    pieces = []
    for n in SHARDED_ORDER:
        (r, c), axis = SHARDED[n]
        g = _ungroup_in(grads[n]) if n == "w_in" else grads[n]
        if axis == 0:
            pieces.append(g.reshape(N_DEV, -1))
        else:
            pieces.append(g.reshape(r, N_DEV, c // N_DEV).transpose(1, 0, 2).reshape(N_DEV, -1))
    flat = jnp.concatenate(pieces, axis=1)
    flat = jnp.pad(flat, ((0, 0), (0, PACK_ROWS * PACK_COLS - flat.shape[1])))
    return flat.reshape(N_DEV, PACK_ROWS, PACK_COLS)


def _pack_small(vals):
    return _pack_rows(jnp.concatenate([vals[n].reshape(-1) for n in SMALL_ORDER]), SMALL_ROWS)


def _unpack_small(packed):
    flat = packed.reshape(-1)
    out, off = {}, 0
    for n in SMALL_ORDER:
        size = DEPTH * SMALL_WIDTH[n]
        out[n] = flat[off:off + size].reshape(DEPTH, SMALL_WIDTH[n])
        off += size
    return out


def kernel(x, mem, norm_mix, w_in, fox_fbias, fox_qnorm, fox_knorm, gdn_conv, gdn_a_log, gdn_dt_bias, gdn_onorm, gate_bias, w_oa, w_ob, w_oc, w_out, norm_xq, norm_mem, w_mq, w_mkv, mq_norm, mk_norm, w_mo, norm_ffn, w_up, ffn_conv, ffn_conv_b, w_down, loss_target, m_norm_mix, m_w_in, m_fox_fbias, m_fox_qnorm, m_fox_knorm, m_gdn_conv, m_gdn_a_log, m_gdn_dt_bias, m_gdn_onorm, m_gate_bias, m_w_oa, m_w_ob, m_w_oc, m_w_out, m_norm_xq, m_norm_mem, m_w_mq, m_w_mkv, m_mq_norm, m_mk_norm, m_w_mo, m_norm_ffn, m_w_up, m_ffn_conv, m_ffn_conv_b, m_w_down, v_norm_mix, v_w_in, v_fox_fbias, v_fox_qnorm, v_fox_knorm, v_gdn_conv, v_gdn_a_log, v_gdn_dt_bias, v_gdn_onorm, v_gate_bias, v_w_oa, v_w_ob, v_w_oc, v_w_out, v_norm_xq, v_norm_mem, v_w_mq, v_w_mkv, v_mq_norm, v_mk_norm, v_w_mo, v_norm_ffn, v_w_up, v_ffn_conv, v_ffn_conv_b, v_w_down):
    given = dict(locals())
    wts = {n: given[n] for n in WEIGHTS}
    mom = {n: given["m_" + n] for n in WEIGHTS}
    var = {n: given["v_" + n] for n in WEIGHTS}

    packed_w = [_pack_local(wts, l) for l in range(DEPTH)]
    layers = []
    for l in range(DEPTH):
        full = _unpack_gathered(_all_gather("gather_weights", packed_w[l]))
        full.update({n: wts[n][l] for n in SMALL_ORDER})
        layers.append(full)

    y, vjp = jax.vjp(lambda xx, ww: _forward(xx, mem[0], ww), x[0], layers)
    loss_part, dy = _loss_call(y, loss_target[0])
    dx, dlayers = vjp(dy)
    loss = lax.psum(loss_part, ("x", "y", "c"))

    out = {}
    per_layer = []
    for l in range(DEPTH):
        slots = _exchange("exchange_grads", _pack_parts(dlayers[l]))
        res = _adam_call("adam_shards", packed_w[l], slots, _pack_local(mom, l), _pack_local(var, l))
        per_layer.append([_unpack_local(r, wts) for r in res])
    for n in SHARDED_ORDER:
        for k, kind in enumerate(("grad_", "delta_", "new_m_", "new_v_")):
            out[kind + n] = jnp.stack([per_layer[l][k][n] for l in range(DEPTH)])
    dsmall = {n: jnp.stack([dlayers[l][n] for l in range(DEPTH)]) for n in SMALL_ORDER}
    slots = _all_gather("gather_small_grads", _pack_small(dsmall)).reshape(N_DEV, SMALL_ROWS, PACK_COLS)
    res = _adam_call("adam_small", _pack_small(wts), slots, _pack_small(mom), _pack_small(var))
    for k, kind in enumerate(("grad_", "delta_", "new_m_", "new_v_")):
        un = _unpack_small(res[k])
        for n in SMALL_ORDER:
            out[kind + n] = un[n].reshape(wts[n].shape)

    return (loss, dx[None], *[out["grad_" + n] for n in WEIGHTS], *[out["delta_" + n] for n in WEIGHTS],
            *[out["new_m_" + n] for n in WEIGHTS], *[out["new_v_" + n] for n in WEIGHTS])
```

```python
import jax
import jax.numpy as jnp
from jax import lax
from jax.experimental import pallas as pl
from jax.experimental.pallas import tpu as pltpu

F32 = jnp.float32
BF16 = jnp.bfloat16

N_DEV = 8
D_MODEL = 1024
DEPTH = 4
CHUNK = 64
EPS = 1e-6
HEADS = 4
HEAD_DIM = 128
HW = HEADS * HEAD_DIM
D_FF = 2816
N_IN = 8204
LANES = 128
SUBLANES = 8
VMEM_LIMIT = 56 * 1024 * 1024

ADAM_LR = 0.001
ADAM_B1 = 0.9
ADAM_B2 = 0.999
ADAM_EPS = 1e-08
ADAM_WD = 0.01
ADAM_STEP = 10

NEG = -1e30
MESH = pl.DeviceIdType.MESH

WEIGHTS = ['norm_mix', 'w_in', 'fox_fbias', 'fox_qnorm', 'fox_knorm', 'gdn_conv', 'gdn_a_log', 'gdn_dt_bias',
           'gdn_onorm', 'gate_bias', 'w_oa', 'w_ob', 'w_oc', 'w_out', 'norm_xq', 'norm_mem', 'w_mq', 'w_mkv',
           'mq_norm', 'mk_norm', 'w_mo', 'norm_ffn', 'w_up', 'ffn_conv', 'ffn_conv_b', 'w_down']
SHARDED = {
    'w_in': ((D_MODEL, N_IN), 0), 'gdn_conv': ((4, 3 * HW), 1), 'w_oa': ((HW, D_MODEL), 1),
    'w_ob': ((HW, D_MODEL), 1), 'w_oc': ((HW, D_MODEL), 1), 'w_out': ((D_MODEL, D_MODEL), 0),
    'w_mq': ((D_MODEL, HW), 0), 'w_mkv': ((D_MODEL, 2 * HW), 0), 'w_mo': ((HW, D_MODEL), 1),
    'w_up': ((D_MODEL, 2 * D_FF), 1), 'ffn_conv': ((3, 2 * D_FF), 1), 'w_down': ((D_FF, D_MODEL), 0),
}
SHARDED_ORDER = [n for n in WEIGHTS if n in SHARDED]
SMALL_ORDER = [n for n in WEIGHTS if n not in SHARDED]
SMALL_WIDTH = {'norm_mix': D_MODEL, 'fox_fbias': HEADS, 'fox_qnorm': HEAD_DIM, 'fox_knorm': HEAD_DIM,
               'gdn_a_log': HEADS, 'gdn_dt_bias': HEADS, 'gdn_onorm': HEAD_DIM, 'gate_bias': 3 * D_MODEL,
               'norm_xq': D_MODEL, 'norm_mem': D_MODEL, 'mq_norm': HEAD_DIM, 'mk_norm': HEAD_DIM,
               'norm_ffn': D_MODEL, 'ffn_conv_b': 2 * D_FF}
PACK_COLS = 1024


def _round_up(n, m):
    return (n + m - 1) // m * m


SMALL_ROWS = _round_up(DEPTH * sum(SMALL_WIDTH.values()), SUBLANES * PACK_COLS) // PACK_COLS

_IN_SRC = [(0, 512), (512, 1024), (1024, 1536),
           (1540, 2052), (2052, 2564), (2564, 3076),
           (3084, 3596),
           (3596, 4108), (4108, 4620), (4620, 5132),
           (5132, 8204),
           (1536, 1540), (3076, 3080), (3080, 3084)]
N_IN_PAD = 8320
LANE_FF, LANE_GB, LANE_GA = 0, 4, 8


def _pick(dim, pref, unit):
    best = None
    t = unit
    while t <= min(dim, pref):
        if dim % t == 0:
            best = t
        t += unit
    return dim if best is None else best


def _params(n_grid):
    return pltpu.CompilerParams(dimension_semantics=("arbitrary",) * n_grid, vmem_limit_bytes=VMEM_LIMIT)


def _pcall(body, *, name, out_shape, grid, in_specs, out_specs, scratch_shapes=()):
    return pl.pallas_call(body, name=name, out_shape=out_shape, grid=grid, in_specs=in_specs, out_specs=out_specs,
                          scratch_shapes=scratch_shapes, compiler_params=_params(len(grid)),
                          interpret=False)


NN = ((1,), (0,))
NT = ((1,), (1,))
TN = ((0,), (0,))


def _dot(a, b, dn):
    return lax.dot_general(a.astype(BF16), b.astype(BF16), (dn, ((), ())), preferred_element_type=F32)


def _dotf(a, b, dn):
    return lax.dot_general(a, b, (dn, ((), ())), precision=lax.Precision.HIGHEST, preferred_element_type=F32)


def _dot2(a, b01, dn):
    hi = a.astype(BF16)
    lo = (a - hi.astype(F32)).astype(BF16)
    b = b01.astype(BF16)
    return (lax.dot_general(hi, b, (dn, ((), ())), preferred_element_type=F32)
            + lax.dot_general(lo, b, (dn, ((), ())), preferred_element_type=F32))


@jax.custom_vjp
def mm(a, b):
    return _dot(a, b, NN)


mm.defvjp(lambda a, b: (_dot(a, b, NN), (a, b)), lambda r, g: (_dot(g, r[1], NT), _dot(r[0], g, TN)))


@jax.custom_vjp
def mm_nt(a, b):
    return _dot(a, b, NT)


mm_nt.defvjp(lambda a, b: (_dot(a, b, NT), (a, b)), lambda r, g: (_dot(g, r[1], NN), _dot(g, r[0], TN)))


@jax.custom_vjp
def mm_tn(a, b):
    return _dot(a, b, TN)


mm_tn.defvjp(lambda a, b: (_dot(a, b, TN), (a, b)), lambda r, g: (_dot(r[1], g, NT), _dot(r[0], g, NN)))


@jax.custom_vjp
def tri_apply(t, x):
    return _dotf(t, x, NN)


tri_apply.defvjp(lambda t, x: (_dotf(t, x, NN), t), lambda t, g: (jnp.zeros_like(t), _dotf(t, g, TN)))


def _sigmoid(x):
    return 1.0 / (1.0 + jnp.exp(-x))


@jax.custom_vjp
def _softplus(x):
    return jnp.maximum(x, 0.0) + jnp.log(1.0 + jnp.exp(-jnp.abs(x)))


_softplus.defvjp(lambda x: (_softplus(x), x), lambda x, g: (g * _sigmoid(x),))


def _silu(x):
    return x * _sigmoid(x)


def _rms(x, g):
    return x * lax.rsqrt(jnp.mean(x * x, axis=-1, keepdims=True) + EPS) * g


def _iota2(n, m, axis):
    return lax.broadcasted_iota(jnp.int32, (n, m), axis)


def _whole(width):
    return [(0, width)]


def _split(width, n):
    w = width // n
    return [(k * w, w) for k in range(n)]


class RowOp:
    def __init__(self, name, f, in_pieces, out_pieces, tm=256):
        self.name, self.f, self.in_pieces, self.out_pieces, self.tm = name, f, in_pieces, out_pieces, tm
        op = jax.custom_vjp(self._fwd_call)
        op.defvjp(lambda *a: (self._fwd_call(*a), a), lambda res, g: self._bwd_call(res, g))
        self.op = op

    def __call__(self, *args):
        return self.op(*args)

    def _width(self, pieces):
        return max(o + w for o, w in pieces)

    def _row_specs(self, pieces_list, tm):
        return [pl.BlockSpec((tm, self._width(p)), lambda i: (i, 0)) for p in pieces_list]

    def _fwd_call(self, *args):
        nr = len(self.in_pieces)
        rows, params = args[:nr], args[nr:]
        m = rows[0].shape[0]
        tm = _pick(m, self.tm, SUBLANES)
        f, in_pieces, out_pieces = self.f, self.in_pieces, self.out_pieces
        no = len(out_pieces)

        def body(*refs):
            rin, pr, ro = refs[:nr], refs[nr:nr + len(params)], refs[nr + len(params):]
            xs = [r[:, o:o + w] for r, ps in zip(rin, in_pieces) for (o, w) in ps]
            ys = f(*xs, *[p[...] for p in pr])
            k = 0
            for r, ps in zip(ro, out_pieces):
                for (o, w) in ps:
                    r[:, o:o + w] = ys[k]
                    k += 1

        outs = _pcall(
            body, name=self.name + "_fwd",
            out_shape=[jax.ShapeDtypeStruct((m, self._width(p)), F32) for p in out_pieces],
            grid=(m // tm,),
            in_specs=self._row_specs(in_pieces, tm) + [pl.BlockSpec(p.shape, lambda i: (0, 0)) for p in params],
            out_specs=self._row_specs(out_pieces, tm),
        )(*rows, *params)
        return tuple(outs) if no > 1 else outs[0]

    def _bwd_call(self, res, g):
        nr = len(self.in_pieces)
        rows, params = res[:nr], res[nr:]
        no = len(self.out_pieces)
        gs = tuple(g) if no > 1 else (g,)
        m = rows[0].shape[0]
        tm = _pick(m, self.tm, SUBLANES)
        f, in_pieces, out_pieces = self.f, self.in_pieces, self.out_pieces
        npar = len(params)

        def body(*refs):
            rin, pr, dro = refs[:nr], refs[nr:nr + npar], refs[nr + npar:nr + npar + no]
            drin, dpr = refs[nr + npar + no:nr + npar + no + nr], refs[nr + npar + no + nr:]
            xs = [r[:, o:o + w] for r, ps in zip(rin, in_pieces) for (o, w) in ps]
            dys = [r[:, o:o + w] for r, ps in zip(dro, out_pieces) for (o, w) in ps]
            _, vjp = jax.vjp(lambda *a: tuple(f(*a)), *xs, *[p[...] for p in pr])
            grads = vjp(tuple(dys))
            k = 0
            for r, ps in zip(drin, in_pieces):
                for (o, w) in ps:
                    r[:, o:o + w] = grads[k]
                    k += 1

            @pl.when(pl.program_id(0) == 0)
            def _():
                for r in dpr:
                    r[...] = jnp.zeros_like(r)

            for j, r in enumerate(dpr):
                r[...] += grads[k + j]

        outs = _pcall(
            body, name=self.name + "_bwd",
            out_shape=[jax.ShapeDtypeStruct(r.shape, F32) for r in rows]
            + [jax.ShapeDtypeStruct(p.shape, F32) for p in params],
            grid=(m // tm,),
            in_specs=self._row_specs(in_pieces, tm) + [pl.BlockSpec(p.shape, lambda i: (0, 0)) for p in params]
            + self._row_specs(out_pieces, tm),
            out_specs=self._row_specs(in_pieces, tm) + [pl.BlockSpec(p.shape, lambda i: (0, 0)) for p in params],
        )(*rows, *params, *gs)
        return tuple(outs)


def _f_rms(x, g):
    return (_rms(x, g),)


def _f_headnorm(x0, x1, x2, x3, g):
    return tuple(_rms(x, g) for x in (x0, x1, x2, x3))


def _f_small(sm, fb, al, db):
    tm = sm.shape[0]
    logf = -_softplus(-(sm + fb))
    beta = _sigmoid(sm)
    glog = -jnp.exp(al) * _softplus(sm + db)
    r, c = _iota2(tm, tm, 0), _iota2(tm, tm, 1)
    bd = jnp.where((r >= c) & (jnp.bitwise_xor(r, c) < CHUNK), 1.0, 0.0).astype(F32)
    return logf, beta, tri_apply(bd, glog)


def _f_gdnpost(o0, o1, o2, o3, z0, z1, z2, z3, g):
    return tuple(_rms(o, g) * _silu(z) for o, z in zip((o0, o1, o2, o3), (z0, z1, z2, z3)))


def _f_merge(t0, t1, t2, a, b, c, b0, b1, b2):
    return (_sigmoid(t0 + b0) * a + _sigmoid(t1 + b1) * b + _sigmoid(t2 + b2) * c,)


def _make_rowops():
    return dict(
        rms=RowOp("rms", _f_rms, [_whole(D_MODEL)], [_whole(D_MODEL)]),
        headnorm=RowOp("headnorm", _f_headnorm, [_split(HW, HEADS)], [_split(HW, HEADS)]),
        small=RowOp("smallprep", _f_small, [_whole(LANES)], [_whole(LANES)] * 3),
        gdnpost=RowOp("gdnpost", _f_gdnpost, [_split(HW, HEADS)] * 2, [_split(HW, HEADS)]),
        merge=RowOp("merge", _f_merge, [_split(3 * D_MODEL, 3)] + [_whole(D_MODEL)] * 3, [_whole(D_MODEL)]),
    )


def _mm_call(name, a, b, mode, c=None):
    if mode == "nn":
        (m, kc), n = a.shape, b.shape[1]
    elif mode == "nt":
        (m, kc), n = a.shape, b.shape[0]
    else:
        (kc, m), n = a.shape, b.shape[1]
    tm = _pick(m, 512, LANES if mode == "tn" else SUBLANES)
    tn = _pick(n, 512, LANES)
    tk = _pick(kc, 512 if mode == "tn" else 1536, LANES if mode != "tn" else SUBLANES)
    dn = {"nn": NN, "nt": NT, "tn": TN}[mode]
    a_spec = {"nn": pl.BlockSpec((tm, tk), lambda i, j, k: (i, k)),
              "nt": pl.BlockSpec((tm, tk), lambda i, j, k: (i, k)),
              "tn": pl.BlockSpec((tk, tm), lambda i, j, k: (k, i))}[mode]
    b_spec = {"nn": pl.BlockSpec((tk, tn), lambda i, j, k: (k, j)),
              "nt": pl.BlockSpec((tn, tk), lambda i, j, k: (j, k)),
              "tn": pl.BlockSpec((tk, tn), lambda i, j, k: (k, j))}[mode]
    o_spec = pl.BlockSpec((tm, tn), lambda i, j, k: (i, j))
    has_c = c is not None

    def body(*refs):
        a_ref, b_ref = refs[0], refs[1]
        o_ref = refs[-1]

        @pl.when(pl.program_id(2) == 0)
        def _():
            o_ref[...] = refs[2][...] if has_c else jnp.zeros_like(o_ref)

        o_ref[...] += _dot(a_ref[...], b_ref[...], dn)

    return _pcall(body, name=name, out_shape=jax.ShapeDtypeStruct((m, n), F32), grid=(m // tm, n // tn, kc // tk),
                  in_specs=[a_spec, b_spec] + ([o_spec] if has_c else []), out_specs=o_spec,
                  )(*((a, b, c) if has_c else (a, b)))


@jax.custom_vjp
def matmul(a, w):
    return _mm_call("mm_nn", a, w, "nn")


matmul.defvjp(lambda a, w: (_mm_call("mm_nn", a, w, "nn"), (a, w)),
              lambda r, g: (_mm_call("mm_nt", g, r[1], "nt"), _mm_call("mm_tn", r[0], g, "tn")))


@jax.custom_vjp
def matmul_add(c, a, w):
    return _mm_call("mm_nn_add", a, w, "nn", c)


matmul_add.defvjp(lambda c, a, w: (_mm_call("mm_nn_add", a, w, "nn", c), (a, w)),
                  lambda r, g: (g, _mm_call("mm_nt", g, r[1], "nt"), _mm_call("mm_tn", r[0], g, "tn")))

_PROJ_GROUPS = [(0, 512), (512, 512), (1024, 512), (1536, 1536), (3072, 512), (3584, 512), (4096, 512), (4608, 512),
                (5120, 3072), (8192, 128)]


def _proj_impl(h, w):
    return tuple(_mm_call("proj_nn", h, w[:, s:s + n], "nn") for s, n in _PROJ_GROUPS)


proj = jax.custom_vjp(_proj_impl)


def _proj_bwd(res, gs):
    h, w = res
    dh = None
    dws = []
    for (s, n), g in zip(_PROJ_GROUPS, gs):
        dh = _mm_call("proj_nt", g, w[:, s:s + n], "nt", dh)
        dws.append(_mm_call("proj_tn", h, g, "tn"))
    return dh, jnp.concatenate(dws, axis=1)


proj.defvjp(lambda h, w: (_proj_impl(h, w), (h, w)), _proj_bwd)


def _cumsum_call(x, reverse):
    s, w = x.shape
    tm = _pick(s, 256, SUBLANES)
    nb = s // tm

    def body(x_ref, o_ref, carry):
        @pl.when(pl.program_id(0) == 0)
        def _():
            carry[...] = jnp.zeros_like(carry)

        blk = x_ref[...]
        r, c = _iota2(tm, tm, 0), _iota2(tm, tm, 1)
        tri = jnp.where((r <= c) if reverse else (r >= c), 1.0, 0.0).astype(F32)
        o_ref[...] = _dotf(tri, blk, NN) + carry[...]
        carry[...] += jnp.sum(blk, axis=0, keepdims=True)

    idx = (lambda i: (nb - 1 - i, 0)) if reverse else (lambda i: (i, 0))
    return _pcall(body, name="cumsum_rev" if reverse else "cumsum", out_shape=jax.ShapeDtypeStruct((s, w), F32),
                  grid=(nb,), in_specs=[pl.BlockSpec((tm, w), idx)], out_specs=pl.BlockSpec((tm, w), idx),
                  scratch_shapes=[pltpu.VMEM((1, w), F32)])(x)


@jax.custom_vjp
def seq_cumsum(x):
    return _cumsum_call(x, False)


seq_cumsum.defvjp(lambda x: (_cumsum_call(x, False), None), lambda _, g: (_cumsum_call(g, True),))


HALO = SUBLANES


class ConvOp:
    def __init__(self, name, width, post, c_pieces, out_widths, has_bias, tm):
        self.name, self.width, self.post, self.c_pieces = name, width, post, c_pieces
        self.out_widths, self.has_bias, self.tm = out_widths, has_bias, tm
        op = jax.custom_vjp(self._fwd_call)
        op.defvjp(lambda *a: (self._fwd_call(*a), a), lambda res, g: self._bwd_call(res, g))
        self.op = op

    def __call__(self, *args):
        return self.op(*args)

    def _conv(self, i, x_ref, prev_ref, w_ref, b_ref, buf):
        tm = x_ref.shape[0]
        buf[0:HALO, :] = jnp.where(i > 0, prev_ref[...], 0.0)
        buf[HALO:HALO + tm, :] = x_ref[...]
        taps = [buf[pl.ds(HALO - (self.width - 1) + j, tm), :] for j in range(self.width)]
        c = taps[0] * w_ref[0:1, :]
        for j in range(1, self.width):
            c = c + taps[j] * w_ref[j:j + 1, :]
        if self.has_bias:
            c = c + b_ref[...]
        return c, taps

    def _fwd_call(self, x, w, *bias):
        s, ch = x.shape
        tm = _pick(s, self.tm, SUBLANES)
        r8 = tm // HALO
        has_bias, post, c_pieces = self.has_bias, self.post, self.c_pieces

        def body(*refs):
            x_ref, prev_ref, w_ref = refs[:3]
            b_ref = refs[3] if has_bias else None
            outs, buf = refs[3 + has_bias:-1], refs[-1]
            c, _ = self._conv(pl.program_id(0), x_ref, prev_ref, w_ref, b_ref, buf)
            ys = post(*[c[:, o:o + n] for o, n in c_pieces])
            for r, y in zip(outs, ys):
                r[...] = y

        outs = _pcall(
            body, name=self.name + "_fwd", out_shape=[jax.ShapeDtypeStruct((s, n), F32) for n in self.out_widths],
            grid=(s // tm,),
            in_specs=[pl.BlockSpec((tm, ch), lambda i: (i, 0)),
                      pl.BlockSpec((HALO, ch), lambda i: (jnp.maximum(i * r8 - 1, 0), 0)),
                      pl.BlockSpec(w.shape, lambda i: (0, 0))]
            + ([pl.BlockSpec((1, ch), lambda i: (0, 0))] if has_bias else []),
            out_specs=[pl.BlockSpec((tm, n), lambda i: (i, 0)) for n in self.out_widths],
            scratch_shapes=[pltpu.VMEM((tm + HALO, ch), F32)],
        )(x, x, w, *bias)
        return tuple(outs) if len(outs) > 1 else outs[0]

    def _bwd_call(self, res, g):
        x, w = res[0], res[1]
        bias = res[2:]
        gs = tuple(g) if len(self.out_widths) > 1 else (g,)
        s, ch = x.shape
        tm = _pick(s, self.tm, SUBLANES)
        r8 = tm // HALO
        nb = s // tm
        has_bias, post, c_pieces, width = self.has_bias, self.post, self.c_pieces, self.width
        ng = len(gs)

        def body1(*refs):
            x_ref, prev_ref, w_ref = refs[:3]
            b_ref = refs[3] if has_bias else None
            k = 3 + has_bias
            g_refs = refs[k:k + ng]
            dc_ref, dw_ref = refs[k + ng], refs[k + ng + 1]
            db_ref = refs[k + ng + 2] if has_bias else None
            buf = refs[-1]
            i = pl.program_id(0)
            c, taps = self._conv(i, x_ref, prev_ref, w_ref, b_ref, buf)
            _, vjp = jax.vjp(lambda *a: tuple(post(*a)), *[c[:, o:o + n] for o, n in c_pieces])
            dcs = vjp(tuple(r[...] for r in g_refs))
            for (o, n), d in zip(c_pieces, dcs):
                dc_ref[:, o:o + n] = d

            @pl.when(i == 0)
            def _():
                dw_ref[...] = jnp.zeros_like(dw_ref)
                if has_bias:
                    db_ref[...] = jnp.zeros_like(db_ref)

            dc = dc_ref[...]
            for j in range(width):
                dw_ref[j:j + 1, :] += jnp.sum(dc * taps[j], axis=0, keepdims=True)
            if has_bias:
                db_ref[...] += jnp.sum(dc, axis=0, keepdims=True)

        outs1 = _pcall(
            body1, name=self.name + "_bwd_act",
            out_shape=[jax.ShapeDtypeStruct((s, ch), F32), jax.ShapeDtypeStruct(w.shape, F32)]
            + ([jax.ShapeDtypeStruct((1, ch), F32)] if has_bias else []),
            grid=(nb,),
            in_specs=[pl.BlockSpec((tm, ch), lambda i: (i, 0)),
                      pl.BlockSpec((HALO, ch), lambda i: (jnp.maximum(i * r8 - 1, 0), 0)),
                      pl.BlockSpec(w.shape, lambda i: (0, 0))]
            + ([pl.BlockSpec((1, ch), lambda i: (0, 0))] if has_bias else [])
            + [pl.BlockSpec((tm, n), lambda i: (i, 0)) for n in self.out_widths],
            out_specs=[pl.BlockSpec((tm, ch), lambda i: (i, 0)), pl.BlockSpec(w.shape, lambda i: (0, 0))]
            + ([pl.BlockSpec((1, ch), lambda i: (0, 0))] if has_bias else []),
            scratch_shapes=[pltpu.VMEM((tm + HALO, ch), F32)],
        )(x, x, w, *bias, *gs)
        dc, dw = outs1[0], outs1[1]

        def body2(dc_ref, next_ref, w_ref, dx_ref, buf):
            i = pl.program_id(0)
            buf[0:tm, :] = dc_ref[...]
            buf[tm:tm + HALO, :] = jnp.where(i < nb - 1, next_ref[...], 0.0)
            dx = buf[pl.ds(width - 1, tm), :] * w_ref[0:1, :]
            for j in range(1, width):
                dx = dx + buf[pl.ds(width - 1 - j, tm), :] * w_ref[j:j + 1, :]
            dx_ref[...] = dx

        dx = _pcall(
            body2, name=self.name + "_bwd_in", out_shape=jax.ShapeDtypeStruct((s, ch), F32), grid=(nb,),
            in_specs=[pl.BlockSpec((tm, ch), lambda i: (i, 0)),
                      pl.BlockSpec((HALO, ch), lambda i: (jnp.minimum((i + 1) * r8, s // HALO - 1), 0)),
                      pl.BlockSpec(w.shape, lambda i: (0, 0))],
            out_specs=pl.BlockSpec((tm, ch), lambda i: (i, 0)),
            scratch_shapes=[pltpu.VMEM((tm + HALO, ch), F32)],
        )(dc, dc, w)
        return (dx, dw) + ((outs1[2],) if has_bias else ())


def _make_convops():
    return dict(
        gdn=ConvOp("gdnconv", 4, lambda q, k, v: (_silu(q), _silu(k), _silu(v)), _split(3 * HW, 3), [HW] * 3,
                   False, 256),
        ffn=ConvOp("ffnconv", 3, lambda a, b: (_silu(a) * b,), _split(2 * D_FF, 2), [D_FF], True, 128),
    )


ATT_Q = 512
ATT_K = 256
SCALE = HEAD_DIM ** -0.5


def _att_tiles(s):
    tk = _pick(s, ATT_K, LANES)
    tq = _pick(s, ATT_Q, tk)
    return tq, tk


def _att_specs(s, tq, tk):
    qspec = pl.BlockSpec((tq, HEAD_DIM), lambda h, i: (i, h))
    kspec = pl.BlockSpec((s, HEAD_DIM), lambda h, i: (0, h))
    colspec = pl.BlockSpec((None, tq, 1), lambda h, i: (h, i, 0))
    rowspec = pl.BlockSpec((None, s // tk, 1, tk), lambda h, i: (h, 0, 0, 0))
    return qspec, kspec, colspec, rowspec


def _krows(kb, tk):
    return pl.ds(pl.multiple_of(kb * tk, tk), tk)


def _stage_bf16(i, pairs):
    @pl.when(i == 0)
    def _():
        for src, dst in pairs:
            dst[...] = src[...].astype(BF16)


def _visible(i, kb, tq, tk, strict):
    rows = i * tq + _iota2(tq, tk, 0)
    cols = kb * tk + _iota2(tq, tk, 1)
    return (cols < rows) if strict else (cols <= rows)


def _fox_fwd_call(q, k, v, ccol, crow):
    s = q.shape[0]
    tq, tk = _att_tiles(s)
    ratio = tq // tk
    qspec, kspec, colspec, rowspec = _att_specs(s, tq, tk)

    def body(q_ref, k_ref, v_ref, cq_ref, ck_ref, o_ref, lse_ref, k16, v16):
        i = pl.program_id(1)
        _stage_bf16(i, [(k_ref, k16), (v_ref, v16)])
        qb = q_ref[...].astype(BF16)
        cq = cq_ref[...]

        def blk(kb, carry, masked):
            m, l, acc = carry
            sc = _dot(qb, k16[_krows(kb, tk), :], NT) * SCALE + (cq - ck_ref[kb])
            if masked:
                sc = jnp.where(_visible(i, kb, tq, tk, False), sc, NEG)
            m_new = jnp.maximum(m, jnp.max(sc, axis=-1, keepdims=True))
            alpha = jnp.exp(m - m_new)
            p = jnp.exp(sc - m_new)
            return (m_new, alpha * l + jnp.sum(p, axis=-1, keepdims=True),
                    alpha * acc + _dot(p, v16[_krows(kb, tk), :], NN))

        carry = (jnp.full((tq, 1), NEG, F32), jnp.zeros((tq, 1), F32), jnp.zeros((tq, HEAD_DIM), F32))
        carry = lax.fori_loop(0, i * ratio, lambda kb, c: blk(kb, c, False), carry)
        for j in range(ratio):
            carry = blk(i * ratio + j, carry, True)
        m, l, acc = carry
        o_ref[...] = acc / l
        lse_ref[...] = m + jnp.log(l)

    return _pcall(body, name="fox_fwd",
                  out_shape=[jax.ShapeDtypeStruct((s, HW), F32), jax.ShapeDtypeStruct((HEADS, s, 1), F32)],
                  grid=(HEADS, s // tq), in_specs=[qspec, kspec, kspec, colspec, rowspec],
                  out_specs=[qspec, colspec],
                  scratch_shapes=[pltpu.VMEM((s, HEAD_DIM), BF16)] * 2)(q, k, v, ccol, crow)


def _fox_bwd_call(q, k, v, ccol, crow, o, lse, do):
    s = q.shape[0]
    tq, tk = _att_tiles(s)
    ratio = tq // tk
    qspec, kspec, colspec, rowspec = _att_specs(s, tq, tk)

    def body(q_ref, k_ref, v_ref, cq_ref, ck_ref, o_ref, lse_ref, do_ref, dq_ref, dk_ref, dv_ref, dcq_ref, dck_ref,
             k16, v16):
        i = pl.program_id(1)
        _stage_bf16(i, [(k_ref, k16), (v_ref, v16)])

        @pl.when(i == 0)
        def _():
            dk_ref[...] = jnp.zeros_like(dk_ref)
            dv_ref[...] = jnp.zeros_like(dv_ref)
            dck_ref[...] = jnp.zeros_like(dck_ref)

        qb = q_ref[...].astype(BF16)
        dob = do_ref[...].astype(BF16)
        cq, lse = cq_ref[...], lse_ref[...]
        dl = jnp.sum(do_ref[...] * o_ref[...], axis=-1, keepdims=True)

        def blk(kb, carry, masked):
            dq, dcq = carry
            rows = _krows(kb, tk)
            kk, vv = k16[rows, :], v16[rows, :]
            sc = _dot(qb, kk, NT) * SCALE + (cq - ck_ref[kb])
            p = jnp.exp(sc - lse)
            if masked:
                p = jnp.where(_visible(i, kb, tq, tk, False), p, 0.0)
            dv_ref[rows, :] += _dot(p, dob, TN)
            ds = p * (_dot(dob, vv, NT) - dl)
            dk_ref[rows, :] += _dot(ds, qb, TN) * SCALE
            dck_ref[kb] += -jnp.sum(ds, axis=0, keepdims=True)
            return dq + _dot(ds, kk, NN) * SCALE, dcq + jnp.sum(ds, axis=-1, keepdims=True)

        carry = (jnp.zeros((tq, HEAD_DIM), F32), jnp.zeros((tq, 1), F32))
        carry = lax.fori_loop(0, i * ratio, lambda kb, c: blk(kb, c, False), carry)
        for j in range(ratio):
            carry = blk(i * ratio + j, carry, True)
        dq_ref[...] = carry[0]
        dcq_ref[...] = carry[1]

    return _pcall(body, name="fox_bwd",
                  out_shape=[jax.ShapeDtypeStruct((s, HW), F32)] * 3
                  + [jax.ShapeDtypeStruct((HEADS, s, 1), F32), jax.ShapeDtypeStruct((HEADS, s // tk, 1, tk), F32)],
                  grid=(HEADS, s // tq),
                  in_specs=[qspec, kspec, kspec, colspec, rowspec, qspec, colspec, qspec],
                  out_specs=[qspec, kspec, kspec, colspec, rowspec],
                  scratch_shapes=[pltpu.VMEM((s, HEAD_DIM), BF16)] * 2)(q, k, v, ccol, crow, o, lse, do)


@jax.custom_vjp
def fox_attention(q, k, v, ccol, crow):
    return _fox_fwd_call(q, k, v, ccol, crow)[0]


def _fox_vjp_fwd(q, k, v, ccol, crow):
    o, lse = _fox_fwd_call(q, k, v, ccol, crow)
    return o, (q, k, v, ccol, crow, o, lse)


fox_attention.defvjp(_fox_vjp_fwd, lambda res, g: tuple(_fox_bwd_call(*res, g)))


def _sb_fwd_call(q, k, v):
    s = q.shape[0]
    tq, tk = _att_tiles(s)
    ratio = tq // tk
    qspec, kspec, colspec, _ = _att_specs(s, tq, tk)

    def body(q_ref, k_ref, v_ref, o_ref, tot_ref, k16, v16):
        i = pl.program_id(1)
        _stage_bf16(i, [(k_ref, k16), (v_ref, v16)])
        qb = q_ref[...].astype(BF16)
        suffix = jnp.where(_iota2(tk, tk, 0) >= _iota2(tk, tk, 1), 1.0, 0.0).astype(BF16)

        def blk(kb, carry, masked):
            run, acc = carry
            z = _dot(qb, k16[_krows(kb, tk), :], NT) * SCALE
            lk = -_softplus(z)
            if masked:
                strict = _visible(i, kb, tq, tk, True)
                lk = jnp.where(strict, lk, 0.0)
            a = jnp.exp(z + _dot2(lk, suffix, NN) + run)
            if masked:
                a = jnp.where(strict, a, 0.0)
            return run + jnp.sum(lk, axis=-1, keepdims=True), acc + _dot(a, v16[_krows(kb, tk), :], NN)

        carry = (jnp.zeros((tq, 1), F32), jnp.zeros((tq, HEAD_DIM), F32))
        for j in reversed(range(ratio)):
            carry = blk(i * ratio + j, carry, True)
        run, acc = lax.fori_loop(0, i * ratio, lambda n, c: blk(i * ratio - 1 - n, c, False), carry)
        o_ref[...] = acc
        tot_ref[...] = run

    return _pcall(body, name="sb_fwd",
                  out_shape=[jax.ShapeDtypeStruct((s, HW), F32), jax.ShapeDtypeStruct((HEADS, s, 1), F32)],
                  grid=(HEADS, s // tq), in_specs=[qspec, kspec, kspec], out_specs=[qspec, colspec],
                  scratch_shapes=[pltpu.VMEM((s, HEAD_DIM), BF16)] * 2)(q, k, v)


def _sb_bwd_call(q, k, v, tot, do):
    s = q.shape[0]
    tq, tk = _att_tiles(s)
    ratio = tq // tk
    qspec, kspec, colspec, _ = _att_specs(s, tq, tk)

    def body(q_ref, k_ref, v_ref, tot_ref, do_ref, dq_ref, dk_ref, dv_ref, k16, v16):
        i = pl.program_id(1)
        _stage_bf16(i, [(k_ref, k16), (v_ref, v16)])

        @pl.when(i == 0)
        def _():
            dk_ref[...] = jnp.zeros_like(dk_ref)
            dv_ref[...] = jnp.zeros_like(dv_ref)

        qb = q_ref[...].astype(BF16)
        dob = do_ref[...].astype(BF16)
        tot = tot_ref[...]
        prefix = jnp.where(_iota2(tk, tk, 0) <= _iota2(tk, tk, 1), 1.0, 0.0).astype(BF16)

        def blk(kb, carry, masked):
            left, esum, dq = carry
            rows = _krows(kb, tk)
            kk, vv = k16[rows, :], v16[rows, :]
            z = _dot(qb, kk, NT) * SCALE
            lk = -_softplus(z)
            if masked:
                strict = _visible(i, kb, tq, tk, True)
                lk = jnp.where(strict, lk, 0.0)
            rc = (tot - left) - (_dot2(lk, prefix, NN) - lk)
            a = jnp.exp(z + rc)
            if masked:
                a = jnp.where(strict, a, 0.0)
            e = a * _dot(dob, vv, NT)
            dv_ref[rows, :] += _dot(a, dob, TN)
            dz = e - _sigmoid(z) * (esum + _dot2(e, prefix, NN))
            if masked:
                dz = jnp.where(strict, dz, 0.0)
            dk_ref[rows, :] += _dot(dz, qb, TN) * SCALE
            return (left + jnp.sum(lk, axis=-1, keepdims=True), esum + jnp.sum(e, axis=-1, keepdims=True),
                    dq + _dot(dz, kk, NN) * SCALE)

        carry = (jnp.zeros((tq, 1), F32), jnp.zeros((tq, 1), F32), jnp.zeros((tq, HEAD_DIM), F32))
        carry = lax.fori_loop(0, i * ratio, lambda kb, c: blk(kb, c, False), carry)
        for j in range(ratio):
            carry = blk(i * ratio + j, carry, True)
        dq_ref[...] = carry[2]

    return _pcall(body, name="sb_bwd", out_shape=[jax.ShapeDtypeStruct((s, HW), F32)] * 3, grid=(HEADS, s // tq),
                  in_specs=[qspec, kspec, kspec, colspec, qspec], out_specs=[qspec, kspec, kspec],
                  scratch_shapes=[pltpu.VMEM((s, HEAD_DIM), BF16)] * 2)(q, k, v, tot, do)


@jax.custom_vjp
def sb_attention(q, k, v):
    return _sb_fwd_call(q, k, v)[0]


def _sb_vjp_fwd(q, k, v):
    o, tot = _sb_fwd_call(q, k, v)
    return o, (q, k, v, tot)


sb_attention.defvjp(_sb_vjp_fwd, lambda res, g: tuple(_sb_bwd_call(*res, g)))


def _mem_specs(s, nk, t):
    return (pl.BlockSpec((t, HEAD_DIM), lambda h, i: (i, h)), pl.BlockSpec((nk, HEAD_DIM), lambda h, i: (0, h)))


def _mem_probs(qb, kk):
    sc = _dot(qb, kk, NT) * SCALE
    p = jnp.exp(sc - jnp.max(sc, axis=-1, keepdims=True))
    return p / jnp.sum(p, axis=-1, keepdims=True)


def _mem_fwd_call(q, k, v):
    s, nk = q.shape[0], k.shape[0]
    t = _pick(s, 512, SUBLANES)
    qspec, kspec = _mem_specs(s, nk, t)

    def body(q_ref, k_ref, v_ref, o_ref):
        o_ref[...] = _dot(_mem_probs(q_ref[...].astype(BF16), k_ref[...]), v_ref[...], NN)

    return _pcall(body, name="mem_fwd", out_shape=jax.ShapeDtypeStruct((s, HW), F32), grid=(HEADS, s // t),
                  in_specs=[qspec, kspec, kspec], out_specs=qspec)(q, k, v)


def _mem_bwd_call(q, k, v, do):
    s, nk = q.shape[0], k.shape[0]
    t = _pick(s, 512, SUBLANES)
    qspec, kspec = _mem_specs(s, nk, t)

    def body(q_ref, k_ref, v_ref, do_ref, dq_ref, dk_ref, dv_ref):
        @pl.when(pl.program_id(1) == 0)
        def _():
            dk_ref[...] = jnp.zeros_like(dk_ref)
            dv_ref[...] = jnp.zeros_like(dv_ref)

        qb = q_ref[...].astype(BF16)
        dob = do_ref[...].astype(BF16)
        p = _mem_probs(qb, k_ref[...])
        dv_ref[...] += _dot(p, dob, TN)
        dp = _dot(dob, v_ref[...], NT)
        ds = p * (dp - jnp.sum(p * dp, axis=-1, keepdims=True))
        dq_ref[...] = _dot(ds, k_ref[...], NN) * SCALE
        dk_ref[...] += _dot(ds, qb, TN) * SCALE

    return _pcall(body, name="mem_bwd",
                  out_shape=[jax.ShapeDtypeStruct((s, HW), F32)] + [jax.ShapeDtypeStruct((nk, HW), F32)] * 2,
                  grid=(HEADS, s // t), in_specs=[qspec, kspec, kspec, qspec],
                  out_specs=[qspec, kspec, kspec])(q, k, v, do)


@jax.custom_vjp
def mem_attention(q, k, v):
    return _mem_fwd_call(q, k, v)


mem_attention.defvjp(lambda q, k, v: (_mem_fwd_call(q, k, v), (q, k, v)),
                     lambda res, g: tuple(_mem_bwd_call(*res, g)))


def _unit_lower_inverse(nm):
    eye = jnp.where(_iota2(CHUNK, CHUNK, 0) == _iota2(CHUNK, CHUNK, 1), 1.0, 0.0).astype(F32)
    p = eye - nm
    m = nm
    for _ in range(5):
        m = _dotf(m, m, NN)
        p = _dotf(p, eye + m, NN)
    return p


@jax.custom_vjp
def _solve2(nm, r1, r2):
    inv = _unit_lower_inverse(nm)
    return _dotf(inv, r1, NN), _dotf(inv, r2, NN)


def _solve2_fwd(nm, r1, r2):
    inv = _unit_lower_inverse(nm)
    u, w = _dotf(inv, r1, NN), _dotf(inv, r2, NN)
    return (u, w), (inv, u, w)


def _solve2_bwd(res, g):
    inv, u, w = res
    d1, d2 = _dotf(inv, g[0], TN), _dotf(inv, g[1], TN)
    return -(_dotf(d1, u, NT) + _dotf(d2, w, NT)), d1, d2


_solve2.defvjp(_solve2_fwd, _solve2_bwd)


def _gdn_chunk(q, k, v, gcc, gcr, b, gl, st):
    qn = q * lax.rsqrt(jnp.sum(q * q, axis=-1, keepdims=True) + EPS) * SCALE
    kn = k * lax.rsqrt(jnp.sum(k * k, axis=-1, keepdims=True) + EPS)
    r, c = _iota2(CHUNK, CHUNK, 0), _iota2(CHUNK, CHUNK, 1)
    decay = jnp.exp(jnp.where(r >= c, gcc - gcr, NEG))
    nm = jnp.where(r > c, b * mm_nt(kn, kn) * decay, 0.0)
    eg = jnp.exp(gcc)
    u, w = _solve2(nm, v * b, kn * (b * eg))
    attn = mm_nt(qn, kn) * decay
    v_new = u - mm(w, st)
    o = mm(qn * eg, st) + mm(attn, v_new)
    st_new = st * jnp.exp(gl) + mm_tn(kn * jnp.exp(gl - gcc), v_new)
    return o, st_new


GDN_ROWS = 512


def _gdn_specs(s, tg, rev):
    nb = s // tg
    cpb = tg // CHUNK
    j_of = (lambda j: nb - 1 - j) if rev else (lambda j: j)
    qspec = pl.BlockSpec((tg, HW), lambda j: (j_of(j), 0))
    colspec = pl.BlockSpec((HEADS, tg, 1), lambda j: (0, j_of(j), 0))
    rowspec = pl.BlockSpec((HEADS, cpb, 1, CHUNK), lambda j: (0, j_of(j), 0, 0))
    onespec = pl.BlockSpec((HEADS, cpb, 1, 1), lambda j: (0, j_of(j), 0, 0))
    stspec = pl.BlockSpec((HEADS, cpb, HEAD_DIM, HEAD_DIM), lambda j: (0, j_of(j), 0, 0))
    return qspec, colspec, rowspec, onespec, stspec


def _head_cols(h):
    return slice(h * HEAD_DIM, (h + 1) * HEAD_DIM)


def _gdn_fwd_call(q, k, v, gcc, gcr, bc, gl):
    s = q.shape[0]
    tg = _pick(s, GDN_ROWS, CHUNK)
    cpb = tg // CHUNK
    qspec, colspec, rowspec, onespec, stspec = _gdn_specs(s, tg, False)

    def body(q_ref, k_ref, v_ref, gcc_ref, gcr_ref, b_ref, gl_ref, o_ref, st_ref, st):
        @pl.when(pl.program_id(0) == 0)
        def _():
            st[...] = jnp.zeros_like(st)

        def chunk(ci, _):
            rows = pl.ds(pl.multiple_of(ci * CHUNK, CHUNK), CHUNK)
            for h in range(HEADS):
                cols = _head_cols(h)
                s_in = st[h]
                st_ref[h, ci] = s_in
                o, s_new = _gdn_chunk(q_ref[rows, cols], k_ref[rows, cols], v_ref[rows, cols], gcc_ref[h, rows, :],
                                      gcr_ref[h, ci], b_ref[h, rows, :], gl_ref[h, ci], s_in)
                o_ref[rows, cols] = o
                st[h] = s_new
            return 0

        lax.fori_loop(0, cpb, chunk, 0)

    return _pcall(body, name="gdn_fwd",
                  out_shape=[jax.ShapeDtypeStruct((s, HW), F32),
                             jax.ShapeDtypeStruct((HEADS, s // CHUNK, HEAD_DIM, HEAD_DIM), F32)],
                  grid=(s // tg,), in_specs=[qspec, qspec, qspec, colspec, rowspec, colspec, onespec],
                  out_specs=[qspec, stspec], scratch_shapes=[pltpu.VMEM((HEADS, HEAD_DIM, HEAD_DIM), F32)],
                  )(q, k, v, gcc, gcr, bc, gl)


def _gdn_bwd_call(q, k, v, gcc, gcr, bc, gl, states, do):
    s = q.shape[0]
    tg = _pick(s, GDN_ROWS, CHUNK)
    cpb = tg // CHUNK
    qspec, colspec, rowspec, onespec, stspec = _gdn_specs(s, tg, True)

    def body(q_ref, k_ref, v_ref, gcc_ref, gcr_ref, b_ref, gl_ref, st_ref, do_ref,
             dq_ref, dk_ref, dv_ref, dgcc_ref, dgcr_ref, db_ref, dgl_ref, dst):
        @pl.when(pl.program_id(0) == 0)
        def _():
            dst[...] = jnp.zeros_like(dst)

        def chunk(n, _):
            ci = cpb - 1 - n
            rows = pl.ds(pl.multiple_of(ci * CHUNK, CHUNK), CHUNK)
            for h in range(HEADS):
                cols = _head_cols(h)
                _, vjp = jax.vjp(_gdn_chunk, q_ref[rows, cols], k_ref[rows, cols], v_ref[rows, cols],
                                 gcc_ref[h, rows, :], gcr_ref[h, ci], b_ref[h, rows, :], gl_ref[h, ci], st_ref[h, ci])
                dq, dk, dv, dgcc, dgcr, db, dgl, ds_in = vjp((do_ref[rows, cols], dst[h]))
                dq_ref[rows, cols] = dq
                dk_ref[rows, cols] = dk
                dv_ref[rows, cols] = dv
                dgcc_ref[h, rows, :] = dgcc
                dgcr_ref[h, ci] = dgcr
                db_ref[h, rows, :] = db
                dgl_ref[h, ci] = dgl
                dst[h] = ds_in
            return 0

        lax.fori_loop(0, cpb, chunk, 0)

    n = s // CHUNK
    return _pcall(body, name="gdn_bwd",
                  out_shape=[jax.ShapeDtypeStruct((s, HW), F32)] * 3
                  + [jax.ShapeDtypeStruct((HEADS, s, 1), F32), jax.ShapeDtypeStruct((HEADS, n, 1, CHUNK), F32),
                     jax.ShapeDtypeStruct((HEADS, s, 1), F32), jax.ShapeDtypeStruct((HEADS, n, 1, 1), F32)],
                  grid=(s // tg,),
                  in_specs=[qspec, qspec, qspec, colspec, rowspec, colspec, onespec, stspec, qspec],
                  out_specs=[qspec, qspec, qspec, colspec, rowspec, colspec, onespec],
                  scratch_shapes=[pltpu.VMEM((HEADS, HEAD_DIM, HEAD_DIM), F32)],
                  )(q, k, v, gcc, gcr, bc, gl, states, do)


@jax.custom_vjp
def gated_delta(q, k, v, gcc, gcr, bc, gl):
    return _gdn_fwd_call(q, k, v, gcc, gcr, bc, gl)[0]


def _gdn_vjp_fwd(q, k, v, gcc, gcr, bc, gl):
    o, states = _gdn_fwd_call(q, k, v, gcc, gcr, bc, gl)
    return o, (q, k, v, gcc, gcr, bc, gl, states)


gated_delta.defvjp(_gdn_vjp_fwd, lambda res, g: tuple(_gdn_bwd_call(*res, g)))


def _loss_call(y, target):
    s, d = y.shape
    tm = _pick(s, 512, SUBLANES)

    def body(y_ref, t_ref, dy_ref, loss_ref):
        @pl.when(pl.program_id(0) == 0)
        def _():
            loss_ref[...] = jnp.zeros_like(loss_ref)

        err = y_ref[...] - t_ref[...]
        dy_ref[...] = err * (1.0 / d)
        loss_ref[...] += 0.5 * jnp.sum(jnp.mean(err * err, axis=-1, keepdims=True), axis=0, keepdims=True)

    dy, part = _pcall(body, name="loss_head",
                      out_shape=[jax.ShapeDtypeStruct((s, d), F32), jax.ShapeDtypeStruct((1, 1), F32)],
                      grid=(s // tm,), in_specs=[pl.BlockSpec((tm, d), lambda i: (i, 0))] * 2,
                      out_specs=[pl.BlockSpec((tm, d), lambda i: (i, 0)), pl.BlockSpec((1, 1), lambda i: (0, 0))],
                      )(y, target)
    return part[0, 0], dy


def _cols_and_rows(a, lane0, t):
    s = a.shape[0]
    at = a[:, lane0:lane0 + HEADS].T
    return at, at[:, :, None], at.reshape(HEADS, s // t, 1, t)


def _pad_lanes(v, lane0):
    return jnp.pad(v, (lane0, LANES - lane0 - v.shape[0])).reshape(1, LANES)


def _layer(x, mem, w, ops, convs):
    s = x.shape[0]
    row = lambda v: v.reshape(1, -1)
    h = ops["rms"](x, row(w["norm_mix"]))
    fq, fk, fv, gqkv, gz, sq, sk, sv, gt, gm = proj(h, w["w_in"])

    logf, beta, gc = ops["small"](gm, _pad_lanes(w["fox_fbias"], LANE_FF), _pad_lanes(w["gdn_a_log"], LANE_GA),
                                  _pad_lanes(w["gdn_dt_bias"], LANE_GA))
    _, ccol, crow = _cols_and_rows(seq_cumsum(logf), LANE_FF, _att_tiles(s)[1])
    ya = fox_attention(ops["headnorm"](fq, row(w["fox_qnorm"])), ops["headnorm"](fk, row(w["fox_knorm"])), fv,
                       ccol, crow)
    cq, ck, cv = convs["gdn"](gqkv, w["gdn_conv"])
    gct, gcc, gcr = _cols_and_rows(gc, LANE_GA, CHUNK)
    gl = gct.reshape(HEADS, s // CHUNK, CHUNK)[:, :, CHUNK - 1].reshape(HEADS, s // CHUNK, 1, 1)
    bc = beta[:, LANE_GB:LANE_GB + HEADS].T[:, :, None]
    yb = ops["gdnpost"](gated_delta(cq, ck, cv, gcc, gcr, bc, gl), gz, row(w["gdn_onorm"]))
    yc = sb_attention(sq, sk, sv)
    gb = w["gate_bias"]
    mixed = ops["merge"](gt, matmul(ya, w["w_oa"]), matmul(yb, w["w_ob"]), matmul(yc, w["w_oc"]),
                         row(gb[:D_MODEL]), row(gb[D_MODEL:2 * D_MODEL]), row(gb[2 * D_MODEL:]))
    x = matmul_add(x, mixed, w["w_out"])
    mq = ops["headnorm"](matmul(ops["rms"](x, row(w["norm_xq"])), w["w_mq"]), row(w["mq_norm"]))
    kv = matmul(ops["rms"](mem, row(w["norm_mem"])), w["w_mkv"])
    mk = ops["headnorm"](kv[:, :HW], row(w["mk_norm"]))
    x = matmul_add(x, mem_attention(mq, mk, kv[:, HW:]), w["w_mo"])
    u = matmul(ops["rms"](x, row(w["norm_ffn"])), w["w_up"])
    act = convs["ffn"](u, w["ffn_conv"], row(w["ffn_conv_b"]))
    return matmul_add(x, act, w["w_down"])


def _forward(x, mem, layers):
    ops, convs = _make_rowops(), _make_convops()
    for w in layers:
        x = _layer(x, mem, w, ops, convs)
    return x


ANY = pl.BlockSpec(memory_space=pl.ANY)


N_PEERS = N_DEV - 1


def _ccall(body, *, name, out_shape, n_arrays):
    return pl.pallas_call(body, name=name, out_shape=out_shape, in_specs=[ANY] * n_arrays,
                          out_specs=[ANY] * n_arrays,
                          scratch_shapes=[pltpu.SemaphoreType.DMA((N_PEERS * n_arrays,)),
                                          pltpu.SemaphoreType.DMA((N_PEERS * n_arrays,)),
                                          pltpu.SemaphoreType.DMA((n_arrays,))],
                          interpret=False)


def _all_gather(name, shards):
    n = len(shards)

    def body(*refs):
        x_refs, out_refs = refs[:n], refs[n:2 * n]
        send_sems, recv_sems, local_sems = refs[2 * n:]
        x, y, c = lax.axis_index("x"), lax.axis_index("y"), lax.axis_index("c")
        me, sibling = (x, y, c), (x, y, 1 - c)
        chips = [(1 - x, y), (x, 1 - y), (1 - x, 1 - y)]

        def slot(a, px, py, pc):
            return out_refs[a].at[4 * px + 2 * py + pc]

        def copy(a, k, block, to, src=None):
            return pltpu.make_async_remote_copy(
                src_ref=slot(a, *block) if src is None else src, dst_ref=slot(a, *block),
                send_sem=send_sems.at[N_PEERS * a + k], recv_sem=recv_sems.at[N_PEERS * a + k], device_id=to,
                device_id_type=MESH)

        mine = [pltpu.make_async_copy(x_refs[a], slot(a, *me), local_sems.at[a]) for a in range(n)]
        first = []
        for a in range(n):
            first.append(copy(a, 0, me, sibling, src=x_refs[a]))
            first += [copy(a, 1 + j, me, (*chip, c), src=x_refs[a]) for j, chip in enumerate(chips)]
        for cp in mine + first:
            cp.start()
        passed = []
        for j, chip in enumerate(chips):
            for a in range(n):
                copy(a, 1 + j, (*chip, c), me).wait_recv()
                passed.append(copy(a, 4 + j, (*chip, c), sibling))
                passed[-1].start()
        for a in range(n):
            copy(a, 0, sibling, me).wait_recv()
            for j, chip in enumerate(chips):
                copy(a, 4 + j, (*chip, 1 - c), me).wait_recv()
        for cp in first + passed:
            cp.wait_send()
        for cp in mine:
            cp.wait()

    return _ccall(body, name=name, out_shape=[jax.ShapeDtypeStruct((N_DEV,) + s.shape, s.dtype) for s in shards],
                  n_arrays=n)(*shards)


def _exchange(name, parts):
    n = len(parts)

    def body(*refs):
        p_refs, out_refs = refs[:n], refs[n:2 * n]
        send_sems, recv_sems, local_sems = refs[2 * n:]
        x, y, c = lax.axis_index("x"), lax.axis_index("y"), lax.axis_index("c")
        me = 4 * x + 2 * y + c

        def peer(k):
            return (x ^ ((k >> 2) & 1), y ^ ((k >> 1) & 1), c ^ (k & 1))

        def copy(a, k, receive):
            px, py, pc = peer(k)
            theirs = 4 * px + 2 * py + pc
            return pltpu.make_async_remote_copy(
                src_ref=out_refs[a].at[theirs] if receive else p_refs[a].at[theirs],
                dst_ref=out_refs[a].at[theirs if receive else me],
                send_sem=send_sems.at[N_PEERS * a + k - 1], recv_sem=recv_sems.at[N_PEERS * a + k - 1],
                device_id=(px, py, pc), device_id_type=MESH)

        mine = [pltpu.make_async_copy(p_refs[a].at[me], out_refs[a].at[me], local_sems.at[a]) for a in range(n)]
        sends = [copy(a, k, False) for k in range(1, N_DEV) for a in range(n)]
        for cp in mine + sends:
            cp.start()
        for k in range(1, N_DEV):
            for a in range(n):
                copy(a, k, True).wait_recv()
        for cp in sends:
            cp.wait_send()
        for cp in mine:
            cp.wait()

    return _ccall(body, name=name, out_shape=[jax.ShapeDtypeStruct(p.shape, p.dtype) for p in parts],
                  n_arrays=n)(*parts)


ADAM_SLOT_BYTES = 4 * 1024 * 1024


def _adam_call(name, w, slots, m, v):
    r, n = w.shape
    tr = _pick(r, max(SUBLANES, ADAM_SLOT_BYTES // (N_DEV * n * 4)), SUBLANES)
    spec = pl.BlockSpec((tr, n), lambda i: (i, 0))

    def body(w_ref, s_ref, m_ref, v_ref, g_ref, d_ref, nm_ref, nv_ref):
        g = s_ref[0]
        for d in range(1, N_DEV):
            g = g + s_ref[d]
        nm = ADAM_B1 * m_ref[...] + (1.0 - ADAM_B1) * g
        nv = ADAM_B2 * v_ref[...] + (1.0 - ADAM_B2) * (g * g)
        m_hat = nm / (1.0 - ADAM_B1 ** ADAM_STEP)
        v_hat = nv / (1.0 - ADAM_B2 ** ADAM_STEP)
        g_ref[...] = g
        d_ref[...] = -ADAM_LR * (m_hat / (jnp.sqrt(v_hat) + ADAM_EPS) + ADAM_WD * w_ref[...])
        nm_ref[...] = nm
        nv_ref[...] = nv

    return _pcall(body, name=name, out_shape=[jax.ShapeDtypeStruct((r, n), F32)] * 4, grid=(r // tr,),
                  in_specs=[spec, pl.BlockSpec((N_DEV, tr, n), lambda i: (0, i, 0)), spec, spec],
                  out_specs=[spec] * 4)(w, slots, m, v)


def _pack_rows(flat, rows):
    return jnp.pad(flat, (0, rows * PACK_COLS - flat.shape[0])).reshape(rows, PACK_COLS)


def _regroup_in(w_in):
    cols = [w_in[:, a:b] for a, b in _IN_SRC]
    return jnp.concatenate(cols + [jnp.zeros((w_in.shape[0], N_IN_PAD - N_IN), F32)], axis=1)


def _ungroup_in(d):
    starts = {}
    off = 0
    for a, b in _IN_SRC:
        starts[a] = (off, b - a)
        off += b - a
    return jnp.concatenate([d[:, starts[a][0]:starts[a][0] + starts[a][1]] for a in sorted(starts)], axis=1)


def _full_weights(gathered):
    out = {}
    for n, g in zip(SHARDED_ORDER, gathered):
        (r, c), axis = SHARDED[n]
        out[n] = g.reshape(r, c) if axis == 0 else g.transpose(1, 0, 2).reshape(r, c)
    out["w_in"] = _regroup_in(out["w_in"])
    return out


def _grad_parts(grads):
    parts = []
    for n in SHARDED_ORDER:
        (r, c), axis = SHARDED[n]
        g = _ungroup_in(grads[n]) if n == "w_in" else grads[n]
        if axis == 0:
            parts.append(g.reshape(N_DEV, r // N_DEV, c))
        else:
            parts.append(g.reshape(r, N_DEV, c // N_DEV).transpose(1, 0, 2))
    return parts


def _pack_small(vals):
    return _pack_rows(jnp.concatenate([vals[n].reshape(-1) for n in SMALL_ORDER]), SMALL_ROWS)


def _unpack_small(packed):
    flat = packed.reshape(-1)
    out, off = {}, 0
    for n in SMALL_ORDER:
        size = DEPTH * SMALL_WIDTH[n]
        out[n] = flat[off:off + size].reshape(DEPTH, SMALL_WIDTH[n])
        off += size
    return out


def kernel(x, mem, norm_mix, w_in, fox_fbias, fox_qnorm, fox_knorm, gdn_conv, gdn_a_log, gdn_dt_bias, gdn_onorm, gate_bias, w_oa, w_ob, w_oc, w_out, norm_xq, norm_mem, w_mq, w_mkv, mq_norm, mk_norm, w_mo, norm_ffn, w_up, ffn_conv, ffn_conv_b, w_down, loss_target, m_norm_mix, m_w_in, m_fox_fbias, m_fox_qnorm, m_fox_knorm, m_gdn_conv, m_gdn_a_log, m_gdn_dt_bias, m_gdn_onorm, m_gate_bias, m_w_oa, m_w_ob, m_w_oc, m_w_out, m_norm_xq, m_norm_mem, m_w_mq, m_w_mkv, m_mq_norm, m_mk_norm, m_w_mo, m_norm_ffn, m_w_up, m_ffn_conv, m_ffn_conv_b, m_w_down, v_norm_mix, v_w_in, v_fox_fbias, v_fox_qnorm, v_fox_knorm, v_gdn_conv, v_gdn_a_log, v_gdn_dt_bias, v_gdn_onorm, v_gate_bias, v_w_oa, v_w_ob, v_w_oc, v_w_out, v_norm_xq, v_norm_mem, v_w_mq, v_w_mkv, v_mq_norm, v_mk_norm, v_w_mo, v_norm_ffn, v_w_up, v_ffn_conv, v_ffn_conv_b, v_w_down):
    given = dict(locals())
    wts = {n: given[n] for n in WEIGHTS}
    mom = {n: given["m_" + n] for n in WEIGHTS}
    var = {n: given["v_" + n] for n in WEIGHTS}

    layers = []
    for l in range(DEPTH):
        full = _full_weights(_all_gather("gather_weights", [wts[n][l] for n in SHARDED_ORDER]))
        full.update({n: wts[n][l] for n in SMALL_ORDER})
        layers.append(full)

    y, vjp = jax.vjp(lambda xx, ww: _forward(xx, mem[0], ww), x[0], layers)
    loss_part, dy = _loss_call(y, loss_target[0])
    dx, dlayers = vjp(dy)
    loss = lax.psum(loss_part, ("x", "y", "c"))

    out = {}
    per_layer = []
    for l in range(DEPTH):
        slots = _exchange("exchange_grads", _grad_parts(dlayers[l]))
        per_layer.append({n: _adam_call("adam_shard", wts[n][l], sl, mom[n][l], var[n][l])
                          for n, sl in zip(SHARDED_ORDER, slots)})
    for n in SHARDED_ORDER:
        for k, kind in enumerate(("grad_", "delta_", "new_m_", "new_v_")):
            out[kind + n] = jnp.stack([per_layer[l][n][k] for l in range(DEPTH)])
    dsmall = {n: jnp.stack([dlayers[l][n] for l in range(DEPTH)]) for n in SMALL_ORDER}
    slots = _all_gather("gather_small_grads", [_pack_small(dsmall)])[0]
    res = _adam_call("adam_small", _pack_small(wts), slots, _pack_small(mom), _pack_small(var))
    for k, kind in enumerate(("grad_", "delta_", "new_m_", "new_v_")):
        un = _unpack_small(res[k])
        for n in SMALL_ORDER:
            out[kind + n] = un[n].reshape(wts[n].shape)

    return (loss, dx[None], *[out["grad_" + n] for n in WEIGHTS], *[out["delta_" + n] for n in WEIGHTS],
            *[out["new_m_" + n] for n in WEIGHTS], *[out["new_v_" + n] for n in WEIGHTS])
```

```python
import jax
import jax.numpy as jnp
from jax import lax
from jax.experimental import pallas as pl
from jax.experimental.pallas import tpu as pltpu

F32 = jnp.float32
BF16 = jnp.bfloat16

N_DEV = 8
D_MODEL = 1024
DEPTH = 4
CHUNK = 64
EPS = 1e-6
HEADS = 4
HEAD_DIM = 128
HW = HEADS * HEAD_DIM
D_FF = 2816
N_IN = 8204
LANES = 128
SUBLANES = 8
VMEM_LIMIT = 56 * 1024 * 1024

ADAM_LR = 0.001
ADAM_B1 = 0.9
ADAM_B2 = 0.999
ADAM_EPS = 1e-08
ADAM_WD = 0.01
ADAM_STEP = 10

NEG = -1e30
MESH = pl.DeviceIdType.MESH

WEIGHTS = ['norm_mix', 'w_in', 'fox_fbias', 'fox_qnorm', 'fox_knorm', 'gdn_conv', 'gdn_a_log', 'gdn_dt_bias',
           'gdn_onorm', 'gate_bias', 'w_oa', 'w_ob', 'w_oc', 'w_out', 'norm_xq', 'norm_mem', 'w_mq', 'w_mkv',
           'mq_norm', 'mk_norm', 'w_mo', 'norm_ffn', 'w_up', 'ffn_conv', 'ffn_conv_b', 'w_down']
SHARDED = {
    'w_in': ((D_MODEL, N_IN), 0), 'gdn_conv': ((4, 3 * HW), 1), 'w_oa': ((HW, D_MODEL), 1),
    'w_ob': ((HW, D_MODEL), 1), 'w_oc': ((HW, D_MODEL), 1), 'w_out': ((D_MODEL, D_MODEL), 0),
    'w_mq': ((D_MODEL, HW), 0), 'w_mkv': ((D_MODEL, 2 * HW), 0), 'w_mo': ((HW, D_MODEL), 1),
    'w_up': ((D_MODEL, 2 * D_FF), 1), 'ffn_conv': ((3, 2 * D_FF), 1), 'w_down': ((D_FF, D_MODEL), 0),
}
SHARDED_ORDER = [n for n in WEIGHTS if n in SHARDED]
SMALL_ORDER = [n for n in WEIGHTS if n not in SHARDED]
SMALL_WIDTH = {'norm_mix': D_MODEL, 'fox_fbias': HEADS, 'fox_qnorm': HEAD_DIM, 'fox_knorm': HEAD_DIM,
               'gdn_a_log': HEADS, 'gdn_dt_bias': HEADS, 'gdn_onorm': HEAD_DIM, 'gate_bias': 3 * D_MODEL,
               'norm_xq': D_MODEL, 'norm_mem': D_MODEL, 'mq_norm': HEAD_DIM, 'mk_norm': HEAD_DIM,
               'norm_ffn': D_MODEL, 'ffn_conv_b': 2 * D_FF}
PACK_COLS = 1024


def _round_up(n, m):
    return (n + m - 1) // m * m


SMALL_ROWS = _round_up(DEPTH * sum(SMALL_WIDTH.values()), SUBLANES * PACK_COLS) // PACK_COLS

_IN_SRC = [(0, 512), (512, 1024), (1024, 1536),
           (1540, 2052), (2052, 2564), (2564, 3076),
           (3084, 3596),
           (3596, 4108), (4108, 4620), (4620, 5132),
           (5132, 8204),
           (1536, 1540), (3076, 3080), (3080, 3084)]
N_IN_PAD = 8320
LANE_FF, LANE_GB, LANE_GA = 0, 4, 8


def _pick(dim, pref, unit):
    best = None
    t = unit
    while t <= min(dim, pref):
        if dim % t == 0:
            best = t
        t += unit
    return dim if best is None else best


def _params(n_grid):
    return pltpu.CompilerParams(dimension_semantics=("arbitrary",) * n_grid, vmem_limit_bytes=VMEM_LIMIT)


def _pcall(body, *, name, out_shape, grid, in_specs, out_specs, scratch_shapes=()):
    return pl.pallas_call(body, name=name, out_shape=out_shape, grid=grid, in_specs=in_specs, out_specs=out_specs,
                          scratch_shapes=scratch_shapes, compiler_params=_params(len(grid)),
                          interpret=False)


NN = ((1,), (0,))
NT = ((1,), (1,))
TN = ((0,), (0,))


def _dot(a, b, dn):
    return lax.dot_general(a.astype(BF16), b.astype(BF16), (dn, ((), ())), preferred_element_type=F32)


def _dotf(a, b, dn):
    return lax.dot_general(a, b, (dn, ((), ())), precision=lax.Precision.HIGHEST, preferred_element_type=F32)


@jax.custom_vjp
def mm(a, b):
    return _dot(a, b, NN)


mm.defvjp(lambda a, b: (_dot(a, b, NN), (a, b)), lambda r, g: (_dot(g, r[1], NT), _dot(r[0], g, TN)))


@jax.custom_vjp
def mm_nt(a, b):
    return _dot(a, b, NT)


mm_nt.defvjp(lambda a, b: (_dot(a, b, NT), (a, b)), lambda r, g: (_dot(g, r[1], NN), _dot(g, r[0], TN)))


@jax.custom_vjp
def mm_tn(a, b):
    return _dot(a, b, TN)


mm_tn.defvjp(lambda a, b: (_dot(a, b, TN), (a, b)), lambda r, g: (_dot(r[1], g, NT), _dot(r[0], g, NN)))


@jax.custom_vjp
def tri_apply(t, x):
    return _dotf(t, x, NN)


tri_apply.defvjp(lambda t, x: (_dotf(t, x, NN), t), lambda t, g: (jnp.zeros_like(t), _dotf(t, g, TN)))


def _sigmoid(x):
    return 1.0 / (1.0 + jnp.exp(-x))


@jax.custom_vjp
def _softplus(x):
    return jnp.maximum(x, 0.0) + jnp.log(1.0 + jnp.exp(-jnp.abs(x)))


_softplus.defvjp(lambda x: (_softplus(x), x), lambda x, g: (g * _sigmoid(x),))


def _silu(x):
    return x * _sigmoid(x)


def _rms(x, g):
    return x * lax.rsqrt(jnp.mean(x * x, axis=-1, keepdims=True) + EPS) * g


def _iota2(n, m, axis):
    return lax.broadcasted_iota(jnp.int32, (n, m), axis)


def _whole(width):
    return [(0, width)]


def _split(width, n):
    w = width // n
    return [(k * w, w) for k in range(n)]


class RowOp:
    def __init__(self, name, f, in_pieces, out_pieces, tm=256):
        self.name, self.f, self.in_pieces, self.out_pieces, self.tm = name, f, in_pieces, out_pieces, tm
        op = jax.custom_vjp(self._fwd_call)
        op.defvjp(lambda *a: (self._fwd_call(*a), a), lambda res, g: self._bwd_call(res, g))
        self.op = op

    def __call__(self, *args):
        return self.op(*args)

    def _width(self, pieces):
        return max(o + w for o, w in pieces)

    def _row_specs(self, pieces_list, tm):
        return [pl.BlockSpec((tm, self._width(p)), lambda i: (i, 0)) for p in pieces_list]

    def _fwd_call(self, *args):
        nr = len(self.in_pieces)
        rows, params = args[:nr], args[nr:]
        m = rows[0].shape[0]
        tm = _pick(m, self.tm, SUBLANES)
        f, in_pieces, out_pieces = self.f, self.in_pieces, self.out_pieces
        no = len(out_pieces)

        def body(*refs):
            rin, pr, ro = refs[:nr], refs[nr:nr + len(params)], refs[nr + len(params):]
            xs = [r[:, o:o + w] for r, ps in zip(rin, in_pieces) for (o, w) in ps]
            ys = f(*xs, *[p[...] for p in pr])
            k = 0
            for r, ps in zip(ro, out_pieces):
                for (o, w) in ps:
                    r[:, o:o + w] = ys[k]
                    k += 1

        outs = _pcall(
            body, name=self.name + "_fwd",
            out_shape=[jax.ShapeDtypeStruct((m, self._width(p)), F32) for p in out_pieces],
            grid=(m // tm,),
            in_specs=self._row_specs(in_pieces, tm) + [pl.BlockSpec(p.shape, lambda i: (0, 0)) for p in params],
            out_specs=self._row_specs(out_pieces, tm),
        )(*rows, *params)
        return tuple(outs) if no > 1 else outs[0]

    def _bwd_call(self, res, g):
        nr = len(self.in_pieces)
        rows, params = res[:nr], res[nr:]
        no = len(self.out_pieces)
        gs = tuple(g) if no > 1 else (g,)
        m = rows[0].shape[0]
        tm = _pick(m, self.tm, SUBLANES)
        f, in_pieces, out_pieces = self.f, self.in_pieces, self.out_pieces
        npar = len(params)

        def body(*refs):
            rin, pr, dro = refs[:nr], refs[nr:nr + npar], refs[nr + npar:nr + npar + no]
            drin, dpr = refs[nr + npar + no:nr + npar + no + nr], refs[nr + npar + no + nr:]
            xs = [r[:, o:o + w] for r, ps in zip(rin, in_pieces) for (o, w) in ps]
            dys = [r[:, o:o + w] for r, ps in zip(dro, out_pieces) for (o, w) in ps]
            _, vjp = jax.vjp(lambda *a: tuple(f(*a)), *xs, *[p[...] for p in pr])
            grads = vjp(tuple(dys))
            k = 0
            for r, ps in zip(drin, in_pieces):
                for (o, w) in ps:
                    r[:, o:o + w] = grads[k]
                    k += 1

            @pl.when(pl.program_id(0) == 0)
            def _():
                for r in dpr:
                    r[...] = jnp.zeros_like(r)

            for j, r in enumerate(dpr):
                r[...] += grads[k + j]

        outs = _pcall(
            body, name=self.name + "_bwd",
            out_shape=[jax.ShapeDtypeStruct(r.shape, F32) for r in rows]
            + [jax.ShapeDtypeStruct(p.shape, F32) for p in params],
            grid=(m // tm,),
            in_specs=self._row_specs(in_pieces, tm) + [pl.BlockSpec(p.shape, lambda i: (0, 0)) for p in params]
            + self._row_specs(out_pieces, tm),
            out_specs=self._row_specs(in_pieces, tm) + [pl.BlockSpec(p.shape, lambda i: (0, 0)) for p in params],
        )(*rows, *params, *gs)
        return tuple(outs)


def _f_rms(x, g):
    return (_rms(x, g),)


def _f_headnorm(x0, x1, x2, x3, g):
    return tuple(_rms(x, g) for x in (x0, x1, x2, x3))


def _f_small(sm, fb, al, db):
    tm = sm.shape[0]
    logf = -_softplus(-(sm + fb))
    beta = _sigmoid(sm)
    glog = -jnp.exp(al) * _softplus(sm + db)
    r, c = _iota2(tm, tm, 0), _iota2(tm, tm, 1)
    bd = jnp.where((r >= c) & (jnp.bitwise_xor(r, c) < CHUNK), 1.0, 0.0).astype(F32)
    return logf, beta, tri_apply(bd, glog)


def _f_gdnpost(o0, o1, o2, o3, z0, z1, z2, z3, g):
    return tuple(_rms(o, g) * _silu(z) for o, z in zip((o0, o1, o2, o3), (z0, z1, z2, z3)))


def _f_merge(t0, t1, t2, a, b, c, b0, b1, b2):
    return (_sigmoid(t0 + b0) * a + _sigmoid(t1 + b1) * b + _sigmoid(t2 + b2) * c,)


def _make_rowops():
    return dict(
        rms=RowOp("rms", _f_rms, [_whole(D_MODEL)], [_whole(D_MODEL)]),
        headnorm=RowOp("headnorm", _f_headnorm, [_split(HW, HEADS)], [_split(HW, HEADS)]),
        small=RowOp("smallprep", _f_small, [_whole(LANES)], [_whole(LANES)] * 3),
        gdnpost=RowOp("gdnpost", _f_gdnpost, [_split(HW, HEADS)] * 2, [_split(HW, HEADS)]),
        merge=RowOp("merge", _f_merge, [_split(3 * D_MODEL, 3)] + [_whole(D_MODEL)] * 3, [_whole(D_MODEL)]),
    )


def _mm_call(name, a, b, mode, c=None):
    if mode == "nn":
        (m, kc), n = a.shape, b.shape[1]
    elif mode == "nt":
        (m, kc), n = a.shape, b.shape[0]
    else:
        (kc, m), n = a.shape, b.shape[1]
    tm = _pick(m, 1408, LANES) if mode == "tn" else _pick(m, 1024, SUBLANES)
    tn = _pick(n, 1024, LANES)
    tk = _pick(kc, 1024, SUBLANES) if mode == "tn" else _pick(kc, 1536, LANES)
    dn = {"nn": NN, "nt": NT, "tn": TN}[mode]
    a_spec = {"nn": pl.BlockSpec((tm, tk), lambda i, j, k: (i, k)),
              "nt": pl.BlockSpec((tm, tk), lambda i, j, k: (i, k)),
              "tn": pl.BlockSpec((tk, tm), lambda i, j, k: (k, i))}[mode]
    b_spec = {"nn": pl.BlockSpec((tk, tn), lambda i, j, k: (k, j)),
              "nt": pl.BlockSpec((tn, tk), lambda i, j, k: (j, k)),
              "tn": pl.BlockSpec((tk, tn), lambda i, j, k: (k, j))}[mode]
    o_spec = pl.BlockSpec((tm, tn), lambda i, j, k: (i, j))
    has_c = c is not None

    def body(*refs):
        a_ref, b_ref = refs[0], refs[1]
        o_ref = refs[-1]

        @pl.when(pl.program_id(2) == 0)
        def _():
            o_ref[...] = refs[2][...] if has_c else jnp.zeros_like(o_ref)

        o_ref[...] += _dot(a_ref[...], b_ref[...], dn)

    return _pcall(body, name=name, out_shape=jax.ShapeDtypeStruct((m, n), F32), grid=(m // tm, n // tn, kc // tk),
                  in_specs=[a_spec, b_spec] + ([o_spec] if has_c else []), out_specs=o_spec,
                  )(*((a, b, c) if has_c else (a, b)))


@jax.custom_vjp
def matmul(a, w):
    return _mm_call("mm_nn", a, w, "nn")


matmul.defvjp(lambda a, w: (_mm_call("mm_nn", a, w, "nn"), (a, w)),
              lambda r, g: (_mm_call("mm_nt", g, r[1], "nt"), _mm_call("mm_tn", r[0], g, "tn")))


@jax.custom_vjp
def matmul_add(c, a, w):
    return _mm_call("mm_nn_add", a, w, "nn", c)


matmul_add.defvjp(lambda c, a, w: (_mm_call("mm_nn_add", a, w, "nn", c), (a, w)),
                  lambda r, g: (g, _mm_call("mm_nt", g, r[1], "nt"), _mm_call("mm_tn", r[0], g, "tn")))

_PROJ_GROUPS = [(0, 512), (512, 512), (1024, 512), (1536, 1536), (3072, 512), (3584, 512), (4096, 512), (4608, 512),
                (5120, 3072), (8192, 128)]


def _proj_impl(h, w):
    return tuple(_mm_call("proj_nn", h, w[:, s:s + n], "nn") for s, n in _PROJ_GROUPS)


proj = jax.custom_vjp(_proj_impl)


def _proj_bwd(res, gs):
    h, w = res
    dh = None
    dws = []
    for (s, n), g in zip(_PROJ_GROUPS, gs):
        dh = _mm_call("proj_nt", g, w[:, s:s + n], "nt", dh)
        dws.append(_mm_call("proj_tn", h, g, "tn"))
    return dh, jnp.concatenate(dws, axis=1)


proj.defvjp(lambda h, w: (_proj_impl(h, w), (h, w)), _proj_bwd)


def _cumsum_call(x, reverse):
    s, w = x.shape
    tm = _pick(s, 256, SUBLANES)
    nb = s // tm

    def body(x_ref, o_ref, carry):
        @pl.when(pl.program_id(0) == 0)
        def _():
            carry[...] = jnp.zeros_like(carry)

        blk = x_ref[...]
        r, c = _iota2(tm, tm, 0), _iota2(tm, tm, 1)
        tri = jnp.where((r <= c) if reverse else (r >= c), 1.0, 0.0).astype(F32)
        o_ref[...] = _dotf(tri, blk, NN) + carry[...]
        carry[...] += jnp.sum(blk, axis=0, keepdims=True)

    idx = (lambda i: (nb - 1 - i, 0)) if reverse else (lambda i: (i, 0))
    return _pcall(body, name="cumsum_rev" if reverse else "cumsum", out_shape=jax.ShapeDtypeStruct((s, w), F32),
                  grid=(nb,), in_specs=[pl.BlockSpec((tm, w), idx)], out_specs=pl.BlockSpec((tm, w), idx),
                  scratch_shapes=[pltpu.VMEM((1, w), F32)])(x)


@jax.custom_vjp
def seq_cumsum(x):
    return _cumsum_call(x, False)


seq_cumsum.defvjp(lambda x: (_cumsum_call(x, False), None), lambda _, g: (_cumsum_call(g, True),))


HALO = SUBLANES


class ConvOp:
    def __init__(self, name, width, post, c_pieces, out_widths, has_bias, tm):
        self.name, self.width, self.post, self.c_pieces = name, width, post, c_pieces
        self.out_widths, self.has_bias, self.tm = out_widths, has_bias, tm
        op = jax.custom_vjp(self._fwd_call)
        op.defvjp(lambda *a: (self._fwd_call(*a), a), lambda res, g: self._bwd_call(res, g))
        self.op = op

    def __call__(self, *args):
        return self.op(*args)

    def _conv(self, i, x_ref, prev_ref, w_ref, b_ref, buf):
        tm = x_ref.shape[0]
        buf[0:HALO, :] = jnp.where(i > 0, prev_ref[...], 0.0)
        buf[HALO:HALO + tm, :] = x_ref[...]
        taps = [buf[pl.ds(HALO - (self.width - 1) + j, tm), :] for j in range(self.width)]
        c = taps[0] * w_ref[0:1, :]
        for j in range(1, self.width):
            c = c + taps[j] * w_ref[j:j + 1, :]
        if self.has_bias:
            c = c + b_ref[...]
        return c, taps

    def _fwd_call(self, x, w, *bias):
        s, ch = x.shape
        tm = _pick(s, self.tm, SUBLANES)
        r8 = tm // HALO
        has_bias, post, c_pieces = self.has_bias, self.post, self.c_pieces

        def body(*refs):
            x_ref, prev_ref, w_ref = refs[:3]
            b_ref = refs[3] if has_bias else None
            outs, buf = refs[3 + has_bias:-1], refs[-1]
            c, _ = self._conv(pl.program_id(0), x_ref, prev_ref, w_ref, b_ref, buf)
            ys = post(*[c[:, o:o + n] for o, n in c_pieces])
            for r, y in zip(outs, ys):
                r[...] = y

        outs = _pcall(
            body, name=self.name + "_fwd", out_shape=[jax.ShapeDtypeStruct((s, n), F32) for n in self.out_widths],
            grid=(s // tm,),
            in_specs=[pl.BlockSpec((tm, ch), lambda i: (i, 0)),
                      pl.BlockSpec((HALO, ch), lambda i: (jnp.maximum(i * r8 - 1, 0), 0)),
                      pl.BlockSpec(w.shape, lambda i: (0, 0))]
            + ([pl.BlockSpec((1, ch), lambda i: (0, 0))] if has_bias else []),
            out_specs=[pl.BlockSpec((tm, n), lambda i: (i, 0)) for n in self.out_widths],
            scratch_shapes=[pltpu.VMEM((tm + HALO, ch), F32)],
        )(x, x, w, *bias)
        return tuple(outs) if len(outs) > 1 else outs[0]

    def _bwd_call(self, res, g):
        x, w = res[0], res[1]
        bias = res[2:]
        gs = tuple(g) if len(self.out_widths) > 1 else (g,)
        s, ch = x.shape
        tm = _pick(s, self.tm, SUBLANES)
        r8 = tm // HALO
        nb = s // tm
        has_bias, post, c_pieces, width = self.has_bias, self.post, self.c_pieces, self.width
        ng = len(gs)

        def body1(*refs):
            x_ref, prev_ref, w_ref = refs[:3]
            b_ref = refs[3] if has_bias else None
            k = 3 + has_bias
            g_refs = refs[k:k + ng]
            dc_ref, dw_ref = refs[k + ng], refs[k + ng + 1]
            db_ref = refs[k + ng + 2] if has_bias else None
            buf = refs[-1]
            i = pl.program_id(0)
            c, taps = self._conv(i, x_ref, prev_ref, w_ref, b_ref, buf)
            _, vjp = jax.vjp(lambda *a: tuple(post(*a)), *[c[:, o:o + n] for o, n in c_pieces])
            dcs = vjp(tuple(r[...] for r in g_refs))
            for (o, n), d in zip(c_pieces, dcs):
                dc_ref[:, o:o + n] = d

            @pl.when(i == 0)
            def _():
                dw_ref[...] = jnp.zeros_like(dw_ref)
                if has_bias:
                    db_ref[...] = jnp.zeros_like(db_ref)

            dc = dc_ref[...]
            for j in range(width):
                dw_ref[j:j + 1, :] += jnp.sum(dc * taps[j], axis=0, keepdims=True)
            if has_bias:
                db_ref[...] += jnp.sum(dc, axis=0, keepdims=True)

        outs1 = _pcall(
            body1, name=self.name + "_bwd_act",
            out_shape=[jax.ShapeDtypeStruct((s, ch), F32), jax.ShapeDtypeStruct(w.shape, F32)]
            + ([jax.ShapeDtypeStruct((1, ch), F32)] if has_bias else []),
            grid=(nb,),
            in_specs=[pl.BlockSpec((tm, ch), lambda i: (i, 0)),
                      pl.BlockSpec((HALO, ch), lambda i: (jnp.maximum(i * r8 - 1, 0), 0)),
                      pl.BlockSpec(w.shape, lambda i: (0, 0))]
            + ([pl.BlockSpec((1, ch), lambda i: (0, 0))] if has_bias else [])
            + [pl.BlockSpec((tm, n), lambda i: (i, 0)) for n in self.out_widths],
            out_specs=[pl.BlockSpec((tm, ch), lambda i: (i, 0)), pl.BlockSpec(w.shape, lambda i: (0, 0))]
            + ([pl.BlockSpec((1, ch), lambda i: (0, 0))] if has_bias else []),
            scratch_shapes=[pltpu.VMEM((tm + HALO, ch), F32)],
        )(x, x, w, *bias, *gs)
        dc, dw = outs1[0], outs1[1]

        def body2(dc_ref, next_ref, w_ref, dx_ref, buf):
            i = pl.program_id(0)
            buf[0:tm, :] = dc_ref[...]
            buf[tm:tm + HALO, :] = jnp.where(i < nb - 1, next_ref[...], 0.0)
            dx = buf[pl.ds(width - 1, tm), :] * w_ref[0:1, :]
            for j in range(1, width):
                dx = dx + buf[pl.ds(width - 1 - j, tm), :] * w_ref[j:j + 1, :]
            dx_ref[...] = dx

        dx = _pcall(
            body2, name=self.name + "_bwd_in", out_shape=jax.ShapeDtypeStruct((s, ch), F32), grid=(nb,),
            in_specs=[pl.BlockSpec((tm, ch), lambda i: (i, 0)),
                      pl.BlockSpec((HALO, ch), lambda i: (jnp.minimum((i + 1) * r8, s // HALO - 1), 0)),
                      pl.BlockSpec(w.shape, lambda i: (0, 0))],
            out_specs=pl.BlockSpec((tm, ch), lambda i: (i, 0)),
            scratch_shapes=[pltpu.VMEM((tm + HALO, ch), F32)],
        )(dc, dc, w)
        return (dx, dw) + ((outs1[2],) if has_bias else ())


def _make_convops():
    return dict(
        gdn=ConvOp("gdnconv", 4, lambda q, k, v: (_silu(q), _silu(k), _silu(v)), _split(3 * HW, 3), [HW] * 3,
                   False, 256),
        ffn=ConvOp("ffnconv", 3, lambda a, b: (_silu(a) * b,), _split(2 * D_FF, 2), [D_FF], True, 128),
    )


ATT_Q = 512
ATT_K = 256
SCALE = HEAD_DIM ** -0.5


def _att_tiles(s):
    tk = _pick(s, ATT_K, LANES)
    tq = _pick(s, ATT_Q, tk)
    return tq, tk


def _att_specs(s, tq, tk):
    qspec = pl.BlockSpec((tq, HEAD_DIM), lambda h, i: (i, h))
    kspec = pl.BlockSpec((s, HEAD_DIM), lambda h, i: (0, h))
    colspec = pl.BlockSpec((None, tq, 1), lambda h, i: (h, i, 0))
    rowspec = pl.BlockSpec((None, s // tk, 1, tk), lambda h, i: (h, 0, 0, 0))
    return qspec, kspec, colspec, rowspec


def _krows(kb, tk):
    return pl.ds(pl.multiple_of(kb * tk, tk), tk)


def _stage_bf16(i, pairs):
    @pl.when(i == 0)
    def _():
        for src, dst in pairs:
            dst[...] = src[...].astype(BF16)


def _visible(i, kb, tq, tk, strict):
    rows = i * tq + _iota2(tq, tk, 0)
    cols = kb * tk + _iota2(tq, tk, 1)
    return (cols < rows) if strict else (cols <= rows)


def _fox_fwd_call(q, k, v, ccol, crow):
    s = q.shape[0]
    tq, tk = _att_tiles(s)
    ratio = tq // tk
    qspec, kspec, colspec, rowspec = _att_specs(s, tq, tk)

    def body(q_ref, k_ref, v_ref, cq_ref, ck_ref, o_ref, lse_ref, k16, v16):
        i = pl.program_id(1)
        _stage_bf16(i, [(k_ref, k16), (v_ref, v16)])
        qb = q_ref[...].astype(BF16)
        cq = cq_ref[...]

        def blk(kb, carry, masked):
            m, l, acc = carry
            sc = _dot(qb, k16[_krows(kb, tk), :], NT) * SCALE + (cq - ck_ref[kb])
            if masked:
                sc = jnp.where(_visible(i, kb, tq, tk, False), sc, NEG)
            m_new = jnp.maximum(m, jnp.max(sc, axis=-1, keepdims=True))
            alpha = jnp.exp(m - m_new)
            p = jnp.exp(sc - m_new)
            return (m_new, alpha * l + jnp.sum(p, axis=-1, keepdims=True),
                    alpha * acc + _dot(p, v16[_krows(kb, tk), :], NN))

        carry = (jnp.full((tq, 1), NEG, F32), jnp.zeros((tq, 1), F32), jnp.zeros((tq, HEAD_DIM), F32))
        carry = lax.fori_loop(0, i * ratio, lambda kb, c: blk(kb, c, False), carry)
        for j in range(ratio):
            carry = blk(i * ratio + j, carry, True)
        m, l, acc = carry
        o_ref[...] = acc / l
        lse_ref[...] = m + jnp.log(l)

    return _pcall(body, name="fox_fwd",
                  out_shape=[jax.ShapeDtypeStruct((s, HW), F32), jax.ShapeDtypeStruct((HEADS, s, 1), F32)],
                  grid=(HEADS, s // tq), in_specs=[qspec, kspec, kspec, colspec, rowspec],
                  out_specs=[qspec, colspec],
                  scratch_shapes=[pltpu.VMEM((s, HEAD_DIM), BF16)] * 2)(q, k, v, ccol, crow)


def _fox_bwd_call(q, k, v, ccol, crow, o, lse, do):
    s = q.shape[0]
    tq, tk = _att_tiles(s)
    ratio = tq // tk
    qspec, kspec, colspec, rowspec = _att_specs(s, tq, tk)

    def body(q_ref, k_ref, v_ref, cq_ref, ck_ref, o_ref, lse_ref, do_ref, dq_ref, dk_ref, dv_ref, dcq_ref, dck_ref,
             k16, v16):
        i = pl.program_id(1)
        _stage_bf16(i, [(k_ref, k16), (v_ref, v16)])

        @pl.when(i == 0)
        def _():
            dk_ref[...] = jnp.zeros_like(dk_ref)
            dv_ref[...] = jnp.zeros_like(dv_ref)
            dck_ref[...] = jnp.zeros_like(dck_ref)

        qb = q_ref[...].astype(BF16)
        dob = do_ref[...].astype(BF16)
        cq, lse = cq_ref[...], lse_ref[...]
        dl = jnp.sum(do_ref[...] * o_ref[...], axis=-1, keepdims=True)

        def blk(kb, carry, masked):
            dq, dcq = carry
            rows = _krows(kb, tk)
            kk, vv = k16[rows, :], v16[rows, :]
            sc = _dot(qb, kk, NT) * SCALE + (cq - ck_ref[kb])
            p = jnp.exp(sc - lse)
            if masked:
                p = jnp.where(_visible(i, kb, tq, tk, False), p, 0.0)
            dv_ref[rows, :] += _dot(p, dob, TN)
            ds = p * (_dot(dob, vv, NT) - dl)
            dk_ref[rows, :] += _dot(ds, qb, TN) * SCALE
            dck_ref[kb] += -jnp.sum(ds, axis=0, keepdims=True)
            return dq + _dot(ds, kk, NN) * SCALE, dcq + jnp.sum(ds, axis=-1, keepdims=True)

        carry = (jnp.zeros((tq, HEAD_DIM), F32), jnp.zeros((tq, 1), F32))
        carry = lax.fori_loop(0, i * ratio, lambda kb, c: blk(kb, c, False), carry)
        for j in range(ratio):
            carry = blk(i * ratio + j, carry, True)
        dq_ref[...] = carry[0]
        dcq_ref[...] = carry[1]

    return _pcall(body, name="fox_bwd",
                  out_shape=[jax.ShapeDtypeStruct((s, HW), F32)] * 3
                  + [jax.ShapeDtypeStruct((HEADS, s, 1), F32), jax.ShapeDtypeStruct((HEADS, s // tk, 1, tk), F32)],
                  grid=(HEADS, s // tq),
                  in_specs=[qspec, kspec, kspec, colspec, rowspec, qspec, colspec, qspec],
                  out_specs=[qspec, kspec, kspec, colspec, rowspec],
                  scratch_shapes=[pltpu.VMEM((s, HEAD_DIM), BF16)] * 2)(q, k, v, ccol, crow, o, lse, do)


@jax.custom_vjp
def fox_attention(q, k, v, ccol, crow):
    return _fox_fwd_call(q, k, v, ccol, crow)[0]


def _fox_vjp_fwd(q, k, v, ccol, crow):
    o, lse = _fox_fwd_call(q, k, v, ccol, crow)
    return o, (q, k, v, ccol, crow, o, lse)


fox_attention.defvjp(_fox_vjp_fwd, lambda res, g: tuple(_fox_bwd_call(*res, g)))


def _sb_fwd_call(q, k, v):
    s = q.shape[0]
    tq, tk = _att_tiles(s)
    ratio = tq // tk
    qspec, kspec, colspec, _ = _att_specs(s, tq, tk)

    def body(q_ref, k_ref, v_ref, o_ref, tot_ref, k16, v16):
        i = pl.program_id(1)
        _stage_bf16(i, [(k_ref, k16), (v_ref, v16)])
        qb = q_ref[...].astype(BF16)
        suffix = jnp.where(_iota2(tk, tk, 0) >= _iota2(tk, tk, 1), 1.0, 0.0).astype(BF16)

        def blk(kb, carry, masked):
            run, acc = carry
            z = _dot(qb, k16[_krows(kb, tk), :], NT) * SCALE
            lk = -_softplus(z)
            if masked:
                strict = _visible(i, kb, tq, tk, True)
                lk = jnp.where(strict, lk, 0.0)
            lk16 = lk.astype(BF16)
            a = jnp.exp(z + _dot(lk16, suffix, NN) + run)
            if masked:
                a = jnp.where(strict, a, 0.0)
            return (run + jnp.sum(lk16.astype(F32), axis=-1, keepdims=True),
                    acc + _dot(a, v16[_krows(kb, tk), :], NN))

        carry = (jnp.zeros((tq, 1), F32), jnp.zeros((tq, HEAD_DIM), F32))
        for j in reversed(range(ratio)):
            carry = blk(i * ratio + j, carry, True)
        run, acc = lax.fori_loop(0, i * ratio, lambda n, c: blk(i * ratio - 1 - n, c, False), carry)
        o_ref[...] = acc
        tot_ref[...] = run

    return _pcall(body, name="sb_fwd",
                  out_shape=[jax.ShapeDtypeStruct((s, HW), F32), jax.ShapeDtypeStruct((HEADS, s, 1), F32)],
                  grid=(HEADS, s // tq), in_specs=[qspec, kspec, kspec], out_specs=[qspec, colspec],
                  scratch_shapes=[pltpu.VMEM((s, HEAD_DIM), BF16)] * 2)(q, k, v)


def _sb_bwd_call(q, k, v, tot, do):
    s = q.shape[0]
    tq, tk = _att_tiles(s)
    ratio = tq // tk
    qspec, kspec, colspec, _ = _att_specs(s, tq, tk)

    def body(q_ref, k_ref, v_ref, tot_ref, do_ref, dq_ref, dk_ref, dv_ref, k16, v16):
        i = pl.program_id(1)
        _stage_bf16(i, [(k_ref, k16), (v_ref, v16)])

        @pl.when(i == 0)
        def _():
            dk_ref[...] = jnp.zeros_like(dk_ref)
            dv_ref[...] = jnp.zeros_like(dv_ref)

        qb = q_ref[...].astype(BF16)
        dob = do_ref[...].astype(BF16)
        tot = tot_ref[...]
        prefix = jnp.where(_iota2(tk, tk, 0) <= _iota2(tk, tk, 1), 1.0, 0.0).astype(BF16)

        def blk(kb, carry, masked):
            left, esum, dq = carry
            rows = _krows(kb, tk)
            kk, vv = k16[rows, :], v16[rows, :]
            z = _dot(qb, kk, NT) * SCALE
            lk = -_softplus(z)
            if masked:
                strict = _visible(i, kb, tq, tk, True)
                lk = jnp.where(strict, lk, 0.0)
            lk16 = lk.astype(BF16)
            lk = lk16.astype(F32)
            rc = (tot - left) - (_dot(lk16, prefix, NN) - lk)
            a = jnp.exp(z + rc)
            if masked:
                a = jnp.where(strict, a, 0.0)
            e = a * _dot(dob, vv, NT)
            dv_ref[rows, :] += _dot(a, dob, TN)
            dz = e - _sigmoid(z) * (esum + _dot(e, prefix, NN))
            if masked:
                dz = jnp.where(strict, dz, 0.0)
            dk_ref[rows, :] += _dot(dz, qb, TN) * SCALE
            return (left + jnp.sum(lk, axis=-1, keepdims=True), esum + jnp.sum(e, axis=-1, keepdims=True),
                    dq + _dot(dz, kk, NN) * SCALE)

        carry = (jnp.zeros((tq, 1), F32), jnp.zeros((tq, 1), F32), jnp.zeros((tq, HEAD_DIM), F32))
        carry = lax.fori_loop(0, i * ratio, lambda kb, c: blk(kb, c, False), carry)
        for j in range(ratio):
            carry = blk(i * ratio + j, carry, True)
        dq_ref[...] = carry[2]

    return _pcall(body, name="sb_bwd", out_shape=[jax.ShapeDtypeStruct((s, HW), F32)] * 3, grid=(HEADS, s // tq),
                  in_specs=[qspec, kspec, kspec, colspec, qspec], out_specs=[qspec, kspec, kspec],
                  scratch_shapes=[pltpu.VMEM((s, HEAD_DIM), BF16)] * 2)(q, k, v, tot, do)


@jax.custom_vjp
def sb_attention(q, k, v):
    return _sb_fwd_call(q, k, v)[0]


def _sb_vjp_fwd(q, k, v):
    o, tot = _sb_fwd_call(q, k, v)
    return o, (q, k, v, tot)


sb_attention.defvjp(_sb_vjp_fwd, lambda res, g: tuple(_sb_bwd_call(*res, g)))


def _mem_specs(s, nk, t):
    return (pl.BlockSpec((t, HEAD_DIM), lambda h, i: (i, h)), pl.BlockSpec((nk, HEAD_DIM), lambda h, i: (0, h)))


def _mem_probs(qb, kk):
    sc = _dot(qb, kk, NT) * SCALE
    p = jnp.exp(sc - jnp.max(sc, axis=-1, keepdims=True))
    return p / jnp.sum(p, axis=-1, keepdims=True)


def _mem_fwd_call(q, k, v):
    s, nk = q.shape[0], k.shape[0]
    t = _pick(s, 512, SUBLANES)
    qspec, kspec = _mem_specs(s, nk, t)

    def body(q_ref, k_ref, v_ref, o_ref):
        o_ref[...] = _dot(_mem_probs(q_ref[...].astype(BF16), k_ref[...]), v_ref[...], NN)

    return _pcall(body, name="mem_fwd", out_shape=jax.ShapeDtypeStruct((s, HW), F32), grid=(HEADS, s // t),
                  in_specs=[qspec, kspec, kspec], out_specs=qspec)(q, k, v)


def _mem_bwd_call(q, k, v, do):
    s, nk = q.shape[0], k.shape[0]
    t = _pick(s, 512, SUBLANES)
    qspec, kspec = _mem_specs(s, nk, t)

    def body(q_ref, k_ref, v_ref, do_ref, dq_ref, dk_ref, dv_ref):
        @pl.when(pl.program_id(1) == 0)
        def _():
            dk_ref[...] = jnp.zeros_like(dk_ref)
            dv_ref[...] = jnp.zeros_like(dv_ref)

        qb = q_ref[...].astype(BF16)
        dob = do_ref[...].astype(BF16)
        p = _mem_probs(qb, k_ref[...])
        dv_ref[...] += _dot(p, dob, TN)
        dp = _dot(dob, v_ref[...], NT)
        ds = p * (dp - jnp.sum(p * dp, axis=-1, keepdims=True))
        dq_ref[...] = _dot(ds, k_ref[...], NN) * SCALE
        dk_ref[...] += _dot(ds, qb, TN) * SCALE

    return _pcall(body, name="mem_bwd",
                  out_shape=[jax.ShapeDtypeStruct((s, HW), F32)] + [jax.ShapeDtypeStruct((nk, HW), F32)] * 2,
                  grid=(HEADS, s // t), in_specs=[qspec, kspec, kspec, qspec],
                  out_specs=[qspec, kspec, kspec])(q, k, v, do)


@jax.custom_vjp
def mem_attention(q, k, v):
    return _mem_fwd_call(q, k, v)


mem_attention.defvjp(lambda q, k, v: (_mem_fwd_call(q, k, v), (q, k, v)),
                     lambda res, g: tuple(_mem_bwd_call(*res, g)))


def _unit_lower_inverse(nm):
    eye = jnp.where(_iota2(CHUNK, CHUNK, 0) == _iota2(CHUNK, CHUNK, 1), 1.0, 0.0).astype(F32)
    p = eye - nm
    m = nm
    for _ in range(5):
        m = _dotf(m, m, NN)
        p = _dotf(p, eye + m, NN)
    return p


@jax.custom_vjp
def _solve2(nm, r1, r2):
    inv = _unit_lower_inverse(nm)
    return _dotf(inv, r1, NN), _dotf(inv, r2, NN)


def _solve2_fwd(nm, r1, r2):
    inv = _unit_lower_inverse(nm)
    u, w = _dotf(inv, r1, NN), _dotf(inv, r2, NN)
    return (u, w), (inv, u, w)


def _solve2_bwd(res, g):
    inv, u, w = res
    d1, d2 = _dotf(inv, g[0], TN), _dotf(inv, g[1], TN)
    return -(_dotf(d1, u, NT) + _dotf(d2, w, NT)), d1, d2


_solve2.defvjp(_solve2_fwd, _solve2_bwd)


def _gdn_chunk(q, k, v, gcc, gcr, b, gl, st):
    qn = q * lax.rsqrt(jnp.sum(q * q, axis=-1, keepdims=True) + EPS) * SCALE
    kn = k * lax.rsqrt(jnp.sum(k * k, axis=-1, keepdims=True) + EPS)
    r, c = _iota2(CHUNK, CHUNK, 0), _iota2(CHUNK, CHUNK, 1)
    decay = jnp.exp(jnp.where(r >= c, gcc - gcr, NEG))
    nm = jnp.where(r > c, b * mm_nt(kn, kn) * decay, 0.0)
    eg = jnp.exp(gcc)
    u, w = _solve2(nm, v * b, kn * (b * eg))
    attn = mm_nt(qn, kn) * decay
    v_new = u - mm(w, st)
    o = mm(qn * eg, st) + mm(attn, v_new)
    st_new = st * jnp.exp(gl) + mm_tn(kn * jnp.exp(gl - gcc), v_new)
    return o, st_new


GDN_ROWS = 512


def _gdn_specs(s, tg, rev):
    nb = s // tg
    cpb = tg // CHUNK
    j_of = (lambda j: nb - 1 - j) if rev else (lambda j: j)
    qspec = pl.BlockSpec((tg, HW), lambda j: (j_of(j), 0))
    colspec = pl.BlockSpec((HEADS, tg, 1), lambda j: (0, j_of(j), 0))
    rowspec = pl.BlockSpec((HEADS, cpb, 1, CHUNK), lambda j: (0, j_of(j), 0, 0))
    onespec = pl.BlockSpec((HEADS, cpb, 1, 1), lambda j: (0, j_of(j), 0, 0))
    stspec = pl.BlockSpec((HEADS, cpb, HEAD_DIM, HEAD_DIM), lambda j: (0, j_of(j), 0, 0))
    return qspec, colspec, rowspec, onespec, stspec


def _head_cols(h):
    return slice(h * HEAD_DIM, (h + 1) * HEAD_DIM)


def _gdn_fwd_call(q, k, v, gcc, gcr, bc, gl):
    s = q.shape[0]
    tg = _pick(s, GDN_ROWS, CHUNK)
    cpb = tg // CHUNK
    qspec, colspec, rowspec, onespec, stspec = _gdn_specs(s, tg, False)

    def body(q_ref, k_ref, v_ref, gcc_ref, gcr_ref, b_ref, gl_ref, o_ref, st_ref, st):
        @pl.when(pl.program_id(0) == 0)
        def _():
            st[...] = jnp.zeros_like(st)

        def chunk(ci, _):
            rows = pl.ds(pl.multiple_of(ci * CHUNK, CHUNK), CHUNK)
            for h in range(HEADS):
                cols = _head_cols(h)
                s_in = st[h]
                st_ref[h, ci] = s_in
                o, s_new = _gdn_chunk(q_ref[rows, cols], k_ref[rows, cols], v_ref[rows, cols], gcc_ref[h, rows, :],
                                      gcr_ref[h, ci], b_ref[h, rows, :], gl_ref[h, ci], s_in)
                o_ref[rows, cols] = o
                st[h] = s_new
            return 0

        lax.fori_loop(0, cpb, chunk, 0)

    return _pcall(body, name="gdn_fwd",
                  out_shape=[jax.ShapeDtypeStruct((s, HW), F32),
                             jax.ShapeDtypeStruct((HEADS, s // CHUNK, HEAD_DIM, HEAD_DIM), F32)],
                  grid=(s // tg,), in_specs=[qspec, qspec, qspec, colspec, rowspec, colspec, onespec],
                  out_specs=[qspec, stspec], scratch_shapes=[pltpu.VMEM((HEADS, HEAD_DIM, HEAD_DIM), F32)],
                  )(q, k, v, gcc, gcr, bc, gl)


def _gdn_bwd_call(q, k, v, gcc, gcr, bc, gl, states, do):
    s = q.shape[0]
    tg = _pick(s, GDN_ROWS, CHUNK)
    cpb = tg // CHUNK
    qspec, colspec, rowspec, onespec, stspec = _gdn_specs(s, tg, True)

    def body(q_ref, k_ref, v_ref, gcc_ref, gcr_ref, b_ref, gl_ref, st_ref, do_ref,
             dq_ref, dk_ref, dv_ref, dgcc_ref, dgcr_ref, db_ref, dgl_ref, dst):
        @pl.when(pl.program_id(0) == 0)
        def _():
            dst[...] = jnp.zeros_like(dst)

        def chunk(n, _):
            ci = cpb - 1 - n
            rows = pl.ds(pl.multiple_of(ci * CHUNK, CHUNK), CHUNK)
            for h in range(HEADS):
                cols = _head_cols(h)
                _, vjp = jax.vjp(_gdn_chunk, q_ref[rows, cols], k_ref[rows, cols], v_ref[rows, cols],
                                 gcc_ref[h, rows, :], gcr_ref[h, ci], b_ref[h, rows, :], gl_ref[h, ci], st_ref[h, ci])
                dq, dk, dv, dgcc, dgcr, db, dgl, ds_in = vjp((do_ref[rows, cols], dst[h]))
                dq_ref[rows, cols] = dq
                dk_ref[rows, cols] = dk
                dv_ref[rows, cols] = dv
                dgcc_ref[h, rows, :] = dgcc
                dgcr_ref[h, ci] = dgcr
                db_ref[h, rows, :] = db
                dgl_ref[h, ci] = dgl
                dst[h] = ds_in
            return 0

        lax.fori_loop(0, cpb, chunk, 0)

    n = s // CHUNK
    return _pcall(body, name="gdn_bwd",
                  out_shape=[jax.ShapeDtypeStruct((s, HW), F32)] * 3
                  + [jax.ShapeDtypeStruct((HEADS, s, 1), F32), jax.ShapeDtypeStruct((HEADS, n, 1, CHUNK), F32),
                     jax.ShapeDtypeStruct((HEADS, s, 1), F32), jax.ShapeDtypeStruct((HEADS, n, 1, 1), F32)],
                  grid=(s // tg,),
                  in_specs=[qspec, qspec, qspec, colspec, rowspec, colspec, onespec, stspec, qspec],
                  out_specs=[qspec, qspec, qspec, colspec, rowspec, colspec, onespec],
                  scratch_shapes=[pltpu.VMEM((HEADS, HEAD_DIM, HEAD_DIM), F32)],
                  )(q, k, v, gcc, gcr, bc, gl, states, do)


@jax.custom_vjp
def gated_delta(q, k, v, gcc, gcr, bc, gl):
    return _gdn_fwd_call(q, k, v, gcc, gcr, bc, gl)[0]


def _gdn_vjp_fwd(q, k, v, gcc, gcr, bc, gl):
    o, states = _gdn_fwd_call(q, k, v, gcc, gcr, bc, gl)
    return o, (q, k, v, gcc, gcr, bc, gl, states)


gated_delta.defvjp(_gdn_vjp_fwd, lambda res, g: tuple(_gdn_bwd_call(*res, g)))


def _loss_call(y, target):
    s, d = y.shape
    tm = _pick(s, 512, SUBLANES)

    def body(y_ref, t_ref, dy_ref, loss_ref):
        @pl.when(pl.program_id(0) == 0)
        def _():
            loss_ref[...] = jnp.zeros_like(loss_ref)

        err = y_ref[...] - t_ref[...]
        dy_ref[...] = err * (1.0 / d)
        loss_ref[...] += 0.5 * jnp.sum(jnp.mean(err * err, axis=-1, keepdims=True), axis=0, keepdims=True)

    dy, part = _pcall(body, name="loss_head",
                      out_shape=[jax.ShapeDtypeStruct((s, d), F32), jax.ShapeDtypeStruct((1, 1), F32)],
                      grid=(s // tm,), in_specs=[pl.BlockSpec((tm, d), lambda i: (i, 0))] * 2,
                      out_specs=[pl.BlockSpec((tm, d), lambda i: (i, 0)), pl.BlockSpec((1, 1), lambda i: (0, 0))],
                      )(y, target)
    return part[0, 0], dy


def _cols_and_rows(a, lane0, t):
    s = a.shape[0]
    at = a[:, lane0:lane0 + HEADS].T
    return at, at[:, :, None], at.reshape(HEADS, s // t, 1, t)


def _pad_lanes(v, lane0):
    return jnp.pad(v, (lane0, LANES - lane0 - v.shape[0])).reshape(1, LANES)


def _layer(x, mem, w, ops, convs):
    s = x.shape[0]
    row = lambda v: v.reshape(1, -1)
    h = ops["rms"](x, row(w["norm_mix"]))
    fq, fk, fv, gqkv, gz, sq, sk, sv, gt, gm = proj(h, w["w_in"])

    logf, beta, gc = ops["small"](gm, _pad_lanes(w["fox_fbias"], LANE_FF), _pad_lanes(w["gdn_a_log"], LANE_GA),
                                  _pad_lanes(w["gdn_dt_bias"], LANE_GA))
    _, ccol, crow = _cols_and_rows(seq_cumsum(logf), LANE_FF, _att_tiles(s)[1])
    ya = fox_attention(ops["headnorm"](fq, row(w["fox_qnorm"])), ops["headnorm"](fk, row(w["fox_knorm"])), fv,
                       ccol, crow)
    cq, ck, cv = convs["gdn"](gqkv, w["gdn_conv"])
    gct, gcc, gcr = _cols_and_rows(gc, LANE_GA, CHUNK)
    gl = gct.reshape(HEADS, s // CHUNK, CHUNK)[:, :, CHUNK - 1].reshape(HEADS, s // CHUNK, 1, 1)
    bc = beta[:, LANE_GB:LANE_GB + HEADS].T[:, :, None]
    yb = ops["gdnpost"](gated_delta(cq, ck, cv, gcc, gcr, bc, gl), gz, row(w["gdn_onorm"]))
    yc = sb_attention(sq, sk, sv)
    gb = w["gate_bias"]
    mixed = ops["merge"](gt, matmul(ya, w["w_oa"]), matmul(yb, w["w_ob"]), matmul(yc, w["w_oc"]),
                         row(gb[:D_MODEL]), row(gb[D_MODEL:2 * D_MODEL]), row(gb[2 * D_MODEL:]))
    x = matmul_add(x, mixed, w["w_out"])
    mq = ops["headnorm"](matmul(ops["rms"](x, row(w["norm_xq"])), w["w_mq"]), row(w["mq_norm"]))
    kv = matmul(ops["rms"](mem, row(w["norm_mem"])), w["w_mkv"])
    mk = ops["headnorm"](kv[:, :HW], row(w["mk_norm"]))
    x = matmul_add(x, mem_attention(mq, mk, kv[:, HW:]), w["w_mo"])
    u = matmul(ops["rms"](x, row(w["norm_ffn"])), w["w_up"])
    act = convs["ffn"](u, w["ffn_conv"], row(w["ffn_conv_b"]))
    return matmul_add(x, act, w["w_down"])


def _forward(x, mem, layers):
    ops, convs = _make_rowops(), _make_convops()
    for w in layers:
        x = _layer(x, mem, w, ops, convs)
    return x


ANY = pl.BlockSpec(memory_space=pl.ANY)


N_PEERS = N_DEV - 1


def _ccall(body, *, name, out_shape, n_arrays):
    return pl.pallas_call(body, name=name, out_shape=out_shape, in_specs=[ANY] * n_arrays,
                          out_specs=[ANY] * n_arrays,
                          scratch_shapes=[pltpu.SemaphoreType.DMA((N_PEERS * n_arrays,)),
                                          pltpu.SemaphoreType.DMA((N_PEERS * n_arrays,)),
                                          pltpu.SemaphoreType.DMA((n_arrays,))],
                          interpret=False)


def _all_gather(name, shards):
    n = len(shards)

    def body(*refs):
        x_refs, out_refs = refs[:n], refs[n:2 * n]
        send_sems, recv_sems, local_sems = refs[2 * n:]
        x, y, c = lax.axis_index("x"), lax.axis_index("y"), lax.axis_index("c")
        me, sibling = (x, y, c), (x, y, 1 - c)
        chips = [(1 - x, y), (x, 1 - y), (1 - x, 1 - y)]

        def slot(a, px, py, pc):
            return out_refs[a].at[4 * px + 2 * py + pc]

        def copy(a, k, block, to, src=None):
            return pltpu.make_async_remote_copy(
                src_ref=slot(a, *block) if src is None else src, dst_ref=slot(a, *block),
                send_sem=send_sems.at[N_PEERS * a + k], recv_sem=recv_sems.at[N_PEERS * a + k], device_id=to,
                device_id_type=MESH)

        mine = [pltpu.make_async_copy(x_refs[a], slot(a, *me), local_sems.at[a]) for a in range(n)]
        first = []
        for a in range(n):
            first.append(copy(a, 0, me, sibling, src=x_refs[a]))
            first += [copy(a, 1 + j, me, (*chip, c), src=x_refs[a]) for j, chip in enumerate(chips)]
        for cp in mine + first:
            cp.start()
        passed = []
        for j, chip in enumerate(chips):
            for a in range(n):
                copy(a, 1 + j, (*chip, c), me).wait_recv()
                passed.append(copy(a, 4 + j, (*chip, c), sibling))
                passed[-1].start()
        for a in range(n):
            copy(a, 0, sibling, me).wait_recv()
            for j, chip in enumerate(chips):
                copy(a, 4 + j, (*chip, 1 - c), me).wait_recv()
        for cp in first + passed:
            cp.wait_send()
        for cp in mine:
            cp.wait()

    return _ccall(body, name=name, out_shape=[jax.ShapeDtypeStruct((N_DEV,) + s.shape, s.dtype) for s in shards],
                  n_arrays=n)(*shards)


def _exchange(name, parts):
    n = len(parts)

    def body(*refs):
        p_refs, out_refs = refs[:n], refs[n:2 * n]
        send_sems, recv_sems, local_sems = refs[2 * n:]
        x, y, c = lax.axis_index("x"), lax.axis_index("y"), lax.axis_index("c")
        me = 4 * x + 2 * y + c

        def peer(k):
            return (x ^ ((k >> 2) & 1), y ^ ((k >> 1) & 1), c ^ (k & 1))

        def copy(a, k, receive):
            px, py, pc = peer(k)
            theirs = 4 * px + 2 * py + pc
            return pltpu.make_async_remote_copy(
                src_ref=out_refs[a].at[theirs] if receive else p_refs[a].at[theirs],
                dst_ref=out_refs[a].at[theirs if receive else me],
                send_sem=send_sems.at[N_PEERS * a + k - 1], recv_sem=recv_sems.at[N_PEERS * a + k - 1],
                device_id=(px, py, pc), device_id_type=MESH)

        mine = [pltpu.make_async_copy(p_refs[a].at[me], out_refs[a].at[me], local_sems.at[a]) for a in range(n)]
        sends = [copy(a, k, False) for k in range(1, N_DEV) for a in range(n)]
        for cp in mine + sends:
            cp.start()
        for k in range(1, N_DEV):
            for a in range(n):
                copy(a, k, True).wait_recv()
        for cp in sends:
            cp.wait_send()
        for cp in mine:
            cp.wait()

    return _ccall(body, name=name, out_shape=[jax.ShapeDtypeStruct(p.shape, p.dtype) for p in parts],
                  n_arrays=n)(*parts)


ADAM_SLOT_BYTES = 4 * 1024 * 1024


def _adam_call(name, w, slots, m, v):
    r, n = w.shape
    rows_unit = 2 * SUBLANES
    tr = _pick(r, max(rows_unit, ADAM_SLOT_BYTES // (N_DEV * n * 4)), rows_unit)
    spec = pl.BlockSpec((tr, n), lambda i: (i, 0))

    def body(w_ref, s_ref, m_ref, v_ref, g_ref, d_ref, nm_ref, nv_ref):
        g = s_ref[0].astype(F32)
        for d in range(1, N_DEV):
            g = g + s_ref[d].astype(F32)
        nm = ADAM_B1 * m_ref[...] + (1.0 - ADAM_B1) * g
        nv = ADAM_B2 * v_ref[...] + (1.0 - ADAM_B2) * (g * g)
        m_hat = nm / (1.0 - ADAM_B1 ** ADAM_STEP)
        v_hat = nv / (1.0 - ADAM_B2 ** ADAM_STEP)
        g_ref[...] = g
        d_ref[...] = -ADAM_LR * (m_hat / (jnp.sqrt(v_hat) + ADAM_EPS) + ADAM_WD * w_ref[...])
        nm_ref[...] = nm
        nv_ref[...] = nv

    return _pcall(body, name=name, out_shape=[jax.ShapeDtypeStruct((r, n), F32)] * 4, grid=(r // tr,),
                  in_specs=[spec, pl.BlockSpec((N_DEV, tr, n), lambda i: (0, i, 0)), spec, spec],
                  out_specs=[spec] * 4)(w, slots, m, v)


def _pack_rows(flat, rows):
    return jnp.pad(flat, (0, rows * PACK_COLS - flat.shape[0])).reshape(rows, PACK_COLS)


def _regroup_in(w_in):
    cols = [w_in[:, a:b] for a, b in _IN_SRC]
    return jnp.concatenate(cols + [jnp.zeros((w_in.shape[0], N_IN_PAD - N_IN), F32)], axis=1)


def _ungroup_in(d):
    starts = {}
    off = 0
    for a, b in _IN_SRC:
        starts[a] = (off, b - a)
        off += b - a
    return jnp.concatenate([d[:, starts[a][0]:starts[a][0] + starts[a][1]] for a in sorted(starts)], axis=1)


def _full_weights(gathered):
    out = {}
    for n, g in zip(SHARDED_ORDER, gathered):
        (r, c), axis = SHARDED[n]
        out[n] = (g.reshape(r, c) if axis == 0 else g.transpose(1, 0, 2).reshape(r, c)).astype(F32)
    out["w_in"] = _regroup_in(out["w_in"])
    return out


def _for_transport(name, shard):
    return shard if name in ("gdn_conv", "ffn_conv") else shard.astype(BF16)


def _grad_parts(grads):
    parts = []
    for n in SHARDED_ORDER:
        (r, c), axis = SHARDED[n]
        g = _ungroup_in(grads[n]) if n == "w_in" else grads[n]
        if axis == 0:
            parts.append(g.reshape(N_DEV, r // N_DEV, c).astype(BF16))
        else:
            parts.append(g.reshape(r, N_DEV, c // N_DEV).transpose(1, 0, 2).astype(BF16))
    return parts


def _pack_small(vals):
    return _pack_rows(jnp.concatenate([vals[n].reshape(-1) for n in SMALL_ORDER]), SMALL_ROWS)


def _unpack_small(packed):
    flat = packed.reshape(-1)
    out, off = {}, 0
    for n in SMALL_ORDER:
        size = DEPTH * SMALL_WIDTH[n]
        out[n] = flat[off:off + size].reshape(DEPTH, SMALL_WIDTH[n])
        off += size
    return out


def kernel(x, mem, norm_mix, w_in, fox_fbias, fox_qnorm, fox_knorm, gdn_conv, gdn_a_log, gdn_dt_bias, gdn_onorm, gate_bias, w_oa, w_ob, w_oc, w_out, norm_xq, norm_mem, w_mq, w_mkv, mq_norm, mk_norm, w_mo, norm_ffn, w_up, ffn_conv, ffn_conv_b, w_down, loss_target, m_norm_mix, m_w_in, m_fox_fbias, m_fox_qnorm, m_fox_knorm, m_gdn_conv, m_gdn_a_log, m_gdn_dt_bias, m_gdn_onorm, m_gate_bias, m_w_oa, m_w_ob, m_w_oc, m_w_out, m_norm_xq, m_norm_mem, m_w_mq, m_w_mkv, m_mq_norm, m_mk_norm, m_w_mo, m_norm_ffn, m_w_up, m_ffn_conv, m_ffn_conv_b, m_w_down, v_norm_mix, v_w_in, v_fox_fbias, v_fox_qnorm, v_fox_knorm, v_gdn_conv, v_gdn_a_log, v_gdn_dt_bias, v_gdn_onorm, v_gate_bias, v_w_oa, v_w_ob, v_w_oc, v_w_out, v_norm_xq, v_norm_mem, v_w_mq, v_w_mkv, v_mq_norm, v_mk_norm, v_w_mo, v_norm_ffn, v_w_up, v_ffn_conv, v_ffn_conv_b, v_w_down):
    given = dict(locals())
    wts = {n: given[n] for n in WEIGHTS}
    mom = {n: given["m_" + n] for n in WEIGHTS}
    var = {n: given["v_" + n] for n in WEIGHTS}

    layers = []
    for l in range(DEPTH):
        full = _full_weights(_all_gather("gather_weights", [_for_transport(n, wts[n][l]) for n in SHARDED_ORDER]))
        full.update({n: wts[n][l] for n in SMALL_ORDER})
        layers.append(full)

    y, vjp = jax.vjp(lambda xx, ww: _forward(xx, mem[0], ww), x[0], layers)
    loss_part, dy = _loss_call(y, loss_target[0])
    dx, dlayers = vjp(dy)
    loss = lax.psum(loss_part, ("x", "y", "c"))

    out = {}
    per_layer = []
    for l in range(DEPTH):
        slots = _exchange("exchange_grads", _grad_parts(dlayers[l]))
        per_layer.append({n: _adam_call("adam_shard", wts[n][l], sl, mom[n][l], var[n][l])
                          for n, sl in zip(SHARDED_ORDER, slots)})
    for n in SHARDED_ORDER:
        for k, kind in enumerate(("grad_", "delta_", "new_m_", "new_v_")):
            out[kind + n] = jnp.stack([per_layer[l][n][k] for l in range(DEPTH)])
    dsmall = {n: jnp.stack([dlayers[l][n] for l in range(DEPTH)]) for n in SMALL_ORDER}
    slots = _all_gather("gather_small_grads", [_pack_small(dsmall)])[0]
    res = _adam_call("adam_small", _pack_small(wts), slots, _pack_small(mom), _pack_small(var))
    for k, kind in enumerate(("grad_", "delta_", "new_m_", "new_v_")):
        un = _unpack_small(res[k])
        for n in SMALL_ORDER:
            out[kind + n] = un[n].reshape(wts[n].shape)

    return (loss, dx[None], *[out["grad_" + n] for n in WEIGHTS], *[out["delta_" + n] for n in WEIGHTS],
            *[out["new_m_" + n] for n in WEIGHTS], *[out["new_v_" + n] for n in WEIGHTS])
```

```python
import jax
import jax.numpy as jnp
from jax import lax
from jax.experimental import pallas as pl
from jax.experimental.pallas import tpu as pltpu

F32 = jnp.float32
BF16 = jnp.bfloat16

N_DEV = 8
D_MODEL = 1024
DEPTH = 4
CHUNK = 64
EPS = 1e-6
HEADS = 4
HEAD_DIM = 128
HW = HEADS * HEAD_DIM
D_FF = 2816
N_IN = 8204
LANES = 128
SUBLANES = 8
VMEM_LIMIT = 56 * 1024 * 1024

ADAM_LR = 0.001
ADAM_B1 = 0.9
ADAM_B2 = 0.999
ADAM_EPS = 1e-08
ADAM_WD = 0.01
ADAM_STEP = 10

NEG = -1e30
MESH = pl.DeviceIdType.MESH

WEIGHTS = ['norm_mix', 'w_in', 'fox_fbias', 'fox_qnorm', 'fox_knorm', 'gdn_conv', 'gdn_a_log', 'gdn_dt_bias',
           'gdn_onorm', 'gate_bias', 'w_oa', 'w_ob', 'w_oc', 'w_out', 'norm_xq', 'norm_mem', 'w_mq', 'w_mkv',
           'mq_norm', 'mk_norm', 'w_mo', 'norm_ffn', 'w_up', 'ffn_conv', 'ffn_conv_b', 'w_down']
SHARDED = {
    'w_in': ((D_MODEL, N_IN), 0), 'gdn_conv': ((4, 3 * HW), 1), 'w_oa': ((HW, D_MODEL), 1),
    'w_ob': ((HW, D_MODEL), 1), 'w_oc': ((HW, D_MODEL), 1), 'w_out': ((D_MODEL, D_MODEL), 0),
    'w_mq': ((D_MODEL, HW), 0), 'w_mkv': ((D_MODEL, 2 * HW), 0), 'w_mo': ((HW, D_MODEL), 1),
    'w_up': ((D_MODEL, 2 * D_FF), 1), 'ffn_conv': ((3, 2 * D_FF), 1), 'w_down': ((D_FF, D_MODEL), 0),
}
SHARDED_ORDER = [n for n in WEIGHTS if n in SHARDED]
SMALL_ORDER = [n for n in WEIGHTS if n not in SHARDED]
SMALL_WIDTH = {'norm_mix': D_MODEL, 'fox_fbias': HEADS, 'fox_qnorm': HEAD_DIM, 'fox_knorm': HEAD_DIM,
               'gdn_a_log': HEADS, 'gdn_dt_bias': HEADS, 'gdn_onorm': HEAD_DIM, 'gate_bias': 3 * D_MODEL,
               'norm_xq': D_MODEL, 'norm_mem': D_MODEL, 'mq_norm': HEAD_DIM, 'mk_norm': HEAD_DIM,
               'norm_ffn': D_MODEL, 'ffn_conv_b': 2 * D_FF}
PACK_COLS = 1024


def _round_up(n, m):
    return (n + m - 1) // m * m


SMALL_ROWS = _round_up(DEPTH * sum(SMALL_WIDTH.values()), SUBLANES * PACK_COLS) // PACK_COLS

_IN_SRC = [(0, 512), (512, 1024), (1024, 1536),
           (1540, 2052), (2052, 2564), (2564, 3076),
           (3084, 3596),
           (3596, 4108), (4108, 4620), (4620, 5132),
           (5132, 8204),
           (1536, 1540), (3076, 3080), (3080, 3084)]
N_IN_PAD = 8320
LANE_FF, LANE_GB, LANE_GA = 0, 4, 8


def _pick(dim, pref, unit):
    best = None
    t = unit
    while t <= min(dim, pref):
        if dim % t == 0:
            best = t
        t += unit
    return dim if best is None else best


def _params(n_grid):
    return pltpu.CompilerParams(dimension_semantics=("arbitrary",) * n_grid, vmem_limit_bytes=VMEM_LIMIT)


def _pcall(body, *, name, out_shape, grid, in_specs, out_specs, scratch_shapes=()):
    return pl.pallas_call(body, name=name, out_shape=out_shape, grid=grid, in_specs=in_specs, out_specs=out_specs,
                          scratch_shapes=scratch_shapes, compiler_params=_params(len(grid)),
                          interpret=False)


NN = ((1,), (0,))
NT = ((1,), (1,))
TN = ((0,), (0,))


def _dot(a, b, dn):
    return lax.dot_general(a.astype(BF16), b.astype(BF16), (dn, ((), ())), preferred_element_type=F32)


def _dotf(a, b, dn):
    return lax.dot_general(a, b, (dn, ((), ())), precision=lax.Precision.HIGHEST, preferred_element_type=F32)


@jax.custom_vjp
def mm(a, b):
    return _dot(a, b, NN)


mm.defvjp(lambda a, b: (_dot(a, b, NN), (a, b)), lambda r, g: (_dot(g, r[1], NT), _dot(r[0], g, TN)))


@jax.custom_vjp
def mm_nt(a, b):
    return _dot(a, b, NT)


mm_nt.defvjp(lambda a, b: (_dot(a, b, NT), (a, b)), lambda r, g: (_dot(g, r[1], NN), _dot(g, r[0], TN)))


@jax.custom_vjp
def mm_tn(a, b):
    return _dot(a, b, TN)


mm_tn.defvjp(lambda a, b: (_dot(a, b, TN), (a, b)), lambda r, g: (_dot(r[1], g, NT), _dot(r[0], g, NN)))


@jax.custom_vjp
def tri_apply(t, x):
    return _dotf(t, x, NN)


tri_apply.defvjp(lambda t, x: (_dotf(t, x, NN), t), lambda t, g: (jnp.zeros_like(t), _dotf(t, g, TN)))


def _sigmoid(x):
    return 1.0 / (1.0 + jnp.exp(-x))


@jax.custom_vjp
def _softplus(x):
    return jnp.maximum(x, 0.0) + jnp.log(1.0 + jnp.exp(-jnp.abs(x)))


_softplus.defvjp(lambda x: (_softplus(x), x), lambda x, g: (g * _sigmoid(x),))


def _silu(x):
    return x * _sigmoid(x)


def _rms(x, g):
    return x * lax.rsqrt(jnp.mean(x * x, axis=-1, keepdims=True) + EPS) * g


def _iota2(n, m, axis):
    return lax.broadcasted_iota(jnp.int32, (n, m), axis)


def _whole(width):
    return [(0, width)]


def _split(width, n):
    w = width // n
    return [(k * w, w) for k in range(n)]


class RowOp:
    def __init__(self, name, f, in_pieces, out_pieces, tm=256):
        self.name, self.f, self.in_pieces, self.out_pieces, self.tm = name, f, in_pieces, out_pieces, tm
        op = jax.custom_vjp(self._fwd_call)
        op.defvjp(lambda *a: (self._fwd_call(*a), a), lambda res, g: self._bwd_call(res, g))
        self.op = op

    def __call__(self, *args):
        return self.op(*args)

    def _width(self, pieces):
        return max(o + w for o, w in pieces)

    def _row_specs(self, pieces_list, tm):
        return [pl.BlockSpec((tm, self._width(p)), lambda i: (i, 0)) for p in pieces_list]

    def _fwd_call(self, *args):
        nr = len(self.in_pieces)
        rows, params = args[:nr], args[nr:]
        m = rows[0].shape[0]
        tm = _pick(m, self.tm, SUBLANES)
        f, in_pieces, out_pieces = self.f, self.in_pieces, self.out_pieces
        no = len(out_pieces)

        def body(*refs):
            rin, pr, ro = refs[:nr], refs[nr:nr + len(params)], refs[nr + len(params):]
            xs = [r[:, o:o + w] for r, ps in zip(rin, in_pieces) for (o, w) in ps]
            ys = f(*xs, *[p[...] for p in pr])
            k = 0
            for r, ps in zip(ro, out_pieces):
                for (o, w) in ps:
                    r[:, o:o + w] = ys[k]
                    k += 1

        outs = _pcall(
            body, name=self.name + "_fwd",
            out_shape=[jax.ShapeDtypeStruct((m, self._width(p)), F32) for p in out_pieces],
            grid=(m // tm,),
            in_specs=self._row_specs(in_pieces, tm) + [pl.BlockSpec(p.shape, lambda i: (0, 0)) for p in params],
            out_specs=self._row_specs(out_pieces, tm),
        )(*rows, *params)
        return tuple(outs) if no > 1 else outs[0]

    def _bwd_call(self, res, g):
        nr = len(self.in_pieces)
        rows, params = res[:nr], res[nr:]
        no = len(self.out_pieces)
        gs = tuple(g) if no > 1 else (g,)
        m = rows[0].shape[0]
        tm = _pick(m, self.tm, SUBLANES)
        f, in_pieces, out_pieces = self.f, self.in_pieces, self.out_pieces
        npar = len(params)

        def body(*refs):
            rin, pr, dro = refs[:nr], refs[nr:nr + npar], refs[nr + npar:nr + npar + no]
            drin, dpr = refs[nr + npar + no:nr + npar + no + nr], refs[nr + npar + no + nr:]
            xs = [r[:, o:o + w] for r, ps in zip(rin, in_pieces) for (o, w) in ps]
            dys = [r[:, o:o + w] for r, ps in zip(dro, out_pieces) for (o, w) in ps]
            _, vjp = jax.vjp(lambda *a: tuple(f(*a)), *xs, *[p[...] for p in pr])
            grads = vjp(tuple(dys))
            k = 0
            for r, ps in zip(drin, in_pieces):
                for (o, w) in ps:
                    r[:, o:o + w] = grads[k]
                    k += 1

            @pl.when(pl.program_id(0) == 0)
            def _():
                for r in dpr:
                    r[...] = jnp.zeros_like(r)

            for j, r in enumerate(dpr):
                r[...] += grads[k + j]

        outs = _pcall(
            body, name=self.name + "_bwd",
            out_shape=[jax.ShapeDtypeStruct(r.shape, F32) for r in rows]
            + [jax.ShapeDtypeStruct(p.shape, F32) for p in params],
            grid=(m // tm,),
            in_specs=self._row_specs(in_pieces, tm) + [pl.BlockSpec(p.shape, lambda i: (0, 0)) for p in params]
            + self._row_specs(out_pieces, tm),
            out_specs=self._row_specs(in_pieces, tm) + [pl.BlockSpec(p.shape, lambda i: (0, 0)) for p in params],
        )(*rows, *params, *gs)
        return tuple(outs)


def _f_rms(x, g):
    return (_rms(x, g),)


def _f_headnorm(x0, x1, x2, x3, g):
    return tuple(_rms(x, g) for x in (x0, x1, x2, x3))


def _f_small(sm, fb, al, db):
    tm = sm.shape[0]
    logf = -_softplus(-(sm + fb))
    beta = _sigmoid(sm)
    glog = -jnp.exp(al) * _softplus(sm + db)
    r, c = _iota2(tm, tm, 0), _iota2(tm, tm, 1)
    bd = jnp.where((r >= c) & (jnp.bitwise_xor(r, c) < CHUNK), 1.0, 0.0).astype(F32)
    return logf, beta, tri_apply(bd, glog)


def _f_gdnpost(o0, o1, o2, o3, z0, z1, z2, z3, g):
    return tuple(_rms(o, g) * _silu(z) for o, z in zip((o0, o1, o2, o3), (z0, z1, z2, z3)))


def _f_merge(t0, t1, t2, a, b, c, b0, b1, b2):
    return (_sigmoid(t0 + b0) * a + _sigmoid(t1 + b1) * b + _sigmoid(t2 + b2) * c,)


def _make_rowops():
    return dict(
        rms=RowOp("rms", _f_rms, [_whole(D_MODEL)], [_whole(D_MODEL)]),
        headnorm=RowOp("headnorm", _f_headnorm, [_split(HW, HEADS)], [_split(HW, HEADS)]),
        small=RowOp("smallprep", _f_small, [_whole(LANES)], [_whole(LANES)] * 3),
        gdnpost=RowOp("gdnpost", _f_gdnpost, [_split(HW, HEADS)] * 2, [_split(HW, HEADS)]),
        merge=RowOp("merge", _f_merge, [_split(3 * D_MODEL, 3)] + [_whole(D_MODEL)] * 3, [_whole(D_MODEL)]),
    )


def _mm_call(name, a, b, mode, c=None):
    if mode == "nn":
        (m, kc), n = a.shape, b.shape[1]
    elif mode == "nt":
        (m, kc), n = a.shape, b.shape[0]
    else:
        (kc, m), n = a.shape, b.shape[1]
    tm = _pick(m, 1408, LANES) if mode == "tn" else _pick(m, 1024, SUBLANES)
    tn = _pick(n, 1024, LANES)
    tk = _pick(kc, 1024, SUBLANES) if mode == "tn" else _pick(kc, 1536, LANES)
    dn = {"nn": NN, "nt": NT, "tn": TN}[mode]
    a_spec = {"nn": pl.BlockSpec((tm, tk), lambda i, j, k: (i, k)),
              "nt": pl.BlockSpec((tm, tk), lambda i, j, k: (i, k)),
              "tn": pl.BlockSpec((tk, tm), lambda i, j, k: (k, i))}[mode]
    b_spec = {"nn": pl.BlockSpec((tk, tn), lambda i, j, k: (k, j)),
              "nt": pl.BlockSpec((tn, tk), lambda i, j, k: (j, k)),
              "tn": pl.BlockSpec((tk, tn), lambda i, j, k: (k, j))}[mode]
    o_spec = pl.BlockSpec((tm, tn), lambda i, j, k: (i, j))
    has_c = c is not None

    def body(*refs):
        a_ref, b_ref = refs[0], refs[1]
        o_ref = refs[-1]

        @pl.when(pl.program_id(2) == 0)
        def _():
            o_ref[...] = refs[2][...] if has_c else jnp.zeros_like(o_ref)

        o_ref[...] += _dot(a_ref[...], b_ref[...], dn)

    return _pcall(body, name=name, out_shape=jax.ShapeDtypeStruct((m, n), F32), grid=(m // tm, n // tn, kc // tk),
                  in_specs=[a_spec, b_spec] + ([o_spec] if has_c else []), out_specs=o_spec,
                  )(*((a, b, c) if has_c else (a, b)))


@jax.custom_vjp
def matmul(a, w):
    return _mm_call("mm_nn", a, w, "nn")


matmul.defvjp(lambda a, w: (_mm_call("mm_nn", a, w, "nn"), (a, w)),
              lambda r, g: (_mm_call("mm_nt", g, r[1], "nt"), _mm_call("mm_tn", r[0], g, "tn")))


@jax.custom_vjp
def matmul_add(c, a, w):
    return _mm_call("mm_nn_add", a, w, "nn", c)


matmul_add.defvjp(lambda c, a, w: (_mm_call("mm_nn_add", a, w, "nn", c), (a, w)),
                  lambda r, g: (g, _mm_call("mm_nt", g, r[1], "nt"), _mm_call("mm_tn", r[0], g, "tn")))

_PROJ_GROUPS = [(0, 512), (512, 512), (1024, 512), (1536, 1536), (3072, 512), (3584, 512), (4096, 512), (4608, 512),
                (5120, 3072), (8192, 128)]


def _proj_impl(h, w):
    return tuple(_mm_call("proj_nn", h, w[:, s:s + n], "nn") for s, n in _PROJ_GROUPS)


proj = jax.custom_vjp(_proj_impl)


def _proj_bwd(res, gs):
    h, w = res
    dh = None
    dws = []
    for (s, n), g in zip(_PROJ_GROUPS, gs):
        dh = _mm_call("proj_nt", g, w[:, s:s + n], "nt", dh)
        dws.append(_mm_call("proj_tn", h, g, "tn"))
    return dh, jnp.concatenate(dws, axis=1)


proj.defvjp(lambda h, w: (_proj_impl(h, w), (h, w)), _proj_bwd)


def _cumsum_call(x, reverse):
    s, w = x.shape
    tm = _pick(s, 256, SUBLANES)
    nb = s // tm

    def body(x_ref, o_ref, carry):
        @pl.when(pl.program_id(0) == 0)
        def _():
            carry[...] = jnp.zeros_like(carry)

        blk = x_ref[...]
        r, c = _iota2(tm, tm, 0), _iota2(tm, tm, 1)
        tri = jnp.where((r <= c) if reverse else (r >= c), 1.0, 0.0).astype(F32)
        o_ref[...] = _dotf(tri, blk, NN) + carry[...]
        carry[...] += jnp.sum(blk, axis=0, keepdims=True)

    idx = (lambda i: (nb - 1 - i, 0)) if reverse else (lambda i: (i, 0))
    return _pcall(body, name="cumsum_rev" if reverse else "cumsum", out_shape=jax.ShapeDtypeStruct((s, w), F32),
                  grid=(nb,), in_specs=[pl.BlockSpec((tm, w), idx)], out_specs=pl.BlockSpec((tm, w), idx),
                  scratch_shapes=[pltpu.VMEM((1, w), F32)])(x)


@jax.custom_vjp
def seq_cumsum(x):
    return _cumsum_call(x, False)


seq_cumsum.defvjp(lambda x: (_cumsum_call(x, False), None), lambda _, g: (_cumsum_call(g, True),))


HALO = SUBLANES


class ConvOp:
    def __init__(self, name, width, post, c_pieces, out_widths, has_bias, tm):
        self.name, self.width, self.post, self.c_pieces = name, width, post, c_pieces
        self.out_widths, self.has_bias, self.tm = out_widths, has_bias, tm
        op = jax.custom_vjp(self._fwd_call)
        op.defvjp(lambda *a: (self._fwd_call(*a), a), lambda res, g: self._bwd_call(res, g))
        self.op = op

    def __call__(self, *args):
        return self.op(*args)

    def _conv(self, i, x_ref, prev_ref, w_ref, b_ref, buf):
        tm = x_ref.shape[0]
        buf[0:HALO, :] = jnp.where(i > 0, prev_ref[...], 0.0)
        buf[HALO:HALO + tm, :] = x_ref[...]
        taps = [buf[pl.ds(HALO - (self.width - 1) + j, tm), :] for j in range(self.width)]
        c = taps[0] * w_ref[0:1, :]
        for j in range(1, self.width):
            c = c + taps[j] * w_ref[j:j + 1, :]
        if self.has_bias:
            c = c + b_ref[...]
        return c, taps

    def _fwd_call(self, x, w, *bias):
        s, ch = x.shape
        tm = _pick(s, self.tm, SUBLANES)
        r8 = tm // HALO
        has_bias, post, c_pieces = self.has_bias, self.post, self.c_pieces

        def body(*refs):
            x_ref, prev_ref, w_ref = refs[:3]
            b_ref = refs[3] if has_bias else None
            outs, buf = refs[3 + has_bias:-1], refs[-1]
            c, _ = self._conv(pl.program_id(0), x_ref, prev_ref, w_ref, b_ref, buf)
            ys = post(*[c[:, o:o + n] for o, n in c_pieces])
            for r, y in zip(outs, ys):
                r[...] = y

        outs = _pcall(
            body, name=self.name + "_fwd", out_shape=[jax.ShapeDtypeStruct((s, n), F32) for n in self.out_widths],
            grid=(s // tm,),
            in_specs=[pl.BlockSpec((tm, ch), lambda i: (i, 0)),
                      pl.BlockSpec((HALO, ch), lambda i: (jnp.maximum(i * r8 - 1, 0), 0)),
                      pl.BlockSpec(w.shape, lambda i: (0, 0))]
            + ([pl.BlockSpec((1, ch), lambda i: (0, 0))] if has_bias else []),
            out_specs=[pl.BlockSpec((tm, n), lambda i: (i, 0)) for n in self.out_widths],
            scratch_shapes=[pltpu.VMEM((tm + HALO, ch), F32)],
        )(x, x, w, *bias)
        return tuple(outs) if len(outs) > 1 else outs[0]

    def _bwd_call(self, res, g):
        x, w = res[0], res[1]
        bias = res[2:]
        gs = tuple(g) if len(self.out_widths) > 1 else (g,)
        s, ch = x.shape
        tm = _pick(s, self.tm, SUBLANES)
        r8 = tm // HALO
        nb = s // tm
        has_bias, post, c_pieces, width = self.has_bias, self.post, self.c_pieces, self.width
        ng = len(gs)

        def body1(*refs):
            x_ref, prev_ref, w_ref = refs[:3]
            b_ref = refs[3] if has_bias else None
            k = 3 + has_bias
            g_refs = refs[k:k + ng]
            dc_ref, dw_ref = refs[k + ng], refs[k + ng + 1]
            db_ref = refs[k + ng + 2] if has_bias else None
            buf = refs[-1]
            i = pl.program_id(0)
            c, taps = self._conv(i, x_ref, prev_ref, w_ref, b_ref, buf)
            _, vjp = jax.vjp(lambda *a: tuple(post(*a)), *[c[:, o:o + n] for o, n in c_pieces])
            dcs = vjp(tuple(r[...] for r in g_refs))
            for (o, n), d in zip(c_pieces, dcs):
                dc_ref[:, o:o + n] = d

            @pl.when(i == 0)
            def _():
                dw_ref[...] = jnp.zeros_like(dw_ref)
                if has_bias:
                    db_ref[...] = jnp.zeros_like(db_ref)

            dc = dc_ref[...]
            for j in range(width):
                dw_ref[j:j + 1, :] += jnp.sum(dc * taps[j], axis=0, keepdims=True)
            if has_bias:
                db_ref[...] += jnp.sum(dc, axis=0, keepdims=True)

        outs1 = _pcall(
            body1, name=self.name + "_bwd_act",
            out_shape=[jax.ShapeDtypeStruct((s, ch), F32), jax.ShapeDtypeStruct(w.shape, F32)]
            + ([jax.ShapeDtypeStruct((1, ch), F32)] if has_bias else []),
            grid=(nb,),
            in_specs=[pl.BlockSpec((tm, ch), lambda i: (i, 0)),
                      pl.BlockSpec((HALO, ch), lambda i: (jnp.maximum(i * r8 - 1, 0), 0)),
                      pl.BlockSpec(w.shape, lambda i: (0, 0))]
            + ([pl.BlockSpec((1, ch), lambda i: (0, 0))] if has_bias else [])
            + [pl.BlockSpec((tm, n), lambda i: (i, 0)) for n in self.out_widths],
            out_specs=[pl.BlockSpec((tm, ch), lambda i: (i, 0)), pl.BlockSpec(w.shape, lambda i: (0, 0))]
            + ([pl.BlockSpec((1, ch), lambda i: (0, 0))] if has_bias else []),
            scratch_shapes=[pltpu.VMEM((tm + HALO, ch), F32)],
        )(x, x, w, *bias, *gs)
        dc, dw = outs1[0], outs1[1]

        def body2(dc_ref, next_ref, w_ref, dx_ref, buf):
            i = pl.program_id(0)
            buf[0:tm, :] = dc_ref[...]
            buf[tm:tm + HALO, :] = jnp.where(i < nb - 1, next_ref[...], 0.0)
            dx = buf[pl.ds(width - 1, tm), :] * w_ref[0:1, :]
            for j in range(1, width):
                dx = dx + buf[pl.ds(width - 1 - j, tm), :] * w_ref[j:j + 1, :]
            dx_ref[...] = dx

        dx = _pcall(
            body2, name=self.name + "_bwd_in", out_shape=jax.ShapeDtypeStruct((s, ch), F32), grid=(nb,),
            in_specs=[pl.BlockSpec((tm, ch), lambda i: (i, 0)),
                      pl.BlockSpec((HALO, ch), lambda i: (jnp.minimum((i + 1) * r8, s // HALO - 1), 0)),
                      pl.BlockSpec(w.shape, lambda i: (0, 0))],
            out_specs=pl.BlockSpec((tm, ch), lambda i: (i, 0)),
            scratch_shapes=[pltpu.VMEM((tm + HALO, ch), F32)],
        )(dc, dc, w)
        return (dx, dw) + ((outs1[2],) if has_bias else ())


def _make_convops():
    return dict(
        gdn=ConvOp("gdnconv", 4, lambda q, k, v: (_silu(q), _silu(k), _silu(v)), _split(3 * HW, 3), [HW] * 3,
                   False, 256),
        ffn=ConvOp("ffnconv", 3, lambda a, b: (_silu(a) * b,), _split(2 * D_FF, 2), [D_FF], True, 128),
    )


ATT_Q = 512
ATT_K = 256
SCALE = HEAD_DIM ** -0.5


def _att_tiles(s):
    tk = _pick(s, ATT_K, LANES)
    tq = _pick(s, ATT_Q, tk)
    return tq, tk


def _att_specs(s, tq, tk):
    qspec = pl.BlockSpec((tq, HEAD_DIM), lambda h, i: (i, h))
    kspec = pl.BlockSpec((s, HEAD_DIM), lambda h, i: (0, h))
    colspec = pl.BlockSpec((None, tq, 1), lambda h, i: (h, i, 0))
    rowspec = pl.BlockSpec((None, s // tk, 1, tk), lambda h, i: (h, 0, 0, 0))
    return qspec, kspec, colspec, rowspec


def _krows(kb, tk):
    return pl.ds(pl.multiple_of(kb * tk, tk), tk)


def _stage_bf16(i, pairs):
    @pl.when(i == 0)
    def _():
        for src, dst in pairs:
            dst[...] = src[...].astype(BF16)


ATT_STRIP = 32


def _strips(tq):
    return [slice(r, r + ATT_STRIP) for r in range(0, tq, ATT_STRIP)]


def _visible(i, kb, rs, tq, tk, strict):
    rows = i * tq + rs.start + _iota2(ATT_STRIP, tk, 0)
    cols = kb * tk + _iota2(ATT_STRIP, tk, 1)
    return (cols < rows) if strict else (cols <= rows)


def _blocks(i, ratio, blk, reverse=False):
    def full(n, carry):
        blk(i * ratio - 1 - n if reverse else n, False)
        return carry

    if reverse:
        for j in reversed(range(ratio)):
            blk(i * ratio + j, True)
    lax.fori_loop(0, i * ratio, full, 0)
    if not reverse:
        for j in range(ratio):
            blk(i * ratio + j, True)


def _vm(shape, dtype):
    return pltpu.VMEM(shape, dtype)


def _fox_fwd_call(q, k, v, ccol, crow):
    s = q.shape[0]
    tq, tk = _att_tiles(s)
    qspec, kspec, colspec, rowspec = _att_specs(s, tq, tk)

    def body(q_ref, k_ref, v_ref, cq_ref, ck_ref, o_ref, lse_ref, k16, v16, q16, s_scr, p16, m_scr, l_scr, a_scr, acc):
        i = pl.program_id(1)
        _stage_bf16(i, [(k_ref, k16), (v_ref, v16)])
        q16[...] = q_ref[...].astype(BF16)
        m_scr[...] = jnp.full_like(m_scr, NEG)
        l_scr[...] = jnp.zeros_like(l_scr)
        acc[...] = jnp.zeros_like(acc)

        def blk(kb, masked):
            rows = _krows(kb, tk)
            s_scr[...] = _dot(q16[...], k16[rows, :], NT)
            ck = ck_ref[kb]
            for rs in _strips(tq):
                sc = s_scr[rs, :] * SCALE + (cq_ref[rs, :] - ck)
                if masked:
                    sc = jnp.where(_visible(i, kb, rs, tq, tk, False), sc, NEG)
                m_old = m_scr[rs, :]
                m_new = jnp.maximum(m_old, jnp.max(sc, axis=-1, keepdims=True))
                alpha = jnp.exp(m_old - m_new)
                p = jnp.exp(sc - m_new)
                l_scr[rs, :] = alpha * l_scr[rs, :] + jnp.sum(p, axis=-1, keepdims=True)
                m_scr[rs, :] = m_new
                a_scr[rs, :] = alpha
                p16[rs, :] = p.astype(BF16)
            acc[...] = a_scr[...] * acc[...] + _dot(p16[...], v16[rows, :], NN)

        _blocks(i, tq // tk, blk)
        o_ref[...] = acc[...] / l_scr[...]
        lse_ref[...] = m_scr[...] + jnp.log(l_scr[...])

    return _pcall(body, name="fox_fwd",
                  out_shape=[jax.ShapeDtypeStruct((s, HW), F32), jax.ShapeDtypeStruct((HEADS, s, 1), F32)],
                  grid=(HEADS, s // tq), in_specs=[qspec, kspec, kspec, colspec, rowspec],
                  out_specs=[qspec, colspec],
                  scratch_shapes=[_vm((s, HEAD_DIM), BF16), _vm((s, HEAD_DIM), BF16), _vm((tq, HEAD_DIM), BF16),
                                  _vm((tq, tk), F32), _vm((tq, tk), BF16), _vm((tq, 1), F32), _vm((tq, 1), F32),
                                  _vm((tq, 1), F32), _vm((tq, HEAD_DIM), F32)])(q, k, v, ccol, crow)


def _fox_bwd_call(q, k, v, ccol, crow, o, lse, do):
    s = q.shape[0]
    tq, tk = _att_tiles(s)
    ratio = tq // tk
    qspec, kspec, colspec, rowspec = _att_specs(s, tq, tk)

    def body(q_ref, k_ref, v_ref, cq_ref, ck_ref, o_ref, lse_ref, do_ref, dq_ref, dk_ref, dv_ref, dcq_ref, dck_ref,
             k16, v16, q16, do16, s_scr, dp_scr, p16, ds16, dl_scr):
        i = pl.program_id(1)
        _stage_bf16(i, [(k_ref, k16), (v_ref, v16)])

        @pl.when(i == 0)
        def _():
            dk_ref[...] = jnp.zeros_like(dk_ref)
            dv_ref[...] = jnp.zeros_like(dv_ref)
            dck_ref[...] = jnp.zeros_like(dck_ref)

        q16[...] = q_ref[...].astype(BF16)
        do16[...] = do_ref[...].astype(BF16)
        dl_scr[...] = jnp.sum(do_ref[...] * o_ref[...], axis=-1, keepdims=True)
        dq_ref[...] = jnp.zeros_like(dq_ref)
        dcq_ref[...] = jnp.zeros_like(dcq_ref)

        def blk(kb, masked):
            rows = _krows(kb, tk)
            s_scr[...] = _dot(q16[...], k16[rows, :], NT)
            dp_scr[...] = _dot(do16[...], v16[rows, :], NT)
            ck = ck_ref[kb]
            colsum = jnp.zeros((1, tk), F32)
            for rs in _strips(tq):
                sc = s_scr[rs, :] * SCALE + (cq_ref[rs, :] - ck)
                p = jnp.exp(sc - lse_ref[rs, :])
                if masked:
                    p = jnp.where(_visible(i, kb, rs, tq, tk, False), p, 0.0)
                ds = p * (dp_scr[rs, :] - dl_scr[rs, :])
                p16[rs, :] = p.astype(BF16)
                ds16[rs, :] = ds.astype(BF16)
                dcq_ref[rs, :] += jnp.sum(ds, axis=-1, keepdims=True)
                colsum = colsum + jnp.sum(ds, axis=0, keepdims=True)
            dck_ref[kb] += -colsum
            dv_ref[rows, :] += _dot(p16[...], do16[...], TN)
            dk_ref[rows, :] += _dot(ds16[...], q16[...], TN) * SCALE
            dq_ref[...] += _dot(ds16[...], k16[rows, :], NN) * SCALE

        _blocks(i, tq // tk, blk)

    return _pcall(body, name="fox_bwd",
                  out_shape=[jax.ShapeDtypeStruct((s, HW), F32)] * 3
                  + [jax.ShapeDtypeStruct((HEADS, s, 1), F32), jax.ShapeDtypeStruct((HEADS, s // tk, 1, tk), F32)],
                  grid=(HEADS, s // tq),
                  in_specs=[qspec, kspec, kspec, colspec, rowspec, qspec, colspec, qspec],
                  out_specs=[qspec, kspec, kspec, colspec, rowspec],
                  scratch_shapes=[_vm((s, HEAD_DIM), BF16), _vm((s, HEAD_DIM), BF16), _vm((tq, HEAD_DIM), BF16),
                                  _vm((tq, HEAD_DIM), BF16), _vm((tq, tk), F32), _vm((tq, tk), F32),
                                  _vm((tq, tk), BF16), _vm((tq, tk), BF16), _vm((tq, 1), F32)],
                  )(q, k, v, ccol, crow, o, lse, do)


@jax.custom_vjp
def fox_attention(q, k, v, ccol, crow):
    return _fox_fwd_call(q, k, v, ccol, crow)[0]


def _fox_vjp_fwd(q, k, v, ccol, crow):
    o, lse = _fox_fwd_call(q, k, v, ccol, crow)
    return o, (q, k, v, ccol, crow, o, lse)


fox_attention.defvjp(_fox_vjp_fwd, lambda res, g: tuple(_fox_bwd_call(*res, g)))


def _sb_fwd_call(q, k, v):
    s = q.shape[0]
    tq, tk = _att_tiles(s)
    ratio = tq // tk
    qspec, kspec, colspec, _ = _att_specs(s, tq, tk)

    def body(q_ref, k_ref, v_ref, o_ref, tot_ref, k16, v16, q16, s_scr, w_scr, lk16, a16, run, acc):
        i = pl.program_id(1)
        _stage_bf16(i, [(k_ref, k16), (v_ref, v16)])
        q16[...] = q_ref[...].astype(BF16)
        run[...] = jnp.zeros_like(run)
        acc[...] = jnp.zeros_like(acc)
        suffix = jnp.where(_iota2(tk, tk, 0) >= _iota2(tk, tk, 1), 1.0, 0.0).astype(BF16)

        def blk(kb, masked):
            rows = _krows(kb, tk)
            s_scr[...] = _dot(q16[...], k16[rows, :], NT)
            for rs in _strips(tq):
                lk = -_softplus(s_scr[rs, :] * SCALE)
                if masked:
                    lk = jnp.where(_visible(i, kb, rs, tq, tk, True), lk, 0.0)
                lk16[rs, :] = lk.astype(BF16)
            w_scr[...] = _dot(lk16[...], suffix, NN)
            for rs in _strips(tq):
                a = jnp.exp(s_scr[rs, :] * SCALE + w_scr[rs, :] + run[rs, :])
                if masked:
                    a = jnp.where(_visible(i, kb, rs, tq, tk, True), a, 0.0)
                a16[rs, :] = a.astype(BF16)
                run[rs, :] += w_scr[rs, 0:1]
            acc[...] += _dot(a16[...], v16[rows, :], NN)

        _blocks(i, ratio, blk, reverse=True)
        o_ref[...] = acc[...]
        tot_ref[...] = run[...]

    return _pcall(body, name="sb_fwd",
                  out_shape=[jax.ShapeDtypeStruct((s, HW), F32), jax.ShapeDtypeStruct((HEADS, s, 1), F32)],
                  grid=(HEADS, s // tq), in_specs=[qspec, kspec, kspec], out_specs=[qspec, colspec],
                  scratch_shapes=[_vm((s, HEAD_DIM), BF16), _vm((s, HEAD_DIM), BF16), _vm((tq, HEAD_DIM), BF16),
                                  _vm((tq, tk), F32), _vm((tq, tk), F32), _vm((tq, tk), BF16), _vm((tq, tk), BF16),
                                  _vm((tq, 1), F32), _vm((tq, HEAD_DIM), F32)])(q, k, v)


def _sb_bwd_call(q, k, v, tot, do):
    s = q.shape[0]
    tq, tk = _att_tiles(s)
    ratio = tq // tk
    qspec, kspec, colspec, _ = _att_specs(s, tq, tk)

    def body(q_ref, k_ref, v_ref, tot_ref, do_ref, dq_ref, dk_ref, dv_ref, k16, v16, q16, do16, s_scr, e_scr, w_scr,
             lz16, a16, e16, left, esum):
        i = pl.program_id(1)
        _stage_bf16(i, [(k_ref, k16), (v_ref, v16)])

        @pl.when(i == 0)
        def _():
            dk_ref[...] = jnp.zeros_like(dk_ref)
            dv_ref[...] = jnp.zeros_like(dv_ref)

        q16[...] = q_ref[...].astype(BF16)
        do16[...] = do_ref[...].astype(BF16)
        left[...] = jnp.zeros_like(left)
        esum[...] = jnp.zeros_like(esum)
        dq_ref[...] = jnp.zeros_like(dq_ref)
        prefix = jnp.where(_iota2(tk, tk, 0) <= _iota2(tk, tk, 1), 1.0, 0.0).astype(BF16)

        def blk(kb, masked):
            rows = _krows(kb, tk)
            s_scr[...] = _dot(q16[...], k16[rows, :], NT)
            e_scr[...] = _dot(do16[...], v16[rows, :], NT)
            for rs in _strips(tq):
                lk = -_softplus(s_scr[rs, :] * SCALE)
                if masked:
                    lk = jnp.where(_visible(i, kb, rs, tq, tk, True), lk, 0.0)
                lz16[rs, :] = lk.astype(BF16)
            w_scr[...] = _dot(lz16[...], prefix, NN)
            for rs in _strips(tq):
                rc = (tot_ref[rs, :] - left[rs, :]) - (w_scr[rs, :] - lz16[rs, :].astype(F32))
                a = jnp.exp(s_scr[rs, :] * SCALE + rc)
                if masked:
                    a = jnp.where(_visible(i, kb, rs, tq, tk, True), a, 0.0)
                e = a * e_scr[rs, :]
                a16[rs, :] = a.astype(BF16)
                e16[rs, :] = e.astype(BF16)
                e_scr[rs, :] = e
                left[rs, :] += w_scr[rs, tk - 1:tk]
            w_scr[...] = _dot(e16[...], prefix, NN)
            for rs in _strips(tq):
                dz = e_scr[rs, :] - _sigmoid(s_scr[rs, :] * SCALE) * (esum[rs, :] + w_scr[rs, :])
                if masked:
                    dz = jnp.where(_visible(i, kb, rs, tq, tk, True), dz, 0.0)
                lz16[rs, :] = dz.astype(BF16)
                esum[rs, :] += w_scr[rs, tk - 1:tk]
            dv_ref[rows, :] += _dot(a16[...], do16[...], TN)
            dk_ref[rows, :] += _dot(lz16[...], q16[...], TN) * SCALE
            dq_ref[...] += _dot(lz16[...], k16[rows, :], NN) * SCALE

        _blocks(i, ratio, blk)

    return _pcall(body, name="sb_bwd", out_shape=[jax.ShapeDtypeStruct((s, HW), F32)] * 3, grid=(HEADS, s // tq),
                  in_specs=[qspec, kspec, kspec, colspec, qspec], out_specs=[qspec, kspec, kspec],
                  scratch_shapes=[_vm((s, HEAD_DIM), BF16), _vm((s, HEAD_DIM), BF16), _vm((tq, HEAD_DIM), BF16),
                                  _vm((tq, HEAD_DIM), BF16), _vm((tq, tk), F32), _vm((tq, tk), F32),
                                  _vm((tq, tk), F32), _vm((tq, tk), BF16), _vm((tq, tk), BF16), _vm((tq, tk), BF16),
                                  _vm((tq, 1), F32), _vm((tq, 1), F32)])(q, k, v, tot, do)


@jax.custom_vjp
def sb_attention(q, k, v):
    return _sb_fwd_call(q, k, v)[0]


def _sb_vjp_fwd(q, k, v):
    o, tot = _sb_fwd_call(q, k, v)
    return o, (q, k, v, tot)


sb_attention.defvjp(_sb_vjp_fwd, lambda res, g: tuple(_sb_bwd_call(*res, g)))


def _mem_specs(s, nk, t):
    return (pl.BlockSpec((t, HEAD_DIM), lambda h, i: (i, h)), pl.BlockSpec((nk, HEAD_DIM), lambda h, i: (0, h)))


def _mem_probs(qb, kk):
    sc = _dot(qb, kk, NT) * SCALE
    p = jnp.exp(sc - jnp.max(sc, axis=-1, keepdims=True))
    return p / jnp.sum(p, axis=-1, keepdims=True)


def _mem_fwd_call(q, k, v):
    s, nk = q.shape[0], k.shape[0]
    t = _pick(s, 512, SUBLANES)
    qspec, kspec = _mem_specs(s, nk, t)

    def body(q_ref, k_ref, v_ref, o_ref):
        o_ref[...] = _dot(_mem_probs(q_ref[...].astype(BF16), k_ref[...]), v_ref[...], NN)

    return _pcall(body, name="mem_fwd", out_shape=jax.ShapeDtypeStruct((s, HW), F32), grid=(HEADS, s // t),
                  in_specs=[qspec, kspec, kspec], out_specs=qspec)(q, k, v)


def _mem_bwd_call(q, k, v, do):
    s, nk = q.shape[0], k.shape[0]
    t = _pick(s, 512, SUBLANES)
    qspec, kspec = _mem_specs(s, nk, t)

    def body(q_ref, k_ref, v_ref, do_ref, dq_ref, dk_ref, dv_ref):
        @pl.when(pl.program_id(1) == 0)
        def _():
            dk_ref[...] = jnp.zeros_like(dk_ref)
            dv_ref[...] = jnp.zeros_like(dv_ref)

        qb = q_ref[...].astype(BF16)
        dob = do_ref[...].astype(BF16)
        p = _mem_probs(qb, k_ref[...])
        dv_ref[...] += _dot(p, dob, TN)
        dp = _dot(dob, v_ref[...], NT)
        ds = p * (dp - jnp.sum(p * dp, axis=-1, keepdims=True))
        dq_ref[...] = _dot(ds, k_ref[...], NN) * SCALE
        dk_ref[...] += _dot(ds, qb, TN) * SCALE

    return _pcall(body, name="mem_bwd",
                  out_shape=[jax.ShapeDtypeStruct((s, HW), F32)] + [jax.ShapeDtypeStruct((nk, HW), F32)] * 2,
                  grid=(HEADS, s // t), in_specs=[qspec, kspec, kspec, qspec],
                  out_specs=[qspec, kspec, kspec])(q, k, v, do)


@jax.custom_vjp
def mem_attention(q, k, v):
    return _mem_fwd_call(q, k, v)


mem_attention.defvjp(lambda q, k, v: (_mem_fwd_call(q, k, v), (q, k, v)),
                     lambda res, g: tuple(_mem_bwd_call(*res, g)))


def _unit_lower_inverse(nm):
    eye = jnp.where(_iota2(CHUNK, CHUNK, 0) == _iota2(CHUNK, CHUNK, 1), 1.0, 0.0).astype(F32)
    p = eye - nm
    m = nm
    for _ in range(5):
        m = _dotf(m, m, NN)
        p = _dotf(p, eye + m, NN)
    return p


@jax.custom_vjp
def _solve2(nm, r1, r2):
    inv = _unit_lower_inverse(nm)
    return _dotf(inv, r1, NN), _dotf(inv, r2, NN)


def _solve2_fwd(nm, r1, r2):
    inv = _unit_lower_inverse(nm)
    u, w = _dotf(inv, r1, NN), _dotf(inv, r2, NN)
    return (u, w), (inv, u, w)


def _solve2_bwd(res, g):
    inv, u, w = res
    d1, d2 = _dotf(inv, g[0], TN), _dotf(inv, g[1], TN)
    return -(_dotf(d1, u, NT) + _dotf(d2, w, NT)), d1, d2


_solve2.defvjp(_solve2_fwd, _solve2_bwd)


def _gdn_chunk(q, k, v, gcc, gcr, b, gl, st):
    qn = q * lax.rsqrt(jnp.sum(q * q, axis=-1, keepdims=True) + EPS) * SCALE
    kn = k * lax.rsqrt(jnp.sum(k * k, axis=-1, keepdims=True) + EPS)
    r, c = _iota2(CHUNK, CHUNK, 0), _iota2(CHUNK, CHUNK, 1)
    decay = jnp.exp(jnp.where(r >= c, gcc - gcr, NEG))
    nm = jnp.where(r > c, b * mm_nt(kn, kn) * decay, 0.0)
    eg = jnp.exp(gcc)
    u, w = _solve2(nm, v * b, kn * (b * eg))
    attn = mm_nt(qn, kn) * decay
    v_new = u - mm(w, st)
    o = mm(qn * eg, st) + mm(attn, v_new)
    st_new = st * jnp.exp(gl) + mm_tn(kn * jnp.exp(gl - gcc), v_new)
    return o, st_new


GDN_ROWS = 512


def _gdn_specs(s, tg, rev):
    nb = s // tg
    cpb = tg // CHUNK
    j_of = (lambda j: nb - 1 - j) if rev else (lambda j: j)
    qspec = pl.BlockSpec((tg, HW), lambda j: (j_of(j), 0))
    colspec = pl.BlockSpec((HEADS, tg, 1), lambda j: (0, j_of(j), 0))
    rowspec = pl.BlockSpec((HEADS, cpb, 1, CHUNK), lambda j: (0, j_of(j), 0, 0))
    onespec = pl.BlockSpec((HEADS, cpb, 1, 1), lambda j: (0, j_of(j), 0, 0))
    stspec = pl.BlockSpec((HEADS, cpb, HEAD_DIM, HEAD_DIM), lambda j: (0, j_of(j), 0, 0))
    return qspec, colspec, rowspec, onespec, stspec


def _head_cols(h):
    return slice(h * HEAD_DIM, (h + 1) * HEAD_DIM)


def _gdn_fwd_call(q, k, v, gcc, gcr, bc, gl):
    s = q.shape[0]
    tg = _pick(s, GDN_ROWS, CHUNK)
    cpb = tg // CHUNK
    qspec, colspec, rowspec, onespec, stspec = _gdn_specs(s, tg, False)

    def body(q_ref, k_ref, v_ref, gcc_ref, gcr_ref, b_ref, gl_ref, o_ref, st_ref, st):
        @pl.when(pl.program_id(0) == 0)
        def _():
            st[...] = jnp.zeros_like(st)

        def chunk(ci, _):
            rows = pl.ds(pl.multiple_of(ci * CHUNK, CHUNK), CHUNK)
            args = [(q_ref[rows, _head_cols(h)], k_ref[rows, _head_cols(h)], v_ref[rows, _head_cols(h)],
                     gcc_ref[h, rows, :], gcr_ref[h, ci], b_ref[h, rows, :], gl_ref[h, ci], st[h])
                    for h in range(HEADS)]
            outs = [_gdn_chunk(*a) for a in args]
            for h in range(HEADS):
                st_ref[h, ci] = args[h][-1]
                o_ref[rows, _head_cols(h)] = outs[h][0]
                st[h] = outs[h][1]
            return 0

        lax.fori_loop(0, cpb, chunk, 0)

    return _pcall(body, name="gdn_fwd",
                  out_shape=[jax.ShapeDtypeStruct((s, HW), F32),
                             jax.ShapeDtypeStruct((HEADS, s // CHUNK, HEAD_DIM, HEAD_DIM), F32)],
                  grid=(s // tg,), in_specs=[qspec, qspec, qspec, colspec, rowspec, colspec, onespec],
                  out_specs=[qspec, stspec], scratch_shapes=[pltpu.VMEM((HEADS, HEAD_DIM, HEAD_DIM), F32)],
                  )(q, k, v, gcc, gcr, bc, gl)


def _gdn_bwd_call(q, k, v, gcc, gcr, bc, gl, states, do):
    s = q.shape[0]
    tg = _pick(s, GDN_ROWS, CHUNK)
    cpb = tg // CHUNK
    qspec, colspec, rowspec, onespec, stspec = _gdn_specs(s, tg, True)

    def body(q_ref, k_ref, v_ref, gcc_ref, gcr_ref, b_ref, gl_ref, st_ref, do_ref,
             dq_ref, dk_ref, dv_ref, dgcc_ref, dgcr_ref, db_ref, dgl_ref, dst):
        @pl.when(pl.program_id(0) == 0)
        def _():
            dst[...] = jnp.zeros_like(dst)

        def chunk(n, _):
            ci = cpb - 1 - n
            rows = pl.ds(pl.multiple_of(ci * CHUNK, CHUNK), CHUNK)
            args = [(q_ref[rows, _head_cols(h)], k_ref[rows, _head_cols(h)], v_ref[rows, _head_cols(h)],
                     gcc_ref[h, rows, :], gcr_ref[h, ci], b_ref[h, rows, :], gl_ref[h, ci], st_ref[h, ci])
                    for h in range(HEADS)]
            cots = [(do_ref[rows, _head_cols(h)], dst[h]) for h in range(HEADS)]
            grads = [jax.vjp(_gdn_chunk, *args[h])[1](cots[h]) for h in range(HEADS)]
            for h in range(HEADS):
                cols = _head_cols(h)
                dq, dk, dv, dgcc, dgcr, db, dgl, ds_in = grads[h]
                dq_ref[rows, cols] = dq
                dk_ref[rows, cols] = dk
                dv_ref[rows, cols] = dv
                dgcc_ref[h, rows, :] = dgcc
                dgcr_ref[h, ci] = dgcr
                db_ref[h, rows, :] = db
                dgl_ref[h, ci] = dgl
                dst[h] = ds_in
            return 0

        lax.fori_loop(0, cpb, chunk, 0)

    n = s // CHUNK
    return _pcall(body, name="gdn_bwd",
                  out_shape=[jax.ShapeDtypeStruct((s, HW), F32)] * 3
                  + [jax.ShapeDtypeStruct((HEADS, s, 1), F32), jax.ShapeDtypeStruct((HEADS, n, 1, CHUNK), F32),
                     jax.ShapeDtypeStruct((HEADS, s, 1), F32), jax.ShapeDtypeStruct((HEADS, n, 1, 1), F32)],
                  grid=(s // tg,),
                  in_specs=[qspec, qspec, qspec, colspec, rowspec, colspec, onespec, stspec, qspec],
                  out_specs=[qspec, qspec, qspec, colspec, rowspec, colspec, onespec],
                  scratch_shapes=[pltpu.VMEM((HEADS, HEAD_DIM, HEAD_DIM), F32)],
                  )(q, k, v, gcc, gcr, bc, gl, states, do)


@jax.custom_vjp
def gated_delta(q, k, v, gcc, gcr, bc, gl):
    return _gdn_fwd_call(q, k, v, gcc, gcr, bc, gl)[0]


def _gdn_vjp_fwd(q, k, v, gcc, gcr, bc, gl):
    o, states = _gdn_fwd_call(q, k, v, gcc, gcr, bc, gl)
    return o, (q, k, v, gcc, gcr, bc, gl, states)


gated_delta.defvjp(_gdn_vjp_fwd, lambda res, g: tuple(_gdn_bwd_call(*res, g)))


def _loss_call(y, target):
    s, d = y.shape
    tm = _pick(s, 512, SUBLANES)

    def body(y_ref, t_ref, dy_ref, loss_ref):
        @pl.when(pl.program_id(0) == 0)
        def _():
            loss_ref[...] = jnp.zeros_like(loss_ref)

        err = y_ref[...] - t_ref[...]
        dy_ref[...] = err * (1.0 / d)
        loss_ref[...] += 0.5 * jnp.sum(jnp.mean(err * err, axis=-1, keepdims=True), axis=0, keepdims=True)

    dy, part = _pcall(body, name="loss_head",
                      out_shape=[jax.ShapeDtypeStruct((s, d), F32), jax.ShapeDtypeStruct((1, 1), F32)],
                      grid=(s // tm,), in_specs=[pl.BlockSpec((tm, d), lambda i: (i, 0))] * 2,
                      out_specs=[pl.BlockSpec((tm, d), lambda i: (i, 0)), pl.BlockSpec((1, 1), lambda i: (0, 0))],
                      )(y, target)
    return part[0, 0], dy


def _cols_and_rows(a, lane0, t):
    s = a.shape[0]
    at = a[:, lane0:lane0 + HEADS].T
    return at, at[:, :, None], at.reshape(HEADS, s // t, 1, t)


def _pad_lanes(v, lane0):
    return jnp.pad(v, (lane0, LANES - lane0 - v.shape[0])).reshape(1, LANES)


def _layer(x, mem, w, ops, convs):
    s = x.shape[0]
    row = lambda v: v.reshape(1, -1)
    h = ops["rms"](x, row(w["norm_mix"]))
    fq, fk, fv, gqkv, gz, sq, sk, sv, gt, gm = proj(h, w["w_in"])

    logf, beta, gc = ops["small"](gm, _pad_lanes(w["fox_fbias"], LANE_FF), _pad_lanes(w["gdn_a_log"], LANE_GA),
                                  _pad_lanes(w["gdn_dt_bias"], LANE_GA))
    _, ccol, crow = _cols_and_rows(seq_cumsum(logf), LANE_FF, _att_tiles(s)[1])
    ya = fox_attention(ops["headnorm"](fq, row(w["fox_qnorm"])), ops["headnorm"](fk, row(w["fox_knorm"])), fv,
                       ccol, crow)
    cq, ck, cv = convs["gdn"](gqkv, w["gdn_conv"])
    gct, gcc, gcr = _cols_and_rows(gc, LANE_GA, CHUNK)
    gl = gct.reshape(HEADS, s // CHUNK, CHUNK)[:, :, CHUNK - 1].reshape(HEADS, s // CHUNK, 1, 1)
    bc = beta[:, LANE_GB:LANE_GB + HEADS].T[:, :, None]
    yb = ops["gdnpost"](gated_delta(cq, ck, cv, gcc, gcr, bc, gl), gz, row(w["gdn_onorm"]))
    yc = sb_attention(sq, sk, sv)
    gb = w["gate_bias"]
    mixed = ops["merge"](gt, matmul(ya, w["w_oa"]), matmul(yb, w["w_ob"]), matmul(yc, w["w_oc"]),
                         row(gb[:D_MODEL]), row(gb[D_MODEL:2 * D_MODEL]), row(gb[2 * D_MODEL:]))
    x = matmul_add(x, mixed, w["w_out"])
    mq = ops["headnorm"](matmul(ops["rms"](x, row(w["norm_xq"])), w["w_mq"]), row(w["mq_norm"]))
    kv = matmul(ops["rms"](mem, row(w["norm_mem"])), w["w_mkv"])
    mk = ops["headnorm"](kv[:, :HW], row(w["mk_norm"]))
    x = matmul_add(x, mem_attention(mq, mk, kv[:, HW:]), w["w_mo"])
    u = matmul(ops["rms"](x, row(w["norm_ffn"])), w["w_up"])
    act = convs["ffn"](u, w["ffn_conv"], row(w["ffn_conv_b"]))
    return matmul_add(x, act, w["w_down"])


def _forward(x, mem, layers):
    ops, convs = _make_rowops(), _make_convops()
    for w in layers:
        x = _layer(x, mem, w, ops, convs)
    return x


ANY = pl.BlockSpec(memory_space=pl.ANY)


N_PEERS = N_DEV - 1


def _ccall(body, *, name, out_shape, n_arrays):
    return pl.pallas_call(body, name=name, out_shape=out_shape, in_specs=[ANY] * n_arrays,
                          out_specs=[ANY] * n_arrays,
                          scratch_shapes=[pltpu.SemaphoreType.DMA((N_PEERS * n_arrays,)),
                                          pltpu.SemaphoreType.DMA((N_PEERS * n_arrays,)),
                                          pltpu.SemaphoreType.DMA((n_arrays,))],
                          interpret=False)


def _all_gather(name, shards):
    n = len(shards)

    def body(*refs):
        x_refs, out_refs = refs[:n], refs[n:2 * n]
        send_sems, recv_sems, local_sems = refs[2 * n:]
        x, y, c = lax.axis_index("x"), lax.axis_index("y"), lax.axis_index("c")
        me, sibling = (x, y, c), (x, y, 1 - c)
        chips = [(1 - x, y), (x, 1 - y), (1 - x, 1 - y)]

        def slot(a, px, py, pc):
            return out_refs[a].at[4 * px + 2 * py + pc]

        def copy(a, k, block, to, src=None):
            return pltpu.make_async_remote_copy(
                src_ref=slot(a, *block) if src is None else src, dst_ref=slot(a, *block),
                send_sem=send_sems.at[N_PEERS * a + k], recv_sem=recv_sems.at[N_PEERS * a + k], device_id=to,
                device_id_type=MESH)

        mine = [pltpu.make_async_copy(x_refs[a], slot(a, *me), local_sems.at[a]) for a in range(n)]
        first = []
        for a in range(n):
            first.append(copy(a, 0, me, sibling, src=x_refs[a]))
            first += [copy(a, 1 + j, me, (*chip, c), src=x_refs[a]) for j, chip in enumerate(chips)]
        for cp in mine + first:
            cp.start()
        passed = []
        for j, chip in enumerate(chips):
            for a in range(n):
                copy(a, 1 + j, (*chip, c), me).wait_recv()
                passed.append(copy(a, 4 + j, (*chip, c), sibling))
                passed[-1].start()
        for a in range(n):
            copy(a, 0, sibling, me).wait_recv()
            for j, chip in enumerate(chips):
                copy(a, 4 + j, (*chip, 1 - c), me).wait_recv()
        for cp in first + passed:
            cp.wait_send()
        for cp in mine:
            cp.wait()

    return _ccall(body, name=name, out_shape=[jax.ShapeDtypeStruct((N_DEV,) + s.shape, s.dtype) for s in shards],
                  n_arrays=n)(*shards)


def _exchange(name, parts):
    n = len(parts)

    def body(*refs):
        p_refs, out_refs = refs[:n], refs[n:2 * n]
        send_sems, recv_sems, local_sems = refs[2 * n:]
        x, y, c = lax.axis_index("x"), lax.axis_index("y"), lax.axis_index("c")
        me = 4 * x + 2 * y + c

        def peer(k):
            return (x ^ ((k >> 2) & 1), y ^ ((k >> 1) & 1), c ^ (k & 1))

        def copy(a, k, receive):
            px, py, pc = peer(k)
            theirs = 4 * px + 2 * py + pc
            return pltpu.make_async_remote_copy(
                src_ref=out_refs[a].at[theirs] if receive else p_refs[a].at[theirs],
                dst_ref=out_refs[a].at[theirs if receive else me],
                send_sem=send_sems.at[N_PEERS * a + k - 1], recv_sem=recv_sems.at[N_PEERS * a + k - 1],
                device_id=(px, py, pc), device_id_type=MESH)

        mine = [pltpu.make_async_copy(p_refs[a].at[me], out_refs[a].at[me], local_sems.at[a]) for a in range(n)]
        sends = [copy(a, k, False) for k in range(1, N_DEV) for a in range(n)]
        for cp in mine + sends:
            cp.start()
        for k in range(1, N_DEV):
            for a in range(n):
                copy(a, k, True).wait_recv()
        for cp in sends:
            cp.wait_send()
        for cp in mine:
            cp.wait()

    return _ccall(body, name=name, out_shape=[jax.ShapeDtypeStruct(p.shape, p.dtype) for p in parts],
                  n_arrays=n)(*parts)


ADAM_SLOT_BYTES = 4 * 1024 * 1024


def _adam_call(name, w, slots, m, v):
    r, n = w.shape
    rows_unit = 2 * SUBLANES
    tr = _pick(r, max(rows_unit, ADAM_SLOT_BYTES // (N_DEV * n * 4)), rows_unit)
    spec = pl.BlockSpec((tr, n), lambda i: (i, 0))

    def body(w_ref, s_ref, m_ref, v_ref, g_ref, d_ref, nm_ref, nv_ref):
        g = s_ref[0].astype(F32)
        for d in range(1, N_DEV):
            g = g + s_ref[d].astype(F32)
        nm = ADAM_B1 * m_ref[...] + (1.0 - ADAM_B1) * g
        nv = ADAM_B2 * v_ref[...] + (1.0 - ADAM_B2) * (g * g)
        m_hat = nm / (1.0 - ADAM_B1 ** ADAM_STEP)
        v_hat = nv / (1.0 - ADAM_B2 ** ADAM_STEP)
        g_ref[...] = g
        d_ref[...] = -ADAM_LR * (m_hat / (jnp.sqrt(v_hat) + ADAM_EPS) + ADAM_WD * w_ref[...])
        nm_ref[...] = nm
        nv_ref[...] = nv

    return _pcall(body, name=name, out_shape=[jax.ShapeDtypeStruct((r, n), F32)] * 4, grid=(r // tr,),
                  in_specs=[spec, pl.BlockSpec((N_DEV, tr, n), lambda i: (0, i, 0)), spec, spec],
                  out_specs=[spec] * 4)(w, slots, m, v)


def _pack_rows(flat, rows):
    return jnp.pad(flat, (0, rows * PACK_COLS - flat.shape[0])).reshape(rows, PACK_COLS)


def _regroup_in(w_in):
    cols = [w_in[:, a:b] for a, b in _IN_SRC]
    return jnp.concatenate(cols + [jnp.zeros((w_in.shape[0], N_IN_PAD - N_IN), F32)], axis=1)


def _ungroup_in(d):
    starts = {}
    off = 0
    for a, b in _IN_SRC:
        starts[a] = (off, b - a)
        off += b - a
    return jnp.concatenate([d[:, starts[a][0]:starts[a][0] + starts[a][1]] for a in sorted(starts)], axis=1)


def _full_weights(gathered):
    out = {}
    for n, g in zip(SHARDED_ORDER, gathered):
        (r, c), axis = SHARDED[n]
        out[n] = (g.reshape(r, c) if axis == 0 else g.transpose(1, 0, 2).reshape(r, c)).astype(F32)
    out["w_in"] = _regroup_in(out["w_in"])
    return out


def _for_transport(name, shard):
    return shard if name in ("gdn_conv", "ffn_conv") else shard.astype(BF16)


def _grad_parts(grads):
    parts = []
    for n in SHARDED_ORDER:
        (r, c), axis = SHARDED[n]
        g = _ungroup_in(grads[n]) if n == "w_in" else grads[n]
        if axis == 0:
            parts.append(g.reshape(N_DEV, r // N_DEV, c).astype(BF16))
        else:
            parts.append(g.reshape(r, N_DEV, c // N_DEV).transpose(1, 0, 2).astype(BF16))
    return parts


def _pack_small(vals):
    return _pack_rows(jnp.concatenate([vals[n].reshape(-1) for n in SMALL_ORDER]), SMALL_ROWS)


def _unpack_small(packed):
    flat = packed.reshape(-1)
    out, off = {}, 0
    for n in SMALL_ORDER:
        size = DEPTH * SMALL_WIDTH[n]
        out[n] = flat[off:off + size].reshape(DEPTH, SMALL_WIDTH[n])
        off += size
    return out


def kernel(x, mem, norm_mix, w_in, fox_fbias, fox_qnorm, fox_knorm, gdn_conv, gdn_a_log, gdn_dt_bias, gdn_onorm, gate_bias, w_oa, w_ob, w_oc, w_out, norm_xq, norm_mem, w_mq, w_mkv, mq_norm, mk_norm, w_mo, norm_ffn, w_up, ffn_conv, ffn_conv_b, w_down, loss_target, m_norm_mix, m_w_in, m_fox_fbias, m_fox_qnorm, m_fox_knorm, m_gdn_conv, m_gdn_a_log, m_gdn_dt_bias, m_gdn_onorm, m_gate_bias, m_w_oa, m_w_ob, m_w_oc, m_w_out, m_norm_xq, m_norm_mem, m_w_mq, m_w_mkv, m_mq_norm, m_mk_norm, m_w_mo, m_norm_ffn, m_w_up, m_ffn_conv, m_ffn_conv_b, m_w_down, v_norm_mix, v_w_in, v_fox_fbias, v_fox_qnorm, v_fox_knorm, v_gdn_conv, v_gdn_a_log, v_gdn_dt_bias, v_gdn_onorm, v_gate_bias, v_w_oa, v_w_ob, v_w_oc, v_w_out, v_norm_xq, v_norm_mem, v_w_mq, v_w_mkv, v_mq_norm, v_mk_norm, v_w_mo, v_norm_ffn, v_w_up, v_ffn_conv, v_ffn_conv_b, v_w_down):
    given = dict(locals())
    wts = {n: given[n] for n in WEIGHTS}
    mom = {n: given["m_" + n] for n in WEIGHTS}
    var = {n: given["v_" + n] for n in WEIGHTS}

    layers = []
    for l in range(DEPTH):
        full = _full_weights(_all_gather("gather_weights", [_for_transport(n, wts[n][l]) for n in SHARDED_ORDER]))
        full.update({n: wts[n][l] for n in SMALL_ORDER})
        layers.append(full)

    y, vjp = jax.vjp(lambda xx, ww: _forward(xx, mem[0], ww), x[0], layers)
    loss_part, dy = _loss_call(y, loss_target[0])
    dx, dlayers = vjp(dy)
    loss = lax.psum(loss_part, ("x", "y", "c"))

    out = {}
    per_layer = []
    for l in range(DEPTH):
        slots = _exchange("exchange_grads", _grad_parts(dlayers[l]))
        per_layer.append({n: _adam_call("adam_shard", wts[n][l], sl, mom[n][l], var[n][l])
                          for n, sl in zip(SHARDED_ORDER, slots)})
    for n in SHARDED_ORDER:
        for k, kind in enumerate(("grad_", "delta_", "new_m_", "new_v_")):
            out[kind + n] = jnp.stack([per_layer[l][n][k] for l in range(DEPTH)])
    dsmall = {n: jnp.stack([dlayers[l][n] for l in range(DEPTH)]) for n in SMALL_ORDER}
    slots = _all_gather("gather_small_grads", [_pack_small(dsmall)])[0]
    res = _adam_call("adam_small", _pack_small(wts), slots, _pack_small(mom), _pack_small(var))
    for k, kind in enumerate(("grad_", "delta_", "new_m_", "new_v_")):
        un = _unpack_small(res[k])
        for n in SMALL_ORDER:
            out[kind + n] = un[n].reshape(wts[n].shape)

    return (loss, dx[None], *[out["grad_" + n] for n in WEIGHTS], *[out["delta_" + n] for n in WEIGHTS],
            *[out["new_m_" + n] for n in WEIGHTS], *[out["new_v_" + n] for n in WEIGHTS])
```

```python
import jax
import jax.numpy as jnp
from jax import lax
from jax.experimental import pallas as pl
from jax.experimental.pallas import tpu as pltpu

F32 = jnp.float32
BF16 = jnp.bfloat16

N_DEV = 8
D_MODEL = 1024
DEPTH = 4
CHUNK = 64
EPS = 1e-6
HEADS = 4
HEAD_DIM = 128
HW = HEADS * HEAD_DIM
D_FF = 2816
N_IN = 8204
LANES = 128
SUBLANES = 8
VMEM_LIMIT = 56 * 1024 * 1024

ADAM_LR = 0.001
ADAM_B1 = 0.9
ADAM_B2 = 0.999
ADAM_EPS = 1e-08
ADAM_WD = 0.01
ADAM_STEP = 10

NEG = -1e30
MESH = pl.DeviceIdType.MESH

WEIGHTS = ['norm_mix', 'w_in', 'fox_fbias', 'fox_qnorm', 'fox_knorm', 'gdn_conv', 'gdn_a_log', 'gdn_dt_bias',
           'gdn_onorm', 'gate_bias', 'w_oa', 'w_ob', 'w_oc', 'w_out', 'norm_xq', 'norm_mem', 'w_mq', 'w_mkv',
           'mq_norm', 'mk_norm', 'w_mo', 'norm_ffn', 'w_up', 'ffn_conv', 'ffn_conv_b', 'w_down']
SHARDED = {
    'w_in': ((D_MODEL, N_IN), 0), 'gdn_conv': ((4, 3 * HW), 1), 'w_oa': ((HW, D_MODEL), 1),
    'w_ob': ((HW, D_MODEL), 1), 'w_oc': ((HW, D_MODEL), 1), 'w_out': ((D_MODEL, D_MODEL), 0),
    'w_mq': ((D_MODEL, HW), 0), 'w_mkv': ((D_MODEL, 2 * HW), 0), 'w_mo': ((HW, D_MODEL), 1),
    'w_up': ((D_MODEL, 2 * D_FF), 1), 'ffn_conv': ((3, 2 * D_FF), 1), 'w_down': ((D_FF, D_MODEL), 0),
}
SHARDED_ORDER = [n for n in WEIGHTS if n in SHARDED]
SMALL_ORDER = [n for n in WEIGHTS if n not in SHARDED]
SMALL_WIDTH = {'norm_mix': D_MODEL, 'fox_fbias': HEADS, 'fox_qnorm': HEAD_DIM, 'fox_knorm': HEAD_DIM,
               'gdn_a_log': HEADS, 'gdn_dt_bias': HEADS, 'gdn_onorm': HEAD_DIM, 'gate_bias': 3 * D_MODEL,
               'norm_xq': D_MODEL, 'norm_mem': D_MODEL, 'mq_norm': HEAD_DIM, 'mk_norm': HEAD_DIM,
               'norm_ffn': D_MODEL, 'ffn_conv_b': 2 * D_FF}
PACK_COLS = 1024


def _round_up(n, m):
    return (n + m - 1) // m * m


SMALL_ROWS = _round_up(DEPTH * sum(SMALL_WIDTH.values()), SUBLANES * PACK_COLS) // PACK_COLS

_IN_SRC = [(0, 512), (512, 1024), (1024, 1536),
           (1540, 2052), (2052, 2564), (2564, 3076),
           (3084, 3596),
           (3596, 4108), (4108, 4620), (4620, 5132),
           (5132, 8204),
           (1536, 1540), (3076, 3080), (3080, 3084)]
N_IN_PAD = 8320
LANE_FF, LANE_GB, LANE_GA = 0, 4, 8


def _pick(dim, pref, unit):
    best = None
    t = unit
    while t <= min(dim, pref):
        if dim % t == 0:
            best = t
        t += unit
    return dim if best is None else best


def _params(n_grid):
    return pltpu.CompilerParams(dimension_semantics=("arbitrary",) * n_grid, vmem_limit_bytes=VMEM_LIMIT)


def _pcall(body, *, name, out_shape, grid, in_specs, out_specs, scratch_shapes=()):
    return pl.pallas_call(body, name=name, out_shape=out_shape, grid=grid, in_specs=in_specs, out_specs=out_specs,
                          scratch_shapes=scratch_shapes, compiler_params=_params(len(grid)),
                          interpret=False)


NN = ((1,), (0,))
NT = ((1,), (1,))
TN = ((0,), (0,))


def _dot(a, b, dn):
    return lax.dot_general(a.astype(BF16), b.astype(BF16), (dn, ((), ())), preferred_element_type=F32)


def _dotf(a, b, dn):
    return lax.dot_general(a, b, (dn, ((), ())), precision=lax.Precision.HIGHEST, preferred_element_type=F32)


@jax.custom_vjp
def mm(a, b):
    return _dot(a, b, NN)


mm.defvjp(lambda a, b: (_dot(a, b, NN), (a, b)), lambda r, g: (_dot(g, r[1], NT), _dot(r[0], g, TN)))


@jax.custom_vjp
def mm_nt(a, b):
    return _dot(a, b, NT)


mm_nt.defvjp(lambda a, b: (_dot(a, b, NT), (a, b)), lambda r, g: (_dot(g, r[1], NN), _dot(g, r[0], TN)))


@jax.custom_vjp
def mm_tn(a, b):
    return _dot(a, b, TN)


mm_tn.defvjp(lambda a, b: (_dot(a, b, TN), (a, b)), lambda r, g: (_dot(r[1], g, NT), _dot(r[0], g, NN)))


@jax.custom_vjp
def tri_apply(t, x):
    return _dotf(t, x, NN)


tri_apply.defvjp(lambda t, x: (_dotf(t, x, NN), t), lambda t, g: (jnp.zeros_like(t), _dotf(t, g, TN)))


def _sigmoid(x):
    return 1.0 / (1.0 + jnp.exp(-x))


@jax.custom_vjp
def _softplus(x):
    return jnp.maximum(x, 0.0) + jnp.log(1.0 + jnp.exp(-jnp.abs(x)))


_softplus.defvjp(lambda x: (_softplus(x), x), lambda x, g: (g * _sigmoid(x),))


def _silu(x):
    return x * _sigmoid(x)


def _rms(x, g):
    return x * lax.rsqrt(jnp.mean(x * x, axis=-1, keepdims=True) + EPS) * g


def _iota2(n, m, axis):
    return lax.broadcasted_iota(jnp.int32, (n, m), axis)


def _whole(width):
    return [(0, width)]


def _split(width, n):
    w = width // n
    return [(k * w, w) for k in range(n)]


class RowOp:
    def __init__(self, name, f, in_pieces, out_pieces, tm=256):
        self.name, self.f, self.in_pieces, self.out_pieces, self.tm = name, f, in_pieces, out_pieces, tm
        op = jax.custom_vjp(self._fwd_call)
        op.defvjp(lambda *a: (self._fwd_call(*a), a), lambda res, g: self._bwd_call(res, g))
        self.op = op

    def __call__(self, *args):
        return self.op(*args)

    def _width(self, pieces):
        return max(o + w for o, w in pieces)

    def _row_specs(self, pieces_list, tm):
        return [pl.BlockSpec((tm, self._width(p)), lambda i: (i, 0)) for p in pieces_list]

    def _fwd_call(self, *args):
        nr = len(self.in_pieces)
        rows, params = args[:nr], args[nr:]
        m = rows[0].shape[0]
        tm = _pick(m, self.tm, SUBLANES)
        f, in_pieces, out_pieces = self.f, self.in_pieces, self.out_pieces
        no = len(out_pieces)

        def body(*refs):
            rin, pr, ro = refs[:nr], refs[nr:nr + len(params)], refs[nr + len(params):]
            xs = [r[:, o:o + w] for r, ps in zip(rin, in_pieces) for (o, w) in ps]
            ys = f(*xs, *[p[...] for p in pr])
            k = 0
            for r, ps in zip(ro, out_pieces):
                for (o, w) in ps:
                    r[:, o:o + w] = ys[k]
                    k += 1

        outs = _pcall(
            body, name=self.name + "_fwd",
            out_shape=[jax.ShapeDtypeStruct((m, self._width(p)), F32) for p in out_pieces],
            grid=(m // tm,),
            in_specs=self._row_specs(in_pieces, tm) + [pl.BlockSpec(p.shape, lambda i: (0, 0)) for p in params],
            out_specs=self._row_specs(out_pieces, tm),
        )(*rows, *params)
        return tuple(outs) if no > 1 else outs[0]

    def _bwd_call(self, res, g):
        nr = len(self.in_pieces)
        rows, params = res[:nr], res[nr:]
        no = len(self.out_pieces)
        gs = tuple(g) if no > 1 else (g,)
        m = rows[0].shape[0]
        tm = _pick(m, self.tm, SUBLANES)
        f, in_pieces, out_pieces = self.f, self.in_pieces, self.out_pieces
        npar = len(params)

        def body(*refs):
            rin, pr, dro = refs[:nr], refs[nr:nr + npar], refs[nr + npar:nr + npar + no]
            drin, dpr = refs[nr + npar + no:nr + npar + no + nr], refs[nr + npar + no + nr:]
            xs = [r[:, o:o + w] for r, ps in zip(rin, in_pieces) for (o, w) in ps]
            dys = [r[:, o:o + w] for r, ps in zip(dro, out_pieces) for (o, w) in ps]
            _, vjp = jax.vjp(lambda *a: tuple(f(*a)), *xs, *[p[...] for p in pr])
            grads = vjp(tuple(dys))
            k = 0
            for r, ps in zip(drin, in_pieces):
                for (o, w) in ps:
                    r[:, o:o + w] = grads[k]
                    k += 1

            @pl.when(pl.program_id(0) == 0)
            def _():
                for r in dpr:
                    r[...] = jnp.zeros_like(r)

            for j, r in enumerate(dpr):
                r[...] += grads[k + j]

        outs = _pcall(
            body, name=self.name + "_bwd",
            out_shape=[jax.ShapeDtypeStruct(r.shape, F32) for r in rows]
            + [jax.ShapeDtypeStruct(p.shape, F32) for p in params],
            grid=(m // tm,),
            in_specs=self._row_specs(in_pieces, tm) + [pl.BlockSpec(p.shape, lambda i: (0, 0)) for p in params]
            + self._row_specs(out_pieces, tm),
            out_specs=self._row_specs(in_pieces, tm) + [pl.BlockSpec(p.shape, lambda i: (0, 0)) for p in params],
        )(*rows, *params, *gs)
        return tuple(outs)


def _f_rms(x, g):
    return (_rms(x, g),)


def _f_headnorm(x0, x1, x2, x3, g):
    return tuple(_rms(x, g) for x in (x0, x1, x2, x3))


def _f_small(sm, fb, al, db):
    tm = sm.shape[0]
    logf = -_softplus(-(sm + fb))
    beta = _sigmoid(sm)
    glog = -jnp.exp(al) * _softplus(sm + db)
    r, c = _iota2(tm, tm, 0), _iota2(tm, tm, 1)
    bd = jnp.where((r >= c) & (jnp.bitwise_xor(r, c) < CHUNK), 1.0, 0.0).astype(F32)
    return logf, beta, tri_apply(bd, glog)


def _f_gdnpost(o0, o1, o2, o3, z0, z1, z2, z3, g):
    return tuple(_rms(o, g) * _silu(z) for o, z in zip((o0, o1, o2, o3), (z0, z1, z2, z3)))


def _f_merge(t0, t1, t2, a, b, c, b0, b1, b2):
    return (_sigmoid(t0 + b0) * a + _sigmoid(t1 + b1) * b + _sigmoid(t2 + b2) * c,)


def _make_rowops():
    return dict(
        rms=RowOp("rms", _f_rms, [_whole(D_MODEL)], [_whole(D_MODEL)]),
        headnorm=RowOp("headnorm", _f_headnorm, [_split(HW, HEADS)], [_split(HW, HEADS)]),
        small=RowOp("smallprep", _f_small, [_whole(LANES)], [_whole(LANES)] * 3),
        gdnpost=RowOp("gdnpost", _f_gdnpost, [_split(HW, HEADS)] * 2, [_split(HW, HEADS)]),
        merge=RowOp("merge", _f_merge, [_split(3 * D_MODEL, 3)] + [_whole(D_MODEL)] * 3, [_whole(D_MODEL)]),
    )


def _mm_call(name, a, b, mode, c=None):
    if mode == "nn":
        (m, kc), n = a.shape, b.shape[1]
    elif mode == "nt":
        (m, kc), n = a.shape, b.shape[0]
    else:
        (kc, m), n = a.shape, b.shape[1]
    tm = _pick(m, 1408, LANES) if mode == "tn" else _pick(m, 1024, SUBLANES)
    tn = _pick(n, 1024, LANES)
    tk = _pick(kc, 1024, SUBLANES) if mode == "tn" else _pick(kc, 1536, LANES)
    dn = {"nn": NN, "nt": NT, "tn": TN}[mode]
    a_spec = {"nn": pl.BlockSpec((tm, tk), lambda i, j, k: (i, k)),
              "nt": pl.BlockSpec((tm, tk), lambda i, j, k: (i, k)),
              "tn": pl.BlockSpec((tk, tm), lambda i, j, k: (k, i))}[mode]
    b_spec = {"nn": pl.BlockSpec((tk, tn), lambda i, j, k: (k, j)),
              "nt": pl.BlockSpec((tn, tk), lambda i, j, k: (j, k)),
              "tn": pl.BlockSpec((tk, tn), lambda i, j, k: (k, j))}[mode]
    o_spec = pl.BlockSpec((tm, tn), lambda i, j, k: (i, j))
    has_c = c is not None

    def body(*refs):
        a_ref, b_ref = refs[0], refs[1]
        o_ref = refs[-1]

        @pl.when(pl.program_id(2) == 0)
        def _():
            o_ref[...] = refs[2][...] if has_c else jnp.zeros_like(o_ref)

        o_ref[...] += _dot(a_ref[...], b_ref[...], dn)

    return _pcall(body, name=name, out_shape=jax.ShapeDtypeStruct((m, n), F32), grid=(m // tm, n // tn, kc // tk),
                  in_specs=[a_spec, b_spec] + ([o_spec] if has_c else []), out_specs=o_spec,
                  )(*((a, b, c) if has_c else (a, b)))


@jax.custom_vjp
def matmul(a, w):
    return _mm_call("mm_nn", a, w, "nn")


matmul.defvjp(lambda a, w: (_mm_call("mm_nn", a, w, "nn"), (a, w)),
              lambda r, g: (_mm_call("mm_nt", g, r[1], "nt"), _mm_call("mm_tn", r[0], g, "tn")))


@jax.custom_vjp
def matmul_add(c, a, w):
    return _mm_call("mm_nn_add", a, w, "nn", c)


matmul_add.defvjp(lambda c, a, w: (_mm_call("mm_nn_add", a, w, "nn", c), (a, w)),
                  lambda r, g: (g, _mm_call("mm_nt", g, r[1], "nt"), _mm_call("mm_tn", r[0], g, "tn")))

_PROJ_GROUPS = [(0, 512), (512, 512), (1024, 512), (1536, 1536), (3072, 512), (3584, 512), (4096, 512), (4608, 512),
                (5120, 3072), (8192, 128)]


def _proj_impl(h, w):
    return tuple(_mm_call("proj_nn", h, w[:, s:s + n], "nn") for s, n in _PROJ_GROUPS)


proj = jax.custom_vjp(_proj_impl)


def _proj_bwd(res, gs):
    h, w = res
    dh = None
    dws = []
    for (s, n), g in zip(_PROJ_GROUPS, gs):
        dh = _mm_call("proj_nt", g, w[:, s:s + n], "nt", dh)
        dws.append(_mm_call("proj_tn", h, g, "tn"))
    return dh, jnp.concatenate(dws, axis=1)


proj.defvjp(lambda h, w: (_proj_impl(h, w), (h, w)), _proj_bwd)


def _cumsum_call(x, reverse):
    s, w = x.shape
    tm = _pick(s, 256, SUBLANES)
    nb = s // tm

    def body(x_ref, o_ref, carry):
        @pl.when(pl.program_id(0) == 0)
        def _():
            carry[...] = jnp.zeros_like(carry)

        blk = x_ref[...]
        r, c = _iota2(tm, tm, 0), _iota2(tm, tm, 1)
        tri = jnp.where((r <= c) if reverse else (r >= c), 1.0, 0.0).astype(F32)
        o_ref[...] = _dotf(tri, blk, NN) + carry[...]
        carry[...] += jnp.sum(blk, axis=0, keepdims=True)

    idx = (lambda i: (nb - 1 - i, 0)) if reverse else (lambda i: (i, 0))
    return _pcall(body, name="cumsum_rev" if reverse else "cumsum", out_shape=jax.ShapeDtypeStruct((s, w), F32),
                  grid=(nb,), in_specs=[pl.BlockSpec((tm, w), idx)], out_specs=pl.BlockSpec((tm, w), idx),
                  scratch_shapes=[pltpu.VMEM((1, w), F32)])(x)


@jax.custom_vjp
def seq_cumsum(x):
    return _cumsum_call(x, False)


seq_cumsum.defvjp(lambda x: (_cumsum_call(x, False), None), lambda _, g: (_cumsum_call(g, True),))


HALO = SUBLANES


class ConvOp:
    def __init__(self, name, width, post, c_pieces, out_widths, has_bias, tm):
        self.name, self.width, self.post, self.c_pieces = name, width, post, c_pieces
        self.out_widths, self.has_bias, self.tm = out_widths, has_bias, tm
        op = jax.custom_vjp(self._fwd_call)
        op.defvjp(lambda *a: (self._fwd_call(*a), a), lambda res, g: self._bwd_call(res, g))
        self.op = op

    def __call__(self, *args):
        return self.op(*args)

    def _conv(self, i, x_ref, prev_ref, w_ref, b_ref, buf):
        tm = x_ref.shape[0]
        buf[0:HALO, :] = jnp.where(i > 0, prev_ref[...], 0.0)
        buf[HALO:HALO + tm, :] = x_ref[...]
        taps = [buf[pl.ds(HALO - (self.width - 1) + j, tm), :] for j in range(self.width)]
        c = taps[0] * w_ref[0:1, :]
        for j in range(1, self.width):
            c = c + taps[j] * w_ref[j:j + 1, :]
        if self.has_bias:
            c = c + b_ref[...]
        return c, taps

    def _fwd_call(self, x, w, *bias):
        s, ch = x.shape
        tm = _pick(s, self.tm, SUBLANES)
        r8 = tm // HALO
        has_bias, post, c_pieces = self.has_bias, self.post, self.c_pieces

        def body(*refs):
            x_ref, prev_ref, w_ref = refs[:3]
            b_ref = refs[3] if has_bias else None
            outs, buf = refs[3 + has_bias:-1], refs[-1]
            c, _ = self._conv(pl.program_id(0), x_ref, prev_ref, w_ref, b_ref, buf)
            ys = post(*[c[:, o:o + n] for o, n in c_pieces])
            for r, y in zip(outs, ys):
                r[...] = y

        outs = _pcall(
            body, name=self.name + "_fwd", out_shape=[jax.ShapeDtypeStruct((s, n), F32) for n in self.out_widths],
            grid=(s // tm,),
            in_specs=[pl.BlockSpec((tm, ch), lambda i: (i, 0)),
                      pl.BlockSpec((HALO, ch), lambda i: (jnp.maximum(i * r8 - 1, 0), 0)),
                      pl.BlockSpec(w.shape, lambda i: (0, 0))]
            + ([pl.BlockSpec((1, ch), lambda i: (0, 0))] if has_bias else []),
            out_specs=[pl.BlockSpec((tm, n), lambda i: (i, 0)) for n in self.out_widths],
            scratch_shapes=[pltpu.VMEM((tm + HALO, ch), F32)],
        )(x, x, w, *bias)
        return tuple(outs) if len(outs) > 1 else outs[0]

    def _bwd_call(self, res, g):
        x, w = res[0], res[1]
        bias = res[2:]
        gs = tuple(g) if len(self.out_widths) > 1 else (g,)
        s, ch = x.shape
        tm = _pick(s, self.tm, SUBLANES)
        r8 = tm // HALO
        nb = s // tm
        has_bias, post, c_pieces, width = self.has_bias, self.post, self.c_pieces, self.width
        ng = len(gs)

        def body1(*refs):
            x_ref, prev_ref, w_ref = refs[:3]
            b_ref = refs[3] if has_bias else None
            k = 3 + has_bias
            g_refs = refs[k:k + ng]
            dc_ref, dw_ref = refs[k + ng], refs[k + ng + 1]
            db_ref = refs[k + ng + 2] if has_bias else None
            buf = refs[-1]
            i = pl.program_id(0)
            c, taps = self._conv(i, x_ref, prev_ref, w_ref, b_ref, buf)
            _, vjp = jax.vjp(lambda *a: tuple(post(*a)), *[c[:, o:o + n] for o, n in c_pieces])
            dcs = vjp(tuple(r[...] for r in g_refs))
            for (o, n), d in zip(c_pieces, dcs):
                dc_ref[:, o:o + n] = d

            @pl.when(i == 0)
            def _():
                dw_ref[...] = jnp.zeros_like(dw_ref)
                if has_bias:
                    db_ref[...] = jnp.zeros_like(db_ref)

            dc = dc_ref[...]
            for j in range(width):
                dw_ref[j:j + 1, :] += jnp.sum(dc * taps[j], axis=0, keepdims=True)
            if has_bias:
                db_ref[...] += jnp.sum(dc, axis=0, keepdims=True)

        outs1 = _pcall(
            body1, name=self.name + "_bwd_act",
            out_shape=[jax.ShapeDtypeStruct((s, ch), F32), jax.ShapeDtypeStruct(w.shape, F32)]
            + ([jax.ShapeDtypeStruct((1, ch), F32)] if has_bias else []),
            grid=(nb,),
            in_specs=[pl.BlockSpec((tm, ch), lambda i: (i, 0)),
                      pl.BlockSpec((HALO, ch), lambda i: (jnp.maximum(i * r8 - 1, 0), 0)),
                      pl.BlockSpec(w.shape, lambda i: (0, 0))]
            + ([pl.BlockSpec((1, ch), lambda i: (0, 0))] if has_bias else [])
            + [pl.BlockSpec((tm, n), lambda i: (i, 0)) for n in self.out_widths],
            out_specs=[pl.BlockSpec((tm, ch), lambda i: (i, 0)), pl.BlockSpec(w.shape, lambda i: (0, 0))]
            + ([pl.BlockSpec((1, ch), lambda i: (0, 0))] if has_bias else []),
            scratch_shapes=[pltpu.VMEM((tm + HALO, ch), F32)],
        )(x, x, w, *bias, *gs)
        dc, dw = outs1[0], outs1[1]

        def body2(dc_ref, next_ref, w_ref, dx_ref, buf):
            i = pl.program_id(0)
            buf[0:tm, :] = dc_ref[...]
            buf[tm:tm + HALO, :] = jnp.where(i < nb - 1, next_ref[...], 0.0)
            dx = buf[pl.ds(width - 1, tm), :] * w_ref[0:1, :]
            for j in range(1, width):
                dx = dx + buf[pl.ds(width - 1 - j, tm), :] * w_ref[j:j + 1, :]
            dx_ref[...] = dx

        dx = _pcall(
            body2, name=self.name + "_bwd_in", out_shape=jax.ShapeDtypeStruct((s, ch), F32), grid=(nb,),
            in_specs=[pl.BlockSpec((tm, ch), lambda i: (i, 0)),
                      pl.BlockSpec((HALO, ch), lambda i: (jnp.minimum((i + 1) * r8, s // HALO - 1), 0)),
                      pl.BlockSpec(w.shape, lambda i: (0, 0))],
            out_specs=pl.BlockSpec((tm, ch), lambda i: (i, 0)),
            scratch_shapes=[pltpu.VMEM((tm + HALO, ch), F32)],
        )(dc, dc, w)
        return (dx, dw) + ((outs1[2],) if has_bias else ())


def _make_convops():
    return dict(
        gdn=ConvOp("gdnconv", 4, lambda q, k, v: (_silu(q), _silu(k), _silu(v)), _split(3 * HW, 3), [HW] * 3,
                   False, 256),
        ffn=ConvOp("ffnconv", 3, lambda a, b: (_silu(a) * b,), _split(2 * D_FF, 2), [D_FF], True, 128),
    )


ATT_Q = 512
ATT_K = 256
SCALE = HEAD_DIM ** -0.5


def _att_tiles(s):
    tk = _pick(s, ATT_K, LANES)
    tq = _pick(s, ATT_Q, tk)
    return tq, tk


def _att_specs(s, tq, tk):
    qspec = pl.BlockSpec((tq, HEAD_DIM), lambda h, i: (i, h))
    kspec = pl.BlockSpec((s, HEAD_DIM), lambda h, i: (0, h))
    colspec = pl.BlockSpec((None, tq, 1), lambda h, i: (h, i, 0))
    rowspec = pl.BlockSpec((None, s // tk, 1, tk), lambda h, i: (h, 0, 0, 0))
    return qspec, kspec, colspec, rowspec


def _krows(kb, tk):
    return pl.ds(pl.multiple_of(kb * tk, tk), tk)


def _stage_bf16(i, pairs):
    @pl.when(i == 0)
    def _():
        for src, dst in pairs:
            dst[...] = src[...].astype(BF16)


ATT_STRIP = 32


def _strips(tq):
    return [slice(r, r + ATT_STRIP) for r in range(0, tq, ATT_STRIP)]


def _visible(i, kb, rs, tq, tk, strict):
    rows = i * tq + rs.start + _iota2(ATT_STRIP, tk, 0)
    cols = kb * tk + _iota2(ATT_STRIP, tk, 1)
    return (cols < rows) if strict else (cols <= rows)


def _visible_block(i, kb, tq, tk):
    return kb * tk + _iota2(tq, tk, 1) <= i * tq + _iota2(tq, tk, 0)


def _blocks(i, ratio, blk, reverse=False):
    def full(n, carry):
        blk(i * ratio - 1 - n if reverse else n, False)
        return carry

    if reverse:
        for j in reversed(range(ratio)):
            blk(i * ratio + j, True)
    lax.fori_loop(0, i * ratio, full, 0)
    if not reverse:
        for j in range(ratio):
            blk(i * ratio + j, True)


def _vm(shape, dtype):
    return pltpu.VMEM(shape, dtype)


def _fox_fwd_call(q, k, v, ccol, crow):
    s = q.shape[0]
    tq, tk = _att_tiles(s)
    qspec, kspec, colspec, rowspec = _att_specs(s, tq, tk)

    def body(q_ref, k_ref, v_ref, cq_ref, ck_ref, o_ref, lse_ref, k16, v16):
        i = pl.program_id(1)
        ratio = tq // tk
        _stage_bf16(i, [(k_ref, k16), (v_ref, v16)])
        qb = q_ref[...].astype(BF16)
        cq = cq_ref[...]

        def blk(kb, carry, masked):
            m, l, acc = carry
            sc = _dot(qb, k16[_krows(kb, tk), :], NT) * SCALE + (cq - ck_ref[kb])
            if masked:
                sc = jnp.where(_visible_block(i, kb, tq, tk), sc, NEG)
            m_new = jnp.maximum(m, jnp.max(sc, axis=-1, keepdims=True))
            alpha = jnp.exp(m - m_new)
            p = jnp.exp(sc - m_new)
            return (m_new, alpha * l + jnp.sum(p, axis=-1, keepdims=True),
                    alpha * acc + _dot(p, v16[_krows(kb, tk), :], NN))

        carry = (jnp.full((tq, 1), NEG, F32), jnp.zeros((tq, 1), F32), jnp.zeros((tq, HEAD_DIM), F32))
        carry = lax.fori_loop(0, i * ratio, lambda kb, c: blk(kb, c, False), carry)
        for j in range(ratio):
            carry = blk(i * ratio + j, carry, True)
        m, l, acc = carry
        o_ref[...] = acc / l
        lse_ref[...] = m + jnp.log(l)

    return _pcall(body, name="fox_fwd",
                  out_shape=[jax.ShapeDtypeStruct((s, HW), F32), jax.ShapeDtypeStruct((HEADS, s, 1), F32)],
                  grid=(HEADS, s // tq), in_specs=[qspec, kspec, kspec, colspec, rowspec],
                  out_specs=[qspec, colspec],
                  scratch_shapes=[_vm((s, HEAD_DIM), BF16), _vm((s, HEAD_DIM), BF16)])(q, k, v, ccol, crow)


def _fox_bwd_call(q, k, v, ccol, crow, o, lse, do):
    s = q.shape[0]
    tq, tk = _att_tiles(s)
    ratio = tq // tk
    qspec, kspec, colspec, rowspec = _att_specs(s, tq, tk)

    def body(q_ref, k_ref, v_ref, cq_ref, ck_ref, o_ref, lse_ref, do_ref, dq_ref, dk_ref, dv_ref, dcq_ref, dck_ref,
             k16, v16):
        i = pl.program_id(1)
        _stage_bf16(i, [(k_ref, k16), (v_ref, v16)])

        @pl.when(i == 0)
        def _():
            dk_ref[...] = jnp.zeros_like(dk_ref)
            dv_ref[...] = jnp.zeros_like(dv_ref)
            dck_ref[...] = jnp.zeros_like(dck_ref)

        qb = q_ref[...].astype(BF16)
        dob = do_ref[...].astype(BF16)
        cq, lse = cq_ref[...], lse_ref[...]
        dl = jnp.sum(do_ref[...] * o_ref[...], axis=-1, keepdims=True)

        def blk(kb, carry, masked):
            dq, dcq = carry
            rows = _krows(kb, tk)
            kk, vv = k16[rows, :], v16[rows, :]
            sc = _dot(qb, kk, NT) * SCALE + (cq - ck_ref[kb])
            p = jnp.exp(sc - lse)
            if masked:
                p = jnp.where(_visible_block(i, kb, tq, tk), p, 0.0)
            dv_ref[rows, :] += _dot(p, dob, TN)
            ds = p * (_dot(dob, vv, NT) - dl)
            dk_ref[rows, :] += _dot(ds, qb, TN) * SCALE
            dck_ref[kb] += -jnp.sum(ds, axis=0, keepdims=True)
            return dq + _dot(ds, kk, NN) * SCALE, dcq + jnp.sum(ds, axis=-1, keepdims=True)

        carry = (jnp.zeros((tq, HEAD_DIM), F32), jnp.zeros((tq, 1), F32))
        carry = lax.fori_loop(0, i * ratio, lambda kb, c: blk(kb, c, False), carry)
        for j in range(ratio):
            carry = blk(i * ratio + j, carry, True)
        dq_ref[...] = carry[0]
        dcq_ref[...] = carry[1]

    return _pcall(body, name="fox_bwd",
                  out_shape=[jax.ShapeDtypeStruct((s, HW), F32)] * 3
                  + [jax.ShapeDtypeStruct((HEADS, s, 1), F32), jax.ShapeDtypeStruct((HEADS, s // tk, 1, tk), F32)],
                  grid=(HEADS, s // tq),
                  in_specs=[qspec, kspec, kspec, colspec, rowspec, qspec, colspec, qspec],
                  out_specs=[qspec, kspec, kspec, colspec, rowspec],
                  scratch_shapes=[_vm((s, HEAD_DIM), BF16), _vm((s, HEAD_DIM), BF16)],
                  )(q, k, v, ccol, crow, o, lse, do)


@jax.custom_vjp
def fox_attention(q, k, v, ccol, crow):
    return _fox_fwd_call(q, k, v, ccol, crow)[0]


def _fox_vjp_fwd(q, k, v, ccol, crow):
    o, lse = _fox_fwd_call(q, k, v, ccol, crow)
    return o, (q, k, v, ccol, crow, o, lse)


fox_attention.defvjp(_fox_vjp_fwd, lambda res, g: tuple(_fox_bwd_call(*res, g)))


def _sb_fwd_call(q, k, v):
    s = q.shape[0]
    tq, tk = _att_tiles(s)
    ratio = tq // tk
    qspec, kspec, colspec, _ = _att_specs(s, tq, tk)

    def body(q_ref, k_ref, v_ref, o_ref, tot_ref, k16, v16, q16, s_scr, w_scr, lk16, a16, run, acc):
        i = pl.program_id(1)
        _stage_bf16(i, [(k_ref, k16), (v_ref, v16)])
        q16[...] = q_ref[...].astype(BF16)
        run[...] = jnp.zeros_like(run)
        acc[...] = jnp.zeros_like(acc)
        suffix = jnp.where(_iota2(tk, tk, 0) >= _iota2(tk, tk, 1), 1.0, 0.0).astype(BF16)

        def blk(kb, masked):
            rows = _krows(kb, tk)
            s_scr[...] = _dot(q16[...], k16[rows, :], NT)
            for rs in _strips(tq):
                lk = -_softplus(s_scr[rs, :] * SCALE)
                if masked:
                    lk = jnp.where(_visible(i, kb, rs, tq, tk, True), lk, 0.0)
                lk16[rs, :] = lk.astype(BF16)
            w_scr[...] = _dot(lk16[...], suffix, NN)
            for rs in _strips(tq):
                a = jnp.exp(s_scr[rs, :] * SCALE + w_scr[rs, :] + run[rs, :])
                if masked:
                    a = jnp.where(_visible(i, kb, rs, tq, tk, True), a, 0.0)
                a16[rs, :] = a.astype(BF16)
                run[rs, :] += w_scr[rs, 0:1]
            acc[...] += _dot(a16[...], v16[rows, :], NN)

        _blocks(i, ratio, blk, reverse=True)
        o_ref[...] = acc[...]
        tot_ref[...] = run[...]

    return _pcall(body, name="sb_fwd",
                  out_shape=[jax.ShapeDtypeStruct((s, HW), F32), jax.ShapeDtypeStruct((HEADS, s, 1), F32)],
                  grid=(HEADS, s // tq), in_specs=[qspec, kspec, kspec], out_specs=[qspec, colspec],
                  scratch_shapes=[_vm((s, HEAD_DIM), BF16), _vm((s, HEAD_DIM), BF16), _vm((tq, HEAD_DIM), BF16),
                                  _vm((tq, tk), F32), _vm((tq, tk), F32), _vm((tq, tk), BF16), _vm((tq, tk), BF16),
                                  _vm((tq, 1), F32), _vm((tq, HEAD_DIM), F32)])(q, k, v)


def _sb_bwd_call(q, k, v, tot, do):
    s = q.shape[0]
    tq, tk = _att_tiles(s)
    ratio = tq // tk
    qspec, kspec, colspec, _ = _att_specs(s, tq, tk)

    def body(q_ref, k_ref, v_ref, tot_ref, do_ref, dq_ref, dk_ref, dv_ref, k16, v16, q16, do16, s_scr, e_scr, w_scr,
             lz16, a16, e16, left, esum):
        i = pl.program_id(1)
        _stage_bf16(i, [(k_ref, k16), (v_ref, v16)])

        @pl.when(i == 0)
        def _():
            dk_ref[...] = jnp.zeros_like(dk_ref)
            dv_ref[...] = jnp.zeros_like(dv_ref)

        q16[...] = q_ref[...].astype(BF16)
        do16[...] = do_ref[...].astype(BF16)
        left[...] = jnp.zeros_like(left)
        esum[...] = jnp.zeros_like(esum)
        dq_ref[...] = jnp.zeros_like(dq_ref)
        prefix = jnp.where(_iota2(tk, tk, 0) <= _iota2(tk, tk, 1), 1.0, 0.0).astype(BF16)

        def blk(kb, masked):
            rows = _krows(kb, tk)
            s_scr[...] = _dot(q16[...], k16[rows, :], NT)
            e_scr[...] = _dot(do16[...], v16[rows, :], NT)
            for rs in _strips(tq):
                lk = -_softplus(s_scr[rs, :] * SCALE)
                if masked:
                    lk = jnp.where(_visible(i, kb, rs, tq, tk, True), lk, 0.0)
                lz16[rs, :] = lk.astype(BF16)
            w_scr[...] = _dot(lz16[...], prefix, NN)
            for rs in _strips(tq):
                rc = (tot_ref[rs, :] - left[rs, :]) - (w_scr[rs, :] - lz16[rs, :].astype(F32))
                a = jnp.exp(s_scr[rs, :] * SCALE + rc)
                if masked:
                    a = jnp.where(_visible(i, kb, rs, tq, tk, True), a, 0.0)
                e = a * e_scr[rs, :]
                a16[rs, :] = a.astype(BF16)
                e16[rs, :] = e.astype(BF16)
                e_scr[rs, :] = e
                left[rs, :] += w_scr[rs, tk - 1:tk]
            w_scr[...] = _dot(e16[...], prefix, NN)
            for rs in _strips(tq):
                dz = e_scr[rs, :] - _sigmoid(s_scr[rs, :] * SCALE) * (esum[rs, :] + w_scr[rs, :])
                if masked:
                    dz = jnp.where(_visible(i, kb, rs, tq, tk, True), dz, 0.0)
                lz16[rs, :] = dz.astype(BF16)
                esum[rs, :] += w_scr[rs, tk - 1:tk]
            dv_ref[rows, :] += _dot(a16[...], do16[...], TN)
            dk_ref[rows, :] += _dot(lz16[...], q16[...], TN) * SCALE
            dq_ref[...] += _dot(lz16[...], k16[rows, :], NN) * SCALE

        _blocks(i, ratio, blk)

    return _pcall(body, name="sb_bwd", out_shape=[jax.ShapeDtypeStruct((s, HW), F32)] * 3, grid=(HEADS, s // tq),
                  in_specs=[qspec, kspec, kspec, colspec, qspec], out_specs=[qspec, kspec, kspec],
                  scratch_shapes=[_vm((s, HEAD_DIM), BF16), _vm((s, HEAD_DIM), BF16), _vm((tq, HEAD_DIM), BF16),
                                  _vm((tq, HEAD_DIM), BF16), _vm((tq, tk), F32), _vm((tq, tk), F32),
                                  _vm((tq, tk), F32), _vm((tq, tk), BF16), _vm((tq, tk), BF16), _vm((tq, tk), BF16),
                                  _vm((tq, 1), F32), _vm((tq, 1), F32)])(q, k, v, tot, do)


@jax.custom_vjp
def sb_attention(q, k, v):
    return _sb_fwd_call(q, k, v)[0]


def _sb_vjp_fwd(q, k, v):
    o, tot = _sb_fwd_call(q, k, v)
    return o, (q, k, v, tot)


sb_attention.defvjp(_sb_vjp_fwd, lambda res, g: tuple(_sb_bwd_call(*res, g)))


def _mem_specs(s, nk, t):
    return (pl.BlockSpec((t, HEAD_DIM), lambda h, i: (i, h)), pl.BlockSpec((nk, HEAD_DIM), lambda h, i: (0, h)))


def _mem_probs(qb, kk):
    sc = _dot(qb, kk, NT) * SCALE
    p = jnp.exp(sc - jnp.max(sc, axis=-1, keepdims=True))
    return p / jnp.sum(p, axis=-1, keepdims=True)


def _mem_fwd_call(q, k, v):
    s, nk = q.shape[0], k.shape[0]
    t = _pick(s, 512, SUBLANES)
    qspec, kspec = _mem_specs(s, nk, t)

    def body(q_ref, k_ref, v_ref, o_ref):
        o_ref[...] = _dot(_mem_probs(q_ref[...].astype(BF16), k_ref[...]), v_ref[...], NN)

    return _pcall(body, name="mem_fwd", out_shape=jax.ShapeDtypeStruct((s, HW), F32), grid=(HEADS, s // t),
                  in_specs=[qspec, kspec, kspec], out_specs=qspec)(q, k, v)


def _mem_bwd_call(q, k, v, do):
    s, nk = q.shape[0], k.shape[0]
    t = _pick(s, 512, SUBLANES)
    qspec, kspec = _mem_specs(s, nk, t)

    def body(q_ref, k_ref, v_ref, do_ref, dq_ref, dk_ref, dv_ref):
        @pl.when(pl.program_id(1) == 0)
        def _():
            dk_ref[...] = jnp.zeros_like(dk_ref)
            dv_ref[...] = jnp.zeros_like(dv_ref)

        qb = q_ref[...].astype(BF16)
        dob = do_ref[...].astype(BF16)
        p = _mem_probs(qb, k_ref[...])
        dv_ref[...] += _dot(p, dob, TN)
        dp = _dot(dob, v_ref[...], NT)
        ds = p * (dp - jnp.sum(p * dp, axis=-1, keepdims=True))
        dq_ref[...] = _dot(ds, k_ref[...], NN) * SCALE
        dk_ref[...] += _dot(ds, qb, TN) * SCALE

    return _pcall(body, name="mem_bwd",
                  out_shape=[jax.ShapeDtypeStruct((s, HW), F32)] + [jax.ShapeDtypeStruct((nk, HW), F32)] * 2,
                  grid=(HEADS, s // t), in_specs=[qspec, kspec, kspec, qspec],
                  out_specs=[qspec, kspec, kspec])(q, k, v, do)


@jax.custom_vjp
def mem_attention(q, k, v):
    return _mem_fwd_call(q, k, v)


mem_attention.defvjp(lambda q, k, v: (_mem_fwd_call(q, k, v), (q, k, v)),
                     lambda res, g: tuple(_mem_bwd_call(*res, g)))


BNN = (((2,), (1,)), ((0,), (0,)))
BNT = (((2,), (2,)), ((0,), (0,)))
BTN = (((1,), (1,)), ((0,), (0,)))


def _bdot(a, b, dn):
    return lax.dot_general(a.astype(BF16), b.astype(BF16), dn, preferred_element_type=F32)


def _bdotf(a, b, dn):
    return lax.dot_general(a, b, dn, precision=lax.Precision.HIGHEST, preferred_element_type=F32)


@jax.custom_vjp
def bmm(a, b):
    return _bdot(a, b, BNN)


bmm.defvjp(lambda a, b: (_bdot(a, b, BNN), (a, b)), lambda r, g: (_bdot(g, r[1], BNT), _bdot(r[0], g, BTN)))


@jax.custom_vjp
def bmm_nt(a, b):
    return _bdot(a, b, BNT)


bmm_nt.defvjp(lambda a, b: (_bdot(a, b, BNT), (a, b)), lambda r, g: (_bdot(g, r[1], BNN), _bdot(g, r[0], BTN)))


@jax.custom_vjp
def bmm_tn(a, b):
    return _bdot(a, b, BTN)


bmm_tn.defvjp(lambda a, b: (_bdot(a, b, BTN), (a, b)), lambda r, g: (_bdot(r[1], g, BNT), _bdot(r[0], g, BNN)))


def _unit_lower_inverse(nm):
    eye = jnp.where(_iota2(CHUNK, CHUNK, 0) == _iota2(CHUNK, CHUNK, 1), 1.0, 0.0).astype(F32)[None]
    p = eye - nm
    m = nm
    for _ in range(5):
        m = _bdotf(m, m, BNN)
        p = _bdotf(p, eye + m, BNN)
    return p


@jax.custom_vjp
def _solve2(nm, r1, r2):
    inv = _unit_lower_inverse(nm)
    return _bdotf(inv, r1, BNN), _bdotf(inv, r2, BNN)


def _solve2_fwd(nm, r1, r2):
    inv = _unit_lower_inverse(nm)
    u, w = _bdotf(inv, r1, BNN), _bdotf(inv, r2, BNN)
    return (u, w), (inv, u, w)


def _solve2_bwd(res, g):
    inv, u, w = res
    d1, d2 = _bdotf(inv, g[0], BTN), _bdotf(inv, g[1], BTN)
    return -(_bdotf(d1, u, BNT) + _bdotf(d2, w, BNT)), d1, d2


_solve2.defvjp(_solve2_fwd, _solve2_bwd)


def _gdn_chunk(q, k, v, gcc, gcr, b, gl, st):
    qn = q * lax.rsqrt(jnp.sum(q * q, axis=-1, keepdims=True) + EPS) * SCALE
    kn = k * lax.rsqrt(jnp.sum(k * k, axis=-1, keepdims=True) + EPS)
    r, c = _iota2(CHUNK, CHUNK, 0)[None], _iota2(CHUNK, CHUNK, 1)[None]
    decay = jnp.exp(jnp.where(r >= c, gcc - gcr, NEG))
    nm = jnp.where(r > c, b * bmm_nt(kn, kn) * decay, 0.0)
    eg = jnp.exp(gcc)
    u, w = _solve2(nm, v * b, kn * (b * eg))
    attn = bmm_nt(qn, kn) * decay
    v_new = u - bmm(w, st)
    o = bmm(qn * eg, st) + bmm(attn, v_new)
    st_new = st * jnp.exp(gl) + bmm_tn(kn * jnp.exp(gl - gcc), v_new)
    return o, st_new


def _heads_of(ref, rows):
    return jnp.stack([ref[rows, _head_cols(h)] for h in range(HEADS)])


def _head_cols(h):
    return slice(h * HEAD_DIM, (h + 1) * HEAD_DIM)


GDN_ROWS = 512


def _gdn_specs(s, tg, rev):
    nb = s // tg
    cpb = tg // CHUNK
    j_of = (lambda j: nb - 1 - j) if rev else (lambda j: j)
    qspec = pl.BlockSpec((tg, HW), lambda j: (j_of(j), 0))
    colspec = pl.BlockSpec((HEADS, tg, 1), lambda j: (0, j_of(j), 0))
    rowspec = pl.BlockSpec((HEADS, cpb, 1, CHUNK), lambda j: (0, j_of(j), 0, 0))
    onespec = pl.BlockSpec((HEADS, cpb, 1, 1), lambda j: (0, j_of(j), 0, 0))
    stspec = pl.BlockSpec((HEADS, cpb, HEAD_DIM, HEAD_DIM), lambda j: (0, j_of(j), 0, 0))
    return qspec, colspec, rowspec, onespec, stspec


def _gdn_fwd_call(q, k, v, gcc, gcr, bc, gl):
    s = q.shape[0]
    tg = _pick(s, GDN_ROWS, CHUNK)
    cpb = tg // CHUNK
    qspec, colspec, rowspec, onespec, stspec = _gdn_specs(s, tg, False)

    def body(q_ref, k_ref, v_ref, gcc_ref, gcr_ref, b_ref, gl_ref, o_ref, st_ref, st):
        @pl.when(pl.program_id(0) == 0)
        def _():
            st[...] = jnp.zeros_like(st)

        def chunk(ci, _):
            rows = pl.ds(pl.multiple_of(ci * CHUNK, CHUNK), CHUNK)
            s_in = st[...]
            st_ref[:, ci] = s_in
            o, s_new = _gdn_chunk(_heads_of(q_ref, rows), _heads_of(k_ref, rows), _heads_of(v_ref, rows),
                                  gcc_ref[:, rows, :], gcr_ref[:, ci], b_ref[:, rows, :], gl_ref[:, ci], s_in)
            for h in range(HEADS):
                o_ref[rows, _head_cols(h)] = o[h]
            st[...] = s_new
            return 0

        lax.fori_loop(0, cpb, chunk, 0)

    return _pcall(body, name="gdn_fwd",
                  out_shape=[jax.ShapeDtypeStruct((s, HW), F32),
                             jax.ShapeDtypeStruct((HEADS, s // CHUNK, HEAD_DIM, HEAD_DIM), F32)],
                  grid=(s // tg,), in_specs=[qspec, qspec, qspec, colspec, rowspec, colspec, onespec],
                  out_specs=[qspec, stspec], scratch_shapes=[pltpu.VMEM((HEADS, HEAD_DIM, HEAD_DIM), F32)],
                  )(q, k, v, gcc, gcr, bc, gl)


def _gdn_bwd_call(q, k, v, gcc, gcr, bc, gl, states, do):
    s = q.shape[0]
    tg = _pick(s, GDN_ROWS, CHUNK)
    cpb = tg // CHUNK
    qspec, colspec, rowspec, onespec, stspec = _gdn_specs(s, tg, True)

    def body(q_ref, k_ref, v_ref, gcc_ref, gcr_ref, b_ref, gl_ref, st_ref, do_ref,
             dq_ref, dk_ref, dv_ref, dgcc_ref, dgcr_ref, db_ref, dgl_ref, dst):
        @pl.when(pl.program_id(0) == 0)
        def _():
            dst[...] = jnp.zeros_like(dst)

        def chunk(n, _):
            ci = cpb - 1 - n
            rows = pl.ds(pl.multiple_of(ci * CHUNK, CHUNK), CHUNK)
            _, vjp = jax.vjp(_gdn_chunk, _heads_of(q_ref, rows), _heads_of(k_ref, rows), _heads_of(v_ref, rows),
                             gcc_ref[:, rows, :], gcr_ref[:, ci], b_ref[:, rows, :], gl_ref[:, ci], st_ref[:, ci])
            dq, dk, dv, dgcc, dgcr, db, dgl, ds_in = vjp((_heads_of(do_ref, rows), dst[...]))
            for h in range(HEADS):
                cols = _head_cols(h)
                dq_ref[rows, cols] = dq[h]
                dk_ref[rows, cols] = dk[h]
                dv_ref[rows, cols] = dv[h]
            dgcc_ref[:, rows, :] = dgcc
            dgcr_ref[:, ci] = dgcr
            db_ref[:, rows, :] = db
            dgl_ref[:, ci] = dgl
            dst[...] = ds_in
            return 0

        lax.fori_loop(0, cpb, chunk, 0)

    n = s // CHUNK
    return _pcall(body, name="gdn_bwd",
                  out_shape=[jax.ShapeDtypeStruct((s, HW), F32)] * 3
                  + [jax.ShapeDtypeStruct((HEADS, s, 1), F32), jax.ShapeDtypeStruct((HEADS, n, 1, CHUNK), F32),
                     jax.ShapeDtypeStruct((HEADS, s, 1), F32), jax.ShapeDtypeStruct((HEADS, n, 1, 1), F32)],
                  grid=(s // tg,),
                  in_specs=[qspec, qspec, qspec, colspec, rowspec, colspec, onespec, stspec, qspec],
                  out_specs=[qspec, qspec, qspec, colspec, rowspec, colspec, onespec],
                  scratch_shapes=[pltpu.VMEM((HEADS, HEAD_DIM, HEAD_DIM), F32)],
                  )(q, k, v, gcc, gcr, bc, gl, states, do)


@jax.custom_vjp
def gated_delta(q, k, v, gcc, gcr, bc, gl):
    return _gdn_fwd_call(q, k, v, gcc, gcr, bc, gl)[0]


def _gdn_vjp_fwd(q, k, v, gcc, gcr, bc, gl):
    o, states = _gdn_fwd_call(q, k, v, gcc, gcr, bc, gl)
    return o, (q, k, v, gcc, gcr, bc, gl, states)


gated_delta.defvjp(_gdn_vjp_fwd, lambda res, g: tuple(_gdn_bwd_call(*res, g)))


def _loss_call(y, target):
    s, d = y.shape
    tm = _pick(s, 512, SUBLANES)

    def body(y_ref, t_ref, dy_ref, loss_ref):
        @pl.when(pl.program_id(0) == 0)
        def _():
            loss_ref[...] = jnp.zeros_like(loss_ref)

        err = y_ref[...] - t_ref[...]
        dy_ref[...] = err * (1.0 / d)
        loss_ref[...] += 0.5 * jnp.sum(jnp.mean(err * err, axis=-1, keepdims=True), axis=0, keepdims=True)

    dy, part = _pcall(body, name="loss_head",
                      out_shape=[jax.ShapeDtypeStruct((s, d), F32), jax.ShapeDtypeStruct((1, 1), F32)],
                      grid=(s // tm,), in_specs=[pl.BlockSpec((tm, d), lambda i: (i, 0))] * 2,
                      out_specs=[pl.BlockSpec((tm, d), lambda i: (i, 0)), pl.BlockSpec((1, 1), lambda i: (0, 0))],
                      )(y, target)
    return part[0, 0], dy


def _cols_and_rows(a, lane0, t):
    s = a.shape[0]
    at = a[:, lane0:lane0 + HEADS].T
    return at, at[:, :, None], at.reshape(HEADS, s // t, 1, t)


def _pad_lanes(v, lane0):
    return jnp.pad(v, (lane0, LANES - lane0 - v.shape[0])).reshape(1, LANES)


def _layer(x, mem, w, ops, convs):
    s = x.shape[0]
    row = lambda v: v.reshape(1, -1)
    h = ops["rms"](x, row(w["norm_mix"]))
    fq, fk, fv, gqkv, gz, sq, sk, sv, gt, gm = proj(h, w["w_in"])

    logf, beta, gc = ops["small"](gm, _pad_lanes(w["fox_fbias"], LANE_FF), _pad_lanes(w["gdn_a_log"], LANE_GA),
                                  _pad_lanes(w["gdn_dt_bias"], LANE_GA))
    _, ccol, crow = _cols_and_rows(seq_cumsum(logf), LANE_FF, _att_tiles(s)[1])
    ya = fox_attention(ops["headnorm"](fq, row(w["fox_qnorm"])), ops["headnorm"](fk, row(w["fox_knorm"])), fv,
                       ccol, crow)
    cq, ck, cv = convs["gdn"](gqkv, w["gdn_conv"])
    gct, gcc, gcr = _cols_and_rows(gc, LANE_GA, CHUNK)
    gl = gct.reshape(HEADS, s // CHUNK, CHUNK)[:, :, CHUNK - 1].reshape(HEADS, s // CHUNK, 1, 1)
    bc = beta[:, LANE_GB:LANE_GB + HEADS].T[:, :, None]
    yb = ops["gdnpost"](gated_delta(cq, ck, cv, gcc, gcr, bc, gl), gz, row(w["gdn_onorm"]))
    yc = sb_attention(sq, sk, sv)
    gb = w["gate_bias"]
    mixed = ops["merge"](gt, matmul(ya, w["w_oa"]), matmul(yb, w["w_ob"]), matmul(yc, w["w_oc"]),
                         row(gb[:D_MODEL]), row(gb[D_MODEL:2 * D_MODEL]), row(gb[2 * D_MODEL:]))
    x = matmul_add(x, mixed, w["w_out"])
    mq = ops["headnorm"](matmul(ops["rms"](x, row(w["norm_xq"])), w["w_mq"]), row(w["mq_norm"]))
    kv = matmul(ops["rms"](mem, row(w["norm_mem"])), w["w_mkv"])
    mk = ops["headnorm"](kv[:, :HW], row(w["mk_norm"]))
    x = matmul_add(x, mem_attention(mq, mk, kv[:, HW:]), w["w_mo"])
    u = matmul(ops["rms"](x, row(w["norm_ffn"])), w["w_up"])
    act = convs["ffn"](u, w["ffn_conv"], row(w["ffn_conv_b"]))
    return matmul_add(x, act, w["w_down"])


def _forward(x, mem, layers):
    ops, convs = _make_rowops(), _make_convops()
    for w in layers:
        x = _layer(x, mem, w, ops, convs)
    return x


ANY = pl.BlockSpec(memory_space=pl.ANY)


N_PEERS = N_DEV - 1


def _ccall(body, *, name, out_shape, n_arrays):
    return pl.pallas_call(body, name=name, out_shape=out_shape, in_specs=[ANY] * n_arrays,
                          out_specs=[ANY] * n_arrays,
                          scratch_shapes=[pltpu.SemaphoreType.DMA((N_PEERS * n_arrays,)),
                                          pltpu.SemaphoreType.DMA((N_PEERS * n_arrays,)),
                                          pltpu.SemaphoreType.DMA((n_arrays,))],
                          interpret=False)


def _all_gather(name, shards):
    n = len(shards)

    def body(*refs):
        x_refs, out_refs = refs[:n], refs[n:2 * n]
        send_sems, recv_sems, local_sems = refs[2 * n:]
        x, y, c = lax.axis_index("x"), lax.axis_index("y"), lax.axis_index("c")
        me, sibling = (x, y, c), (x, y, 1 - c)
        chips = [(1 - x, y), (x, 1 - y), (1 - x, 1 - y)]

        def slot(a, px, py, pc):
            return out_refs[a].at[4 * px + 2 * py + pc]

        def copy(a, k, block, to, src=None):
            return pltpu.make_async_remote_copy(
                src_ref=slot(a, *block) if src is None else src, dst_ref=slot(a, *block),
                send_sem=send_sems.at[N_PEERS * a + k], recv_sem=recv_sems.at[N_PEERS * a + k], device_id=to,
                device_id_type=MESH)

        mine = [pltpu.make_async_copy(x_refs[a], slot(a, *me), local_sems.at[a]) for a in range(n)]
        first = []
        for a in range(n):
            first.append(copy(a, 0, me, sibling, src=x_refs[a]))
            first += [copy(a, 1 + j, me, (*chip, c), src=x_refs[a]) for j, chip in enumerate(chips)]
        for cp in mine + first:
            cp.start()
        passed = []
        for j, chip in enumerate(chips):
            for a in range(n):
                copy(a, 1 + j, (*chip, c), me).wait_recv()
                passed.append(copy(a, 4 + j, (*chip, c), sibling))
                passed[-1].start()
        for a in range(n):
            copy(a, 0, sibling, me).wait_recv()
            for j, chip in enumerate(chips):
                copy(a, 4 + j, (*chip, 1 - c), me).wait_recv()
        for cp in first + passed:
            cp.wait_send()
        for cp in mine:
            cp.wait()

    return _ccall(body, name=name, out_shape=[jax.ShapeDtypeStruct((N_DEV,) + s.shape, s.dtype) for s in shards],
                  n_arrays=n)(*shards)


def _exchange(name, parts):
    n = len(parts)

    def body(*refs):
        p_refs, out_refs = refs[:n], refs[n:2 * n]
        send_sems, recv_sems, local_sems = refs[2 * n:]
        x, y, c = lax.axis_index("x"), lax.axis_index("y"), lax.axis_index("c")
        me = 4 * x + 2 * y + c

        def peer(k):
            return (x ^ ((k >> 2) & 1), y ^ ((k >> 1) & 1), c ^ (k & 1))

        def copy(a, k, receive):
            px, py, pc = peer(k)
            theirs = 4 * px + 2 * py + pc
            return pltpu.make_async_remote_copy(
                src_ref=out_refs[a].at[theirs] if receive else p_refs[a].at[theirs],
                dst_ref=out_refs[a].at[theirs if receive else me],
                send_sem=send_sems.at[N_PEERS * a + k - 1], recv_sem=recv_sems.at[N_PEERS * a + k - 1],
                device_id=(px, py, pc), device_id_type=MESH)

        mine = [pltpu.make_async_copy(p_refs[a].at[me], out_refs[a].at[me], local_sems.at[a]) for a in range(n)]
        sends = [copy(a, k, False) for k in range(1, N_DEV) for a in range(n)]
        for cp in mine + sends:
            cp.start()
        for k in range(1, N_DEV):
            for a in range(n):
                copy(a, k, True).wait_recv()
        for cp in sends:
            cp.wait_send()
        for cp in mine:
            cp.wait()

    return _ccall(body, name=name, out_shape=[jax.ShapeDtypeStruct(p.shape, p.dtype) for p in parts],
                  n_arrays=n)(*parts)


ADAM_SLOT_BYTES = 4 * 1024 * 1024


def _adam_call(name, w, slots, m, v):
    r, n = w.shape
    rows_unit = 2 * SUBLANES
    tr = _pick(r, max(rows_unit, ADAM_SLOT_BYTES // (N_DEV * n * 4)), rows_unit)
    spec = pl.BlockSpec((tr, n), lambda i: (i, 0))

    def body(w_ref, s_ref, m_ref, v_ref, g_ref, d_ref, nm_ref, nv_ref):
        g = s_ref[0].astype(F32)
        for d in range(1, N_DEV):
            g = g + s_ref[d].astype(F32)
        nm = ADAM_B1 * m_ref[...] + (1.0 - ADAM_B1) * g
        nv = ADAM_B2 * v_ref[...] + (1.0 - ADAM_B2) * (g * g)
        m_hat = nm / (1.0 - ADAM_B1 ** ADAM_STEP)
        v_hat = nv / (1.0 - ADAM_B2 ** ADAM_STEP)
        g_ref[...] = g
        d_ref[...] = -ADAM_LR * (m_hat / (jnp.sqrt(v_hat) + ADAM_EPS) + ADAM_WD * w_ref[...])
        nm_ref[...] = nm
        nv_ref[...] = nv

    return _pcall(body, name=name, out_shape=[jax.ShapeDtypeStruct((r, n), F32)] * 4, grid=(r // tr,),
                  in_specs=[spec, pl.BlockSpec((N_DEV, tr, n), lambda i: (0, i, 0)), spec, spec],
                  out_specs=[spec] * 4)(w, slots, m, v)


def _pack_rows(flat, rows):
    return jnp.pad(flat, (0, rows * PACK_COLS - flat.shape[0])).reshape(rows, PACK_COLS)


def _regroup_in(w_in):
    cols = [w_in[:, a:b] for a, b in _IN_SRC]
    return jnp.concatenate(cols + [jnp.zeros((w_in.shape[0], N_IN_PAD - N_IN), F32)], axis=1)


def _ungroup_in(d):
    starts = {}
    off = 0
    for a, b in _IN_SRC:
        starts[a] = (off, b - a)
        off += b - a
    return jnp.concatenate([d[:, starts[a][0]:starts[a][0] + starts[a][1]] for a in sorted(starts)], axis=1)


def _full_weights(gathered):
    out = {}
    for n, g in zip(SHARDED_ORDER, gathered):
        (r, c), axis = SHARDED[n]
        out[n] = (g.reshape(r, c) if axis == 0 else g.transpose(1, 0, 2).reshape(r, c)).astype(F32)
    out["w_in"] = _regroup_in(out["w_in"])
    return out


def _for_transport(name, shard):
    return shard if name in ("gdn_conv", "ffn_conv") else shard.astype(BF16)


def _grad_parts(grads):
    parts = []
    for n in SHARDED_ORDER:
        (r, c), axis = SHARDED[n]
        g = _ungroup_in(grads[n]) if n == "w_in" else grads[n]
        if axis == 0:
            parts.append(g.reshape(N_DEV, r // N_DEV, c).astype(BF16))
        else:
            parts.append(g.reshape(r, N_DEV, c // N_DEV).transpose(1, 0, 2).astype(BF16))
    return parts


def _pack_small(vals):
    return _pack_rows(jnp.concatenate([vals[n].reshape(-1) for n in SMALL_ORDER]), SMALL_ROWS)


def _unpack_small(packed):
    flat = packed.reshape(-1)
    out, off = {}, 0
    for n in SMALL_ORDER:
        size = DEPTH * SMALL_WIDTH[n]
        out[n] = flat[off:off + size].reshape(DEPTH, SMALL_WIDTH[n])
        off += size
    return out


def kernel(x, mem, norm_mix, w_in, fox_fbias, fox_qnorm, fox_knorm, gdn_conv, gdn_a_log, gdn_dt_bias, gdn_onorm, gate_bias, w_oa, w_ob, w_oc, w_out, norm_xq, norm_mem, w_mq, w_mkv, mq_norm, mk_norm, w_mo, norm_ffn, w_up, ffn_conv, ffn_conv_b, w_down, loss_target, m_norm_mix, m_w_in, m_fox_fbias, m_fox_qnorm, m_fox_knorm, m_gdn_conv, m_gdn_a_log, m_gdn_dt_bias, m_gdn_onorm, m_gate_bias, m_w_oa, m_w_ob, m_w_oc, m_w_out, m_norm_xq, m_norm_mem, m_w_mq, m_w_mkv, m_mq_norm, m_mk_norm, m_w_mo, m_norm_ffn, m_w_up, m_ffn_conv, m_ffn_conv_b, m_w_down, v_norm_mix, v_w_in, v_fox_fbias, v_fox_qnorm, v_fox_knorm, v_gdn_conv, v_gdn_a_log, v_gdn_dt_bias, v_gdn_onorm, v_gate_bias, v_w_oa, v_w_ob, v_w_oc, v_w_out, v_norm_xq, v_norm_mem, v_w_mq, v_w_mkv, v_mq_norm, v_mk_norm, v_w_mo, v_norm_ffn, v_w_up, v_ffn_conv, v_ffn_conv_b, v_w_down):
    given = dict(locals())
    wts = {n: given[n] for n in WEIGHTS}
    mom = {n: given["m_" + n] for n in WEIGHTS}
    var = {n: given["v_" + n] for n in WEIGHTS}

    layers = []
    for l in range(DEPTH):
        full = _full_weights(_all_gather("gather_weights", [_for_transport(n, wts[n][l]) for n in SHARDED_ORDER]))
        full.update({n: wts[n][l] for n in SMALL_ORDER})
        layers.append(full)

    y, vjp = jax.vjp(lambda xx, ww: _forward(xx, mem[0], ww), x[0], layers)
    loss_part, dy = _loss_call(y, loss_target[0])
    dx, dlayers = vjp(dy)
    loss = lax.psum(loss_part, ("x", "y", "c"))

    out = {}
    per_layer = []
    for l in range(DEPTH):
        slots = _exchange("exchange_grads", _grad_parts(dlayers[l]))
        per_layer.append({n: _adam_call("adam_shard", wts[n][l], sl, mom[n][l], var[n][l])
                          for n, sl in zip(SHARDED_ORDER, slots)})
    for n in SHARDED_ORDER:
        for k, kind in enumerate(("grad_", "delta_", "new_m_", "new_v_")):
            out[kind + n] = jnp.stack([per_layer[l][n][k] for l in range(DEPTH)])
    dsmall = {n: jnp.stack([dlayers[l][n] for l in range(DEPTH)]) for n in SMALL_ORDER}
    slots = _all_gather("gather_small_grads", [_pack_small(dsmall)])[0]
    res = _adam_call("adam_small", _pack_small(wts), slots, _pack_small(mom), _pack_small(var))
    for k, kind in enumerate(("grad_", "delta_", "new_m_", "new_v_")):
        un = _unpack_small(res[k])
        for n in SMALL_ORDER:
            out[kind + n] = un[n].reshape(wts[n].shape)

    return (loss, dx[None], *[out["grad_" + n] for n in WEIGHTS], *[out["delta_" + n] for n in WEIGHTS],
            *[out["new_m_" + n] for n in WEIGHTS], *[out["new_v_" + n] for n in WEIGHTS])
```

```python
import jax
import jax.numpy as jnp
from jax import lax
from jax.experimental import pallas as pl
from jax.experimental.pallas import tpu as pltpu

F32 = jnp.float32
BF16 = jnp.bfloat16

N_DEV = 8
D_MODEL = 1024
DEPTH = 4
CHUNK = 64
EPS = 1e-6
HEADS = 4
HEAD_DIM = 128
HW = HEADS * HEAD_DIM
D_FF = 2816
N_IN = 8204
LANES = 128
SUBLANES = 8
VMEM_LIMIT = 56 * 1024 * 1024

ADAM_LR = 0.001
ADAM_B1 = 0.9
ADAM_B2 = 0.999
ADAM_EPS = 1e-08
ADAM_WD = 0.01
ADAM_STEP = 10

NEG = -1e30
MESH = pl.DeviceIdType.MESH

WEIGHTS = ['norm_mix', 'w_in', 'fox_fbias', 'fox_qnorm', 'fox_knorm', 'gdn_conv', 'gdn_a_log', 'gdn_dt_bias',
           'gdn_onorm', 'gate_bias', 'w_oa', 'w_ob', 'w_oc', 'w_out', 'norm_xq', 'norm_mem', 'w_mq', 'w_mkv',
           'mq_norm', 'mk_norm', 'w_mo', 'norm_ffn', 'w_up', 'ffn_conv', 'ffn_conv_b', 'w_down']
SHARDED = {
    'w_in': ((D_MODEL, N_IN), 0), 'gdn_conv': ((4, 3 * HW), 1), 'w_oa': ((HW, D_MODEL), 1),
    'w_ob': ((HW, D_MODEL), 1), 'w_oc': ((HW, D_MODEL), 1), 'w_out': ((D_MODEL, D_MODEL), 0),
    'w_mq': ((D_MODEL, HW), 0), 'w_mkv': ((D_MODEL, 2 * HW), 0), 'w_mo': ((HW, D_MODEL), 1),
    'w_up': ((D_MODEL, 2 * D_FF), 1), 'ffn_conv': ((3, 2 * D_FF), 1), 'w_down': ((D_FF, D_MODEL), 0),
}
SHARDED_ORDER = [n for n in WEIGHTS if n in SHARDED]
SMALL_ORDER = [n for n in WEIGHTS if n not in SHARDED]
SMALL_WIDTH = {'norm_mix': D_MODEL, 'fox_fbias': HEADS, 'fox_qnorm': HEAD_DIM, 'fox_knorm': HEAD_DIM,
               'gdn_a_log': HEADS, 'gdn_dt_bias': HEADS, 'gdn_onorm': HEAD_DIM, 'gate_bias': 3 * D_MODEL,
               'norm_xq': D_MODEL, 'norm_mem': D_MODEL, 'mq_norm': HEAD_DIM, 'mk_norm': HEAD_DIM,
               'norm_ffn': D_MODEL, 'ffn_conv_b': 2 * D_FF}
PACK_COLS = 1024


def _round_up(n, m):
    return (n + m - 1) // m * m


SMALL_ROWS = _round_up(DEPTH * sum(SMALL_WIDTH.values()), SUBLANES * PACK_COLS) // PACK_COLS

_IN_SRC = [(0, 512), (512, 1024), (1024, 1536),
           (1540, 2052), (2052, 2564), (2564, 3076),
           (3084, 3596),
           (3596, 4108), (4108, 4620), (4620, 5132),
           (5132, 8204),
           (1536, 1540), (3076, 3080), (3080, 3084)]
N_IN_PAD = 8320
LANE_FF, LANE_GB, LANE_GA = 0, 4, 8


def _pick(dim, pref, unit):
    best = None
    t = unit
    while t <= min(dim, pref):
        if dim % t == 0:
            best = t
        t += unit
    return dim if best is None else best


def _params(n_grid):
    return pltpu.CompilerParams(dimension_semantics=("arbitrary",) * n_grid, vmem_limit_bytes=VMEM_LIMIT)


def _pcall(body, *, name, out_shape, grid, in_specs, out_specs, scratch_shapes=()):
    return pl.pallas_call(body, name=name, out_shape=out_shape, grid=grid, in_specs=in_specs, out_specs=out_specs,
                          scratch_shapes=scratch_shapes, compiler_params=_params(len(grid)),
                          interpret=False)


NN = ((1,), (0,))
NT = ((1,), (1,))
TN = ((0,), (0,))


def _dot(a, b, dn):
    return lax.dot_general(a.astype(BF16), b.astype(BF16), (dn, ((), ())), preferred_element_type=F32)


def _dotf(a, b, dn):
    return lax.dot_general(a, b, (dn, ((), ())), precision=lax.Precision.HIGHEST, preferred_element_type=F32)


@jax.custom_vjp
def mm(a, b):
    return _dot(a, b, NN)


mm.defvjp(lambda a, b: (_dot(a, b, NN), (a, b)), lambda r, g: (_dot(g, r[1], NT), _dot(r[0], g, TN)))


@jax.custom_vjp
def mm_nt(a, b):
    return _dot(a, b, NT)


mm_nt.defvjp(lambda a, b: (_dot(a, b, NT), (a, b)), lambda r, g: (_dot(g, r[1], NN), _dot(g, r[0], TN)))


@jax.custom_vjp
def mm_tn(a, b):
    return _dot(a, b, TN)


mm_tn.defvjp(lambda a, b: (_dot(a, b, TN), (a, b)), lambda r, g: (_dot(r[1], g, NT), _dot(r[0], g, NN)))


@jax.custom_vjp
def tri_apply(t, x):
    return _dotf(t, x, NN)


tri_apply.defvjp(lambda t, x: (_dotf(t, x, NN), t), lambda t, g: (jnp.zeros_like(t), _dotf(t, g, TN)))


def _sigmoid(x):
    return 1.0 / (1.0 + jnp.exp(-x))


@jax.custom_vjp
def _softplus(x):
    return jnp.maximum(x, 0.0) + jnp.log(1.0 + jnp.exp(-jnp.abs(x)))


_softplus.defvjp(lambda x: (_softplus(x), x), lambda x, g: (g * _sigmoid(x),))


@jax.custom_vjp
def _silu(x):
    return x * _sigmoid(x)


def _silu_fwd(x):
    s = _sigmoid(x)
    return x * s, (x, s)


_silu.defvjp(_silu_fwd, lambda r, g: (g * r[1] * (1.0 + r[0] * (1.0 - r[1])),))


def _rms(x, g):
    return x * lax.rsqrt(jnp.mean(x * x, axis=-1, keepdims=True) + EPS) * g


def _iota2(n, m, axis):
    return lax.broadcasted_iota(jnp.int32, (n, m), axis)


def _whole(width):
    return [(0, width)]


def _split(width, n):
    w = width // n
    return [(k * w, w) for k in range(n)]


class RowOp:
    def __init__(self, name, f, in_pieces, out_pieces, tm=256):
        self.name, self.f, self.in_pieces, self.out_pieces, self.tm = name, f, in_pieces, out_pieces, tm
        op = jax.custom_vjp(self._fwd_call)
        op.defvjp(lambda *a: (self._fwd_call(*a), a), lambda res, g: self._bwd_call(res, g))
        self.op = op

    def __call__(self, *args):
        return self.op(*args)

    def _width(self, pieces):
        return max(o + w for o, w in pieces)

    def _row_specs(self, pieces_list, tm):
        return [pl.BlockSpec((tm, self._width(p)), lambda i: (i, 0)) for p in pieces_list]

    def _fwd_call(self, *args):
        nr = len(self.in_pieces)
        rows, params = args[:nr], args[nr:]
        m = rows[0].shape[0]
        tm = _pick(m, self.tm, SUBLANES)
        f, in_pieces, out_pieces = self.f, self.in_pieces, self.out_pieces
        no = len(out_pieces)

        def body(*refs):
            rin, pr, ro = refs[:nr], refs[nr:nr + len(params)], refs[nr + len(params):]
            xs = [r[:, o:o + w] for r, ps in zip(rin, in_pieces) for (o, w) in ps]
            ys = f(*xs, *[p[...] for p in pr])
            k = 0
            for r, ps in zip(ro, out_pieces):
                for (o, w) in ps:
                    r[:, o:o + w] = ys[k]
                    k += 1

        outs = _pcall(
            body, name=self.name + "_fwd",
            out_shape=[jax.ShapeDtypeStruct((m, self._width(p)), F32) for p in out_pieces],
            grid=(m // tm,),
            in_specs=self._row_specs(in_pieces, tm) + [pl.BlockSpec(p.shape, lambda i: (0, 0)) for p in params],
            out_specs=self._row_specs(out_pieces, tm),
        )(*rows, *params)
        return tuple(outs) if no > 1 else outs[0]

    def _bwd_call(self, res, g):
        nr = len(self.in_pieces)
        rows, params = res[:nr], res[nr:]
        no = len(self.out_pieces)
        gs = tuple(g) if no > 1 else (g,)
        m = rows[0].shape[0]
        tm = _pick(m, self.tm, SUBLANES)
        f, in_pieces, out_pieces = self.f, self.in_pieces, self.out_pieces
        npar = len(params)

        def body(*refs):
            rin, pr, dro = refs[:nr], refs[nr:nr + npar], refs[nr + npar:nr + npar + no]
            drin, dpr = refs[nr + npar + no:nr + npar + no + nr], refs[nr + npar + no + nr:]
            xs = [r[:, o:o + w] for r, ps in zip(rin, in_pieces) for (o, w) in ps]
            dys = [r[:, o:o + w] for r, ps in zip(dro, out_pieces) for (o, w) in ps]
            _, vjp = jax.vjp(lambda *a: tuple(f(*a)), *xs, *[p[...] for p in pr])
            grads = vjp(tuple(dys))
            k = 0
            for r, ps in zip(drin, in_pieces):
                for (o, w) in ps:
                    r[:, o:o + w] = grads[k]
                    k += 1

            @pl.when(pl.program_id(0) == 0)
            def _():
                for r in dpr:
                    r[...] = jnp.zeros_like(r)

            for j, r in enumerate(dpr):
                r[...] += grads[k + j]

        outs = _pcall(
            body, name=self.name + "_bwd",
            out_shape=[jax.ShapeDtypeStruct(r.shape, F32) for r in rows]
            + [jax.ShapeDtypeStruct(p.shape, F32) for p in params],
            grid=(m // tm,),
            in_specs=self._row_specs(in_pieces, tm) + [pl.BlockSpec(p.shape, lambda i: (0, 0)) for p in params]
            + self._row_specs(out_pieces, tm),
            out_specs=self._row_specs(in_pieces, tm) + [pl.BlockSpec(p.shape, lambda i: (0, 0)) for p in params],
        )(*rows, *params, *gs)
        return tuple(outs)


def _f_rms(x, g):
    return (_rms(x, g),)


def _f_headnorm(x0, x1, x2, x3, g):
    return tuple(_rms(x, g) for x in (x0, x1, x2, x3))


def _f_small(sm, fb, al, db):
    tm = sm.shape[0]
    logf = -_softplus(-(sm + fb))
    beta = _sigmoid(sm)
    glog = -jnp.exp(al) * _softplus(sm + db)
    r, c = _iota2(tm, tm, 0), _iota2(tm, tm, 1)
    bd = jnp.where((r >= c) & (jnp.bitwise_xor(r, c) < CHUNK), 1.0, 0.0).astype(F32)
    return logf, beta, tri_apply(bd, glog)


def _f_gdnpost(o0, o1, o2, o3, z0, z1, z2, z3, g):
    return tuple(_rms(o, g) * _silu(z) for o, z in zip((o0, o1, o2, o3), (z0, z1, z2, z3)))


def _f_merge(t0, t1, t2, a, b, c, b0, b1, b2):
    return (_sigmoid(t0 + b0) * a + _sigmoid(t1 + b1) * b + _sigmoid(t2 + b2) * c,)


def _make_rowops():
    return dict(
        rms=RowOp("rms", _f_rms, [_whole(D_MODEL)], [_whole(D_MODEL)], tm=512),
        headnorm=RowOp("headnorm", _f_headnorm, [_split(HW, HEADS)], [_split(HW, HEADS)], tm=1024),
        small=RowOp("smallprep", _f_small, [_whole(LANES)], [_whole(LANES)] * 3),
        gdnpost=RowOp("gdnpost", _f_gdnpost, [_split(HW, HEADS)] * 2, [_split(HW, HEADS)], tm=512),
        merge=RowOp("merge", _f_merge, [_split(3 * D_MODEL, 3)] + [_whole(D_MODEL)] * 3, [_whole(D_MODEL)]),
    )


def _mm_call(name, a, b, mode, c=None):
    if mode == "nn":
        (m, kc), n = a.shape, b.shape[1]
    elif mode == "nt":
        (m, kc), n = a.shape, b.shape[0]
    else:
        (kc, m), n = a.shape, b.shape[1]
    tm = _pick(m, 1408, LANES) if mode == "tn" else _pick(m, 1024, SUBLANES)
    tn = _pick(n, 1024, LANES)
    tk = _pick(kc, 1024, SUBLANES) if mode == "tn" else _pick(kc, 1536, LANES)
    dn = {"nn": NN, "nt": NT, "tn": TN}[mode]
    a_spec = {"nn": pl.BlockSpec((tm, tk), lambda i, j, k: (i, k)),
              "nt": pl.BlockSpec((tm, tk), lambda i, j, k: (i, k)),
              "tn": pl.BlockSpec((tk, tm), lambda i, j, k: (k, i))}[mode]
    b_spec = {"nn": pl.BlockSpec((tk, tn), lambda i, j, k: (k, j)),
              "nt": pl.BlockSpec((tn, tk), lambda i, j, k: (j, k)),
              "tn": pl.BlockSpec((tk, tn), lambda i, j, k: (k, j))}[mode]
    o_spec = pl.BlockSpec((tm, tn), lambda i, j, k: (i, j))
    has_c = c is not None

    def body(*refs):
        a_ref, b_ref = refs[0], refs[1]
        o_ref = refs[-1]

        @pl.when(pl.program_id(2) == 0)
        def _():
            o_ref[...] = refs[2][...] if has_c else jnp.zeros_like(o_ref)

        o_ref[...] += _dot(a_ref[...], b_ref[...], dn)

    return _pcall(body, name=name, out_shape=jax.ShapeDtypeStruct((m, n), F32), grid=(m // tm, n // tn, kc // tk),
                  in_specs=[a_spec, b_spec] + ([o_spec] if has_c else []), out_specs=o_spec,
                  )(*((a, b, c) if has_c else (a, b)))


def _b16(x):
    return x.astype(BF16)


@jax.custom_vjp
def matmul(a, w, w16):
    return _mm_call("mm_nn", _b16(a), w16, "nn")


def _matmul_fwd(a, w, w16):
    a16 = _b16(a)
    return _mm_call("mm_nn", a16, w16, "nn"), (a16, w16)


def _matmul_bwd(res, g):
    a16, w16 = res
    g16 = _b16(g)
    return _mm_call("mm_nt", g16, w16, "nt"), _mm_call("mm_tn", a16, g16, "tn"), jnp.zeros_like(w16)


matmul.defvjp(_matmul_fwd, _matmul_bwd)


@jax.custom_vjp
def matmul_add(c, a, w, w16):
    return _mm_call("mm_nn_add", _b16(a), w16, "nn", c)


def _matmul_add_fwd(c, a, w, w16):
    a16 = _b16(a)
    return _mm_call("mm_nn_add", a16, w16, "nn", c), (a16, w16)


matmul_add.defvjp(_matmul_add_fwd, lambda res, g: (g,) + _matmul_bwd(res, g))

_PROJ_GROUPS = [(0, 512), (512, 512), (1024, 512), (1536, 1536), (3072, 512), (3584, 512), (4096, 512), (4608, 512),
                (5120, 3072), (8192, 128)]


def _proj_fwd(h, w, w16):
    h16 = _b16(h)
    return tuple(_mm_call("proj_nn", h16, w16[:, s:s + n], "nn") for s, n in _PROJ_GROUPS), (h16, w16)


proj = jax.custom_vjp(lambda h, w, w16: _proj_fwd(h, w, w16)[0])


def _proj_bwd(res, gs):
    h16, w16 = res
    dh = None
    dws = []
    for (s, n), g in zip(_PROJ_GROUPS, gs):
        g16 = _b16(g)
        dh = _mm_call("proj_nt", g16, w16[:, s:s + n], "nt", dh)
        dws.append(_mm_call("proj_tn", h16, g16, "tn"))
    return dh, jnp.concatenate(dws, axis=1), jnp.zeros_like(w16)


proj.defvjp(_proj_fwd, _proj_bwd)


def _cumsum_call(x, reverse):
    s, w = x.shape
    tm = _pick(s, 256, SUBLANES)
    nb = s // tm

    def body(x_ref, o_ref, carry):
        @pl.when(pl.program_id(0) == 0)
        def _():
            carry[...] = jnp.zeros_like(carry)

        blk = x_ref[...]
        r, c = _iota2(tm, tm, 0), _iota2(tm, tm, 1)
        tri = jnp.where((r <= c) if reverse else (r >= c), 1.0, 0.0).astype(F32)
        o_ref[...] = _dotf(tri, blk, NN) + carry[...]
        carry[...] += jnp.sum(blk, axis=0, keepdims=True)

    idx = (lambda i: (nb - 1 - i, 0)) if reverse else (lambda i: (i, 0))
    return _pcall(body, name="cumsum_rev" if reverse else "cumsum", out_shape=jax.ShapeDtypeStruct((s, w), F32),
                  grid=(nb,), in_specs=[pl.BlockSpec((tm, w), idx)], out_specs=pl.BlockSpec((tm, w), idx),
                  scratch_shapes=[pltpu.VMEM((1, w), F32)])(x)


@jax.custom_vjp
def seq_cumsum(x):
    return _cumsum_call(x, False)


seq_cumsum.defvjp(lambda x: (_cumsum_call(x, False), None), lambda _, g: (_cumsum_call(g, True),))


HALO = SUBLANES


class ConvOp:
    def __init__(self, name, width, post, c_pieces, out_widths, has_bias, tm):
        self.name, self.width, self.post, self.c_pieces = name, width, post, c_pieces
        self.out_widths, self.has_bias, self.tm = out_widths, has_bias, tm
        op = jax.custom_vjp(self._fwd_call)
        op.defvjp(lambda *a: (self._fwd_call(*a), a), lambda res, g: self._bwd_call(res, g))
        self.op = op

    def __call__(self, *args):
        return self.op(*args)

    def _conv(self, i, x_ref, prev_ref, w_ref, b_ref, buf):
        tm = x_ref.shape[0]
        buf[0:HALO, :] = jnp.where(i > 0, prev_ref[...], 0.0)
        buf[HALO:HALO + tm, :] = x_ref[...]
        taps = [buf[pl.ds(HALO - (self.width - 1) + j, tm), :] for j in range(self.width)]
        c = taps[0] * w_ref[0:1, :]
        for j in range(1, self.width):
            c = c + taps[j] * w_ref[j:j + 1, :]
        if self.has_bias:
            c = c + b_ref[...]
        return c, taps

    def _fwd_call(self, x, w, *bias):
        s, ch = x.shape
        tm = _pick(s, self.tm, SUBLANES)
        r8 = tm // HALO
        has_bias, post, c_pieces = self.has_bias, self.post, self.c_pieces

        def body(*refs):
            x_ref, prev_ref, w_ref = refs[:3]
            b_ref = refs[3] if has_bias else None
            outs, buf = refs[3 + has_bias:-1], refs[-1]
            c, _ = self._conv(pl.program_id(0), x_ref, prev_ref, w_ref, b_ref, buf)
            ys = post(*[c[:, o:o + n] for o, n in c_pieces])
            for r, y in zip(outs, ys):
                r[...] = y

        outs = _pcall(
            body, name=self.name + "_fwd", out_shape=[jax.ShapeDtypeStruct((s, n), F32) for n in self.out_widths],
            grid=(s // tm,),
            in_specs=[pl.BlockSpec((tm, ch), lambda i: (i, 0)),
                      pl.BlockSpec((HALO, ch), lambda i: (jnp.maximum(i * r8 - 1, 0), 0)),
                      pl.BlockSpec(w.shape, lambda i: (0, 0))]
            + ([pl.BlockSpec((1, ch), lambda i: (0, 0))] if has_bias else []),
            out_specs=[pl.BlockSpec((tm, n), lambda i: (i, 0)) for n in self.out_widths],
            scratch_shapes=[pltpu.VMEM((tm + HALO, ch), F32)],
        )(x, x, w, *bias)
        return tuple(outs) if len(outs) > 1 else outs[0]

    def _bwd_call(self, res, g):
        x, w = res[0], res[1]
        bias = res[2:]
        gs = tuple(g) if len(self.out_widths) > 1 else (g,)
        s, ch = x.shape
        tm = _pick(s, self.tm, SUBLANES)
        r8 = tm // HALO
        nb = s // tm
        has_bias, post, c_pieces, width = self.has_bias, self.post, self.c_pieces, self.width
        ng = len(gs)

        def body1(*refs):
            x_ref, prev_ref, w_ref = refs[:3]
            b_ref = refs[3] if has_bias else None
            k = 3 + has_bias
            g_refs = refs[k:k + ng]
            dc_ref, dw_ref = refs[k + ng], refs[k + ng + 1]
            db_ref = refs[k + ng + 2] if has_bias else None
            buf = refs[-1]
            i = pl.program_id(0)
            c, taps = self._conv(i, x_ref, prev_ref, w_ref, b_ref, buf)
            _, vjp = jax.vjp(lambda *a: tuple(post(*a)), *[c[:, o:o + n] for o, n in c_pieces])
            dcs = vjp(tuple(r[...] for r in g_refs))
            for (o, n), d in zip(c_pieces, dcs):
                dc_ref[:, o:o + n] = d

            @pl.when(i == 0)
            def _():
                dw_ref[...] = jnp.zeros_like(dw_ref)
                if has_bias:
                    db_ref[...] = jnp.zeros_like(db_ref)

            dc = dc_ref[...]
            for j in range(width):
                dw_ref[j:j + 1, :] += jnp.sum(dc * taps[j], axis=0, keepdims=True)
            if has_bias:
                db_ref[...] += jnp.sum(dc, axis=0, keepdims=True)

        outs1 = _pcall(
            body1, name=self.name + "_bwd_act",
            out_shape=[jax.ShapeDtypeStruct((s, ch), F32), jax.ShapeDtypeStruct(w.shape, F32)]
            + ([jax.ShapeDtypeStruct((1, ch), F32)] if has_bias else []),
            grid=(nb,),
            in_specs=[pl.BlockSpec((tm, ch), lambda i: (i, 0)),
                      pl.BlockSpec((HALO, ch), lambda i: (jnp.maximum(i * r8 - 1, 0), 0)),
                      pl.BlockSpec(w.shape, lambda i: (0, 0))]
            + ([pl.BlockSpec((1, ch), lambda i: (0, 0))] if has_bias else [])
            + [pl.BlockSpec((tm, n), lambda i: (i, 0)) for n in self.out_widths],
            out_specs=[pl.BlockSpec((tm, ch), lambda i: (i, 0)), pl.BlockSpec(w.shape, lambda i: (0, 0))]
            + ([pl.BlockSpec((1, ch), lambda i: (0, 0))] if has_bias else []),
            scratch_shapes=[pltpu.VMEM((tm + HALO, ch), F32)],
        )(x, x, w, *bias, *gs)
        dc, dw = outs1[0], outs1[1]

        def body2(dc_ref, next_ref, w_ref, dx_ref, buf):
            i = pl.program_id(0)
            buf[0:tm, :] = dc_ref[...]
            buf[tm:tm + HALO, :] = jnp.where(i < nb - 1, next_ref[...], 0.0)
            dx = buf[pl.ds(width - 1, tm), :] * w_ref[0:1, :]
            for j in range(1, width):
                dx = dx + buf[pl.ds(width - 1 - j, tm), :] * w_ref[j:j + 1, :]
            dx_ref[...] = dx

        dx = _pcall(
            body2, name=self.name + "_bwd_in", out_shape=jax.ShapeDtypeStruct((s, ch), F32), grid=(nb,),
            in_specs=[pl.BlockSpec((tm, ch), lambda i: (i, 0)),
                      pl.BlockSpec((HALO, ch), lambda i: (jnp.minimum((i + 1) * r8, s // HALO - 1), 0)),
                      pl.BlockSpec(w.shape, lambda i: (0, 0))],
            out_specs=pl.BlockSpec((tm, ch), lambda i: (i, 0)),
            scratch_shapes=[pltpu.VMEM((tm + HALO, ch), F32)],
        )(dc, dc, w)
        return (dx, dw) + ((outs1[2],) if has_bias else ())


def _make_convops():
    return dict(
        gdn=ConvOp("gdnconv", 4, lambda q, k, v: (_silu(q), _silu(k), _silu(v)), _split(3 * HW, 3), [HW] * 3,
                   False, 256),
        ffn=ConvOp("ffnconv", 3, lambda a, b: (_silu(a) * b,), _split(2 * D_FF, 2), [D_FF], True, 128),
    )


ATT_Q = 512
ATT_K = 256
SCALE = HEAD_DIM ** -0.5


def _att_tiles(s):
    tk = _pick(s, ATT_K, LANES)
    tq = _pick(s, ATT_Q, tk)
    return tq, tk


def _att_specs(s, tq, tk):
    qspec = pl.BlockSpec((tq, HEAD_DIM), lambda h, i: (i, h))
    kspec = pl.BlockSpec((s, HEAD_DIM), lambda h, i: (0, h))
    colspec = pl.BlockSpec((None, tq, 1), lambda h, i: (h, i, 0))
    rowspec = pl.BlockSpec((None, s // tk, 1, tk), lambda h, i: (h, 0, 0, 0))
    return qspec, kspec, colspec, rowspec


def _krows(kb, tk):
    return pl.ds(pl.multiple_of(kb * tk, tk), tk)


def _stage_bf16(i, pairs):
    @pl.when(i == 0)
    def _():
        for src, dst in pairs:
            dst[...] = src[...].astype(BF16)


ATT_STRIP = 32


def _strips(tq):
    return [slice(r, r + ATT_STRIP) for r in range(0, tq, ATT_STRIP)]


def _visible(i, kb, rs, tq, tk, strict):
    rows = i * tq + rs.start + _iota2(ATT_STRIP, tk, 0)
    cols = kb * tk + _iota2(ATT_STRIP, tk, 1)
    return (cols < rows) if strict else (cols <= rows)


def _visible_block(i, kb, tq, tk):
    return kb * tk + _iota2(tq, tk, 1) <= i * tq + _iota2(tq, tk, 0)


def _blocks(i, ratio, blk, reverse=False):
    def full(n, carry):
        blk(i * ratio - 1 - n if reverse else n, False)
        return carry

    if reverse:
        for j in reversed(range(ratio)):
            blk(i * ratio + j, True)
    lax.fori_loop(0, i * ratio, full, 0)
    if not reverse:
        for j in range(ratio):
            blk(i * ratio + j, True)


def _vm(shape, dtype):
    return pltpu.VMEM(shape, dtype)


def _fox_fwd_call(q, k, v, ccol, crow):
    s = q.shape[0]
    tq, tk = _att_tiles(s)
    qspec, kspec, colspec, rowspec = _att_specs(s, tq, tk)

    def body(q_ref, k_ref, v_ref, cq_ref, ck_ref, o_ref, lse_ref, k16, v16):
        i = pl.program_id(1)
        ratio = tq // tk
        _stage_bf16(i, [(k_ref, k16), (v_ref, v16)])
        qb = q_ref[...].astype(BF16)
        cq = cq_ref[...]

        def blk(kb, carry, masked):
            m, l, acc = carry
            sc = _dot(qb, k16[_krows(kb, tk), :], NT) * SCALE + (cq - ck_ref[kb])
            if masked:
                sc = jnp.where(_visible_block(i, kb, tq, tk), sc, NEG)
            m_new = jnp.maximum(m, jnp.max(sc, axis=-1, keepdims=True))
            alpha = jnp.exp(m - m_new)
            p = jnp.exp(sc - m_new)
            return (m_new, alpha * l + jnp.sum(p, axis=-1, keepdims=True),
                    alpha * acc + _dot(p, v16[_krows(kb, tk), :], NN))

        carry = (jnp.full((tq, 1), NEG, F32), jnp.zeros((tq, 1), F32), jnp.zeros((tq, HEAD_DIM), F32))
        carry = lax.fori_loop(0, i * ratio, lambda kb, c: blk(kb, c, False), carry)
        for j in range(ratio):
            carry = blk(i * ratio + j, carry, True)
        m, l, acc = carry
        o_ref[...] = acc / l
        lse_ref[...] = m + jnp.log(l)

    return _pcall(body, name="fox_fwd",
                  out_shape=[jax.ShapeDtypeStruct((s, HW), F32), jax.ShapeDtypeStruct((HEADS, s, 1), F32)],
                  grid=(HEADS, s // tq), in_specs=[qspec, kspec, kspec, colspec, rowspec],
                  out_specs=[qspec, colspec],
                  scratch_shapes=[_vm((s, HEAD_DIM), BF16), _vm((s, HEAD_DIM), BF16)])(q, k, v, ccol, crow)


def _fox_bwd_call(q, k, v, ccol, crow, o, lse, do):
    s = q.shape[0]
    tq, tk = _att_tiles(s)
    ratio = tq // tk
    qspec, kspec, colspec, rowspec = _att_specs(s, tq, tk)

    def body(q_ref, k_ref, v_ref, cq_ref, ck_ref, o_ref, lse_ref, do_ref, dq_ref, dk_ref, dv_ref, dcq_ref, dck_ref,
             k16, v16):
        i = pl.program_id(1)
        _stage_bf16(i, [(k_ref, k16), (v_ref, v16)])

        @pl.when(i == 0)
        def _():
            dk_ref[...] = jnp.zeros_like(dk_ref)
            dv_ref[...] = jnp.zeros_like(dv_ref)
            dck_ref[...] = jnp.zeros_like(dck_ref)

        qb = q_ref[...].astype(BF16)
        dob = do_ref[...].astype(BF16)
        cq, lse = cq_ref[...], lse_ref[...]
        dl = jnp.sum(do_ref[...] * o_ref[...], axis=-1, keepdims=True)

        def blk(kb, carry, masked):
            dq, dcq = carry
            rows = _krows(kb, tk)
            kk, vv = k16[rows, :], v16[rows, :]
            sc = _dot(qb, kk, NT) * SCALE + (cq - ck_ref[kb])
            p = jnp.exp(sc - lse)
            if masked:
                p = jnp.where(_visible_block(i, kb, tq, tk), p, 0.0)
            dv_ref[rows, :] += _dot(p, dob, TN)
            ds = p * (_dot(dob, vv, NT) - dl)
            dk_ref[rows, :] += _dot(ds, qb, TN) * SCALE
            dck_ref[kb] += -jnp.sum(ds, axis=0, keepdims=True)
            return dq + _dot(ds, kk, NN) * SCALE, dcq + jnp.sum(ds, axis=-1, keepdims=True)

        carry = (jnp.zeros((tq, HEAD_DIM), F32), jnp.zeros((tq, 1), F32))
        carry = lax.fori_loop(0, i * ratio, lambda kb, c: blk(kb, c, False), carry)
        for j in range(ratio):
            carry = blk(i * ratio + j, carry, True)
        dq_ref[...] = carry[0]
        dcq_ref[...] = carry[1]

    return _pcall(body, name="fox_bwd",
                  out_shape=[jax.ShapeDtypeStruct((s, HW), F32)] * 3
                  + [jax.ShapeDtypeStruct((HEADS, s, 1), F32), jax.ShapeDtypeStruct((HEADS, s // tk, 1, tk), F32)],
                  grid=(HEADS, s // tq),
                  in_specs=[qspec, kspec, kspec, colspec, rowspec, qspec, colspec, qspec],
                  out_specs=[qspec, kspec, kspec, colspec, rowspec],
                  scratch_shapes=[_vm((s, HEAD_DIM), BF16), _vm((s, HEAD_DIM), BF16)],
                  )(q, k, v, ccol, crow, o, lse, do)


@jax.custom_vjp
def fox_attention(q, k, v, ccol, crow):
    return _fox_fwd_call(q, k, v, ccol, crow)[0]


def _fox_vjp_fwd(q, k, v, ccol, crow):
    o, lse = _fox_fwd_call(q, k, v, ccol, crow)
    return o, (q, k, v, ccol, crow, o, lse)


fox_attention.defvjp(_fox_vjp_fwd, lambda res, g: tuple(_fox_bwd_call(*res, g)))


def _sb_fwd_call(q, k, v):
    s = q.shape[0]
    tq, tk = _att_tiles(s)
    ratio = tq // tk
    qspec, kspec, colspec, _ = _att_specs(s, tq, tk)

    def body(q_ref, k_ref, v_ref, o_ref, tot_ref, k16, v16, q16, s_scr, w_scr, lk16, a16, run, acc):
        i = pl.program_id(1)
        _stage_bf16(i, [(k_ref, k16), (v_ref, v16)])
        q16[...] = q_ref[...].astype(BF16)
        run[...] = jnp.zeros_like(run)
        acc[...] = jnp.zeros_like(acc)
        suffix = jnp.where(_iota2(tk, tk, 0) >= _iota2(tk, tk, 1), 1.0, 0.0).astype(BF16)

        def blk(kb, masked):
            rows = _krows(kb, tk)
            s_scr[...] = _dot(q16[...], k16[rows, :], NT)
            for rs in _strips(tq):
                lk = -_softplus(s_scr[rs, :] * SCALE)
                if masked:
                    lk = jnp.where(_visible(i, kb, rs, tq, tk, True), lk, 0.0)
                lk16[rs, :] = lk.astype(BF16)
            w_scr[...] = _dot(lk16[...], suffix, NN)
            for rs in _strips(tq):
                a = jnp.exp(s_scr[rs, :] * SCALE + w_scr[rs, :] + run[rs, :])
                if masked:
                    a = jnp.where(_visible(i, kb, rs, tq, tk, True), a, 0.0)
                a16[rs, :] = a.astype(BF16)
                run[rs, :] += w_scr[rs, 0:1]
            acc[...] += _dot(a16[...], v16[rows, :], NN)

        _blocks(i, ratio, blk, reverse=True)
        o_ref[...] = acc[...]
        tot_ref[...] = run[...]

    return _pcall(body, name="sb_fwd",
                  out_shape=[jax.ShapeDtypeStruct((s, HW), F32), jax.ShapeDtypeStruct((HEADS, s, 1), F32)],
                  grid=(HEADS, s // tq), in_specs=[qspec, kspec, kspec], out_specs=[qspec, colspec],
                  scratch_shapes=[_vm((s, HEAD_DIM), BF16), _vm((s, HEAD_DIM), BF16), _vm((tq, HEAD_DIM), BF16),
                                  _vm((tq, tk), F32), _vm((tq, tk), F32), _vm((tq, tk), BF16), _vm((tq, tk), BF16),
                                  _vm((tq, 1), F32), _vm((tq, HEAD_DIM), F32)])(q, k, v)


def _sb_bwd_call(q, k, v, tot, do):
    s = q.shape[0]
    tq, tk = _att_tiles(s)
    ratio = tq // tk
    qspec, kspec, colspec, _ = _att_specs(s, tq, tk)

    def body(q_ref, k_ref, v_ref, tot_ref, do_ref, dq_ref, dk_ref, dv_ref, k16, v16, q16, do16, s_scr, e_scr, w_scr,
             lz16, a16, e16, left, esum):
        i = pl.program_id(1)
        _stage_bf16(i, [(k_ref, k16), (v_ref, v16)])

        @pl.when(i == 0)
        def _():
            dk_ref[...] = jnp.zeros_like(dk_ref)
            dv_ref[...] = jnp.zeros_like(dv_ref)

        q16[...] = q_ref[...].astype(BF16)
        do16[...] = do_ref[...].astype(BF16)
        left[...] = jnp.zeros_like(left)
        esum[...] = jnp.zeros_like(esum)
        dq_ref[...] = jnp.zeros_like(dq_ref)
        prefix = jnp.where(_iota2(tk, tk, 0) <= _iota2(tk, tk, 1), 1.0, 0.0).astype(BF16)

        def blk(kb, masked):
            rows = _krows(kb, tk)
            s_scr[...] = _dot(q16[...], k16[rows, :], NT)
            e_scr[...] = _dot(do16[...], v16[rows, :], NT)
            for rs in _strips(tq):
                lk = -_softplus(s_scr[rs, :] * SCALE)
                if masked:
                    lk = jnp.where(_visible(i, kb, rs, tq, tk, True), lk, 0.0)
                lz16[rs, :] = lk.astype(BF16)
            w_scr[...] = _dot(lz16[...], prefix, NN)
            for rs in _strips(tq):
                rc = (tot_ref[rs, :] - left[rs, :]) - (w_scr[rs, :] - lz16[rs, :].astype(F32))
                a = jnp.exp(s_scr[rs, :] * SCALE + rc)
                if masked:
                    a = jnp.where(_visible(i, kb, rs, tq, tk, True), a, 0.0)
                e = a * e_scr[rs, :]
                a16[rs, :] = a.astype(BF16)
                e16[rs, :] = e.astype(BF16)
                e_scr[rs, :] = e
                left[rs, :] += w_scr[rs, tk - 1:tk]
            w_scr[...] = _dot(e16[...], prefix, NN)
            for rs in _strips(tq):
                dz = e_scr[rs, :] - _sigmoid(s_scr[rs, :] * SCALE) * (esum[rs, :] + w_scr[rs, :])
                if masked:
                    dz = jnp.where(_visible(i, kb, rs, tq, tk, True), dz, 0.0)
                lz16[rs, :] = dz.astype(BF16)
                esum[rs, :] += w_scr[rs, tk - 1:tk]
            dv_ref[rows, :] += _dot(a16[...], do16[...], TN)
            dk_ref[rows, :] += _dot(lz16[...], q16[...], TN) * SCALE
            dq_ref[...] += _dot(lz16[...], k16[rows, :], NN) * SCALE

        _blocks(i, ratio, blk)

    return _pcall(body, name="sb_bwd", out_shape=[jax.ShapeDtypeStruct((s, HW), F32)] * 3, grid=(HEADS, s // tq),
                  in_specs=[qspec, kspec, kspec, colspec, qspec], out_specs=[qspec, kspec, kspec],
                  scratch_shapes=[_vm((s, HEAD_DIM), BF16), _vm((s, HEAD_DIM), BF16), _vm((tq, HEAD_DIM), BF16),
                                  _vm((tq, HEAD_DIM), BF16), _vm((tq, tk), F32), _vm((tq, tk), F32),
                                  _vm((tq, tk), F32), _vm((tq, tk), BF16), _vm((tq, tk), BF16), _vm((tq, tk), BF16),
                                  _vm((tq, 1), F32), _vm((tq, 1), F32)])(q, k, v, tot, do)


@jax.custom_vjp
def sb_attention(q, k, v):
    return _sb_fwd_call(q, k, v)[0]


def _sb_vjp_fwd(q, k, v):
    o, tot = _sb_fwd_call(q, k, v)
    return o, (q, k, v, tot)


sb_attention.defvjp(_sb_vjp_fwd, lambda res, g: tuple(_sb_bwd_call(*res, g)))


def _mem_specs(s, nk, t):
    return (pl.BlockSpec((t, HEAD_DIM), lambda h, i: (i, h)), pl.BlockSpec((nk, HEAD_DIM), lambda h, i: (0, h)))


def _mem_probs(qb, kk):
    sc = _dot(qb, kk, NT) * SCALE
    p = jnp.exp(sc - jnp.max(sc, axis=-1, keepdims=True))
    return p / jnp.sum(p, axis=-1, keepdims=True)


def _mem_fwd_call(q, k, v):
    s, nk = q.shape[0], k.shape[0]
    t = _pick(s, 512, SUBLANES)
    qspec, kspec = _mem_specs(s, nk, t)

    def body(q_ref, k_ref, v_ref, o_ref):
        o_ref[...] = _dot(_mem_probs(q_ref[...].astype(BF16), k_ref[...]), v_ref[...], NN)

    return _pcall(body, name="mem_fwd", out_shape=jax.ShapeDtypeStruct((s, HW), F32), grid=(HEADS, s // t),
                  in_specs=[qspec, kspec, kspec], out_specs=qspec)(q, k, v)


def _mem_bwd_call(q, k, v, do):
    s, nk = q.shape[0], k.shape[0]
    t = _pick(s, 512, SUBLANES)
    qspec, kspec = _mem_specs(s, nk, t)

    def body(q_ref, k_ref, v_ref, do_ref, dq_ref, dk_ref, dv_ref):
        @pl.when(pl.program_id(1) == 0)
        def _():
            dk_ref[...] = jnp.zeros_like(dk_ref)
            dv_ref[...] = jnp.zeros_like(dv_ref)

        qb = q_ref[...].astype(BF16)
        dob = do_ref[...].astype(BF16)
        p = _mem_probs(qb, k_ref[...])
        dv_ref[...] += _dot(p, dob, TN)
        dp = _dot(dob, v_ref[...], NT)
        ds = p * (dp - jnp.sum(p * dp, axis=-1, keepdims=True))
        dq_ref[...] = _dot(ds, k_ref[...], NN) * SCALE
        dk_ref[...] += _dot(ds, qb, TN) * SCALE

    return _pcall(body, name="mem_bwd",
                  out_shape=[jax.ShapeDtypeStruct((s, HW), F32)] + [jax.ShapeDtypeStruct((nk, HW), F32)] * 2,
                  grid=(HEADS, s // t), in_specs=[qspec, kspec, kspec, qspec],
                  out_specs=[qspec, kspec, kspec])(q, k, v, do)


@jax.custom_vjp
def mem_attention(q, k, v):
    return _mem_fwd_call(q, k, v)


mem_attention.defvjp(lambda q, k, v: (_mem_fwd_call(q, k, v), (q, k, v)),
                     lambda res, g: tuple(_mem_bwd_call(*res, g)))


BNN = (((2,), (1,)), ((0,), (0,)))
BNT = (((2,), (2,)), ((0,), (0,)))
BTN = (((1,), (1,)), ((0,), (0,)))


def _bdot(a, b, dn):
    return lax.dot_general(a.astype(BF16), b.astype(BF16), dn, preferred_element_type=F32)


def _bdotf(a, b, dn):
    return lax.dot_general(a, b, dn, precision=lax.Precision.HIGHEST, preferred_element_type=F32)


@jax.custom_vjp
def bmm(a, b):
    return _bdot(a, b, BNN)


bmm.defvjp(lambda a, b: (_bdot(a, b, BNN), (a, b)), lambda r, g: (_bdot(g, r[1], BNT), _bdot(r[0], g, BTN)))


@jax.custom_vjp
def bmm_nt(a, b):
    return _bdot(a, b, BNT)


bmm_nt.defvjp(lambda a, b: (_bdot(a, b, BNT), (a, b)), lambda r, g: (_bdot(g, r[1], BNN), _bdot(g, r[0], BTN)))


@jax.custom_vjp
def bmm_tn(a, b):
    return _bdot(a, b, BTN)


bmm_tn.defvjp(lambda a, b: (_bdot(a, b, BTN), (a, b)), lambda r, g: (_bdot(r[1], g, BNT), _bdot(r[0], g, BNN)))


def _unit_lower_inverse(nm):
    eye = jnp.where(_iota2(CHUNK, CHUNK, 0) == _iota2(CHUNK, CHUNK, 1), 1.0, 0.0).astype(F32)[None]
    p = eye - nm
    m = nm
    for _ in range(5):
        m = _bdotf(m, m, BNN)
        p = _bdotf(p, eye + m, BNN)
    return p


@jax.custom_vjp
def _solve2(nm, r1, r2):
    inv = _unit_lower_inverse(nm)
    return _bdotf(inv, r1, BNN), _bdotf(inv, r2, BNN)


def _solve2_fwd(nm, r1, r2):
    inv = _unit_lower_inverse(nm)
    u, w = _bdotf(inv, r1, BNN), _bdotf(inv, r2, BNN)
    return (u, w), (inv, u, w)


def _solve2_bwd(res, g):
    inv, u, w = res
    d1, d2 = _bdotf(inv, g[0], BTN), _bdotf(inv, g[1], BTN)
    return -(_bdotf(d1, u, BNT) + _bdotf(d2, w, BNT)), d1, d2


_solve2.defvjp(_solve2_fwd, _solve2_bwd)


def _gdn_chunk(q, k, v, gcc, gcr, b, gl, st):
    qn = q * lax.rsqrt(jnp.sum(q * q, axis=-1, keepdims=True) + EPS) * SCALE
    kn = k * lax.rsqrt(jnp.sum(k * k, axis=-1, keepdims=True) + EPS)
    r, c = _iota2(CHUNK, CHUNK, 0)[None], _iota2(CHUNK, CHUNK, 1)[None]
    decay = jnp.exp(jnp.where(r >= c, gcc - gcr, NEG))
    nm = jnp.where(r > c, b * bmm_nt(kn, kn) * decay, 0.0)
    eg = jnp.exp(gcc)
    u, w = _solve2(nm, v * b, kn * (b * eg))
    attn = bmm_nt(qn, kn) * decay
    v_new = u - bmm(w, st)
    o = bmm(qn * eg, st) + bmm(attn, v_new)
    st_new = st * jnp.exp(gl) + bmm_tn(kn * jnp.exp(gl - gcc), v_new)
    return o, st_new


def _heads_of(ref, rows):
    return jnp.stack([ref[rows, _head_cols(h)] for h in range(HEADS)])


def _head_cols(h):
    return slice(h * HEAD_DIM, (h + 1) * HEAD_DIM)


GDN_ROWS = 512


def _gdn_specs(s, tg, rev):
    nb = s // tg
    cpb = tg // CHUNK
    j_of = (lambda j: nb - 1 - j) if rev else (lambda j: j)
    qspec = pl.BlockSpec((tg, HW), lambda j: (j_of(j), 0))
    colspec = pl.BlockSpec((HEADS, tg, 1), lambda j: (0, j_of(j), 0))
    rowspec = pl.BlockSpec((HEADS, cpb, 1, CHUNK), lambda j: (0, j_of(j), 0, 0))
    onespec = pl.BlockSpec((HEADS, cpb, 1, 1), lambda j: (0, j_of(j), 0, 0))
    stspec = pl.BlockSpec((HEADS, cpb, HEAD_DIM, HEAD_DIM), lambda j: (0, j_of(j), 0, 0))
    return qspec, colspec, rowspec, onespec, stspec


def _gdn_fwd_call(q, k, v, gcc, gcr, bc, gl):
    s = q.shape[0]
    tg = _pick(s, GDN_ROWS, CHUNK)
    cpb = tg // CHUNK
    qspec, colspec, rowspec, onespec, stspec = _gdn_specs(s, tg, False)

    def body(q_ref, k_ref, v_ref, gcc_ref, gcr_ref, b_ref, gl_ref, o_ref, st_ref, st):
        @pl.when(pl.program_id(0) == 0)
        def _():
            st[...] = jnp.zeros_like(st)

        def chunk(ci, _):
            rows = pl.ds(pl.multiple_of(ci * CHUNK, CHUNK), CHUNK)
            s_in = st[...]
            st_ref[:, ci] = s_in
            o, s_new = _gdn_chunk(_heads_of(q_ref, rows), _heads_of(k_ref, rows), _heads_of(v_ref, rows),
                                  gcc_ref[:, rows, :], gcr_ref[:, ci], b_ref[:, rows, :], gl_ref[:, ci], s_in)
            for h in range(HEADS):
                o_ref[rows, _head_cols(h)] = o[h]
            st[...] = s_new
            return 0

        lax.fori_loop(0, cpb, chunk, 0)

    return _pcall(body, name="gdn_fwd",
                  out_shape=[jax.ShapeDtypeStruct((s, HW), F32),
                             jax.ShapeDtypeStruct((HEADS, s // CHUNK, HEAD_DIM, HEAD_DIM), F32)],
                  grid=(s // tg,), in_specs=[qspec, qspec, qspec, colspec, rowspec, colspec, onespec],
                  out_specs=[qspec, stspec], scratch_shapes=[pltpu.VMEM((HEADS, HEAD_DIM, HEAD_DIM), F32)],
                  )(q, k, v, gcc, gcr, bc, gl)


def _gdn_bwd_call(q, k, v, gcc, gcr, bc, gl, states, do):
    s = q.shape[0]
    tg = _pick(s, GDN_ROWS, CHUNK)
    cpb = tg // CHUNK
    qspec, colspec, rowspec, onespec, stspec = _gdn_specs(s, tg, True)

    def body(q_ref, k_ref, v_ref, gcc_ref, gcr_ref, b_ref, gl_ref, st_ref, do_ref,
             dq_ref, dk_ref, dv_ref, dgcc_ref, dgcr_ref, db_ref, dgl_ref, dst):
        @pl.when(pl.program_id(0) == 0)
        def _():
            dst[...] = jnp.zeros_like(dst)

        def chunk(n, _):
            ci = cpb - 1 - n
            rows = pl.ds(pl.multiple_of(ci * CHUNK, CHUNK), CHUNK)
            _, vjp = jax.vjp(_gdn_chunk, _heads_of(q_ref, rows), _heads_of(k_ref, rows), _heads_of(v_ref, rows),
                             gcc_ref[:, rows, :], gcr_ref[:, ci], b_ref[:, rows, :], gl_ref[:, ci], st_ref[:, ci])
            dq, dk, dv, dgcc, dgcr, db, dgl, ds_in = vjp((_heads_of(do_ref, rows), dst[...]))
            for h in range(HEADS):
                cols = _head_cols(h)
                dq_ref[rows, cols] = dq[h]
                dk_ref[rows, cols] = dk[h]
                dv_ref[rows, cols] = dv[h]
            dgcc_ref[:, rows, :] = dgcc
            dgcr_ref[:, ci] = dgcr
            db_ref[:, rows, :] = db
            dgl_ref[:, ci] = dgl
            dst[...] = ds_in
            return 0

        lax.fori_loop(0, cpb, chunk, 0)

    n = s // CHUNK
    return _pcall(body, name="gdn_bwd",
                  out_shape=[jax.ShapeDtypeStruct((s, HW), F32)] * 3
                  + [jax.ShapeDtypeStruct((HEADS, s, 1), F32), jax.ShapeDtypeStruct((HEADS, n, 1, CHUNK), F32),
                     jax.ShapeDtypeStruct((HEADS, s, 1), F32), jax.ShapeDtypeStruct((HEADS, n, 1, 1), F32)],
                  grid=(s // tg,),
                  in_specs=[qspec, qspec, qspec, colspec, rowspec, colspec, onespec, stspec, qspec],
                  out_specs=[qspec, qspec, qspec, colspec, rowspec, colspec, onespec],
                  scratch_shapes=[pltpu.VMEM((HEADS, HEAD_DIM, HEAD_DIM), F32)],
                  )(q, k, v, gcc, gcr, bc, gl, states, do)


@jax.custom_vjp
def gated_delta(q, k, v, gcc, gcr, bc, gl):
    return _gdn_fwd_call(q, k, v, gcc, gcr, bc, gl)[0]


def _gdn_vjp_fwd(q, k, v, gcc, gcr, bc, gl):
    o, states = _gdn_fwd_call(q, k, v, gcc, gcr, bc, gl)
    return o, (q, k, v, gcc, gcr, bc, gl, states)


gated_delta.defvjp(_gdn_vjp_fwd, lambda res, g: tuple(_gdn_bwd_call(*res, g)))


def _loss_call(y, target):
    s, d = y.shape
    tm = _pick(s, 512, SUBLANES)

    def body(y_ref, t_ref, dy_ref, loss_ref):
        @pl.when(pl.program_id(0) == 0)
        def _():
            loss_ref[...] = jnp.zeros_like(loss_ref)

        err = y_ref[...] - t_ref[...]
        dy_ref[...] = err * (1.0 / d)
        loss_ref[...] += 0.5 * jnp.sum(jnp.mean(err * err, axis=-1, keepdims=True), axis=0, keepdims=True)

    dy, part = _pcall(body, name="loss_head",
                      out_shape=[jax.ShapeDtypeStruct((s, d), F32), jax.ShapeDtypeStruct((1, 1), F32)],
                      grid=(s // tm,), in_specs=[pl.BlockSpec((tm, d), lambda i: (i, 0))] * 2,
                      out_specs=[pl.BlockSpec((tm, d), lambda i: (i, 0)), pl.BlockSpec((1, 1), lambda i: (0, 0))],
                      )(y, target)
    return part[0, 0], dy


def _cols_and_rows(a, lane0, t):
    s = a.shape[0]
    at = a[:, lane0:lane0 + HEADS].T
    return at, at[:, :, None], at.reshape(HEADS, s // t, 1, t)


def _pad_lanes(v, lane0):
    return jnp.pad(v, (lane0, LANES - lane0 - v.shape[0])).reshape(1, LANES)


def _layer(x, mem, w, w16, ops, convs):
    s = x.shape[0]
    row = lambda v: v.reshape(1, -1)
    mw = lambda n: (w[n], w16[n])
    h = ops["rms"](x, row(w["norm_mix"]))
    fq, fk, fv, gqkv, gz, sq, sk, sv, gt, gm = proj(h, *mw("w_in"))

    logf, beta, gc = ops["small"](gm, _pad_lanes(w["fox_fbias"], LANE_FF), _pad_lanes(w["gdn_a_log"], LANE_GA),
                                  _pad_lanes(w["gdn_dt_bias"], LANE_GA))
    _, ccol, crow = _cols_and_rows(seq_cumsum(logf), LANE_FF, _att_tiles(s)[1])
    ya = fox_attention(ops["headnorm"](fq, row(w["fox_qnorm"])), ops["headnorm"](fk, row(w["fox_knorm"])), fv,
                       ccol, crow)
    cq, ck, cv = convs["gdn"](gqkv, w["gdn_conv"])
    gct, gcc, gcr = _cols_and_rows(gc, LANE_GA, CHUNK)
    gl = gct.reshape(HEADS, s // CHUNK, CHUNK)[:, :, CHUNK - 1].reshape(HEADS, s // CHUNK, 1, 1)
    bc = beta[:, LANE_GB:LANE_GB + HEADS].T[:, :, None]
    yb = ops["gdnpost"](gated_delta(cq, ck, cv, gcc, gcr, bc, gl), gz, row(w["gdn_onorm"]))
    yc = sb_attention(sq, sk, sv)
    gb = w["gate_bias"]
    mixed = ops["merge"](gt, matmul(ya, *mw("w_oa")), matmul(yb, *mw("w_ob")), matmul(yc, *mw("w_oc")),
                         row(gb[:D_MODEL]), row(gb[D_MODEL:2 * D_MODEL]), row(gb[2 * D_MODEL:]))
    x = matmul_add(x, mixed, *mw("w_out"))
    mq = ops["headnorm"](matmul(ops["rms"](x, row(w["norm_xq"])), *mw("w_mq")), row(w["mq_norm"]))
    kv = matmul(ops["rms"](mem, row(w["norm_mem"])), *mw("w_mkv"))
    mk = ops["headnorm"](kv[:, :HW], row(w["mk_norm"]))
    x = matmul_add(x, mem_attention(mq, mk, kv[:, HW:]), *mw("w_mo"))
    u = matmul(ops["rms"](x, row(w["norm_ffn"])), *mw("w_up"))
    act = convs["ffn"](u, w["ffn_conv"], row(w["ffn_conv_b"]))
    return matmul_add(x, act, *mw("w_down"))


def _forward(x, mem, layers, layers16):
    ops, convs = _make_rowops(), _make_convops()
    for w, w16 in zip(layers, layers16):
        x = _layer(x, mem, w, w16, ops, convs)
    return x


ANY = pl.BlockSpec(memory_space=pl.ANY)


N_PEERS = N_DEV - 1


def _ccall(body, *, name, out_shape, n_arrays):
    return pl.pallas_call(body, name=name, out_shape=out_shape, in_specs=[ANY] * n_arrays,
                          out_specs=[ANY] * n_arrays,
                          scratch_shapes=[pltpu.SemaphoreType.DMA((N_PEERS * n_arrays,)),
                                          pltpu.SemaphoreType.DMA((N_PEERS * n_arrays,)),
                                          pltpu.SemaphoreType.DMA((n_arrays,))],
                          interpret=False)


def _all_gather(name, shards):
    n = len(shards)

    def body(*refs):
        x_refs, out_refs = refs[:n], refs[n:2 * n]
        send_sems, recv_sems, local_sems = refs[2 * n:]
        x, y, c = lax.axis_index("x"), lax.axis_index("y"), lax.axis_index("c")
        me, sibling = (x, y, c), (x, y, 1 - c)
        chips = [(1 - x, y), (x, 1 - y), (1 - x, 1 - y)]

        def slot(a, px, py, pc):
            return out_refs[a].at[4 * px + 2 * py + pc]

        def copy(a, k, block, to, src=None):
            return pltpu.make_async_remote_copy(
                src_ref=slot(a, *block) if src is None else src, dst_ref=slot(a, *block),
                send_sem=send_sems.at[N_PEERS * a + k], recv_sem=recv_sems.at[N_PEERS * a + k], device_id=to,
                device_id_type=MESH)

        mine = [pltpu.make_async_copy(x_refs[a], slot(a, *me), local_sems.at[a]) for a in range(n)]
        first = []
        for a in range(n):
            first.append(copy(a, 0, me, sibling, src=x_refs[a]))
            first += [copy(a, 1 + j, me, (*chip, c), src=x_refs[a]) for j, chip in enumerate(chips)]
        for cp in mine + first:
            cp.start()
        passed = []
        for j, chip in enumerate(chips):
            for a in range(n):
                copy(a, 1 + j, (*chip, c), me).wait_recv()
                passed.append(copy(a, 4 + j, (*chip, c), sibling))
                passed[-1].start()
        for a in range(n):
            copy(a, 0, sibling, me).wait_recv()
            for j, chip in enumerate(chips):
                copy(a, 4 + j, (*chip, 1 - c), me).wait_recv()
        for cp in first + passed:
            cp.wait_send()
        for cp in mine:
            cp.wait()

    return _ccall(body, name=name, out_shape=[jax.ShapeDtypeStruct((N_DEV,) + s.shape, s.dtype) for s in shards],
                  n_arrays=n)(*shards)


def _exchange(name, parts):
    n = len(parts)

    def body(*refs):
        p_refs, out_refs = refs[:n], refs[n:2 * n]
        send_sems, recv_sems, local_sems = refs[2 * n:]
        x, y, c = lax.axis_index("x"), lax.axis_index("y"), lax.axis_index("c")
        me = 4 * x + 2 * y + c

        def peer(k):
            return (x ^ ((k >> 2) & 1), y ^ ((k >> 1) & 1), c ^ (k & 1))

        def copy(a, k, receive):
            px, py, pc = peer(k)
            theirs = 4 * px + 2 * py + pc
            return pltpu.make_async_remote_copy(
                src_ref=out_refs[a].at[theirs] if receive else p_refs[a].at[theirs],
                dst_ref=out_refs[a].at[theirs if receive else me],
                send_sem=send_sems.at[N_PEERS * a + k - 1], recv_sem=recv_sems.at[N_PEERS * a + k - 1],
                device_id=(px, py, pc), device_id_type=MESH)

        mine = [pltpu.make_async_copy(p_refs[a].at[me], out_refs[a].at[me], local_sems.at[a]) for a in range(n)]
        sends = [copy(a, k, False) for k in range(1, N_DEV) for a in range(n)]
        for cp in mine + sends:
            cp.start()
        for k in range(1, N_DEV):
            for a in range(n):
                copy(a, k, True).wait_recv()
        for cp in sends:
            cp.wait_send()
        for cp in mine:
            cp.wait()

    return _ccall(body, name=name, out_shape=[jax.ShapeDtypeStruct(p.shape, p.dtype) for p in parts],
                  n_arrays=n)(*parts)


ADAM_SLOT_BYTES = 4 * 1024 * 1024


def _adam_call(name, w, slots, m, v):
    r, n = w.shape
    rows_unit = 2 * SUBLANES
    tr = _pick(r, max(rows_unit, ADAM_SLOT_BYTES // (N_DEV * n * 4)), rows_unit)
    spec = pl.BlockSpec((tr, n), lambda i: (i, 0))

    def body(w_ref, s_ref, m_ref, v_ref, g_ref, d_ref, nm_ref, nv_ref):
        g = s_ref[0].astype(F32)
        for d in range(1, N_DEV):
            g = g + s_ref[d].astype(F32)
        nm = ADAM_B1 * m_ref[...] + (1.0 - ADAM_B1) * g
        nv = ADAM_B2 * v_ref[...] + (1.0 - ADAM_B2) * (g * g)
        m_hat = nm / (1.0 - ADAM_B1 ** ADAM_STEP)
        v_hat = nv / (1.0 - ADAM_B2 ** ADAM_STEP)
        g_ref[...] = g
        d_ref[...] = -ADAM_LR * (m_hat / (jnp.sqrt(v_hat) + ADAM_EPS) + ADAM_WD * w_ref[...])
        nm_ref[...] = nm
        nv_ref[...] = nv

    return _pcall(body, name=name, out_shape=[jax.ShapeDtypeStruct((r, n), F32)] * 4, grid=(r // tr,),
                  in_specs=[spec, pl.BlockSpec((N_DEV, tr, n), lambda i: (0, i, 0)), spec, spec],
                  out_specs=[spec] * 4)(w, slots, m, v)


def _pack_rows(flat, rows):
    return jnp.pad(flat, (0, rows * PACK_COLS - flat.shape[0])).reshape(rows, PACK_COLS)


def _regroup_in(w_in):
    cols = [w_in[:, a:b] for a, b in _IN_SRC]
    return jnp.concatenate(cols + [jnp.zeros((w_in.shape[0], N_IN_PAD - N_IN), w_in.dtype)], axis=1)


def _ungroup_in(d):
    starts = {}
    off = 0
    for a, b in _IN_SRC:
        starts[a] = (off, b - a)
        off += b - a
    return jnp.concatenate([d[:, starts[a][0]:starts[a][0] + starts[a][1]] for a in sorted(starts)], axis=1)


def _full_weights(gathered):
    out = {}
    for n, g in zip(SHARDED_ORDER, gathered):
        (r, c), axis = SHARDED[n]
        out[n] = g.reshape(r, c) if axis == 0 else g.transpose(1, 0, 2).reshape(r, c)
    out["w_in"] = _regroup_in(out["w_in"])
    return out


def _in_f32(full):
    return {n: v.astype(F32) for n, v in full.items()}


def _for_transport(name, shard):
    return shard if name in ("gdn_conv", "ffn_conv") else shard.astype(BF16)


def _grad_parts(grads):
    parts = []
    for n in SHARDED_ORDER:
        (r, c), axis = SHARDED[n]
        g = _ungroup_in(grads[n]) if n == "w_in" else grads[n]
        if axis == 0:
            parts.append(g.reshape(N_DEV, r // N_DEV, c).astype(BF16))
        else:
            parts.append(g.reshape(r, N_DEV, c // N_DEV).transpose(1, 0, 2).astype(BF16))
    return parts


def _pack_small(vals):
    return _pack_rows(jnp.concatenate([vals[n].reshape(-1) for n in SMALL_ORDER]), SMALL_ROWS)


def _unpack_small(packed):
    flat = packed.reshape(-1)
    out, off = {}, 0
    for n in SMALL_ORDER:
        size = DEPTH * SMALL_WIDTH[n]
        out[n] = flat[off:off + size].reshape(DEPTH, SMALL_WIDTH[n])
        off += size
    return out


def kernel(x, mem, norm_mix, w_in, fox_fbias, fox_qnorm, fox_knorm, gdn_conv, gdn_a_log, gdn_dt_bias, gdn_onorm, gate_bias, w_oa, w_ob, w_oc, w_out, norm_xq, norm_mem, w_mq, w_mkv, mq_norm, mk_norm, w_mo, norm_ffn, w_up, ffn_conv, ffn_conv_b, w_down, loss_target, m_norm_mix, m_w_in, m_fox_fbias, m_fox_qnorm, m_fox_knorm, m_gdn_conv, m_gdn_a_log, m_gdn_dt_bias, m_gdn_onorm, m_gate_bias, m_w_oa, m_w_ob, m_w_oc, m_w_out, m_norm_xq, m_norm_mem, m_w_mq, m_w_mkv, m_mq_norm, m_mk_norm, m_w_mo, m_norm_ffn, m_w_up, m_ffn_conv, m_ffn_conv_b, m_w_down, v_norm_mix, v_w_in, v_fox_fbias, v_fox_qnorm, v_fox_knorm, v_gdn_conv, v_gdn_a_log, v_gdn_dt_bias, v_gdn_onorm, v_gate_bias, v_w_oa, v_w_ob, v_w_oc, v_w_out, v_norm_xq, v_norm_mem, v_w_mq, v_w_mkv, v_mq_norm, v_mk_norm, v_w_mo, v_norm_ffn, v_w_up, v_ffn_conv, v_ffn_conv_b, v_w_down):
    given = dict(locals())
    wts = {n: given[n] for n in WEIGHTS}
    mom = {n: given["m_" + n] for n in WEIGHTS}
    var = {n: given["v_" + n] for n in WEIGHTS}

    layers, layers16 = [], []
    for l in range(DEPTH):
        full = _full_weights(_all_gather("gather_weights", [_for_transport(n, wts[n][l]) for n in SHARDED_ORDER]))
        layers16.append(full)
        layers.append({**_in_f32(full), **{n: wts[n][l] for n in SMALL_ORDER}})

    y, vjp = jax.vjp(lambda xx, ww: _forward(xx, mem[0], ww, layers16), x[0], layers)
    loss_part, dy = _loss_call(y, loss_target[0])
    dx, dlayers = vjp(dy)
    loss = lax.psum(loss_part, ("x", "y", "c"))

    out = {}
    per_layer = []
    for l in range(DEPTH):
        slots = _exchange("exchange_grads", _grad_parts(dlayers[l]))
        per_layer.append({n: _adam_call("adam_shard", wts[n][l], sl, mom[n][l], var[n][l])
                          for n, sl in zip(SHARDED_ORDER, slots)})
    for n in SHARDED_ORDER:
        for k, kind in enumerate(("grad_", "delta_", "new_m_", "new_v_")):
            out[kind + n] = jnp.stack([per_layer[l][n][k] for l in range(DEPTH)])
    dsmall = {n: jnp.stack([dlayers[l][n] for l in range(DEPTH)]) for n in SMALL_ORDER}
    slots = _all_gather("gather_small_grads", [_pack_small(dsmall)])[0]
    res = _adam_call("adam_small", _pack_small(wts), slots, _pack_small(mom), _pack_small(var))
    for k, kind in enumerate(("grad_", "delta_", "new_m_", "new_v_")):
        un = _unpack_small(res[k])
        for n in SMALL_ORDER:
            out[kind + n] = un[n].reshape(wts[n].shape)

    return (loss, dx[None], *[out["grad_" + n] for n in WEIGHTS], *[out["delta_" + n] for n in WEIGHTS],
            *[out["new_m_" + n] for n in WEIGHTS], *[out["new_v_" + n] for n in WEIGHTS])
```

```python
import jax
import jax.numpy as jnp
from jax import lax
from jax.experimental import pallas as pl
from jax.experimental.pallas import tpu as pltpu

F32 = jnp.float32
BF16 = jnp.bfloat16

N_DEV = 8
D_MODEL = 1024
DEPTH = 4
CHUNK = 64
EPS = 1e-6
HEADS = 4
HEAD_DIM = 128
HW = HEADS * HEAD_DIM
D_FF = 2816
N_IN = 8204
LANES = 128
SUBLANES = 8
VMEM_LIMIT = 56 * 1024 * 1024

ADAM_LR = 0.001
ADAM_B1 = 0.9
ADAM_B2 = 0.999
ADAM_EPS = 1e-08
ADAM_WD = 0.01
ADAM_STEP = 10

NEG = -1e30
MESH = pl.DeviceIdType.MESH

WEIGHTS = ['norm_mix', 'w_in', 'fox_fbias', 'fox_qnorm', 'fox_knorm', 'gdn_conv', 'gdn_a_log', 'gdn_dt_bias',
           'gdn_onorm', 'gate_bias', 'w_oa', 'w_ob', 'w_oc', 'w_out', 'norm_xq', 'norm_mem', 'w_mq', 'w_mkv',
           'mq_norm', 'mk_norm', 'w_mo', 'norm_ffn', 'w_up', 'ffn_conv', 'ffn_conv_b', 'w_down']
SHARDED = {
    'w_in': ((D_MODEL, N_IN), 0), 'gdn_conv': ((4, 3 * HW), 1), 'w_oa': ((HW, D_MODEL), 1),
    'w_ob': ((HW, D_MODEL), 1), 'w_oc': ((HW, D_MODEL), 1), 'w_out': ((D_MODEL, D_MODEL), 0),
    'w_mq': ((D_MODEL, HW), 0), 'w_mkv': ((D_MODEL, 2 * HW), 0), 'w_mo': ((HW, D_MODEL), 1),
    'w_up': ((D_MODEL, 2 * D_FF), 1), 'ffn_conv': ((3, 2 * D_FF), 1), 'w_down': ((D_FF, D_MODEL), 0),
}
SHARDED_ORDER = [n for n in WEIGHTS if n in SHARDED]
SMALL_ORDER = [n for n in WEIGHTS if n not in SHARDED]
SMALL_WIDTH = {'norm_mix': D_MODEL, 'fox_fbias': HEADS, 'fox_qnorm': HEAD_DIM, 'fox_knorm': HEAD_DIM,
               'gdn_a_log': HEADS, 'gdn_dt_bias': HEADS, 'gdn_onorm': HEAD_DIM, 'gate_bias': 3 * D_MODEL,
               'norm_xq': D_MODEL, 'norm_mem': D_MODEL, 'mq_norm': HEAD_DIM, 'mk_norm': HEAD_DIM,
               'norm_ffn': D_MODEL, 'ffn_conv_b': 2 * D_FF}
PACK_COLS = 1024


def _round_up(n, m):
    return (n + m - 1) // m * m


SMALL_ROWS = _round_up(DEPTH * sum(SMALL_WIDTH.values()), SUBLANES * PACK_COLS) // PACK_COLS

_IN_SRC = [(0, 512), (512, 1024), (1024, 1536),
           (1540, 2052), (2052, 2564), (2564, 3076),
           (3084, 3596),
           (3596, 4108), (4108, 4620), (4620, 5132),
           (5132, 8204),
           (1536, 1540), (3076, 3080), (3080, 3084)]
N_IN_PAD = 8320
LANE_FF, LANE_GB, LANE_GA = 0, 4, 8


def _pick(dim, pref, unit):
    best = None
    t = unit
    while t <= min(dim, pref):
        if dim % t == 0:
            best = t
        t += unit
    return dim if best is None else best


def _params(n_grid):
    return pltpu.CompilerParams(dimension_semantics=("arbitrary",) * n_grid, vmem_limit_bytes=VMEM_LIMIT)


def _pcall(body, *, name, out_shape, grid, in_specs, out_specs, scratch_shapes=()):
    return pl.pallas_call(body, name=name, out_shape=out_shape, grid=grid, in_specs=in_specs, out_specs=out_specs,
                          scratch_shapes=scratch_shapes, compiler_params=_params(len(grid)),
                          interpret=False)


NN = ((1,), (0,))
NT = ((1,), (1,))
TN = ((0,), (0,))


def _dot(a, b, dn):
    return lax.dot_general(a.astype(BF16), b.astype(BF16), (dn, ((), ())), preferred_element_type=F32)


def _dotf(a, b, dn):
    return lax.dot_general(a, b, (dn, ((), ())), precision=lax.Precision.HIGHEST, preferred_element_type=F32)


@jax.custom_vjp
def mm(a, b):
    return _dot(a, b, NN)


mm.defvjp(lambda a, b: (_dot(a, b, NN), (a, b)), lambda r, g: (_dot(g, r[1], NT), _dot(r[0], g, TN)))


@jax.custom_vjp
def mm_nt(a, b):
    return _dot(a, b, NT)


mm_nt.defvjp(lambda a, b: (_dot(a, b, NT), (a, b)), lambda r, g: (_dot(g, r[1], NN), _dot(g, r[0], TN)))


@jax.custom_vjp
def mm_tn(a, b):
    return _dot(a, b, TN)


mm_tn.defvjp(lambda a, b: (_dot(a, b, TN), (a, b)), lambda r, g: (_dot(r[1], g, NT), _dot(r[0], g, NN)))


@jax.custom_vjp
def tri_apply(t, x):
    return _dotf(t, x, NN)


tri_apply.defvjp(lambda t, x: (_dotf(t, x, NN), t), lambda t, g: (jnp.zeros_like(t), _dotf(t, g, TN)))


def _sigmoid(x):
    return 1.0 / (1.0 + jnp.exp(-x))


@jax.custom_vjp
def _softplus(x):
    return jnp.maximum(x, 0.0) + jnp.log(1.0 + jnp.exp(-jnp.abs(x)))


_softplus.defvjp(lambda x: (_softplus(x), x), lambda x, g: (g * _sigmoid(x),))


@jax.custom_vjp
def _silu(x):
    return x * _sigmoid(x)


def _silu_fwd(x):
    s = _sigmoid(x)
    return x * s, (x, s)


_silu.defvjp(_silu_fwd, lambda r, g: (g * r[1] * (1.0 + r[0] * (1.0 - r[1])),))


def _rms(x, g):
    return x * lax.rsqrt(jnp.mean(x * x, axis=-1, keepdims=True) + EPS) * g


def _iota2(n, m, axis):
    return lax.broadcasted_iota(jnp.int32, (n, m), axis)


def _whole(width):
    return [(0, width)]


def _split(width, n):
    w = width // n
    return [(k * w, w) for k in range(n)]


class RowOp:
    def __init__(self, name, f, in_pieces, out_pieces, tm=256):
        self.name, self.f, self.in_pieces, self.out_pieces, self.tm = name, f, in_pieces, out_pieces, tm
        op = jax.custom_vjp(self._fwd_call)
        op.defvjp(lambda *a: (self._fwd_call(*a), a), lambda res, g: self._bwd_call(res, g))
        self.op = op

    def __call__(self, *args):
        return self.op(*args)

    def _width(self, pieces):
        return max(o + w for o, w in pieces)

    def _row_specs(self, pieces_list, tm):
        return [pl.BlockSpec((tm, self._width(p)), lambda i: (i, 0)) for p in pieces_list]

    def _fwd_call(self, *args):
        nr = len(self.in_pieces)
        rows, params = args[:nr], args[nr:]
        m = rows[0].shape[0]
        tm = _pick(m, self.tm, SUBLANES)
        f, in_pieces, out_pieces = self.f, self.in_pieces, self.out_pieces
        no = len(out_pieces)

        def body(*refs):
            rin, pr, ro = refs[:nr], refs[nr:nr + len(params)], refs[nr + len(params):]
            xs = [r[:, o:o + w] for r, ps in zip(rin, in_pieces) for (o, w) in ps]
            ys = f(*xs, *[p[...] for p in pr])
            k = 0
            for r, ps in zip(ro, out_pieces):
                for (o, w) in ps:
                    r[:, o:o + w] = ys[k]
                    k += 1

        outs = _pcall(
            body, name=self.name + "_fwd",
            out_shape=[jax.ShapeDtypeStruct((m, self._width(p)), F32) for p in out_pieces],
            grid=(m // tm,),
            in_specs=self._row_specs(in_pieces, tm) + [pl.BlockSpec(p.shape, lambda i: (0, 0)) for p in params],
            out_specs=self._row_specs(out_pieces, tm),
        )(*rows, *params)
        return tuple(outs) if no > 1 else outs[0]

    def _bwd_call(self, res, g):
        nr = len(self.in_pieces)
        rows, params = res[:nr], res[nr:]
        no = len(self.out_pieces)
        gs = tuple(g) if no > 1 else (g,)
        m = rows[0].shape[0]
        tm = _pick(m, self.tm, SUBLANES)
        f, in_pieces, out_pieces = self.f, self.in_pieces, self.out_pieces
        npar = len(params)

        def body(*refs):
            rin, pr, dro = refs[:nr], refs[nr:nr + npar], refs[nr + npar:nr + npar + no]
            drin, dpr = refs[nr + npar + no:nr + npar + no + nr], refs[nr + npar + no + nr:]
            xs = [r[:, o:o + w] for r, ps in zip(rin, in_pieces) for (o, w) in ps]
            dys = [r[:, o:o + w] for r, ps in zip(dro, out_pieces) for (o, w) in ps]
            _, vjp = jax.vjp(lambda *a: tuple(f(*a)), *xs, *[p[...] for p in pr])
            grads = vjp(tuple(dys))
            k = 0
            for r, ps in zip(drin, in_pieces):
                for (o, w) in ps:
                    r[:, o:o + w] = grads[k]
                    k += 1

            @pl.when(pl.program_id(0) == 0)
            def _():
                for r in dpr:
                    r[...] = jnp.zeros_like(r)

            for j, r in enumerate(dpr):
                r[...] += grads[k + j]

        outs = _pcall(
            body, name=self.name + "_bwd",
            out_shape=[jax.ShapeDtypeStruct(r.shape, F32) for r in rows]
            + [jax.ShapeDtypeStruct(p.shape, F32) for p in params],
            grid=(m // tm,),
            in_specs=self._row_specs(in_pieces, tm) + [pl.BlockSpec(p.shape, lambda i: (0, 0)) for p in params]
            + self._row_specs(out_pieces, tm),
            out_specs=self._row_specs(in_pieces, tm) + [pl.BlockSpec(p.shape, lambda i: (0, 0)) for p in params],
        )(*rows, *params, *gs)
        return tuple(outs)


def _f_rms(x, g):
    return (_rms(x, g),)


def _f_headnorm(x0, x1, x2, x3, g):
    return tuple(_rms(x, g) for x in (x0, x1, x2, x3))


def _f_small(sm, fb, al, db):
    tm = sm.shape[0]
    logf = -_softplus(-(sm + fb))
    beta = _sigmoid(sm)
    glog = -jnp.exp(al) * _softplus(sm + db)
    r, c = _iota2(tm, tm, 0), _iota2(tm, tm, 1)
    bd = jnp.where((r >= c) & (jnp.bitwise_xor(r, c) < CHUNK), 1.0, 0.0).astype(F32)
    return logf, beta, tri_apply(bd, glog)


def _f_gdnpost(o0, o1, o2, o3, z0, z1, z2, z3, g):
    return tuple(_rms(o, g) * _silu(z) for o, z in zip((o0, o1, o2, o3), (z0, z1, z2, z3)))


def _f_merge(t0, t1, t2, a, b, c, b0, b1, b2):
    return (_sigmoid(t0 + b0) * a + _sigmoid(t1 + b1) * b + _sigmoid(t2 + b2) * c,)


def _make_rowops():
    return dict(
        rms=RowOp("rms", _f_rms, [_whole(D_MODEL)], [_whole(D_MODEL)], tm=512),
        headnorm=RowOp("headnorm", _f_headnorm, [_split(HW, HEADS)], [_split(HW, HEADS)], tm=1024),
        small=RowOp("smallprep", _f_small, [_whole(LANES)], [_whole(LANES)] * 3),
        gdnpost=RowOp("gdnpost", _f_gdnpost, [_split(HW, HEADS)] * 2, [_split(HW, HEADS)], tm=512),
        merge=RowOp("merge", _f_merge, [_split(3 * D_MODEL, 3)] + [_whole(D_MODEL)] * 3, [_whole(D_MODEL)]),
    )


ROUND_ONCE_READS = 3


def _b16(x):
    return x.astype(BF16)


def _mm_call(name, a, b, mode, c=None):
    if mode == "nn":
        (m, kc), n = a.shape, b.shape[1]
    elif mode == "nt":
        (m, kc), n = a.shape, b.shape[0]
    else:
        (kc, m), n = a.shape, b.shape[1]
    tm = _pick(m, 1408, LANES) if mode == "tn" else _pick(m, 1024, SUBLANES)
    tn = _pick(n, 1408, LANES)
    tk = _pick(kc, 1024, SUBLANES) if mode == "tn" else _pick(kc, 1536, LANES)
    if a.dtype == F32 and n // tn >= ROUND_ONCE_READS:
        a = _b16(a)
    if b.dtype == F32 and m // tm >= ROUND_ONCE_READS:
        b = _b16(b)
    dn = {"nn": NN, "nt": NT, "tn": TN}[mode]
    a_spec = {"nn": pl.BlockSpec((tm, tk), lambda i, j, k: (i, k)),
              "nt": pl.BlockSpec((tm, tk), lambda i, j, k: (i, k)),
              "tn": pl.BlockSpec((tk, tm), lambda i, j, k: (k, i))}[mode]
    b_spec = {"nn": pl.BlockSpec((tk, tn), lambda i, j, k: (k, j)),
              "nt": pl.BlockSpec((tn, tk), lambda i, j, k: (j, k)),
              "tn": pl.BlockSpec((tk, tn), lambda i, j, k: (k, j))}[mode]
    o_spec = pl.BlockSpec((tm, tn), lambda i, j, k: (i, j))
    has_c = c is not None

    def body(*refs):
        a_ref, b_ref = refs[0], refs[1]
        o_ref = refs[-1]

        @pl.when(pl.program_id(2) == 0)
        def _():
            o_ref[...] = refs[2][...] if has_c else jnp.zeros_like(o_ref)

        o_ref[...] += _dot(a_ref[...], b_ref[...], dn)

    return _pcall(body, name=name, out_shape=jax.ShapeDtypeStruct((m, n), F32), grid=(m // tm, n // tn, kc // tk),
                  in_specs=[a_spec, b_spec] + ([o_spec] if has_c else []), out_specs=o_spec,
                  )(*((a, b, c) if has_c else (a, b)))


@jax.custom_vjp
def matmul(a, w, w16):
    return _mm_call("mm_nn", a, w16, "nn")


def _matmul_bwd(res, g):
    a, w16 = res
    return _mm_call("mm_nt", g, w16, "nt"), _mm_call("mm_tn", a, g, "tn"), jnp.zeros_like(w16)


matmul.defvjp(lambda a, w, w16: (_mm_call("mm_nn", a, w16, "nn"), (a, w16)), _matmul_bwd)


@jax.custom_vjp
def matmul_add(c, a, w, w16):
    return _mm_call("mm_nn_add", a, w16, "nn", c)


matmul_add.defvjp(lambda c, a, w, w16: (_mm_call("mm_nn_add", a, w16, "nn", c), (a, w16)),
                  lambda res, g: (g,) + _matmul_bwd(res, g))

_PROJ_GROUPS = [(0, 512), (512, 512), (1024, 512), (1536, 1536), (3072, 512), (3584, 512), (4096, 512), (4608, 512),
                (5120, 3072), (8192, 128)]


def _proj_fwd(h, w, w16):
    h16 = _b16(h)
    return tuple(_mm_call("proj_nn", h16, w16[:, s:s + n], "nn") for s, n in _PROJ_GROUPS), (h16, w16)


proj = jax.custom_vjp(lambda h, w, w16: _proj_fwd(h, w, w16)[0])


def _proj_bwd(res, gs):
    h16, w16 = res
    dh = None
    dws = []
    for (s, n), g in zip(_PROJ_GROUPS, gs):
        dh = _mm_call("proj_nt", g, w16[:, s:s + n], "nt", dh)
        dws.append(_mm_call("proj_tn", h16, g, "tn"))
    return dh, jnp.concatenate(dws, axis=1), jnp.zeros_like(w16)


proj.defvjp(_proj_fwd, _proj_bwd)


def _cumsum_call(x, reverse):
    s, w = x.shape
    tm = _pick(s, 256, SUBLANES)
    nb = s // tm

    def body(x_ref, o_ref, carry):
        @pl.when(pl.program_id(0) == 0)
        def _():
            carry[...] = jnp.zeros_like(carry)

        blk = x_ref[...]
        r, c = _iota2(tm, tm, 0), _iota2(tm, tm, 1)
        tri = jnp.where((r <= c) if reverse else (r >= c), 1.0, 0.0).astype(F32)
        o_ref[...] = _dotf(tri, blk, NN) + carry[...]
        carry[...] += jnp.sum(blk, axis=0, keepdims=True)

    idx = (lambda i: (nb - 1 - i, 0)) if reverse else (lambda i: (i, 0))
    return _pcall(body, name="cumsum_rev" if reverse else "cumsum", out_shape=jax.ShapeDtypeStruct((s, w), F32),
                  grid=(nb,), in_specs=[pl.BlockSpec((tm, w), idx)], out_specs=pl.BlockSpec((tm, w), idx),
                  scratch_shapes=[pltpu.VMEM((1, w), F32)])(x)


@jax.custom_vjp
def seq_cumsum(x):
    return _cumsum_call(x, False)


seq_cumsum.defvjp(lambda x: (_cumsum_call(x, False), None), lambda _, g: (_cumsum_call(g, True),))


HALO = SUBLANES


class ConvOp:
    def __init__(self, name, width, post, c_pieces, out_widths, has_bias, tm):
        self.name, self.width, self.post, self.c_pieces = name, width, post, c_pieces
        self.out_widths, self.has_bias, self.tm = out_widths, has_bias, tm
        op = jax.custom_vjp(self._fwd_call)
        op.defvjp(lambda *a: (self._fwd_call(*a), a), lambda res, g: self._bwd_call(res, g))
        self.op = op

    def __call__(self, *args):
        return self.op(*args)

    def _conv(self, i, x_ref, prev_ref, w_ref, b_ref, buf):
        tm = x_ref.shape[0]
        buf[0:HALO, :] = jnp.where(i > 0, prev_ref[...], 0.0)
        buf[HALO:HALO + tm, :] = x_ref[...]
        taps = [buf[pl.ds(HALO - (self.width - 1) + j, tm), :] for j in range(self.width)]
        c = taps[0] * w_ref[0:1, :]
        for j in range(1, self.width):
            c = c + taps[j] * w_ref[j:j + 1, :]
        if self.has_bias:
            c = c + b_ref[...]
        return c, taps

    def _fwd_call(self, x, w, *bias):
        s, ch = x.shape
        tm = _pick(s, self.tm, SUBLANES)
        r8 = tm // HALO
        has_bias, post, c_pieces = self.has_bias, self.post, self.c_pieces

        def body(*refs):
            x_ref, prev_ref, w_ref = refs[:3]
            b_ref = refs[3] if has_bias else None
            outs, buf = refs[3 + has_bias:-1], refs[-1]
            c, _ = self._conv(pl.program_id(0), x_ref, prev_ref, w_ref, b_ref, buf)
            ys = post(*[c[:, o:o + n] for o, n in c_pieces])
            for r, y in zip(outs, ys):
                r[...] = y

        outs = _pcall(
            body, name=self.name + "_fwd", out_shape=[jax.ShapeDtypeStruct((s, n), F32) for n in self.out_widths],
            grid=(s // tm,),
            in_specs=[pl.BlockSpec((tm, ch), lambda i: (i, 0)),
                      pl.BlockSpec((HALO, ch), lambda i: (jnp.maximum(i * r8 - 1, 0), 0)),
                      pl.BlockSpec(w.shape, lambda i: (0, 0))]
            + ([pl.BlockSpec((1, ch), lambda i: (0, 0))] if has_bias else []),
            out_specs=[pl.BlockSpec((tm, n), lambda i: (i, 0)) for n in self.out_widths],
            scratch_shapes=[pltpu.VMEM((tm + HALO, ch), F32)],
        )(x, x, w, *bias)
        return tuple(outs) if len(outs) > 1 else outs[0]

    def _bwd_call(self, res, g):
        x, w = res[0], res[1]
        bias = res[2:]
        gs = tuple(g) if len(self.out_widths) > 1 else (g,)
        s, ch = x.shape
        tm = _pick(s, self.tm, SUBLANES)
        r8 = tm // HALO
        nb = s // tm
        has_bias, post, c_pieces, width = self.has_bias, self.post, self.c_pieces, self.width
        ng = len(gs)

        def body1(*refs):
            x_ref, prev_ref, w_ref = refs[:3]
            b_ref = refs[3] if has_bias else None
            k = 3 + has_bias
            g_refs = refs[k:k + ng]
            dc_ref, dw_ref = refs[k + ng], refs[k + ng + 1]
            db_ref = refs[k + ng + 2] if has_bias else None
            buf = refs[-1]
            i = pl.program_id(0)
            c, taps = self._conv(i, x_ref, prev_ref, w_ref, b_ref, buf)
            _, vjp = jax.vjp(lambda *a: tuple(post(*a)), *[c[:, o:o + n] for o, n in c_pieces])
            dcs = vjp(tuple(r[...] for r in g_refs))
            for (o, n), d in zip(c_pieces, dcs):
                dc_ref[:, o:o + n] = d

            @pl.when(i == 0)
            def _():
                dw_ref[...] = jnp.zeros_like(dw_ref)
                if has_bias:
                    db_ref[...] = jnp.zeros_like(db_ref)

            dc = dc_ref[...]
            for j in range(width):
                dw_ref[j:j + 1, :] += jnp.sum(dc * taps[j], axis=0, keepdims=True)
            if has_bias:
                db_ref[...] += jnp.sum(dc, axis=0, keepdims=True)

        outs1 = _pcall(
            body1, name=self.name + "_bwd_act",
            out_shape=[jax.ShapeDtypeStruct((s, ch), F32), jax.ShapeDtypeStruct(w.shape, F32)]
            + ([jax.ShapeDtypeStruct((1, ch), F32)] if has_bias else []),
            grid=(nb,),
            in_specs=[pl.BlockSpec((tm, ch), lambda i: (i, 0)),
                      pl.BlockSpec((HALO, ch), lambda i: (jnp.maximum(i * r8 - 1, 0), 0)),
                      pl.BlockSpec(w.shape, lambda i: (0, 0))]
            + ([pl.BlockSpec((1, ch), lambda i: (0, 0))] if has_bias else [])
            + [pl.BlockSpec((tm, n), lambda i: (i, 0)) for n in self.out_widths],
            out_specs=[pl.BlockSpec((tm, ch), lambda i: (i, 0)), pl.BlockSpec(w.shape, lambda i: (0, 0))]
            + ([pl.BlockSpec((1, ch), lambda i: (0, 0))] if has_bias else []),
            scratch_shapes=[pltpu.VMEM((tm + HALO, ch), F32)],
        )(x, x, w, *bias, *gs)
        dc, dw = outs1[0], outs1[1]

        def body2(dc_ref, next_ref, w_ref, dx_ref, buf):
            i = pl.program_id(0)
            buf[0:tm, :] = dc_ref[...]
            buf[tm:tm + HALO, :] = jnp.where(i < nb - 1, next_ref[...], 0.0)
            dx = buf[pl.ds(width - 1, tm), :] * w_ref[0:1, :]
            for j in range(1, width):
                dx = dx + buf[pl.ds(width - 1 - j, tm), :] * w_ref[j:j + 1, :]
            dx_ref[...] = dx

        dx = _pcall(
            body2, name=self.name + "_bwd_in", out_shape=jax.ShapeDtypeStruct((s, ch), F32), grid=(nb,),
            in_specs=[pl.BlockSpec((tm, ch), lambda i: (i, 0)),
                      pl.BlockSpec((HALO, ch), lambda i: (jnp.minimum((i + 1) * r8, s // HALO - 1), 0)),
                      pl.BlockSpec(w.shape, lambda i: (0, 0))],
            out_specs=pl.BlockSpec((tm, ch), lambda i: (i, 0)),
            scratch_shapes=[pltpu.VMEM((tm + HALO, ch), F32)],
        )(dc, dc, w)
        return (dx, dw) + ((outs1[2],) if has_bias else ())


def _make_convops():
    return dict(
        gdn=ConvOp("gdnconv", 4, lambda q, k, v: (_silu(q), _silu(k), _silu(v)), _split(3 * HW, 3), [HW] * 3,
                   False, 256),
        ffn=ConvOp("ffnconv", 3, lambda a, b: (_silu(a) * b,), _split(2 * D_FF, 2), [D_FF], True, 128),
    )


ATT_Q = 512
ATT_K = 256
ATT_K_FOX = 512
SCALE = HEAD_DIM ** -0.5


def _att_tiles(s, fox=False):
    tk = _pick(s, ATT_K_FOX if fox else ATT_K, LANES)
    tq = _pick(s, ATT_Q, tk)
    return tq, tk


def _att_specs(s, tq, tk):
    qspec = pl.BlockSpec((tq, HEAD_DIM), lambda h, i: (i, h))
    kspec = pl.BlockSpec((s, HEAD_DIM), lambda h, i: (0, h))
    colspec = pl.BlockSpec((None, tq, 1), lambda h, i: (h, i, 0))
    rowspec = pl.BlockSpec((None, s // tk, 1, tk), lambda h, i: (h, 0, 0, 0))
    return qspec, kspec, colspec, rowspec


def _krows(kb, tk):
    return pl.ds(pl.multiple_of(kb * tk, tk), tk)


def _stage_bf16(i, pairs):
    @pl.when(i == 0)
    def _():
        for src, dst in pairs:
            dst[...] = src[...].astype(BF16)


ATT_STRIP = 32


def _strips(tq):
    return [slice(r, r + ATT_STRIP) for r in range(0, tq, ATT_STRIP)]


def _visible(i, kb, rs, tq, tk, strict):
    rows = i * tq + rs.start + _iota2(ATT_STRIP, tk, 0)
    cols = kb * tk + _iota2(ATT_STRIP, tk, 1)
    return (cols < rows) if strict else (cols <= rows)


def _visible_block(i, kb, tq, tk):
    return kb * tk + _iota2(tq, tk, 1) <= i * tq + _iota2(tq, tk, 0)


def _blocks(i, ratio, blk, reverse=False):
    def full(n, carry):
        blk(i * ratio - 1 - n if reverse else n, False)
        return carry

    if reverse:
        for j in reversed(range(ratio)):
            blk(i * ratio + j, True)
    lax.fori_loop(0, i * ratio, full, 0)
    if not reverse:
        for j in range(ratio):
            blk(i * ratio + j, True)


def _vm(shape, dtype):
    return pltpu.VMEM(shape, dtype)


def _fox_fwd_call(q, k, v, ccol, crow):
    s = q.shape[0]
    tq, tk = _att_tiles(s, fox=True)
    qspec, kspec, colspec, rowspec = _att_specs(s, tq, tk)

    def body(q_ref, k_ref, v_ref, cq_ref, ck_ref, o_ref, lse_ref, k16, v16):
        i = pl.program_id(1)
        ratio = tq // tk
        _stage_bf16(i, [(k_ref, k16), (v_ref, v16)])
        qb = q_ref[...].astype(BF16)
        cq = cq_ref[...]

        def blk(kb, carry, masked):
            m, l, acc = carry
            sc = _dot(qb, k16[_krows(kb, tk), :], NT) * SCALE + (cq - ck_ref[kb])
            if masked:
                sc = jnp.where(_visible_block(i, kb, tq, tk), sc, NEG)
            m_new = jnp.maximum(m, jnp.max(sc, axis=-1, keepdims=True))
            alpha = jnp.exp(m - m_new)
            p = jnp.exp(sc - m_new)
            return (m_new, alpha * l + jnp.sum(p, axis=-1, keepdims=True),
                    alpha * acc + _dot(p, v16[_krows(kb, tk), :], NN))

        carry = (jnp.full((tq, 1), NEG, F32), jnp.zeros((tq, 1), F32), jnp.zeros((tq, HEAD_DIM), F32))
        carry = lax.fori_loop(0, i * ratio, lambda kb, c: blk(kb, c, False), carry)
        for j in range(ratio):
            carry = blk(i * ratio + j, carry, True)
        m, l, acc = carry
        o_ref[...] = acc / l
        lse_ref[...] = m + jnp.log(l)

    return _pcall(body, name="fox_fwd",
                  out_shape=[jax.ShapeDtypeStruct((s, HW), F32), jax.ShapeDtypeStruct((HEADS, s, 1), F32)],
                  grid=(HEADS, s // tq), in_specs=[qspec, kspec, kspec, colspec, rowspec],
                  out_specs=[qspec, colspec],
                  scratch_shapes=[_vm((s, HEAD_DIM), BF16), _vm((s, HEAD_DIM), BF16)])(q, k, v, ccol, crow)


def _fox_bwd_call(q, k, v, ccol, crow, o, lse, do):
    s = q.shape[0]
    tq, tk = _att_tiles(s, fox=True)
    ratio = tq // tk
    qspec, kspec, colspec, rowspec = _att_specs(s, tq, tk)

    def body(q_ref, k_ref, v_ref, cq_ref, ck_ref, o_ref, lse_ref, do_ref, dq_ref, dk_ref, dv_ref, dcq_ref, dck_ref,
             k16, v16):
        i = pl.program_id(1)
        _stage_bf16(i, [(k_ref, k16), (v_ref, v16)])

        @pl.when(i == 0)
        def _():
            dk_ref[...] = jnp.zeros_like(dk_ref)
            dv_ref[...] = jnp.zeros_like(dv_ref)
            dck_ref[...] = jnp.zeros_like(dck_ref)

        qb = q_ref[...].astype(BF16)
        dob = do_ref[...].astype(BF16)
        cq, lse = cq_ref[...], lse_ref[...]
        dl = jnp.sum(do_ref[...] * o_ref[...], axis=-1, keepdims=True)

        def blk(kb, carry, masked):
            dq, dcq = carry
            rows = _krows(kb, tk)
            kk, vv = k16[rows, :], v16[rows, :]
            sc = _dot(qb, kk, NT) * SCALE + (cq - ck_ref[kb])
            p = jnp.exp(sc - lse)
            if masked:
                p = jnp.where(_visible_block(i, kb, tq, tk), p, 0.0)
            dv_ref[rows, :] += _dot(p, dob, TN)
            ds = p * (_dot(dob, vv, NT) - dl)
            dk_ref[rows, :] += _dot(ds, qb, TN) * SCALE
            dck_ref[kb] += -jnp.sum(ds, axis=0, keepdims=True)
            return dq + _dot(ds, kk, NN) * SCALE, dcq + jnp.sum(ds, axis=-1, keepdims=True)

        carry = (jnp.zeros((tq, HEAD_DIM), F32), jnp.zeros((tq, 1), F32))
        carry = lax.fori_loop(0, i * ratio, lambda kb, c: blk(kb, c, False), carry)
        for j in range(ratio):
            carry = blk(i * ratio + j, carry, True)
        dq_ref[...] = carry[0]
        dcq_ref[...] = carry[1]

    return _pcall(body, name="fox_bwd",
                  out_shape=[jax.ShapeDtypeStruct((s, HW), F32)] * 3
                  + [jax.ShapeDtypeStruct((HEADS, s, 1), F32), jax.ShapeDtypeStruct((HEADS, s // tk, 1, tk), F32)],
                  grid=(HEADS, s // tq),
                  in_specs=[qspec, kspec, kspec, colspec, rowspec, qspec, colspec, qspec],
                  out_specs=[qspec, kspec, kspec, colspec, rowspec],
                  scratch_shapes=[_vm((s, HEAD_DIM), BF16), _vm((s, HEAD_DIM), BF16)],
                  )(q, k, v, ccol, crow, o, lse, do)


@jax.custom_vjp
def fox_attention(q, k, v, ccol, crow):
    return _fox_fwd_call(q, k, v, ccol, crow)[0]


def _fox_vjp_fwd(q, k, v, ccol, crow):
    o, lse = _fox_fwd_call(q, k, v, ccol, crow)
    return o, (q, k, v, ccol, crow, o, lse)


fox_attention.defvjp(_fox_vjp_fwd, lambda res, g: tuple(_fox_bwd_call(*res, g)))


def _sb_fwd_call(q, k, v):
    s = q.shape[0]
    tq, tk = _att_tiles(s)
    ratio = tq // tk
    qspec, kspec, colspec, _ = _att_specs(s, tq, tk)

    def body(q_ref, k_ref, v_ref, o_ref, tot_ref, k16, v16, q16, s_scr, w_scr, lk16, a16, run, acc):
        i = pl.program_id(1)
        _stage_bf16(i, [(k_ref, k16), (v_ref, v16)])
        q16[...] = q_ref[...].astype(BF16)
        run[...] = jnp.zeros_like(run)
        acc[...] = jnp.zeros_like(acc)
        suffix = jnp.where(_iota2(tk, tk, 0) >= _iota2(tk, tk, 1), 1.0, 0.0).astype(BF16)

        def blk(kb, masked):
            rows = _krows(kb, tk)
            s_scr[...] = _dot(q16[...], k16[rows, :], NT)
            for rs in _strips(tq):
                lk = -_softplus(s_scr[rs, :] * SCALE)
                if masked:
                    lk = jnp.where(_visible(i, kb, rs, tq, tk, True), lk, 0.0)
                lk16[rs, :] = lk.astype(BF16)
            w_scr[...] = _dot(lk16[...], suffix, NN)
            for rs in _strips(tq):
                a = jnp.exp(s_scr[rs, :] * SCALE + w_scr[rs, :] + run[rs, :])
                if masked:
                    a = jnp.where(_visible(i, kb, rs, tq, tk, True), a, 0.0)
                a16[rs, :] = a.astype(BF16)
                run[rs, :] += w_scr[rs, 0:1]
            acc[...] += _dot(a16[...], v16[rows, :], NN)

        _blocks(i, ratio, blk, reverse=True)
        o_ref[...] = acc[...]
        tot_ref[...] = run[...]

    return _pcall(body, name="sb_fwd",
                  out_shape=[jax.ShapeDtypeStruct((s, HW), F32), jax.ShapeDtypeStruct((HEADS, s, 1), F32)],
                  grid=(HEADS, s // tq), in_specs=[qspec, kspec, kspec], out_specs=[qspec, colspec],
                  scratch_shapes=[_vm((s, HEAD_DIM), BF16), _vm((s, HEAD_DIM), BF16), _vm((tq, HEAD_DIM), BF16),
                                  _vm((tq, tk), F32), _vm((tq, tk), F32), _vm((tq, tk), BF16), _vm((tq, tk), BF16),
                                  _vm((tq, 1), F32), _vm((tq, HEAD_DIM), F32)])(q, k, v)


def _sb_bwd_call(q, k, v, tot, do):
    s = q.shape[0]
    tq, tk = _att_tiles(s)
    ratio = tq // tk
    qspec, kspec, colspec, _ = _att_specs(s, tq, tk)

    def body(q_ref, k_ref, v_ref, tot_ref, do_ref, dq_ref, dk_ref, dv_ref, k16, v16, q16, do16, s_scr, e_scr, w_scr,
             lz16, a16, e16, left, esum):
        i = pl.program_id(1)
        _stage_bf16(i, [(k_ref, k16), (v_ref, v16)])

        @pl.when(i == 0)
        def _():
            dk_ref[...] = jnp.zeros_like(dk_ref)
            dv_ref[...] = jnp.zeros_like(dv_ref)

        q16[...] = q_ref[...].astype(BF16)
        do16[...] = do_ref[...].astype(BF16)
        left[...] = jnp.zeros_like(left)
        esum[...] = jnp.zeros_like(esum)
        dq_ref[...] = jnp.zeros_like(dq_ref)
        prefix = jnp.where(_iota2(tk, tk, 0) <= _iota2(tk, tk, 1), 1.0, 0.0).astype(BF16)

        def blk(kb, masked):
            rows = _krows(kb, tk)
            s_scr[...] = _dot(q16[...], k16[rows, :], NT)
            e_scr[...] = _dot(do16[...], v16[rows, :], NT)
            for rs in _strips(tq):
                lk = -_softplus(s_scr[rs, :] * SCALE)
                if masked:
                    lk = jnp.where(_visible(i, kb, rs, tq, tk, True), lk, 0.0)
                lz16[rs, :] = lk.astype(BF16)
            w_scr[...] = _dot(lz16[...], prefix, NN)
            for rs in _strips(tq):
                rc = (tot_ref[rs, :] - left[rs, :]) - (w_scr[rs, :] - lz16[rs, :].astype(F32))
                a = jnp.exp(s_scr[rs, :] * SCALE + rc)
                if masked:
                    a = jnp.where(_visible(i, kb, rs, tq, tk, True), a, 0.0)
                e = a * e_scr[rs, :]
                a16[rs, :] = a.astype(BF16)
                e16[rs, :] = e.astype(BF16)
                e_scr[rs, :] = e
                left[rs, :] += w_scr[rs, tk - 1:tk]
            w_scr[...] = _dot(e16[...], prefix, NN)
            for rs in _strips(tq):
                dz = e_scr[rs, :] - _sigmoid(s_scr[rs, :] * SCALE) * (esum[rs, :] + w_scr[rs, :])
                if masked:
                    dz = jnp.where(_visible(i, kb, rs, tq, tk, True), dz, 0.0)
                lz16[rs, :] = dz.astype(BF16)
                esum[rs, :] += w_scr[rs, tk - 1:tk]
            dv_ref[rows, :] += _dot(a16[...], do16[...], TN)
            dk_ref[rows, :] += _dot(lz16[...], q16[...], TN) * SCALE
            dq_ref[...] += _dot(lz16[...], k16[rows, :], NN) * SCALE

        _blocks(i, ratio, blk)

    return _pcall(body, name="sb_bwd", out_shape=[jax.ShapeDtypeStruct((s, HW), F32)] * 3, grid=(HEADS, s // tq),
                  in_specs=[qspec, kspec, kspec, colspec, qspec], out_specs=[qspec, kspec, kspec],
                  scratch_shapes=[_vm((s, HEAD_DIM), BF16), _vm((s, HEAD_DIM), BF16), _vm((tq, HEAD_DIM), BF16),
                                  _vm((tq, HEAD_DIM), BF16), _vm((tq, tk), F32), _vm((tq, tk), F32),
                                  _vm((tq, tk), F32), _vm((tq, tk), BF16), _vm((tq, tk), BF16), _vm((tq, tk), BF16),
                                  _vm((tq, 1), F32), _vm((tq, 1), F32)])(q, k, v, tot, do)


@jax.custom_vjp
def sb_attention(q, k, v):
    return _sb_fwd_call(q, k, v)[0]


def _sb_vjp_fwd(q, k, v):
    o, tot = _sb_fwd_call(q, k, v)
    return o, (q, k, v, tot)


sb_attention.defvjp(_sb_vjp_fwd, lambda res, g: tuple(_sb_bwd_call(*res, g)))


def _mem_specs(s, nk, t):
    return (pl.BlockSpec((t, HEAD_DIM), lambda h, i: (i, h)), pl.BlockSpec((nk, HEAD_DIM), lambda h, i: (0, h)))


def _mem_probs(qb, kk):
    sc = _dot(qb, kk, NT) * SCALE
    p = jnp.exp(sc - jnp.max(sc, axis=-1, keepdims=True))
    return p / jnp.sum(p, axis=-1, keepdims=True)


def _mem_fwd_call(q, k, v):
    s, nk = q.shape[0], k.shape[0]
    t = _pick(s, 512, SUBLANES)
    qspec, kspec = _mem_specs(s, nk, t)

    def body(q_ref, k_ref, v_ref, o_ref):
        o_ref[...] = _dot(_mem_probs(q_ref[...].astype(BF16), k_ref[...]), v_ref[...], NN)

    return _pcall(body, name="mem_fwd", out_shape=jax.ShapeDtypeStruct((s, HW), F32), grid=(HEADS, s // t),
                  in_specs=[qspec, kspec, kspec], out_specs=qspec)(q, k, v)


def _mem_bwd_call(q, k, v, do):
    s, nk = q.shape[0], k.shape[0]
    t = _pick(s, 512, SUBLANES)
    qspec, kspec = _mem_specs(s, nk, t)

    def body(q_ref, k_ref, v_ref, do_ref, dq_ref, dk_ref, dv_ref):
        @pl.when(pl.program_id(1) == 0)
        def _():
            dk_ref[...] = jnp.zeros_like(dk_ref)
            dv_ref[...] = jnp.zeros_like(dv_ref)

        qb = q_ref[...].astype(BF16)
        dob = do_ref[...].astype(BF16)
        p = _mem_probs(qb, k_ref[...])
        dv_ref[...] += _dot(p, dob, TN)
        dp = _dot(dob, v_ref[...], NT)
        ds = p * (dp - jnp.sum(p * dp, axis=-1, keepdims=True))
        dq_ref[...] = _dot(ds, k_ref[...], NN) * SCALE
        dk_ref[...] += _dot(ds, qb, TN) * SCALE

    return _pcall(body, name="mem_bwd",
                  out_shape=[jax.ShapeDtypeStruct((s, HW), F32)] + [jax.ShapeDtypeStruct((nk, HW), F32)] * 2,
                  grid=(HEADS, s // t), in_specs=[qspec, kspec, kspec, qspec],
                  out_specs=[qspec, kspec, kspec])(q, k, v, do)


@jax.custom_vjp
def mem_attention(q, k, v):
    return _mem_fwd_call(q, k, v)


mem_attention.defvjp(lambda q, k, v: (_mem_fwd_call(q, k, v), (q, k, v)),
                     lambda res, g: tuple(_mem_bwd_call(*res, g)))


BNN = (((2,), (1,)), ((0,), (0,)))
BNT = (((2,), (2,)), ((0,), (0,)))
BTN = (((1,), (1,)), ((0,), (0,)))


def _bdot(a, b, dn):
    return lax.dot_general(a.astype(BF16), b.astype(BF16), dn, preferred_element_type=F32)


def _bdotf(a, b, dn):
    return lax.dot_general(a, b, dn, precision=lax.Precision.HIGHEST, preferred_element_type=F32)


@jax.custom_vjp
def bmm(a, b):
    return _bdot(a, b, BNN)


bmm.defvjp(lambda a, b: (_bdot(a, b, BNN), (a, b)), lambda r, g: (_bdot(g, r[1], BNT), _bdot(r[0], g, BTN)))


@jax.custom_vjp
def bmm_nt(a, b):
    return _bdot(a, b, BNT)


bmm_nt.defvjp(lambda a, b: (_bdot(a, b, BNT), (a, b)), lambda r, g: (_bdot(g, r[1], BNN), _bdot(g, r[0], BTN)))


@jax.custom_vjp
def bmm_tn(a, b):
    return _bdot(a, b, BTN)


bmm_tn.defvjp(lambda a, b: (_bdot(a, b, BTN), (a, b)), lambda r, g: (_bdot(r[1], g, BNT), _bdot(r[0], g, BNN)))


def _unit_lower_inverse(nm):
    eye = jnp.where(_iota2(CHUNK, CHUNK, 0) == _iota2(CHUNK, CHUNK, 1), 1.0, 0.0).astype(F32)[None]
    p = eye - nm
    m = nm
    for _ in range(5):
        m = _bdotf(m, m, BNN)
        p = _bdotf(p, eye + m, BNN)
    return p


@jax.custom_vjp
def _solve2(nm, r1, r2):
    inv = _unit_lower_inverse(nm)
    return _bdotf(inv, r1, BNN), _bdotf(inv, r2, BNN)


def _solve2_fwd(nm, r1, r2):
    inv = _unit_lower_inverse(nm)
    u, w = _bdotf(inv, r1, BNN), _bdotf(inv, r2, BNN)
    return (u, w), (inv, u, w)


def _solve2_bwd(res, g):
    inv, u, w = res
    d1, d2 = _bdotf(inv, g[0], BTN), _bdotf(inv, g[1], BTN)
    return -(_bdotf(d1, u, BNT) + _bdotf(d2, w, BNT)), d1, d2


_solve2.defvjp(_solve2_fwd, _solve2_bwd)


def _gdn_chunk(q, k, v, gcc, gcr, b, gl, st):
    qn = q * lax.rsqrt(jnp.sum(q * q, axis=-1, keepdims=True) + EPS) * SCALE
    kn = k * lax.rsqrt(jnp.sum(k * k, axis=-1, keepdims=True) + EPS)
    r, c = _iota2(CHUNK, CHUNK, 0)[None], _iota2(CHUNK, CHUNK, 1)[None]
    decay = jnp.exp(jnp.where(r >= c, gcc - gcr, NEG))
    nm = jnp.where(r > c, b * bmm_nt(kn, kn) * decay, 0.0)
    eg = jnp.exp(gcc)
    u, w = _solve2(nm, v * b, kn * (b * eg))
    attn = bmm_nt(qn, kn) * decay
    v_new = u - bmm(w, st)
    o = bmm(qn * eg, st) + bmm(attn, v_new)
    st_new = st * jnp.exp(gl) + bmm_tn(kn * jnp.exp(gl - gcc), v_new)
    return o, st_new


def _heads_of(ref, rows):
    return jnp.stack([ref[rows, _head_cols(h)] for h in range(HEADS)])


def _head_cols(h):
    return slice(h * HEAD_DIM, (h + 1) * HEAD_DIM)


GDN_ROWS = 512


def _gdn_specs(s, tg, rev):
    nb = s // tg
    cpb = tg // CHUNK
    j_of = (lambda j: nb - 1 - j) if rev else (lambda j: j)
    qspec = pl.BlockSpec((tg, HW), lambda j: (j_of(j), 0))
    colspec = pl.BlockSpec((HEADS, tg, 1), lambda j: (0, j_of(j), 0))
    rowspec = pl.BlockSpec((HEADS, cpb, 1, CHUNK), lambda j: (0, j_of(j), 0, 0))
    onespec = pl.BlockSpec((HEADS, cpb, 1, 1), lambda j: (0, j_of(j), 0, 0))
    stspec = pl.BlockSpec((HEADS, cpb, HEAD_DIM, HEAD_DIM), lambda j: (0, j_of(j), 0, 0))
    return qspec, colspec, rowspec, onespec, stspec


def _gdn_fwd_call(q, k, v, gcc, gcr, bc, gl):
    s = q.shape[0]
    tg = _pick(s, GDN_ROWS, CHUNK)
    cpb = tg // CHUNK
    qspec, colspec, rowspec, onespec, stspec = _gdn_specs(s, tg, False)

    def body(q_ref, k_ref, v_ref, gcc_ref, gcr_ref, b_ref, gl_ref, o_ref, st_ref, st):
        @pl.when(pl.program_id(0) == 0)
        def _():
            st[...] = jnp.zeros_like(st)

        def chunk(ci, _):
            rows = pl.ds(pl.multiple_of(ci * CHUNK, CHUNK), CHUNK)
            s_in = st[...]
            st_ref[:, ci] = s_in
            o, s_new = _gdn_chunk(_heads_of(q_ref, rows), _heads_of(k_ref, rows), _heads_of(v_ref, rows),
                                  gcc_ref[:, rows, :], gcr_ref[:, ci], b_ref[:, rows, :], gl_ref[:, ci], s_in)
            for h in range(HEADS):
                o_ref[rows, _head_cols(h)] = o[h]
            st[...] = s_new
            return 0

        lax.fori_loop(0, cpb, chunk, 0)

    return _pcall(body, name="gdn_fwd",
                  out_shape=[jax.ShapeDtypeStruct((s, HW), F32),
                             jax.ShapeDtypeStruct((HEADS, s // CHUNK, HEAD_DIM, HEAD_DIM), F32)],
                  grid=(s // tg,), in_specs=[qspec, qspec, qspec, colspec, rowspec, colspec, onespec],
                  out_specs=[qspec, stspec], scratch_shapes=[pltpu.VMEM((HEADS, HEAD_DIM, HEAD_DIM), F32)],
                  )(q, k, v, gcc, gcr, bc, gl)


def _gdn_bwd_call(q, k, v, gcc, gcr, bc, gl, states, do):
    s = q.shape[0]
    tg = _pick(s, GDN_ROWS, CHUNK)
    cpb = tg // CHUNK
    qspec, colspec, rowspec, onespec, stspec = _gdn_specs(s, tg, True)

    def body(q_ref, k_ref, v_ref, gcc_ref, gcr_ref, b_ref, gl_ref, st_ref, do_ref,
             dq_ref, dk_ref, dv_ref, dgcc_ref, dgcr_ref, db_ref, dgl_ref, dst):
        @pl.when(pl.program_id(0) == 0)
        def _():
            dst[...] = jnp.zeros_like(dst)

        def chunk(n, _):
            ci = cpb - 1 - n
            rows = pl.ds(pl.multiple_of(ci * CHUNK, CHUNK), CHUNK)
            _, vjp = jax.vjp(_gdn_chunk, _heads_of(q_ref, rows), _heads_of(k_ref, rows), _heads_of(v_ref, rows),
                             gcc_ref[:, rows, :], gcr_ref[:, ci], b_ref[:, rows, :], gl_ref[:, ci], st_ref[:, ci])
            dq, dk, dv, dgcc, dgcr, db, dgl, ds_in = vjp((_heads_of(do_ref, rows), dst[...]))
            for h in range(HEADS):
                cols = _head_cols(h)
                dq_ref[rows, cols] = dq[h]
                dk_ref[rows, cols] = dk[h]
                dv_ref[rows, cols] = dv[h]
            dgcc_ref[:, rows, :] = dgcc
            dgcr_ref[:, ci] = dgcr
            db_ref[:, rows, :] = db
            dgl_ref[:, ci] = dgl
            dst[...] = ds_in
            return 0

        lax.fori_loop(0, cpb, chunk, 0)

    n = s // CHUNK
    return _pcall(body, name="gdn_bwd",
                  out_shape=[jax.ShapeDtypeStruct((s, HW), F32)] * 3
                  + [jax.ShapeDtypeStruct((HEADS, s, 1), F32), jax.ShapeDtypeStruct((HEADS, n, 1, CHUNK), F32),
                     jax.ShapeDtypeStruct((HEADS, s, 1), F32), jax.ShapeDtypeStruct((HEADS, n, 1, 1), F32)],
                  grid=(s // tg,),
                  in_specs=[qspec, qspec, qspec, colspec, rowspec, colspec, onespec, stspec, qspec],
                  out_specs=[qspec, qspec, qspec, colspec, rowspec, colspec, onespec],
                  scratch_shapes=[pltpu.VMEM((HEADS, HEAD_DIM, HEAD_DIM), F32)],
                  )(q, k, v, gcc, gcr, bc, gl, states, do)


@jax.custom_vjp
def gated_delta(q, k, v, gcc, gcr, bc, gl):
    return _gdn_fwd_call(q, k, v, gcc, gcr, bc, gl)[0]


def _gdn_vjp_fwd(q, k, v, gcc, gcr, bc, gl):
    o, states = _gdn_fwd_call(q, k, v, gcc, gcr, bc, gl)
    return o, (q, k, v, gcc, gcr, bc, gl, states)


gated_delta.defvjp(_gdn_vjp_fwd, lambda res, g: tuple(_gdn_bwd_call(*res, g)))


def _loss_call(y, target):
    s, d = y.shape
    tm = _pick(s, 512, SUBLANES)

    def body(y_ref, t_ref, dy_ref, loss_ref):
        @pl.when(pl.program_id(0) == 0)
        def _():
            loss_ref[...] = jnp.zeros_like(loss_ref)

        err = y_ref[...] - t_ref[...]
        dy_ref[...] = err * (1.0 / d)
        loss_ref[...] += 0.5 * jnp.sum(jnp.mean(err * err, axis=-1, keepdims=True), axis=0, keepdims=True)

    dy, part = _pcall(body, name="loss_head",
                      out_shape=[jax.ShapeDtypeStruct((s, d), F32), jax.ShapeDtypeStruct((1, 1), F32)],
                      grid=(s // tm,), in_specs=[pl.BlockSpec((tm, d), lambda i: (i, 0))] * 2,
                      out_specs=[pl.BlockSpec((tm, d), lambda i: (i, 0)), pl.BlockSpec((1, 1), lambda i: (0, 0))],
                      )(y, target)
    return part[0, 0], dy


def _cols_and_rows(a, lane0, t):
    s = a.shape[0]
    at = a[:, lane0:lane0 + HEADS].T
    return at, at[:, :, None], at.reshape(HEADS, s // t, 1, t)


def _pad_lanes(v, lane0):
    return jnp.pad(v, (lane0, LANES - lane0 - v.shape[0])).reshape(1, LANES)


def _layer(x, mem, w, w16, ops, convs):
    s = x.shape[0]
    row = lambda v: v.reshape(1, -1)
    mw = lambda n: (w[n], w16[n])
    h = ops["rms"](x, row(w["norm_mix"]))
    fq, fk, fv, gqkv, gz, sq, sk, sv, gt, gm = proj(h, *mw("w_in"))

    logf, beta, gc = ops["small"](gm, _pad_lanes(w["fox_fbias"], LANE_FF), _pad_lanes(w["gdn_a_log"], LANE_GA),
                                  _pad_lanes(w["gdn_dt_bias"], LANE_GA))
    _, ccol, crow = _cols_and_rows(seq_cumsum(logf), LANE_FF, _att_tiles(s, fox=True)[1])
    ya = fox_attention(ops["headnorm"](fq, row(w["fox_qnorm"])), ops["headnorm"](fk, row(w["fox_knorm"])), fv,
                       ccol, crow)
    cq, ck, cv = convs["gdn"](gqkv, w["gdn_conv"])
    gct, gcc, gcr = _cols_and_rows(gc, LANE_GA, CHUNK)
    gl = gct.reshape(HEADS, s // CHUNK, CHUNK)[:, :, CHUNK - 1].reshape(HEADS, s // CHUNK, 1, 1)
    bc = beta[:, LANE_GB:LANE_GB + HEADS].T[:, :, None]
    yb = ops["gdnpost"](gated_delta(cq, ck, cv, gcc, gcr, bc, gl), gz, row(w["gdn_onorm"]))
    yc = sb_attention(sq, sk, sv)
    gb = w["gate_bias"]
    mixed = ops["merge"](gt, matmul(ya, *mw("w_oa")), matmul(yb, *mw("w_ob")), matmul(yc, *mw("w_oc")),
                         row(gb[:D_MODEL]), row(gb[D_MODEL:2 * D_MODEL]), row(gb[2 * D_MODEL:]))
    x = matmul_add(x, mixed, *mw("w_out"))
    mq = ops["headnorm"](matmul(ops["rms"](x, row(w["norm_xq"])), *mw("w_mq")), row(w["mq_norm"]))
    kv = matmul(ops["rms"](mem, row(w["norm_mem"])), *mw("w_mkv"))
    mk = ops["headnorm"](kv[:, :HW], row(w["mk_norm"]))
    x = matmul_add(x, mem_attention(mq, mk, kv[:, HW:]), *mw("w_mo"))
    u = matmul(ops["rms"](x, row(w["norm_ffn"])), *mw("w_up"))
    act = convs["ffn"](u, w["ffn_conv"], row(w["ffn_conv_b"]))
    return matmul_add(x, act, *mw("w_down"))


def _forward(x, mem, layers, layers16):
    ops, convs = _make_rowops(), _make_convops()
    for w, w16 in zip(layers, layers16):
        x = _layer(x, mem, w, w16, ops, convs)
    return x


ANY = pl.BlockSpec(memory_space=pl.ANY)


N_PEERS = N_DEV - 1


def _ccall(body, *, name, out_shape, n_arrays):
    return pl.pallas_call(body, name=name, out_shape=out_shape, in_specs=[ANY] * n_arrays,
                          out_specs=[ANY] * n_arrays,
                          scratch_shapes=[pltpu.SemaphoreType.DMA((N_PEERS * n_arrays,)),
                                          pltpu.SemaphoreType.DMA((N_PEERS * n_arrays,)),
                                          pltpu.SemaphoreType.DMA((n_arrays,))],
                          interpret=False)


def _all_gather(name, shards):
    n = len(shards)

    def body(*refs):
        x_refs, out_refs = refs[:n], refs[n:2 * n]
        send_sems, recv_sems, local_sems = refs[2 * n:]
        x, y, c = lax.axis_index("x"), lax.axis_index("y"), lax.axis_index("c")
        me, sibling = (x, y, c), (x, y, 1 - c)
        chips = [(1 - x, y), (x, 1 - y), (1 - x, 1 - y)]

        def slot(a, px, py, pc):
            return out_refs[a].at[4 * px + 2 * py + pc]

        def copy(a, k, block, to, src=None):
            return pltpu.make_async_remote_copy(
                src_ref=slot(a, *block) if src is None else src, dst_ref=slot(a, *block),
                send_sem=send_sems.at[N_PEERS * a + k], recv_sem=recv_sems.at[N_PEERS * a + k], device_id=to,
                device_id_type=MESH)

        mine = [pltpu.make_async_copy(x_refs[a], slot(a, *me), local_sems.at[a]) for a in range(n)]
        first = []
        for a in range(n):
            first.append(copy(a, 0, me, sibling, src=x_refs[a]))
            first += [copy(a, 1 + j, me, (*chip, c), src=x_refs[a]) for j, chip in enumerate(chips)]
        for cp in mine + first:
            cp.start()
        passed = []
        for j, chip in enumerate(chips):
            for a in range(n):
                copy(a, 1 + j, (*chip, c), me).wait_recv()
                passed.append(copy(a, 4 + j, (*chip, c), sibling))
                passed[-1].start()
        for a in range(n):
            copy(a, 0, sibling, me).wait_recv()
            for j, chip in enumerate(chips):
                copy(a, 4 + j, (*chip, 1 - c), me).wait_recv()
        for cp in first + passed:
            cp.wait_send()
        for cp in mine:
            cp.wait()

    return _ccall(body, name=name, out_shape=[jax.ShapeDtypeStruct((N_DEV,) + s.shape, s.dtype) for s in shards],
                  n_arrays=n)(*shards)


def _exchange(name, parts):
    n = len(parts)

    def body(*refs):
        p_refs, out_refs = refs[:n], refs[n:2 * n]
        send_sems, recv_sems, local_sems = refs[2 * n:]
        x, y, c = lax.axis_index("x"), lax.axis_index("y"), lax.axis_index("c")
        me = 4 * x + 2 * y + c

        def peer(k):
            return (x ^ ((k >> 2) & 1), y ^ ((k >> 1) & 1), c ^ (k & 1))

        def copy(a, k, receive):
            px, py, pc = peer(k)
            theirs = 4 * px + 2 * py + pc
            return pltpu.make_async_remote_copy(
                src_ref=out_refs[a].at[theirs] if receive else p_refs[a].at[theirs],
                dst_ref=out_refs[a].at[theirs if receive else me],
                send_sem=send_sems.at[N_PEERS * a + k - 1], recv_sem=recv_sems.at[N_PEERS * a + k - 1],
                device_id=(px, py, pc), device_id_type=MESH)

        mine = [pltpu.make_async_copy(p_refs[a].at[me], out_refs[a].at[me], local_sems.at[a]) for a in range(n)]
        sends = [copy(a, k, False) for k in range(1, N_DEV) for a in range(n)]
        for cp in mine + sends:
            cp.start()
        for k in range(1, N_DEV):
            for a in range(n):
                copy(a, k, True).wait_recv()
        for cp in sends:
            cp.wait_send()
        for cp in mine:
            cp.wait()

    return _ccall(body, name=name, out_shape=[jax.ShapeDtypeStruct(p.shape, p.dtype) for p in parts],
                  n_arrays=n)(*parts)


ADAM_SLOT_BYTES = 4 * 1024 * 1024


def _adam_call(name, w, slots, m, v):
    r, n = w.shape
    rows_unit = 2 * SUBLANES
    tr = _pick(r, max(rows_unit, ADAM_SLOT_BYTES // (N_DEV * n * 4)), rows_unit)
    spec = pl.BlockSpec((tr, n), lambda i: (i, 0))

    def body(w_ref, s_ref, m_ref, v_ref, g_ref, d_ref, nm_ref, nv_ref):
        g = s_ref[0].astype(F32)
        for d in range(1, N_DEV):
            g = g + s_ref[d].astype(F32)
        nm = ADAM_B1 * m_ref[...] + (1.0 - ADAM_B1) * g
        nv = ADAM_B2 * v_ref[...] + (1.0 - ADAM_B2) * (g * g)
        m_hat = nm / (1.0 - ADAM_B1 ** ADAM_STEP)
        v_hat = nv / (1.0 - ADAM_B2 ** ADAM_STEP)
        g_ref[...] = g
        d_ref[...] = -ADAM_LR * (m_hat / (jnp.sqrt(v_hat) + ADAM_EPS) + ADAM_WD * w_ref[...])
        nm_ref[...] = nm
        nv_ref[...] = nv

    return _pcall(body, name=name, out_shape=[jax.ShapeDtypeStruct((r, n), F32)] * 4, grid=(r // tr,),
                  in_specs=[spec, pl.BlockSpec((N_DEV, tr, n), lambda i: (0, i, 0)), spec, spec],
                  out_specs=[spec] * 4)(w, slots, m, v)


def _pack_rows(flat, rows):
    return jnp.pad(flat, (0, rows * PACK_COLS - flat.shape[0])).reshape(rows, PACK_COLS)


def _regroup_in(w_in):
    cols = [w_in[:, a:b] for a, b in _IN_SRC]
    return jnp.concatenate(cols + [jnp.zeros((w_in.shape[0], N_IN_PAD - N_IN), w_in.dtype)], axis=1)


def _ungroup_in(d):
    starts = {}
    off = 0
    for a, b in _IN_SRC:
        starts[a] = (off, b - a)
        off += b - a
    return jnp.concatenate([d[:, starts[a][0]:starts[a][0] + starts[a][1]] for a in sorted(starts)], axis=1)


def _full_weights(gathered):
    out = {}
    for n, g in zip(SHARDED_ORDER, gathered):
        (r, c), axis = SHARDED[n]
        out[n] = g.reshape(r, c) if axis == 0 else g.transpose(1, 0, 2).reshape(r, c)
    out["w_in"] = _regroup_in(out["w_in"])
    return out


def _in_f32(full):
    return {n: v.astype(F32) for n, v in full.items()}


def _for_transport(name, shard):
    return shard if name in ("gdn_conv", "ffn_conv") else shard.astype(BF16)


def _grad_parts(grads):
    parts = []
    for n in SHARDED_ORDER:
        (r, c), axis = SHARDED[n]
        g = _ungroup_in(grads[n]) if n == "w_in" else grads[n]
        if axis == 0:
            parts.append(g.reshape(N_DEV, r // N_DEV, c).astype(BF16))
        else:
            parts.append(g.reshape(r, N_DEV, c // N_DEV).transpose(1, 0, 2).astype(BF16))
    return parts


def _pack_small(vals):
    return _pack_rows(jnp.concatenate([vals[n].reshape(-1) for n in SMALL_ORDER]), SMALL_ROWS)


def _unpack_small(packed):
    flat = packed.reshape(-1)
    out, off = {}, 0
    for n in SMALL_ORDER:
        size = DEPTH * SMALL_WIDTH[n]
        out[n] = flat[off:off + size].reshape(DEPTH, SMALL_WIDTH[n])
        off += size
    return out


def kernel(x, mem, norm_mix, w_in, fox_fbias, fox_qnorm, fox_knorm, gdn_conv, gdn_a_log, gdn_dt_bias, gdn_onorm, gate_bias, w_oa, w_ob, w_oc, w_out, norm_xq, norm_mem, w_mq, w_mkv, mq_norm, mk_norm, w_mo, norm_ffn, w_up, ffn_conv, ffn_conv_b, w_down, loss_target, m_norm_mix, m_w_in, m_fox_fbias, m_fox_qnorm, m_fox_knorm, m_gdn_conv, m_gdn_a_log, m_gdn_dt_bias, m_gdn_onorm, m_gate_bias, m_w_oa, m_w_ob, m_w_oc, m_w_out, m_norm_xq, m_norm_mem, m_w_mq, m_w_mkv, m_mq_norm, m_mk_norm, m_w_mo, m_norm_ffn, m_w_up, m_ffn_conv, m_ffn_conv_b, m_w_down, v_norm_mix, v_w_in, v_fox_fbias, v_fox_qnorm, v_fox_knorm, v_gdn_conv, v_gdn_a_log, v_gdn_dt_bias, v_gdn_onorm, v_gate_bias, v_w_oa, v_w_ob, v_w_oc, v_w_out, v_norm_xq, v_norm_mem, v_w_mq, v_w_mkv, v_mq_norm, v_mk_norm, v_w_mo, v_norm_ffn, v_w_up, v_ffn_conv, v_ffn_conv_b, v_w_down):
    given = dict(locals())
    wts = {n: given[n] for n in WEIGHTS}
    mom = {n: given["m_" + n] for n in WEIGHTS}
    var = {n: given["v_" + n] for n in WEIGHTS}

    layers, layers16 = [], []
    for l in range(DEPTH):
        full = _full_weights(_all_gather("gather_weights", [_for_transport(n, wts[n][l]) for n in SHARDED_ORDER]))
        layers16.append(full)
        layers.append({**_in_f32(full), **{n: wts[n][l] for n in SMALL_ORDER}})

    y, vjp = jax.vjp(lambda xx, ww: _forward(xx, mem[0], ww, layers16), x[0], layers)
    loss_part, dy = _loss_call(y, loss_target[0])
    dx, dlayers = vjp(dy)
    loss = lax.psum(loss_part, ("x", "y", "c"))

    out = {}
    per_layer = []
    for l in range(DEPTH):
        slots = _exchange("exchange_grads", _grad_parts(dlayers[l]))
        per_layer.append({n: _adam_call("adam_shard", wts[n][l], sl, mom[n][l], var[n][l])
                          for n, sl in zip(SHARDED_ORDER, slots)})
    for n in SHARDED_ORDER:
        for k, kind in enumerate(("grad_", "delta_", "new_m_", "new_v_")):
            out[kind + n] = jnp.stack([per_layer[l][n][k] for l in range(DEPTH)])
    dsmall = {n: jnp.stack([dlayers[l][n] for l in range(DEPTH)]) for n in SMALL_ORDER}
    slots = _all_gather("gather_small_grads", [_pack_small(dsmall)])[0]
    res = _adam_call("adam_small", _pack_small(wts), slots, _pack_small(mom), _pack_small(var))
    for k, kind in enumerate(("grad_", "delta_", "new_m_", "new_v_")):
        un = _unpack_small(res[k])
        for n in SMALL_ORDER:
            out[kind + n] = un[n].reshape(wts[n].shape)

    return (loss, dx[None], *[out["grad_" + n] for n in WEIGHTS], *[out["delta_" + n] for n in WEIGHTS],
            *[out["new_m_" + n] for n in WEIGHTS], *[out["new_v_" + n] for n in WEIGHTS])
```

```python
import jax
import jax.numpy as jnp
from jax import lax
from jax.experimental import pallas as pl
from jax.experimental.pallas import tpu as pltpu

F32 = jnp.float32
BF16 = jnp.bfloat16

N_DEV = 8
D_MODEL = 1024
DEPTH = 4
CHUNK = 64
EPS = 1e-6
HEADS = 4
HEAD_DIM = 128
HW = HEADS * HEAD_DIM
D_FF = 2816
N_IN = 8204
LANES = 128
SUBLANES = 8
VMEM_LIMIT = 56 * 1024 * 1024

ADAM_LR = 0.001
ADAM_B1 = 0.9
ADAM_B2 = 0.999
ADAM_EPS = 1e-08
ADAM_WD = 0.01
ADAM_STEP = 10

NEG = -1e30
MESH = pl.DeviceIdType.MESH

WEIGHTS = ['norm_mix', 'w_in', 'fox_fbias', 'fox_qnorm', 'fox_knorm', 'gdn_conv', 'gdn_a_log', 'gdn_dt_bias',
           'gdn_onorm', 'gate_bias', 'w_oa', 'w_ob', 'w_oc', 'w_out', 'norm_xq', 'norm_mem', 'w_mq', 'w_mkv',
           'mq_norm', 'mk_norm', 'w_mo', 'norm_ffn', 'w_up', 'ffn_conv', 'ffn_conv_b', 'w_down']
SHARDED = {
    'w_in': ((D_MODEL, N_IN), 0), 'gdn_conv': ((4, 3 * HW), 1), 'w_oa': ((HW, D_MODEL), 1),
    'w_ob': ((HW, D_MODEL), 1), 'w_oc': ((HW, D_MODEL), 1), 'w_out': ((D_MODEL, D_MODEL), 0),
    'w_mq': ((D_MODEL, HW), 0), 'w_mkv': ((D_MODEL, 2 * HW), 0), 'w_mo': ((HW, D_MODEL), 1),
    'w_up': ((D_MODEL, 2 * D_FF), 1), 'ffn_conv': ((3, 2 * D_FF), 1), 'w_down': ((D_FF, D_MODEL), 0),
}
SHARDED_ORDER = [n for n in WEIGHTS if n in SHARDED]
SMALL_ORDER = [n for n in WEIGHTS if n not in SHARDED]
SMALL_WIDTH = {'norm_mix': D_MODEL, 'fox_fbias': HEADS, 'fox_qnorm': HEAD_DIM, 'fox_knorm': HEAD_DIM,
               'gdn_a_log': HEADS, 'gdn_dt_bias': HEADS, 'gdn_onorm': HEAD_DIM, 'gate_bias': 3 * D_MODEL,
               'norm_xq': D_MODEL, 'norm_mem': D_MODEL, 'mq_norm': HEAD_DIM, 'mk_norm': HEAD_DIM,
               'norm_ffn': D_MODEL, 'ffn_conv_b': 2 * D_FF}
PACK_COLS = 1024


def _round_up(n, m):
    return (n + m - 1) // m * m


SMALL_ROWS = _round_up(DEPTH * sum(SMALL_WIDTH.values()), SUBLANES * PACK_COLS) // PACK_COLS

_IN_SRC = [(0, 512), (512, 1024), (1024, 1536),
           (1540, 2052), (2052, 2564), (2564, 3076),
           (3084, 3596),
           (3596, 4108), (4108, 4620), (4620, 5132),
           (5132, 8204),
           (1536, 1540), (3076, 3080), (3080, 3084)]
N_IN_PAD = 8320
LANE_FF, LANE_GB, LANE_GA = 0, 4, 8


def _pick(dim, pref, unit):
    best = None
    t = unit
    while t <= min(dim, pref):
        if dim % t == 0:
            best = t
        t += unit
    return dim if best is None else best


def _params(n_grid):
    return pltpu.CompilerParams(dimension_semantics=("arbitrary",) * n_grid, vmem_limit_bytes=VMEM_LIMIT)


def _pcall(body, *, name, out_shape, grid, in_specs, out_specs, scratch_shapes=()):
    return pl.pallas_call(body, name=name, out_shape=out_shape, grid=grid, in_specs=in_specs, out_specs=out_specs,
                          scratch_shapes=scratch_shapes, compiler_params=_params(len(grid)),
                          interpret=False)


NN = ((1,), (0,))
NT = ((1,), (1,))
TN = ((0,), (0,))


def _dot(a, b, dn):
    return lax.dot_general(a.astype(BF16), b.astype(BF16), (dn, ((), ())), preferred_element_type=F32)


def _dotf(a, b, dn):
    return lax.dot_general(a, b, (dn, ((), ())), precision=lax.Precision.HIGHEST, preferred_element_type=F32)


@jax.custom_vjp
def mm(a, b):
    return _dot(a, b, NN)


mm.defvjp(lambda a, b: (_dot(a, b, NN), (a, b)), lambda r, g: (_dot(g, r[1], NT), _dot(r[0], g, TN)))


@jax.custom_vjp
def mm_nt(a, b):
    return _dot(a, b, NT)


mm_nt.defvjp(lambda a, b: (_dot(a, b, NT), (a, b)), lambda r, g: (_dot(g, r[1], NN), _dot(g, r[0], TN)))


@jax.custom_vjp
def mm_tn(a, b):
    return _dot(a, b, TN)


mm_tn.defvjp(lambda a, b: (_dot(a, b, TN), (a, b)), lambda r, g: (_dot(r[1], g, NT), _dot(r[0], g, NN)))


@jax.custom_vjp
def tri_apply(t, x):
    return _dotf(t, x, NN)


tri_apply.defvjp(lambda t, x: (_dotf(t, x, NN), t), lambda t, g: (jnp.zeros_like(t), _dotf(t, g, TN)))


def _sigmoid(x):
    return 1.0 / (1.0 + jnp.exp(-x))


@jax.custom_vjp
def _softplus(x):
    return jnp.maximum(x, 0.0) + jnp.log(1.0 + jnp.exp(-jnp.abs(x)))


_softplus.defvjp(lambda x: (_softplus(x), x), lambda x, g: (g * _sigmoid(x),))


@jax.custom_vjp
def _silu(x):
    return x * _sigmoid(x)


def _silu_fwd(x):
    s = _sigmoid(x)
    return x * s, (x, s)


_silu.defvjp(_silu_fwd, lambda r, g: (g * r[1] * (1.0 + r[0] * (1.0 - r[1])),))


def _rms(x, g):
    return x * lax.rsqrt(jnp.mean(x * x, axis=-1, keepdims=True) + EPS) * g


def _iota2(n, m, axis):
    return lax.broadcasted_iota(jnp.int32, (n, m), axis)


def _whole(width):
    return [(0, width)]


def _split(width, n):
    w = width // n
    return [(k * w, w) for k in range(n)]


class RowOp:
    def __init__(self, name, f, in_pieces, out_pieces, tm=256):
        self.name, self.f, self.in_pieces, self.out_pieces, self.tm = name, f, in_pieces, out_pieces, tm
        op = jax.custom_vjp(self._fwd_call)
        op.defvjp(lambda *a: (self._fwd_call(*a), a), lambda res, g: self._bwd_call(res, g))
        self.op = op

    def __call__(self, *args):
        return self.op(*args)

    def _width(self, pieces):
        return max(o + w for o, w in pieces)

    def _row_specs(self, pieces_list, tm):
        return [pl.BlockSpec((tm, self._width(p)), lambda i: (i, 0)) for p in pieces_list]

    def _fwd_call(self, *args):
        nr = len(self.in_pieces)
        rows, params = args[:nr], args[nr:]
        m = rows[0].shape[0]
        tm = _pick(m, self.tm, SUBLANES)
        f, in_pieces, out_pieces = self.f, self.in_pieces, self.out_pieces
        no = len(out_pieces)

        def body(*refs):
            rin, pr, ro = refs[:nr], refs[nr:nr + len(params)], refs[nr + len(params):]
            xs = [r[:, o:o + w] for r, ps in zip(rin, in_pieces) for (o, w) in ps]
            ys = f(*xs, *[p[...] for p in pr])
            k = 0
            for r, ps in zip(ro, out_pieces):
                for (o, w) in ps:
                    r[:, o:o + w] = ys[k]
                    k += 1

        outs = _pcall(
            body, name=self.name + "_fwd",
            out_shape=[jax.ShapeDtypeStruct((m, self._width(p)), F32) for p in out_pieces],
            grid=(m // tm,),
            in_specs=self._row_specs(in_pieces, tm) + [pl.BlockSpec(p.shape, lambda i: (0, 0)) for p in params],
            out_specs=self._row_specs(out_pieces, tm),
        )(*rows, *params)
        return tuple(outs) if no > 1 else outs[0]

    def _bwd_call(self, res, g):
        nr = len(self.in_pieces)
        rows, params = res[:nr], res[nr:]
        no = len(self.out_pieces)
        gs = tuple(g) if no > 1 else (g,)
        m = rows[0].shape[0]
        tm = _pick(m, self.tm, SUBLANES)
        f, in_pieces, out_pieces = self.f, self.in_pieces, self.out_pieces
        npar = len(params)

        def body(*refs):
            rin, pr, dro = refs[:nr], refs[nr:nr + npar], refs[nr + npar:nr + npar + no]
            drin, dpr = refs[nr + npar + no:nr + npar + no + nr], refs[nr + npar + no + nr:]
            xs = [r[:, o:o + w] for r, ps in zip(rin, in_pieces) for (o, w) in ps]
            dys = [r[:, o:o + w] for r, ps in zip(dro, out_pieces) for (o, w) in ps]
            _, vjp = jax.vjp(lambda *a: tuple(f(*a)), *xs, *[p[...] for p in pr])
            grads = vjp(tuple(dys))
            k = 0
            for r, ps in zip(drin, in_pieces):
                for (o, w) in ps:
                    r[:, o:o + w] = grads[k]
                    k += 1

            @pl.when(pl.program_id(0) == 0)
            def _():
                for r in dpr:
                    r[...] = jnp.zeros_like(r)

            for j, r in enumerate(dpr):
                r[...] += grads[k + j]

        outs = _pcall(
            body, name=self.name + "_bwd",
            out_shape=[jax.ShapeDtypeStruct(r.shape, F32) for r in rows]
            + [jax.ShapeDtypeStruct(p.shape, F32) for p in params],
            grid=(m // tm,),
            in_specs=self._row_specs(in_pieces, tm) + [pl.BlockSpec(p.shape, lambda i: (0, 0)) for p in params]
            + self._row_specs(out_pieces, tm),
            out_specs=self._row_specs(in_pieces, tm) + [pl.BlockSpec(p.shape, lambda i: (0, 0)) for p in params],
        )(*rows, *params, *gs)
        return tuple(outs)


def _f_rms(x, g):
    return (_rms(x, g),)


def _f_headnorm(x0, x1, x2, x3, g):
    return tuple(_rms(x, g) for x in (x0, x1, x2, x3))


def _f_small(sm, fb, al, db):
    tm = sm.shape[0]
    logf = -_softplus(-(sm + fb))
    beta = _sigmoid(sm)
    glog = -jnp.exp(al) * _softplus(sm + db)
    r, c = _iota2(tm, tm, 0), _iota2(tm, tm, 1)
    bd = jnp.where((r >= c) & (jnp.bitwise_xor(r, c) < CHUNK), 1.0, 0.0).astype(F32)
    return logf, beta, tri_apply(bd, glog)


def _f_gdnpost(o0, o1, o2, o3, z0, z1, z2, z3, g):
    return tuple(_rms(o, g) * _silu(z) for o, z in zip((o0, o1, o2, o3), (z0, z1, z2, z3)))


def _f_merge(t0, t1, t2, a, b, c, b0, b1, b2):
    return (_sigmoid(t0 + b0) * a + _sigmoid(t1 + b1) * b + _sigmoid(t2 + b2) * c,)


def _make_rowops():
    return dict(
        rms=RowOp("rms", _f_rms, [_whole(D_MODEL)], [_whole(D_MODEL)], tm=512),
        headnorm=RowOp("headnorm", _f_headnorm, [_split(HW, HEADS)], [_split(HW, HEADS)], tm=1024),
        small=RowOp("smallprep", _f_small, [_whole(LANES)], [_whole(LANES)] * 3),
        gdnpost=RowOp("gdnpost", _f_gdnpost, [_split(HW, HEADS)] * 2, [_split(HW, HEADS)], tm=512),
        merge=RowOp("merge", _f_merge, [_split(3 * D_MODEL, 3)] + [_whole(D_MODEL)] * 3, [_whole(D_MODEL)]),
    )


ROUND_ONCE_READS = 3


def _b16(x):
    return x.astype(BF16)


def _mm_call(name, a, b, mode, c=None):
    if mode == "nn":
        (m, kc), n = a.shape, b.shape[1]
    elif mode == "nt":
        (m, kc), n = a.shape, b.shape[0]
    else:
        (kc, m), n = a.shape, b.shape[1]
    tm = _pick(m, 1408, LANES) if mode == "tn" else _pick(m, 1024, SUBLANES)
    tn = _pick(n, 1408, LANES)
    tk = _pick(kc, 1024, SUBLANES) if mode == "tn" else _pick(kc, 1536, LANES)
    if a.dtype == F32 and n // tn >= ROUND_ONCE_READS:
        a = _b16(a)
    if b.dtype == F32 and m // tm >= ROUND_ONCE_READS:
        b = _b16(b)
    dn = {"nn": NN, "nt": NT, "tn": TN}[mode]
    a_spec = {"nn": pl.BlockSpec((tm, tk), lambda i, j, k: (i, k)),
              "nt": pl.BlockSpec((tm, tk), lambda i, j, k: (i, k)),
              "tn": pl.BlockSpec((tk, tm), lambda i, j, k: (k, i))}[mode]
    b_spec = {"nn": pl.BlockSpec((tk, tn), lambda i, j, k: (k, j)),
              "nt": pl.BlockSpec((tn, tk), lambda i, j, k: (j, k)),
              "tn": pl.BlockSpec((tk, tn), lambda i, j, k: (k, j))}[mode]
    o_spec = pl.BlockSpec((tm, tn), lambda i, j, k: (i, j))
    has_c = c is not None

    def body(*refs):
        a_ref, b_ref = refs[0], refs[1]
        o_ref = refs[-1]

        @pl.when(pl.program_id(2) == 0)
        def _():
            o_ref[...] = refs[2][...] if has_c else jnp.zeros_like(o_ref)

        o_ref[...] += _dot(a_ref[...], b_ref[...], dn)

    return _pcall(body, name=name, out_shape=jax.ShapeDtypeStruct((m, n), F32), grid=(m // tm, n // tn, kc // tk),
                  in_specs=[a_spec, b_spec] + ([o_spec] if has_c else []), out_specs=o_spec,
                  )(*((a, b, c) if has_c else (a, b)))


@jax.custom_vjp
def matmul(a, w, w16):
    return _mm_call("mm_nn", a, w16, "nn")


def _matmul_bwd(res, g):
    a, w16 = res
    return _mm_call("mm_nt", g, w16, "nt"), _mm_call("mm_tn", a, g, "tn"), jnp.zeros_like(w16)


matmul.defvjp(lambda a, w, w16: (_mm_call("mm_nn", a, w16, "nn"), (a, w16)), _matmul_bwd)


@jax.custom_vjp
def matmul_add(c, a, w, w16):
    return _mm_call("mm_nn_add", a, w16, "nn", c)


matmul_add.defvjp(lambda c, a, w, w16: (_mm_call("mm_nn_add", a, w16, "nn", c), (a, w16)),
                  lambda res, g: (g,) + _matmul_bwd(res, g))

_PROJ_GROUPS = [(0, 512), (512, 512), (1024, 512), (1536, 1536), (3072, 512), (3584, 512), (4096, 512), (4608, 512),
                (5120, 3072), (8192, 128)]


def _proj_fwd(h, w, w16):
    h16 = _b16(h)
    return tuple(_mm_call("proj_nn", h16, w16[:, s:s + n], "nn") for s, n in _PROJ_GROUPS), (h16, w16)


proj = jax.custom_vjp(lambda h, w, w16: _proj_fwd(h, w, w16)[0])


def _proj_bwd(res, gs):
    h16, w16 = res
    dh = None
    dws = []
    for (s, n), g in zip(_PROJ_GROUPS, gs):
        dh = _mm_call("proj_nt", g, w16[:, s:s + n], "nt", dh)
        dws.append(_mm_call("proj_tn", h16, g, "tn"))
    return dh, jnp.concatenate(dws, axis=1), jnp.zeros_like(w16)


proj.defvjp(_proj_fwd, _proj_bwd)


def _cumsum_call(x, reverse):
    s, w = x.shape
    tm = _pick(s, 256, SUBLANES)
    nb = s // tm

    def body(x_ref, o_ref, carry):
        @pl.when(pl.program_id(0) == 0)
        def _():
            carry[...] = jnp.zeros_like(carry)

        blk = x_ref[...]
        r, c = _iota2(tm, tm, 0), _iota2(tm, tm, 1)
        tri = jnp.where((r <= c) if reverse else (r >= c), 1.0, 0.0).astype(F32)
        o_ref[...] = _dotf(tri, blk, NN) + carry[...]
        carry[...] += jnp.sum(blk, axis=0, keepdims=True)

    idx = (lambda i: (nb - 1 - i, 0)) if reverse else (lambda i: (i, 0))
    return _pcall(body, name="cumsum_rev" if reverse else "cumsum", out_shape=jax.ShapeDtypeStruct((s, w), F32),
                  grid=(nb,), in_specs=[pl.BlockSpec((tm, w), idx)], out_specs=pl.BlockSpec((tm, w), idx),
                  scratch_shapes=[pltpu.VMEM((1, w), F32)])(x)


@jax.custom_vjp
def seq_cumsum(x):
    return _cumsum_call(x, False)


seq_cumsum.defvjp(lambda x: (_cumsum_call(x, False), None), lambda _, g: (_cumsum_call(g, True),))


HALO = SUBLANES


class ConvOp:
    def __init__(self, name, width, post, c_pieces, out_widths, has_bias, tm):
        self.name, self.width, self.post, self.c_pieces = name, width, post, c_pieces
        self.out_widths, self.has_bias, self.tm = out_widths, has_bias, tm
        op = jax.custom_vjp(self._fwd_call)
        op.defvjp(lambda *a: (self._fwd_call(*a), a), lambda res, g: self._bwd_call(res, g))
        self.op = op

    def __call__(self, *args):
        return self.op(*args)

    def _conv(self, i, x_ref, prev_ref, w_ref, b_ref, buf):
        tm = x_ref.shape[0]
        buf[0:HALO, :] = jnp.where(i > 0, prev_ref[...], 0.0)
        buf[HALO:HALO + tm, :] = x_ref[...]
        taps = [buf[pl.ds(HALO - (self.width - 1) + j, tm), :] for j in range(self.width)]
        c = taps[0] * w_ref[0:1, :]
        for j in range(1, self.width):
            c = c + taps[j] * w_ref[j:j + 1, :]
        if self.has_bias:
            c = c + b_ref[...]
        return c, taps

    def _fwd_call(self, x, w, *bias):
        s, ch = x.shape
        tm = _pick(s, self.tm, SUBLANES)
        r8 = tm // HALO
        has_bias, post, c_pieces = self.has_bias, self.post, self.c_pieces

        def body(*refs):
            x_ref, prev_ref, w_ref = refs[:3]
            b_ref = refs[3] if has_bias else None
            outs, buf = refs[3 + has_bias:-1], refs[-1]
            c, _ = self._conv(pl.program_id(0), x_ref, prev_ref, w_ref, b_ref, buf)
            ys = post(*[c[:, o:o + n] for o, n in c_pieces])
            for r, y in zip(outs, ys):
                r[...] = y

        outs = _pcall(
            body, name=self.name + "_fwd", out_shape=[jax.ShapeDtypeStruct((s, n), F32) for n in self.out_widths],
            grid=(s // tm,),
            in_specs=[pl.BlockSpec((tm, ch), lambda i: (i, 0)),
                      pl.BlockSpec((HALO, ch), lambda i: (jnp.maximum(i * r8 - 1, 0), 0)),
                      pl.BlockSpec(w.shape, lambda i: (0, 0))]
            + ([pl.BlockSpec((1, ch), lambda i: (0, 0))] if has_bias else []),
            out_specs=[pl.BlockSpec((tm, n), lambda i: (i, 0)) for n in self.out_widths],
            scratch_shapes=[pltpu.VMEM((tm + HALO, ch), F32)],
        )(x, x, w, *bias)
        return tuple(outs) if len(outs) > 1 else outs[0]

    def _bwd_call(self, res, g):
        x, w = res[0], res[1]
        bias = res[2:]
        gs = tuple(g) if len(self.out_widths) > 1 else (g,)
        s, ch = x.shape
        tm = _pick(s, self.tm, SUBLANES)
        r8 = tm // HALO
        nb = s // tm
        has_bias, post, c_pieces, width = self.has_bias, self.post, self.c_pieces, self.width
        ng = len(gs)

        def body1(*refs):
            x_ref, prev_ref, w_ref = refs[:3]
            b_ref = refs[3] if has_bias else None
            k = 3 + has_bias
            g_refs = refs[k:k + ng]
            dc_ref, dw_ref = refs[k + ng], refs[k + ng + 1]
            db_ref = refs[k + ng + 2] if has_bias else None
            buf = refs[-1]
            i = pl.program_id(0)
            c, taps = self._conv(i, x_ref, prev_ref, w_ref, b_ref, buf)
            _, vjp = jax.vjp(lambda *a: tuple(post(*a)), *[c[:, o:o + n] for o, n in c_pieces])
            dcs = vjp(tuple(r[...] for r in g_refs))
            for (o, n), d in zip(c_pieces, dcs):
                dc_ref[:, o:o + n] = d

            @pl.when(i == 0)
            def _():
                dw_ref[...] = jnp.zeros_like(dw_ref)
                if has_bias:
                    db_ref[...] = jnp.zeros_like(db_ref)

            dc = dc_ref[...]
            for j in range(width):
                dw_ref[j:j + 1, :] += jnp.sum(dc * taps[j], axis=0, keepdims=True)
            if has_bias:
                db_ref[...] += jnp.sum(dc, axis=0, keepdims=True)

        outs1 = _pcall(
            body1, name=self.name + "_bwd_act",
            out_shape=[jax.ShapeDtypeStruct((s, ch), F32), jax.ShapeDtypeStruct(w.shape, F32)]
            + ([jax.ShapeDtypeStruct((1, ch), F32)] if has_bias else []),
            grid=(nb,),
            in_specs=[pl.BlockSpec((tm, ch), lambda i: (i, 0)),
                      pl.BlockSpec((HALO, ch), lambda i: (jnp.maximum(i * r8 - 1, 0), 0)),
                      pl.BlockSpec(w.shape, lambda i: (0, 0))]
            + ([pl.BlockSpec((1, ch), lambda i: (0, 0))] if has_bias else [])
            + [pl.BlockSpec((tm, n), lambda i: (i, 0)) for n in self.out_widths],
            out_specs=[pl.BlockSpec((tm, ch), lambda i: (i, 0)), pl.BlockSpec(w.shape, lambda i: (0, 0))]
            + ([pl.BlockSpec((1, ch), lambda i: (0, 0))] if has_bias else []),
            scratch_shapes=[pltpu.VMEM((tm + HALO, ch), F32)],
        )(x, x, w, *bias, *gs)
        dc, dw = outs1[0], outs1[1]

        def body2(dc_ref, next_ref, w_ref, dx_ref, buf):
            i = pl.program_id(0)
            buf[0:tm, :] = dc_ref[...]
            buf[tm:tm + HALO, :] = jnp.where(i < nb - 1, next_ref[...], 0.0)
            dx = buf[pl.ds(width - 1, tm), :] * w_ref[0:1, :]
            for j in range(1, width):
                dx = dx + buf[pl.ds(width - 1 - j, tm), :] * w_ref[j:j + 1, :]
            dx_ref[...] = dx

        dx = _pcall(
            body2, name=self.name + "_bwd_in", out_shape=jax.ShapeDtypeStruct((s, ch), F32), grid=(nb,),
            in_specs=[pl.BlockSpec((tm, ch), lambda i: (i, 0)),
                      pl.BlockSpec((HALO, ch), lambda i: (jnp.minimum((i + 1) * r8, s // HALO - 1), 0)),
                      pl.BlockSpec(w.shape, lambda i: (0, 0))],
            out_specs=pl.BlockSpec((tm, ch), lambda i: (i, 0)),
            scratch_shapes=[pltpu.VMEM((tm + HALO, ch), F32)],
        )(dc, dc, w)
        return (dx, dw) + ((outs1[2],) if has_bias else ())


def _make_convops():
    return dict(
        gdn=ConvOp("gdnconv", 4, lambda q, k, v: (_silu(q), _silu(k), _silu(v)), _split(3 * HW, 3), [HW] * 3,
                   False, 256),
        ffn=ConvOp("ffnconv", 3, lambda a, b: (_silu(a) * b,), _split(2 * D_FF, 2), [D_FF], True, 256),
    )


ATT_Q = 512
ATT_K = 256
ATT_K_FOX = 512
SCALE = HEAD_DIM ** -0.5


ATT_Q_SB = 1024


def _att_tiles(s, fox=False):
    tk = _pick(s, ATT_K_FOX if fox else ATT_K, LANES)
    tq = _pick(s, ATT_Q if fox else ATT_Q_SB, tk)
    return tq, tk


def _att_specs(s, tq, tk):
    qspec = pl.BlockSpec((tq, HEAD_DIM), lambda h, i: (i, h))
    kspec = pl.BlockSpec((s, HEAD_DIM), lambda h, i: (0, h))
    colspec = pl.BlockSpec((None, tq, 1), lambda h, i: (h, i, 0))
    rowspec = pl.BlockSpec((None, s // tk, 1, tk), lambda h, i: (h, 0, 0, 0))
    return qspec, kspec, colspec, rowspec


def _krows(kb, tk):
    return pl.ds(pl.multiple_of(kb * tk, tk), tk)


def _stage_bf16(i, pairs):
    @pl.when(i == 0)
    def _():
        for src, dst in pairs:
            dst[...] = src[...].astype(BF16)


ATT_STRIP = 32


def _strips(tq):
    return [slice(r, r + ATT_STRIP) for r in range(0, tq, ATT_STRIP)]


def _visible(i, kb, rs, tq, tk, strict):
    rows = i * tq + rs.start + _iota2(ATT_STRIP, tk, 0)
    cols = kb * tk + _iota2(ATT_STRIP, tk, 1)
    return (cols < rows) if strict else (cols <= rows)


def _visible_block(i, kb, tq, tk):
    return kb * tk + _iota2(tq, tk, 1) <= i * tq + _iota2(tq, tk, 0)


def _blocks(i, ratio, blk, reverse=False):
    def full(n, carry):
        blk(i * ratio - 1 - n if reverse else n, False)
        return carry

    if reverse:
        for j in reversed(range(ratio)):
            blk(i * ratio + j, True)
    lax.fori_loop(0, i * ratio, full, 0)
    if not reverse:
        for j in range(ratio):
            blk(i * ratio + j, True)


def _vm(shape, dtype):
    return pltpu.VMEM(shape, dtype)


def _fox_fwd_call(q, k, v, ccol, crow):
    s = q.shape[0]
    tq, tk = _att_tiles(s, fox=True)
    qspec, kspec, colspec, rowspec = _att_specs(s, tq, tk)

    def body(q_ref, k_ref, v_ref, cq_ref, ck_ref, o_ref, lse_ref, k16, v16):
        i = pl.program_id(1)
        ratio = tq // tk
        _stage_bf16(i, [(k_ref, k16), (v_ref, v16)])
        qb = q_ref[...].astype(BF16)
        cq = cq_ref[...]

        def blk(kb, carry, masked):
            m, l, acc = carry
            sc = _dot(qb, k16[_krows(kb, tk), :], NT) * SCALE + (cq - ck_ref[kb])
            if masked:
                sc = jnp.where(_visible_block(i, kb, tq, tk), sc, NEG)
            m_new = jnp.maximum(m, jnp.max(sc, axis=-1, keepdims=True))
            alpha = jnp.exp(m - m_new)
            p = jnp.exp(sc - m_new)
            return (m_new, alpha * l + jnp.sum(p, axis=-1, keepdims=True),
                    alpha * acc + _dot(p, v16[_krows(kb, tk), :], NN))

        carry = (jnp.full((tq, 1), NEG, F32), jnp.zeros((tq, 1), F32), jnp.zeros((tq, HEAD_DIM), F32))
        carry = lax.fori_loop(0, i * ratio, lambda kb, c: blk(kb, c, False), carry)
        for j in range(ratio):
            carry = blk(i * ratio + j, carry, True)
        m, l, acc = carry
        o_ref[...] = acc / l
        lse_ref[...] = m + jnp.log(l)

    return _pcall(body, name="fox_fwd",
                  out_shape=[jax.ShapeDtypeStruct((s, HW), F32), jax.ShapeDtypeStruct((HEADS, s, 1), F32)],
                  grid=(HEADS, s // tq), in_specs=[qspec, kspec, kspec, colspec, rowspec],
                  out_specs=[qspec, colspec],
                  scratch_shapes=[_vm((s, HEAD_DIM), BF16), _vm((s, HEAD_DIM), BF16)])(q, k, v, ccol, crow)


def _fox_bwd_call(q, k, v, ccol, crow, o, lse, do):
    s = q.shape[0]
    tq, tk = _att_tiles(s, fox=True)
    ratio = tq // tk
    qspec, kspec, colspec, rowspec = _att_specs(s, tq, tk)

    def body(q_ref, k_ref, v_ref, cq_ref, ck_ref, o_ref, lse_ref, do_ref, dq_ref, dk_ref, dv_ref, dcq_ref, dck_ref,
             k16, v16):
        i = pl.program_id(1)
        _stage_bf16(i, [(k_ref, k16), (v_ref, v16)])

        @pl.when(i == 0)
        def _():
            dk_ref[...] = jnp.zeros_like(dk_ref)
            dv_ref[...] = jnp.zeros_like(dv_ref)
            dck_ref[...] = jnp.zeros_like(dck_ref)

        qb = q_ref[...].astype(BF16)
        dob = do_ref[...].astype(BF16)
        cq, lse = cq_ref[...], lse_ref[...]
        dl = jnp.sum(do_ref[...] * o_ref[...], axis=-1, keepdims=True)

        def blk(kb, carry, masked):
            dq, dcq = carry
            rows = _krows(kb, tk)
            kk, vv = k16[rows, :], v16[rows, :]
            sc = _dot(qb, kk, NT) * SCALE + (cq - ck_ref[kb])
            p = jnp.exp(sc - lse)
            if masked:
                p = jnp.where(_visible_block(i, kb, tq, tk), p, 0.0)
            dv_ref[rows, :] += _dot(p, dob, TN)
            ds = p * (_dot(dob, vv, NT) - dl)
            dk_ref[rows, :] += _dot(ds, qb, TN) * SCALE
            dck_ref[kb] += -jnp.sum(ds, axis=0, keepdims=True)
            return dq + _dot(ds, kk, NN) * SCALE, dcq + jnp.sum(ds, axis=-1, keepdims=True)

        carry = (jnp.zeros((tq, HEAD_DIM), F32), jnp.zeros((tq, 1), F32))
        carry = lax.fori_loop(0, i * ratio, lambda kb, c: blk(kb, c, False), carry)
        for j in range(ratio):
            carry = blk(i * ratio + j, carry, True)
        dq_ref[...] = carry[0]
        dcq_ref[...] = carry[1]

    return _pcall(body, name="fox_bwd",
                  out_shape=[jax.ShapeDtypeStruct((s, HW), F32)] * 3
                  + [jax.ShapeDtypeStruct((HEADS, s, 1), F32), jax.ShapeDtypeStruct((HEADS, s // tk, 1, tk), F32)],
                  grid=(HEADS, s // tq),
                  in_specs=[qspec, kspec, kspec, colspec, rowspec, qspec, colspec, qspec],
                  out_specs=[qspec, kspec, kspec, colspec, rowspec],
                  scratch_shapes=[_vm((s, HEAD_DIM), BF16), _vm((s, HEAD_DIM), BF16)],
                  )(q, k, v, ccol, crow, o, lse, do)


@jax.custom_vjp
def fox_attention(q, k, v, ccol, crow):
    return _fox_fwd_call(q, k, v, ccol, crow)[0]


def _fox_vjp_fwd(q, k, v, ccol, crow):
    o, lse = _fox_fwd_call(q, k, v, ccol, crow)
    return o, (q, k, v, ccol, crow, o, lse)


fox_attention.defvjp(_fox_vjp_fwd, lambda res, g: tuple(_fox_bwd_call(*res, g)))


def _sb_fwd_call(q, k, v):
    s = q.shape[0]
    tq, tk = _att_tiles(s)
    ratio = tq // tk
    qspec, kspec, colspec, _ = _att_specs(s, tq, tk)

    def body(q_ref, k_ref, v_ref, o_ref, tot_ref, k16, v16, q16, s_scr, w_scr, lk16, a16, run, acc):
        i = pl.program_id(1)
        _stage_bf16(i, [(k_ref, k16), (v_ref, v16)])
        q16[...] = q_ref[...].astype(BF16)
        run[...] = jnp.zeros_like(run)
        acc[...] = jnp.zeros_like(acc)
        suffix = jnp.where(_iota2(tk, tk, 0) >= _iota2(tk, tk, 1), 1.0, 0.0).astype(BF16)

        def blk(kb, masked):
            rows = _krows(kb, tk)
            s_scr[...] = _dot(q16[...], k16[rows, :], NT)
            for rs in _strips(tq):
                lk = -_softplus(s_scr[rs, :] * SCALE)
                if masked:
                    lk = jnp.where(_visible(i, kb, rs, tq, tk, True), lk, 0.0)
                lk16[rs, :] = lk.astype(BF16)
            w_scr[...] = _dot(lk16[...], suffix, NN)
            for rs in _strips(tq):
                a = jnp.exp(s_scr[rs, :] * SCALE + w_scr[rs, :] + run[rs, :])
                if masked:
                    a = jnp.where(_visible(i, kb, rs, tq, tk, True), a, 0.0)
                a16[rs, :] = a.astype(BF16)
                run[rs, :] += w_scr[rs, 0:1]
            acc[...] += _dot(a16[...], v16[rows, :], NN)

        _blocks(i, ratio, blk, reverse=True)
        o_ref[...] = acc[...]
        tot_ref[...] = run[...]

    return _pcall(body, name="sb_fwd",
                  out_shape=[jax.ShapeDtypeStruct((s, HW), F32), jax.ShapeDtypeStruct((HEADS, s, 1), F32)],
                  grid=(HEADS, s // tq), in_specs=[qspec, kspec, kspec], out_specs=[qspec, colspec],
                  scratch_shapes=[_vm((s, HEAD_DIM), BF16), _vm((s, HEAD_DIM), BF16), _vm((tq, HEAD_DIM), BF16),
                                  _vm((tq, tk), F32), _vm((tq, tk), F32), _vm((tq, tk), BF16), _vm((tq, tk), BF16),
                                  _vm((tq, 1), F32), _vm((tq, HEAD_DIM), F32)])(q, k, v)


def _sb_bwd_call(q, k, v, tot, do):
    s = q.shape[0]
    tq, tk = _att_tiles(s)
    ratio = tq // tk
    qspec, kspec, colspec, _ = _att_specs(s, tq, tk)

    def body(q_ref, k_ref, v_ref, tot_ref, do_ref, dq_ref, dk_ref, dv_ref, k16, v16, q16, do16, s_scr, e_scr, w_scr,
             lz16, a16, e16, left, esum):
        i = pl.program_id(1)
        _stage_bf16(i, [(k_ref, k16), (v_ref, v16)])

        @pl.when(i == 0)
        def _():
            dk_ref[...] = jnp.zeros_like(dk_ref)
            dv_ref[...] = jnp.zeros_like(dv_ref)

        q16[...] = q_ref[...].astype(BF16)
        do16[...] = do_ref[...].astype(BF16)
        left[...] = jnp.zeros_like(left)
        esum[...] = jnp.zeros_like(esum)
        dq_ref[...] = jnp.zeros_like(dq_ref)
        prefix = jnp.where(_iota2(tk, tk, 0) <= _iota2(tk, tk, 1), 1.0, 0.0).astype(BF16)

        def blk(kb, masked):
            rows = _krows(kb, tk)
            s_scr[...] = _dot(q16[...], k16[rows, :], NT)
            e_scr[...] = _dot(do16[...], v16[rows, :], NT)
            for rs in _strips(tq):
                lk = -_softplus(s_scr[rs, :] * SCALE)
                if masked:
                    lk = jnp.where(_visible(i, kb, rs, tq, tk, True), lk, 0.0)
                lz16[rs, :] = lk.astype(BF16)
            w_scr[...] = _dot(lz16[...], prefix, NN)
            for rs in _strips(tq):
                rc = (tot_ref[rs, :] - left[rs, :]) - (w_scr[rs, :] - lz16[rs, :].astype(F32))
                a = jnp.exp(s_scr[rs, :] * SCALE + rc)
                if masked:
                    a = jnp.where(_visible(i, kb, rs, tq, tk, True), a, 0.0)
                e = a * e_scr[rs, :]
                a16[rs, :] = a.astype(BF16)
                e16[rs, :] = e.astype(BF16)
                e_scr[rs, :] = e
                left[rs, :] += w_scr[rs, tk - 1:tk]
            w_scr[...] = _dot(e16[...], prefix, NN)
            for rs in _strips(tq):
                dz = e_scr[rs, :] - _sigmoid(s_scr[rs, :] * SCALE) * (esum[rs, :] + w_scr[rs, :])
                if masked:
                    dz = jnp.where(_visible(i, kb, rs, tq, tk, True), dz, 0.0)
                lz16[rs, :] = dz.astype(BF16)
                esum[rs, :] += w_scr[rs, tk - 1:tk]
            dv_ref[rows, :] += _dot(a16[...], do16[...], TN)
            dk_ref[rows, :] += _dot(lz16[...], q16[...], TN) * SCALE
            dq_ref[...] += _dot(lz16[...], k16[rows, :], NN) * SCALE

        _blocks(i, ratio, blk)

    return _pcall(body, name="sb_bwd", out_shape=[jax.ShapeDtypeStruct((s, HW), F32)] * 3, grid=(HEADS, s // tq),
                  in_specs=[qspec, kspec, kspec, colspec, qspec], out_specs=[qspec, kspec, kspec],
                  scratch_shapes=[_vm((s, HEAD_DIM), BF16), _vm((s, HEAD_DIM), BF16), _vm((tq, HEAD_DIM), BF16),
                                  _vm((tq, HEAD_DIM), BF16), _vm((tq, tk), F32), _vm((tq, tk), F32),
                                  _vm((tq, tk), F32), _vm((tq, tk), BF16), _vm((tq, tk), BF16), _vm((tq, tk), BF16),
                                  _vm((tq, 1), F32), _vm((tq, 1), F32)])(q, k, v, tot, do)


@jax.custom_vjp
def sb_attention(q, k, v):
    return _sb_fwd_call(q, k, v)[0]


def _sb_vjp_fwd(q, k, v):
    o, tot = _sb_fwd_call(q, k, v)
    return o, (q, k, v, tot)


sb_attention.defvjp(_sb_vjp_fwd, lambda res, g: tuple(_sb_bwd_call(*res, g)))


def _mem_specs(s, nk, t):
    return (pl.BlockSpec((t, HEAD_DIM), lambda h, i: (i, h)), pl.BlockSpec((nk, HEAD_DIM), lambda h, i: (0, h)))


def _mem_probs(qb, kk):
    sc = _dot(qb, kk, NT) * SCALE
    p = jnp.exp(sc - jnp.max(sc, axis=-1, keepdims=True))
    return p / jnp.sum(p, axis=-1, keepdims=True)


def _mem_fwd_call(q, k, v):
    s, nk = q.shape[0], k.shape[0]
    t = _pick(s, 512, SUBLANES)
    qspec, kspec = _mem_specs(s, nk, t)

    def body(q_ref, k_ref, v_ref, o_ref):
        o_ref[...] = _dot(_mem_probs(q_ref[...].astype(BF16), k_ref[...]), v_ref[...], NN)

    return _pcall(body, name="mem_fwd", out_shape=jax.ShapeDtypeStruct((s, HW), F32), grid=(HEADS, s // t),
                  in_specs=[qspec, kspec, kspec], out_specs=qspec)(q, k, v)


def _mem_bwd_call(q, k, v, do):
    s, nk = q.shape[0], k.shape[0]
    t = _pick(s, 512, SUBLANES)
    qspec, kspec = _mem_specs(s, nk, t)

    def body(q_ref, k_ref, v_ref, do_ref, dq_ref, dk_ref, dv_ref):
        @pl.when(pl.program_id(1) == 0)
        def _():
            dk_ref[...] = jnp.zeros_like(dk_ref)
            dv_ref[...] = jnp.zeros_like(dv_ref)

        qb = q_ref[...].astype(BF16)
        dob = do_ref[...].astype(BF16)
        p = _mem_probs(qb, k_ref[...])
        dv_ref[...] += _dot(p, dob, TN)
        dp = _dot(dob, v_ref[...], NT)
        ds = p * (dp - jnp.sum(p * dp, axis=-1, keepdims=True))
        dq_ref[...] = _dot(ds, k_ref[...], NN) * SCALE
        dk_ref[...] += _dot(ds, qb, TN) * SCALE

    return _pcall(body, name="mem_bwd",
                  out_shape=[jax.ShapeDtypeStruct((s, HW), F32)] + [jax.ShapeDtypeStruct((nk, HW), F32)] * 2,
                  grid=(HEADS, s // t), in_specs=[qspec, kspec, kspec, qspec],
                  out_specs=[qspec, kspec, kspec])(q, k, v, do)


@jax.custom_vjp
def mem_attention(q, k, v):
    return _mem_fwd_call(q, k, v)


mem_attention.defvjp(lambda q, k, v: (_mem_fwd_call(q, k, v), (q, k, v)),
                     lambda res, g: tuple(_mem_bwd_call(*res, g)))


BNN = (((2,), (1,)), ((0,), (0,)))
BNT = (((2,), (2,)), ((0,), (0,)))
BTN = (((1,), (1,)), ((0,), (0,)))


def _bdot(a, b, dn):
    return lax.dot_general(a.astype(BF16), b.astype(BF16), dn, preferred_element_type=F32)


def _bdotf(a, b, dn):
    return lax.dot_general(a, b, dn, precision=lax.Precision.HIGHEST, preferred_element_type=F32)


@jax.custom_vjp
def bmm(a, b):
    return _bdot(a, b, BNN)


bmm.defvjp(lambda a, b: (_bdot(a, b, BNN), (a, b)), lambda r, g: (_bdot(g, r[1], BNT), _bdot(r[0], g, BTN)))


@jax.custom_vjp
def bmm_nt(a, b):
    return _bdot(a, b, BNT)


bmm_nt.defvjp(lambda a, b: (_bdot(a, b, BNT), (a, b)), lambda r, g: (_bdot(g, r[1], BNN), _bdot(g, r[0], BTN)))


@jax.custom_vjp
def bmm_tn(a, b):
    return _bdot(a, b, BTN)


bmm_tn.defvjp(lambda a, b: (_bdot(a, b, BTN), (a, b)), lambda r, g: (_bdot(r[1], g, BNT), _bdot(r[0], g, BNN)))


def _unit_lower_inverse(nm):
    eye = jnp.where(_iota2(CHUNK, CHUNK, 0) == _iota2(CHUNK, CHUNK, 1), 1.0, 0.0).astype(F32)[None]
    p = eye - nm
    m = nm
    for _ in range(5):
        m = _bdotf(m, m, BNN)
        p = _bdotf(p, eye + m, BNN)
    return p


@jax.custom_vjp
def _solve2(nm, r1, r2):
    inv = _unit_lower_inverse(nm)
    return _bdotf(inv, r1, BNN), _bdotf(inv, r2, BNN)


def _solve2_fwd(nm, r1, r2):
    inv = _unit_lower_inverse(nm)
    u, w = _bdotf(inv, r1, BNN), _bdotf(inv, r2, BNN)
    return (u, w), (inv, u, w)


def _solve2_bwd(res, g):
    inv, u, w = res
    d1, d2 = _bdotf(inv, g[0], BTN), _bdotf(inv, g[1], BTN)
    return -(_bdotf(d1, u, BNT) + _bdotf(d2, w, BNT)), d1, d2


_solve2.defvjp(_solve2_fwd, _solve2_bwd)


def _gdn_chunk(q, k, v, gcc, gcr, b, gl, st):
    qn = q * lax.rsqrt(jnp.sum(q * q, axis=-1, keepdims=True) + EPS) * SCALE
    kn = k * lax.rsqrt(jnp.sum(k * k, axis=-1, keepdims=True) + EPS)
    r, c = _iota2(CHUNK, CHUNK, 0)[None], _iota2(CHUNK, CHUNK, 1)[None]
    decay = jnp.exp(jnp.where(r >= c, gcc - gcr, NEG))
    nm = jnp.where(r > c, b * bmm_nt(kn, kn) * decay, 0.0)
    eg = jnp.exp(gcc)
    u, w = _solve2(nm, v * b, kn * (b * eg))
    attn = bmm_nt(qn, kn) * decay
    v_new = u - bmm(w, st)
    o = bmm(qn * eg, st) + bmm(attn, v_new)
    st_new = st * jnp.exp(gl) + bmm_tn(kn * jnp.exp(gl - gcc), v_new)
    return o, st_new


def _heads_of(ref, rows):
    return jnp.stack([ref[rows, _head_cols(h)] for h in range(HEADS)])


def _head_cols(h):
    return slice(h * HEAD_DIM, (h + 1) * HEAD_DIM)


GDN_ROWS = 512


def _gdn_specs(s, tg, rev):
    nb = s // tg
    cpb = tg // CHUNK
    j_of = (lambda j: nb - 1 - j) if rev else (lambda j: j)
    qspec = pl.BlockSpec((tg, HW), lambda j: (j_of(j), 0))
    colspec = pl.BlockSpec((HEADS, tg, 1), lambda j: (0, j_of(j), 0))
    rowspec = pl.BlockSpec((HEADS, cpb, 1, CHUNK), lambda j: (0, j_of(j), 0, 0))
    onespec = pl.BlockSpec((HEADS, cpb, 1, 1), lambda j: (0, j_of(j), 0, 0))
    stspec = pl.BlockSpec((HEADS, cpb, HEAD_DIM, HEAD_DIM), lambda j: (0, j_of(j), 0, 0))
    return qspec, colspec, rowspec, onespec, stspec


def _gdn_fwd_call(q, k, v, gcc, gcr, bc, gl):
    s = q.shape[0]
    tg = _pick(s, GDN_ROWS, CHUNK)
    cpb = tg // CHUNK
    qspec, colspec, rowspec, onespec, stspec = _gdn_specs(s, tg, False)

    def body(q_ref, k_ref, v_ref, gcc_ref, gcr_ref, b_ref, gl_ref, o_ref, st_ref, st):
        @pl.when(pl.program_id(0) == 0)
        def _():
            st[...] = jnp.zeros_like(st)

        def chunk(ci, _):
            rows = pl.ds(pl.multiple_of(ci * CHUNK, CHUNK), CHUNK)
            s_in = st[...]
            st_ref[:, ci] = s_in
            o, s_new = _gdn_chunk(_heads_of(q_ref, rows), _heads_of(k_ref, rows), _heads_of(v_ref, rows),
                                  gcc_ref[:, rows, :], gcr_ref[:, ci], b_ref[:, rows, :], gl_ref[:, ci], s_in)
            for h in range(HEADS):
                o_ref[rows, _head_cols(h)] = o[h]
            st[...] = s_new
            return 0

        lax.fori_loop(0, cpb, chunk, 0)

    return _pcall(body, name="gdn_fwd",
                  out_shape=[jax.ShapeDtypeStruct((s, HW), F32),
                             jax.ShapeDtypeStruct((HEADS, s // CHUNK, HEAD_DIM, HEAD_DIM), F32)],
                  grid=(s // tg,), in_specs=[qspec, qspec, qspec, colspec, rowspec, colspec, onespec],
                  out_specs=[qspec, stspec], scratch_shapes=[pltpu.VMEM((HEADS, HEAD_DIM, HEAD_DIM), F32)],
                  )(q, k, v, gcc, gcr, bc, gl)


def _gdn_bwd_call(q, k, v, gcc, gcr, bc, gl, states, do):
    s = q.shape[0]
    tg = _pick(s, GDN_ROWS, CHUNK)
    cpb = tg // CHUNK
    qspec, colspec, rowspec, onespec, stspec = _gdn_specs(s, tg, True)

    def body(q_ref, k_ref, v_ref, gcc_ref, gcr_ref, b_ref, gl_ref, st_ref, do_ref,
             dq_ref, dk_ref, dv_ref, dgcc_ref, dgcr_ref, db_ref, dgl_ref, dst):
        @pl.when(pl.program_id(0) == 0)
        def _():
            dst[...] = jnp.zeros_like(dst)

        def chunk(n, _):
            ci = cpb - 1 - n
            rows = pl.ds(pl.multiple_of(ci * CHUNK, CHUNK), CHUNK)
            _, vjp = jax.vjp(_gdn_chunk, _heads_of(q_ref, rows), _heads_of(k_ref, rows), _heads_of(v_ref, rows),
                             gcc_ref[:, rows, :], gcr_ref[:, ci], b_ref[:, rows, :], gl_ref[:, ci], st_ref[:, ci])
            dq, dk, dv, dgcc, dgcr, db, dgl, ds_in = vjp((_heads_of(do_ref, rows), dst[...]))
            for h in range(HEADS):
                cols = _head_cols(h)
                dq_ref[rows, cols] = dq[h]
                dk_ref[rows, cols] = dk[h]
                dv_ref[rows, cols] = dv[h]
            dgcc_ref[:, rows, :] = dgcc
            dgcr_ref[:, ci] = dgcr
            db_ref[:, rows, :] = db
            dgl_ref[:, ci] = dgl
            dst[...] = ds_in
            return 0

        lax.fori_loop(0, cpb, chunk, 0)

    n = s // CHUNK
    return _pcall(body, name="gdn_bwd",
                  out_shape=[jax.ShapeDtypeStruct((s, HW), F32)] * 3
                  + [jax.ShapeDtypeStruct((HEADS, s, 1), F32), jax.ShapeDtypeStruct((HEADS, n, 1, CHUNK), F32),
                     jax.ShapeDtypeStruct((HEADS, s, 1), F32), jax.ShapeDtypeStruct((HEADS, n, 1, 1), F32)],
                  grid=(s // tg,),
                  in_specs=[qspec, qspec, qspec, colspec, rowspec, colspec, onespec, stspec, qspec],
                  out_specs=[qspec, qspec, qspec, colspec, rowspec, colspec, onespec],
                  scratch_shapes=[pltpu.VMEM((HEADS, HEAD_DIM, HEAD_DIM), F32)],
                  )(q, k, v, gcc, gcr, bc, gl, states, do)


@jax.custom_vjp
def gated_delta(q, k, v, gcc, gcr, bc, gl):
    return _gdn_fwd_call(q, k, v, gcc, gcr, bc, gl)[0]


def _gdn_vjp_fwd(q, k, v, gcc, gcr, bc, gl):
    o, states = _gdn_fwd_call(q, k, v, gcc, gcr, bc, gl)
    return o, (q, k, v, gcc, gcr, bc, gl, states)


gated_delta.defvjp(_gdn_vjp_fwd, lambda res, g: tuple(_gdn_bwd_call(*res, g)))


def _loss_call(y, target):
    s, d = y.shape
    tm = _pick(s, 512, SUBLANES)

    def body(y_ref, t_ref, dy_ref, loss_ref):
        @pl.when(pl.program_id(0) == 0)
        def _():
            loss_ref[...] = jnp.zeros_like(loss_ref)

        err = y_ref[...] - t_ref[...]
        dy_ref[...] = err * (1.0 / d)
        loss_ref[...] += 0.5 * jnp.sum(jnp.mean(err * err, axis=-1, keepdims=True), axis=0, keepdims=True)

    dy, part = _pcall(body, name="loss_head",
                      out_shape=[jax.ShapeDtypeStruct((s, d), F32), jax.ShapeDtypeStruct((1, 1), F32)],
                      grid=(s // tm,), in_specs=[pl.BlockSpec((tm, d), lambda i: (i, 0))] * 2,
                      out_specs=[pl.BlockSpec((tm, d), lambda i: (i, 0)), pl.BlockSpec((1, 1), lambda i: (0, 0))],
                      )(y, target)
    return part[0, 0], dy


def _cols_and_rows(a, lane0, t):
    s = a.shape[0]
    at = a[:, lane0:lane0 + HEADS].T
    return at, at[:, :, None], at.reshape(HEADS, s // t, 1, t)


def _pad_lanes(v, lane0):
    return jnp.pad(v, (lane0, LANES - lane0 - v.shape[0])).reshape(1, LANES)


def _layer(x, mem, w, w16, ops, convs):
    s = x.shape[0]
    row = lambda v: v.reshape(1, -1)
    mw = lambda n: (w[n], w16[n])
    h = ops["rms"](x, row(w["norm_mix"]))
    fq, fk, fv, gqkv, gz, sq, sk, sv, gt, gm = proj(h, *mw("w_in"))

    logf, beta, gc = ops["small"](gm, _pad_lanes(w["fox_fbias"], LANE_FF), _pad_lanes(w["gdn_a_log"], LANE_GA),
                                  _pad_lanes(w["gdn_dt_bias"], LANE_GA))
    _, ccol, crow = _cols_and_rows(seq_cumsum(logf), LANE_FF, _att_tiles(s, fox=True)[1])
    ya = fox_attention(ops["headnorm"](fq, row(w["fox_qnorm"])), ops["headnorm"](fk, row(w["fox_knorm"])), fv,
                       ccol, crow)
    cq, ck, cv = convs["gdn"](gqkv, w["gdn_conv"])
    gct, gcc, gcr = _cols_and_rows(gc, LANE_GA, CHUNK)
    gl = gct.reshape(HEADS, s // CHUNK, CHUNK)[:, :, CHUNK - 1].reshape(HEADS, s // CHUNK, 1, 1)
    bc = beta[:, LANE_GB:LANE_GB + HEADS].T[:, :, None]
    yb = ops["gdnpost"](gated_delta(cq, ck, cv, gcc, gcr, bc, gl), gz, row(w["gdn_onorm"]))
    yc = sb_attention(sq, sk, sv)
    gb = w["gate_bias"]
    mixed = ops["merge"](gt, matmul(ya, *mw("w_oa")), matmul(yb, *mw("w_ob")), matmul(yc, *mw("w_oc")),
                         row(gb[:D_MODEL]), row(gb[D_MODEL:2 * D_MODEL]), row(gb[2 * D_MODEL:]))
    x = matmul_add(x, mixed, *mw("w_out"))
    mq = ops["headnorm"](matmul(ops["rms"](x, row(w["norm_xq"])), *mw("w_mq")), row(w["mq_norm"]))
    kv = matmul(ops["rms"](mem, row(w["norm_mem"])), *mw("w_mkv"))
    mk = ops["headnorm"](kv[:, :HW], row(w["mk_norm"]))
    x = matmul_add(x, mem_attention(mq, mk, kv[:, HW:]), *mw("w_mo"))
    u = matmul(ops["rms"](x, row(w["norm_ffn"])), *mw("w_up"))
    act = convs["ffn"](u, w["ffn_conv"], row(w["ffn_conv_b"]))
    return matmul_add(x, act, *mw("w_down"))


def _forward(x, mem, layers, layers16):
    ops, convs = _make_rowops(), _make_convops()
    for w, w16 in zip(layers, layers16):
        x = _layer(x, mem, w, w16, ops, convs)
    return x


ANY = pl.BlockSpec(memory_space=pl.ANY)


N_PEERS = N_DEV - 1


def _ccall(body, *, name, out_shape, n_arrays):
    return pl.pallas_call(body, name=name, out_shape=out_shape, in_specs=[ANY] * n_arrays,
                          out_specs=[ANY] * n_arrays,
                          scratch_shapes=[pltpu.SemaphoreType.DMA((N_PEERS * n_arrays,)),
                                          pltpu.SemaphoreType.DMA((N_PEERS * n_arrays,)),
                                          pltpu.SemaphoreType.DMA((n_arrays,))],
                          interpret=False)


def _all_gather(name, shards):
    n = len(shards)

    def body(*refs):
        x_refs, out_refs = refs[:n], refs[n:2 * n]
        send_sems, recv_sems, local_sems = refs[2 * n:]
        x, y, c = lax.axis_index("x"), lax.axis_index("y"), lax.axis_index("c")
        me, sibling = (x, y, c), (x, y, 1 - c)
        chips = [(1 - x, y), (x, 1 - y), (1 - x, 1 - y)]

        def slot(a, px, py, pc):
            return out_refs[a].at[4 * px + 2 * py + pc]

        def copy(a, k, block, to, src=None):
            return pltpu.make_async_remote_copy(
                src_ref=slot(a, *block) if src is None else src, dst_ref=slot(a, *block),
                send_sem=send_sems.at[N_PEERS * a + k], recv_sem=recv_sems.at[N_PEERS * a + k], device_id=to,
                device_id_type=MESH)

        mine = [pltpu.make_async_copy(x_refs[a], slot(a, *me), local_sems.at[a]) for a in range(n)]
        first = []
        for a in range(n):
            first.append(copy(a, 0, me, sibling, src=x_refs[a]))
            first += [copy(a, 1 + j, me, (*chip, c), src=x_refs[a]) for j, chip in enumerate(chips)]
        for cp in mine + first:
            cp.start()
        passed = []
        for j, chip in enumerate(chips):
            for a in range(n):
                copy(a, 1 + j, (*chip, c), me).wait_recv()
                passed.append(copy(a, 4 + j, (*chip, c), sibling))
                passed[-1].start()
        for a in range(n):
            copy(a, 0, sibling, me).wait_recv()
            for j, chip in enumerate(chips):
                copy(a, 4 + j, (*chip, 1 - c), me).wait_recv()
        for cp in first + passed:
            cp.wait_send()
        for cp in mine:
            cp.wait()

    return _ccall(body, name=name, out_shape=[jax.ShapeDtypeStruct((N_DEV,) + s.shape, s.dtype) for s in shards],
                  n_arrays=n)(*shards)


def _exchange(name, parts):
    n = len(parts)

    def body(*refs):
        p_refs, out_refs = refs[:n], refs[n:2 * n]
        send_sems, recv_sems, local_sems = refs[2 * n:]
        x, y, c = lax.axis_index("x"), lax.axis_index("y"), lax.axis_index("c")
        me = 4 * x + 2 * y + c

        def peer(k):
            return (x ^ ((k >> 2) & 1), y ^ ((k >> 1) & 1), c ^ (k & 1))

        def copy(a, k, receive):
            px, py, pc = peer(k)
            theirs = 4 * px + 2 * py + pc
            return pltpu.make_async_remote_copy(
                src_ref=out_refs[a].at[theirs] if receive else p_refs[a].at[theirs],
                dst_ref=out_refs[a].at[theirs if receive else me],
                send_sem=send_sems.at[N_PEERS * a + k - 1], recv_sem=recv_sems.at[N_PEERS * a + k - 1],
                device_id=(px, py, pc), device_id_type=MESH)

        mine = [pltpu.make_async_copy(p_refs[a].at[me], out_refs[a].at[me], local_sems.at[a]) for a in range(n)]
        sends = [copy(a, k, False) for k in range(1, N_DEV) for a in range(n)]
        for cp in mine + sends:
            cp.start()
        for k in range(1, N_DEV):
            for a in range(n):
                copy(a, k, True).wait_recv()
        for cp in sends:
            cp.wait_send()
        for cp in mine:
            cp.wait()

    return _ccall(body, name=name, out_shape=[jax.ShapeDtypeStruct(p.shape, p.dtype) for p in parts],
                  n_arrays=n)(*parts)


ADAM_SLOT_BYTES = 4 * 1024 * 1024


def _adam_call(name, w, slots, m, v):
    r, n = w.shape
    rows_unit = 2 * SUBLANES
    tr = _pick(r, max(rows_unit, ADAM_SLOT_BYTES // (N_DEV * n * 4)), rows_unit)
    spec = pl.BlockSpec((tr, n), lambda i: (i, 0))

    def body(w_ref, s_ref, m_ref, v_ref, g_ref, d_ref, nm_ref, nv_ref):
        g = s_ref[0].astype(F32)
        for d in range(1, N_DEV):
            g = g + s_ref[d].astype(F32)
        nm = ADAM_B1 * m_ref[...] + (1.0 - ADAM_B1) * g
        nv = ADAM_B2 * v_ref[...] + (1.0 - ADAM_B2) * (g * g)
        m_hat = nm / (1.0 - ADAM_B1 ** ADAM_STEP)
        v_hat = nv / (1.0 - ADAM_B2 ** ADAM_STEP)
        g_ref[...] = g
        d_ref[...] = -ADAM_LR * (m_hat / (jnp.sqrt(v_hat) + ADAM_EPS) + ADAM_WD * w_ref[...])
        nm_ref[...] = nm
        nv_ref[...] = nv

    return _pcall(body, name=name, out_shape=[jax.ShapeDtypeStruct((r, n), F32)] * 4, grid=(r // tr,),
                  in_specs=[spec, pl.BlockSpec((N_DEV, tr, n), lambda i: (0, i, 0)), spec, spec],
                  out_specs=[spec] * 4)(w, slots, m, v)


def _pack_rows(flat, rows):
    return jnp.pad(flat, (0, rows * PACK_COLS - flat.shape[0])).reshape(rows, PACK_COLS)


def _regroup_in(w_in):
    cols = [w_in[:, a:b] for a, b in _IN_SRC]
    return jnp.concatenate(cols + [jnp.zeros((w_in.shape[0], N_IN_PAD - N_IN), w_in.dtype)], axis=1)


def _ungroup_in(d):
    starts = {}
    off = 0
    for a, b in _IN_SRC:
        starts[a] = (off, b - a)
        off += b - a
    return jnp.concatenate([d[:, starts[a][0]:starts[a][0] + starts[a][1]] for a in sorted(starts)], axis=1)


def _full_weights(gathered):
    out = {}
    for n, g in zip(SHARDED_ORDER, gathered):
        (r, c), axis = SHARDED[n]
        out[n] = g.reshape(r, c) if axis == 0 else g.transpose(1, 0, 2).reshape(r, c)
    out["w_in"] = _regroup_in(out["w_in"])
    return out


def _in_f32(full):
    return {n: v.astype(F32) for n, v in full.items()}


def _for_transport(name, shard):
    return shard if name in ("gdn_conv", "ffn_conv") else shard.astype(BF16)


def _grad_parts(grads):
    parts = []
    for n in SHARDED_ORDER:
        (r, c), axis = SHARDED[n]
        g = _ungroup_in(grads[n]) if n == "w_in" else grads[n]
        if axis == 0:
            parts.append(g.reshape(N_DEV, r // N_DEV, c).astype(BF16))
        else:
            parts.append(g.reshape(r, N_DEV, c // N_DEV).transpose(1, 0, 2).astype(BF16))
    return parts


def _pack_small(vals):
    return _pack_rows(jnp.concatenate([vals[n].reshape(-1) for n in SMALL_ORDER]), SMALL_ROWS)


def _unpack_small(packed):
    flat = packed.reshape(-1)
    out, off = {}, 0
    for n in SMALL_ORDER:
        size = DEPTH * SMALL_WIDTH[n]
        out[n] = flat[off:off + size].reshape(DEPTH, SMALL_WIDTH[n])
        off += size
    return out


def kernel(x, mem, norm_mix, w_in, fox_fbias, fox_qnorm, fox_knorm, gdn_conv, gdn_a_log, gdn_dt_bias, gdn_onorm, gate_bias, w_oa, w_ob, w_oc, w_out, norm_xq, norm_mem, w_mq, w_mkv, mq_norm, mk_norm, w_mo, norm_ffn, w_up, ffn_conv, ffn_conv_b, w_down, loss_target, m_norm_mix, m_w_in, m_fox_fbias, m_fox_qnorm, m_fox_knorm, m_gdn_conv, m_gdn_a_log, m_gdn_dt_bias, m_gdn_onorm, m_gate_bias, m_w_oa, m_w_ob, m_w_oc, m_w_out, m_norm_xq, m_norm_mem, m_w_mq, m_w_mkv, m_mq_norm, m_mk_norm, m_w_mo, m_norm_ffn, m_w_up, m_ffn_conv, m_ffn_conv_b, m_w_down, v_norm_mix, v_w_in, v_fox_fbias, v_fox_qnorm, v_fox_knorm, v_gdn_conv, v_gdn_a_log, v_gdn_dt_bias, v_gdn_onorm, v_gate_bias, v_w_oa, v_w_ob, v_w_oc, v_w_out, v_norm_xq, v_norm_mem, v_w_mq, v_w_mkv, v_mq_norm, v_mk_norm, v_w_mo, v_norm_ffn, v_w_up, v_ffn_conv, v_ffn_conv_b, v_w_down):
    given = dict(locals())
    wts = {n: given[n] for n in WEIGHTS}
    mom = {n: given["m_" + n] for n in WEIGHTS}
    var = {n: given["v_" + n] for n in WEIGHTS}

    layers, layers16 = [], []
    for l in range(DEPTH):
        full = _full_weights(_all_gather("gather_weights", [_for_transport(n, wts[n][l]) for n in SHARDED_ORDER]))
        layers16.append(full)
        layers.append({**_in_f32(full), **{n: wts[n][l] for n in SMALL_ORDER}})

    y, vjp = jax.vjp(lambda xx, ww: _forward(xx, mem[0], ww, layers16), x[0], layers)
    loss_part, dy = _loss_call(y, loss_target[0])
    dx, dlayers = vjp(dy)
    loss = lax.psum(loss_part, ("x", "y", "c"))

    out = {}
    per_layer = []
    for l in range(DEPTH):
        slots = _exchange("exchange_grads", _grad_parts(dlayers[l]))
        per_layer.append({n: _adam_call("adam_shard", wts[n][l], sl, mom[n][l], var[n][l])
                          for n, sl in zip(SHARDED_ORDER, slots)})
    for n in SHARDED_ORDER:
        for k, kind in enumerate(("grad_", "delta_", "new_m_", "new_v_")):
            out[kind + n] = jnp.stack([per_layer[l][n][k] for l in range(DEPTH)])
    dsmall = {n: jnp.stack([dlayers[l][n] for l in range(DEPTH)]) for n in SMALL_ORDER}
    slots = _all_gather("gather_small_grads", [_pack_small(dsmall)])[0]
    res = _adam_call("adam_small", _pack_small(wts), slots, _pack_small(mom), _pack_small(var))
    for k, kind in enumerate(("grad_", "delta_", "new_m_", "new_v_")):
        un = _unpack_small(res[k])
        for n in SMALL_ORDER:
            out[kind + n] = un[n].reshape(wts[n].shape)

    return (loss, dx[None], *[out["grad_" + n] for n in WEIGHTS], *[out["delta_" + n] for n in WEIGHTS],
            *[out["new_m_" + n] for n in WEIGHTS], *[out["new_v_" + n] for n in WEIGHTS])
```

```python
import jax
import jax.numpy as jnp
from jax import lax
from jax.experimental import pallas as pl
from jax.experimental.pallas import tpu as pltpu

F32 = jnp.float32
BF16 = jnp.bfloat16

N_DEV = 8
D_MODEL = 1024
DEPTH = 4
CHUNK = 64
EPS = 1e-6
HEADS = 4
HEAD_DIM = 128
HW = HEADS * HEAD_DIM
D_FF = 2816
N_IN = 8204
LANES = 128
SUBLANES = 8
VMEM_LIMIT = 56 * 1024 * 1024

ADAM_LR = 0.001
ADAM_B1 = 0.9
ADAM_B2 = 0.999
ADAM_EPS = 1e-08
ADAM_WD = 0.01
ADAM_STEP = 10

NEG = -1e30
MESH = pl.DeviceIdType.MESH

WEIGHTS = ['norm_mix', 'w_in', 'fox_fbias', 'fox_qnorm', 'fox_knorm', 'gdn_conv', 'gdn_a_log', 'gdn_dt_bias',
           'gdn_onorm', 'gate_bias', 'w_oa', 'w_ob', 'w_oc', 'w_out', 'norm_xq', 'norm_mem', 'w_mq', 'w_mkv',
           'mq_norm', 'mk_norm', 'w_mo', 'norm_ffn', 'w_up', 'ffn_conv', 'ffn_conv_b', 'w_down']
SHARDED = {
    'w_in': ((D_MODEL, N_IN), 0), 'gdn_conv': ((4, 3 * HW), 1), 'w_oa': ((HW, D_MODEL), 1),
    'w_ob': ((HW, D_MODEL), 1), 'w_oc': ((HW, D_MODEL), 1), 'w_out': ((D_MODEL, D_MODEL), 0),
    'w_mq': ((D_MODEL, HW), 0), 'w_mkv': ((D_MODEL, 2 * HW), 0), 'w_mo': ((HW, D_MODEL), 1),
    'w_up': ((D_MODEL, 2 * D_FF), 1), 'ffn_conv': ((3, 2 * D_FF), 1), 'w_down': ((D_FF, D_MODEL), 0),
}
SHARDED_ORDER = [n for n in WEIGHTS if n in SHARDED]
SMALL_ORDER = [n for n in WEIGHTS if n not in SHARDED]
SMALL_WIDTH = {'norm_mix': D_MODEL, 'fox_fbias': HEADS, 'fox_qnorm': HEAD_DIM, 'fox_knorm': HEAD_DIM,
               'gdn_a_log': HEADS, 'gdn_dt_bias': HEADS, 'gdn_onorm': HEAD_DIM, 'gate_bias': 3 * D_MODEL,
               'norm_xq': D_MODEL, 'norm_mem': D_MODEL, 'mq_norm': HEAD_DIM, 'mk_norm': HEAD_DIM,
               'norm_ffn': D_MODEL, 'ffn_conv_b': 2 * D_FF}
PACK_COLS = 1024


def _round_up(n, m):
    return (n + m - 1) // m * m


SMALL_ROWS = _round_up(DEPTH * sum(SMALL_WIDTH.values()), SUBLANES * PACK_COLS) // PACK_COLS

_IN_SRC = [(0, 512), (512, 1024), (1024, 1536),
           (1540, 2052), (2052, 2564), (2564, 3076),
           (3084, 3596),
           (3596, 4108), (4108, 4620), (4620, 5132),
           (5132, 8204),
           (1536, 1540), (3076, 3080), (3080, 3084)]
N_IN_PAD = 8320
LANE_FF, LANE_GB, LANE_GA = 0, 4, 8


def _pick(dim, pref, unit):
    best = None
    t = unit
    while t <= min(dim, pref):
        if dim % t == 0:
            best = t
        t += unit
    return dim if best is None else best


def _params(n_grid):
    return pltpu.CompilerParams(dimension_semantics=("arbitrary",) * n_grid, vmem_limit_bytes=VMEM_LIMIT)


def _pcall(body, *, name, out_shape, grid, in_specs, out_specs, scratch_shapes=()):
    return pl.pallas_call(body, name=name, out_shape=out_shape, grid=grid, in_specs=in_specs, out_specs=out_specs,
                          scratch_shapes=scratch_shapes, compiler_params=_params(len(grid)),
                          interpret=False)


NN = ((1,), (0,))
NT = ((1,), (1,))
TN = ((0,), (0,))


def _dot(a, b, dn):
    return lax.dot_general(a.astype(BF16), b.astype(BF16), (dn, ((), ())), preferred_element_type=F32)


def _dotf(a, b, dn):
    return lax.dot_general(a, b, (dn, ((), ())), precision=lax.Precision.HIGHEST, preferred_element_type=F32)


@jax.custom_vjp
def mm(a, b):
    return _dot(a, b, NN)


mm.defvjp(lambda a, b: (_dot(a, b, NN), (a, b)), lambda r, g: (_dot(g, r[1], NT), _dot(r[0], g, TN)))


@jax.custom_vjp
def mm_nt(a, b):
    return _dot(a, b, NT)


mm_nt.defvjp(lambda a, b: (_dot(a, b, NT), (a, b)), lambda r, g: (_dot(g, r[1], NN), _dot(g, r[0], TN)))


@jax.custom_vjp
def mm_tn(a, b):
    return _dot(a, b, TN)


mm_tn.defvjp(lambda a, b: (_dot(a, b, TN), (a, b)), lambda r, g: (_dot(r[1], g, NT), _dot(r[0], g, NN)))


@jax.custom_vjp
def tri_apply(t, x):
    return _dotf(t, x, NN)


tri_apply.defvjp(lambda t, x: (_dotf(t, x, NN), t), lambda t, g: (jnp.zeros_like(t), _dotf(t, g, TN)))


def _sigmoid(x):
    return 1.0 / (1.0 + jnp.exp(-x))


@jax.custom_vjp
def _softplus(x):
    return jnp.maximum(x, 0.0) + jnp.log(1.0 + jnp.exp(-jnp.abs(x)))


_softplus.defvjp(lambda x: (_softplus(x), x), lambda x, g: (g * _sigmoid(x),))


@jax.custom_vjp
def _silu(x):
    return x * _sigmoid(x)


def _silu_fwd(x):
    s = _sigmoid(x)
    return x * s, (x, s)


_silu.defvjp(_silu_fwd, lambda r, g: (g * r[1] * (1.0 + r[0] * (1.0 - r[1])),))


def _rms(x, g):
    return x * lax.rsqrt(jnp.mean(x * x, axis=-1, keepdims=True) + EPS) * g


def _iota2(n, m, axis):
    return lax.broadcasted_iota(jnp.int32, (n, m), axis)


def _whole(width):
    return [(0, width)]


def _split(width, n):
    w = width // n
    return [(k * w, w) for k in range(n)]


class RowOp:
    def __init__(self, name, f, in_pieces, out_pieces, tm=256):
        self.name, self.f, self.in_pieces, self.out_pieces, self.tm = name, f, in_pieces, out_pieces, tm
        op = jax.custom_vjp(self._fwd_call)
        op.defvjp(lambda *a: (self._fwd_call(*a), a), lambda res, g: self._bwd_call(res, g))
        self.op = op

    def __call__(self, *args):
        return self.op(*args)

    def _width(self, pieces):
        return max(o + w for o, w in pieces)

    def _row_specs(self, pieces_list, tm):
        return [pl.BlockSpec((tm, self._width(p)), lambda i: (i, 0)) for p in pieces_list]

    def _fwd_call(self, *args):
        nr = len(self.in_pieces)
        rows, params = args[:nr], args[nr:]
        m = rows[0].shape[0]
        tm = _pick(m, self.tm, SUBLANES)
        f, in_pieces, out_pieces = self.f, self.in_pieces, self.out_pieces
        no = len(out_pieces)

        def body(*refs):
            rin, pr, ro = refs[:nr], refs[nr:nr + len(params)], refs[nr + len(params):]
            xs = [r[:, o:o + w] for r, ps in zip(rin, in_pieces) for (o, w) in ps]
            ys = f(*xs, *[p[...] for p in pr])
            k = 0
            for r, ps in zip(ro, out_pieces):
                for (o, w) in ps:
                    r[:, o:o + w] = ys[k]
                    k += 1

        outs = _pcall(
            body, name=self.name + "_fwd",
            out_shape=[jax.ShapeDtypeStruct((m, self._width(p)), F32) for p in out_pieces],
            grid=(m // tm,),
            in_specs=self._row_specs(in_pieces, tm) + [pl.BlockSpec(p.shape, lambda i: (0, 0)) for p in params],
            out_specs=self._row_specs(out_pieces, tm),
        )(*rows, *params)
        return tuple(outs) if no > 1 else outs[0]

    def _bwd_call(self, res, g):
        nr = len(self.in_pieces)
        rows, params = res[:nr], res[nr:]
        no = len(self.out_pieces)
        gs = tuple(g) if no > 1 else (g,)
        m = rows[0].shape[0]
        tm = _pick(m, self.tm, SUBLANES)
        f, in_pieces, out_pieces = self.f, self.in_pieces, self.out_pieces
        npar = len(params)

        def body(*refs):
            rin, pr, dro = refs[:nr], refs[nr:nr + npar], refs[nr + npar:nr + npar + no]
            drin, dpr = refs[nr + npar + no:nr + npar + no + nr], refs[nr + npar + no + nr:]
            xs = [r[:, o:o + w] for r, ps in zip(rin, in_pieces) for (o, w) in ps]
            dys = [r[:, o:o + w] for r, ps in zip(dro, out_pieces) for (o, w) in ps]
            _, vjp = jax.vjp(lambda *a: tuple(f(*a)), *xs, *[p[...] for p in pr])
            grads = vjp(tuple(dys))
            k = 0
            for r, ps in zip(drin, in_pieces):
                for (o, w) in ps:
                    r[:, o:o + w] = grads[k]
                    k += 1

            @pl.when(pl.program_id(0) == 0)
            def _():
                for r in dpr:
                    r[...] = jnp.zeros_like(r)

            for j, r in enumerate(dpr):
                r[...] += grads[k + j]

        outs = _pcall(
            body, name=self.name + "_bwd",
            out_shape=[jax.ShapeDtypeStruct(r.shape, F32) for r in rows]
            + [jax.ShapeDtypeStruct(p.shape, F32) for p in params],
            grid=(m // tm,),
            in_specs=self._row_specs(in_pieces, tm) + [pl.BlockSpec(p.shape, lambda i: (0, 0)) for p in params]
            + self._row_specs(out_pieces, tm),
            out_specs=self._row_specs(in_pieces, tm) + [pl.BlockSpec(p.shape, lambda i: (0, 0)) for p in params],
        )(*rows, *params, *gs)
        return tuple(outs)


def _f_rms(x, g):
    return (_rms(x, g),)


def _f_headnorm(x0, x1, x2, x3, g):
    return tuple(_rms(x, g) for x in (x0, x1, x2, x3))


def _f_small(sm, fb, al, db):
    tm = sm.shape[0]
    logf = -_softplus(-(sm + fb))
    beta = _sigmoid(sm)
    glog = -jnp.exp(al) * _softplus(sm + db)
    r, c = _iota2(tm, tm, 0), _iota2(tm, tm, 1)
    bd = jnp.where((r >= c) & (jnp.bitwise_xor(r, c) < CHUNK), 1.0, 0.0).astype(F32)
    return logf, beta, tri_apply(bd, glog)


def _f_gdnpost(o0, o1, o2, o3, z0, z1, z2, z3, g):
    return tuple(_rms(o, g) * _silu(z) for o, z in zip((o0, o1, o2, o3), (z0, z1, z2, z3)))


def _f_merge(t0, t1, t2, a, b, c, b0, b1, b2):
    return (_sigmoid(t0 + b0) * a + _sigmoid(t1 + b1) * b + _sigmoid(t2 + b2) * c,)


def _make_rowops():
    return dict(
        rms=RowOp("rms", _f_rms, [_whole(D_MODEL)], [_whole(D_MODEL)], tm=512),
        headnorm=RowOp("headnorm", _f_headnorm, [_split(HW, HEADS)], [_split(HW, HEADS)], tm=1024),
        small=RowOp("smallprep", _f_small, [_whole(LANES)], [_whole(LANES)] * 3),
        gdnpost=RowOp("gdnpost", _f_gdnpost, [_split(HW, HEADS)] * 2, [_split(HW, HEADS)], tm=512),
        merge=RowOp("merge", _f_merge, [_split(3 * D_MODEL, 3)] + [_whole(D_MODEL)] * 3, [_whole(D_MODEL)]),
    )


ROUND_ONCE_READS = 3


def _b16(x):
    return x.astype(BF16)


def _mm_call(name, a, b, mode, c=None):
    if mode == "nn":
        (m, kc), n = a.shape, b.shape[1]
    elif mode == "nt":
        (m, kc), n = a.shape, b.shape[0]
    else:
        (kc, m), n = a.shape, b.shape[1]
    tm = _pick(m, 1408, LANES) if mode == "tn" else _pick(m, 1024, SUBLANES)
    tn = _pick(n, 1408, LANES)
    tk = _pick(kc, 1024, SUBLANES) if mode == "tn" else _pick(kc, 1536, LANES)
    if a.dtype == F32 and n // tn >= ROUND_ONCE_READS:
        a = _b16(a)
    if b.dtype == F32 and m // tm >= ROUND_ONCE_READS:
        b = _b16(b)
    dn = {"nn": NN, "nt": NT, "tn": TN}[mode]
    a_spec = {"nn": pl.BlockSpec((tm, tk), lambda i, j, k: (i, k)),
              "nt": pl.BlockSpec((tm, tk), lambda i, j, k: (i, k)),
              "tn": pl.BlockSpec((tk, tm), lambda i, j, k: (k, i))}[mode]
    b_spec = {"nn": pl.BlockSpec((tk, tn), lambda i, j, k: (k, j)),
              "nt": pl.BlockSpec((tn, tk), lambda i, j, k: (j, k)),
              "tn": pl.BlockSpec((tk, tn), lambda i, j, k: (k, j))}[mode]
    o_spec = pl.BlockSpec((tm, tn), lambda i, j, k: (i, j))
    has_c = c is not None

    def body(*refs):
        a_ref, b_ref = refs[0], refs[1]
        o_ref = refs[-1]

        @pl.when(pl.program_id(2) == 0)
        def _():
            o_ref[...] = refs[2][...] if has_c else jnp.zeros_like(o_ref)

        o_ref[...] += _dot(a_ref[...], b_ref[...], dn)

    return _pcall(body, name=name, out_shape=jax.ShapeDtypeStruct((m, n), F32), grid=(m // tm, n // tn, kc // tk),
                  in_specs=[a_spec, b_spec] + ([o_spec] if has_c else []), out_specs=o_spec,
                  )(*((a, b, c) if has_c else (a, b)))


@jax.custom_vjp
def matmul(a, w, w16):
    return _mm_call("mm_nn", a, w16, "nn")


def _matmul_bwd(res, g):
    a, w16 = res
    return _mm_call("mm_nt", g, w16, "nt"), _mm_call("mm_tn", a, g, "tn"), jnp.zeros_like(w16)


matmul.defvjp(lambda a, w, w16: (_mm_call("mm_nn", a, w16, "nn"), (a, w16)), _matmul_bwd)


@jax.custom_vjp
def matmul_add(c, a, w, w16):
    return _mm_call("mm_nn_add", a, w16, "nn", c)


matmul_add.defvjp(lambda c, a, w, w16: (_mm_call("mm_nn_add", a, w16, "nn", c), (a, w16)),
                  lambda res, g: (g,) + _matmul_bwd(res, g))

_PROJ_GROUPS = [(0, 512), (512, 512), (1024, 512), (1536, 1536), (3072, 512), (3584, 512), (4096, 512), (4608, 512),
                (5120, 3072), (8192, 128)]


def _proj_fwd(h, w, w16):
    h16 = _b16(h)
    return tuple(_mm_call("proj_nn", h16, w16[:, s:s + n], "nn") for s, n in _PROJ_GROUPS), (h16, w16)


proj = jax.custom_vjp(lambda h, w, w16: _proj_fwd(h, w, w16)[0])


def _proj_bwd(res, gs):
    h16, w16 = res
    dh = None
    dws = []
    for (s, n), g in zip(_PROJ_GROUPS, gs):
        dh = _mm_call("proj_nt", g, w16[:, s:s + n], "nt", dh)
        dws.append(_mm_call("proj_tn", h16, g, "tn"))
    return dh, jnp.concatenate(dws, axis=1), jnp.zeros_like(w16)


proj.defvjp(_proj_fwd, _proj_bwd)


def _cumsum_call(x, reverse):
    s, w = x.shape
    tm = _pick(s, 256, SUBLANES)
    nb = s // tm

    def body(x_ref, o_ref, carry):
        @pl.when(pl.program_id(0) == 0)
        def _():
            carry[...] = jnp.zeros_like(carry)

        blk = x_ref[...]
        r, c = _iota2(tm, tm, 0), _iota2(tm, tm, 1)
        tri = jnp.where((r <= c) if reverse else (r >= c), 1.0, 0.0).astype(F32)
        o_ref[...] = _dotf(tri, blk, NN) + carry[...]
        carry[...] += jnp.sum(blk, axis=0, keepdims=True)

    idx = (lambda i: (nb - 1 - i, 0)) if reverse else (lambda i: (i, 0))
    return _pcall(body, name="cumsum_rev" if reverse else "cumsum", out_shape=jax.ShapeDtypeStruct((s, w), F32),
                  grid=(nb,), in_specs=[pl.BlockSpec((tm, w), idx)], out_specs=pl.BlockSpec((tm, w), idx),
                  scratch_shapes=[pltpu.VMEM((1, w), F32)])(x)


@jax.custom_vjp
def seq_cumsum(x):
    return _cumsum_call(x, False)


seq_cumsum.defvjp(lambda x: (_cumsum_call(x, False), None), lambda _, g: (_cumsum_call(g, True),))


HALO = SUBLANES


class ConvOp:
    def __init__(self, name, width, post, c_pieces, out_widths, has_bias, tm):
        self.name, self.width, self.post, self.c_pieces = name, width, post, c_pieces
        self.out_widths, self.has_bias, self.tm = out_widths, has_bias, tm
        op = jax.custom_vjp(self._fwd_call)
        op.defvjp(lambda *a: (self._fwd_call(*a), a), lambda res, g: self._bwd_call(res, g))
        self.op = op

    def __call__(self, *args):
        return self.op(*args)

    def _conv(self, i, x_ref, prev_ref, w_ref, b_ref, buf):
        tm = x_ref.shape[0]
        buf[0:HALO, :] = jnp.where(i > 0, prev_ref[...], 0.0)
        buf[HALO:HALO + tm, :] = x_ref[...]
        taps = [buf[pl.ds(HALO - (self.width - 1) + j, tm), :] for j in range(self.width)]
        c = taps[0] * w_ref[0:1, :]
        for j in range(1, self.width):
            c = c + taps[j] * w_ref[j:j + 1, :]
        if self.has_bias:
            c = c + b_ref[...]
        return c, taps

    def _fwd_call(self, x, w, *bias):
        s, ch = x.shape
        tm = _pick(s, self.tm, SUBLANES)
        r8 = tm // HALO
        has_bias, post, c_pieces = self.has_bias, self.post, self.c_pieces

        def body(*refs):
            x_ref, prev_ref, w_ref = refs[:3]
            b_ref = refs[3] if has_bias else None
            outs, buf = refs[3 + has_bias:-1], refs[-1]
            c, _ = self._conv(pl.program_id(0), x_ref, prev_ref, w_ref, b_ref, buf)
            ys = post(*[c[:, o:o + n] for o, n in c_pieces])
            for r, y in zip(outs, ys):
                r[...] = y

        outs = _pcall(
            body, name=self.name + "_fwd", out_shape=[jax.ShapeDtypeStruct((s, n), F32) for n in self.out_widths],
            grid=(s // tm,),
            in_specs=[pl.BlockSpec((tm, ch), lambda i: (i, 0)),
                      pl.BlockSpec((HALO, ch), lambda i: (jnp.maximum(i * r8 - 1, 0), 0)),
                      pl.BlockSpec(w.shape, lambda i: (0, 0))]
            + ([pl.BlockSpec((1, ch), lambda i: (0, 0))] if has_bias else []),
            out_specs=[pl.BlockSpec((tm, n), lambda i: (i, 0)) for n in self.out_widths],
            scratch_shapes=[pltpu.VMEM((tm + HALO, ch), F32)],
        )(x, x, w, *bias)
        return tuple(outs) if len(outs) > 1 else outs[0]

    def _bwd_call(self, res, g):
        x, w = res[0], res[1]
        bias = res[2:]
        gs = tuple(g) if len(self.out_widths) > 1 else (g,)
        s, ch = x.shape
        tm = _pick(s, self.tm, SUBLANES)
        r8 = tm // HALO
        nb = s // tm
        has_bias, post, c_pieces, width = self.has_bias, self.post, self.c_pieces, self.width
        ng = len(gs)

        def body1(*refs):
            x_ref, prev_ref, w_ref = refs[:3]
            b_ref = refs[3] if has_bias else None
            k = 3 + has_bias
            g_refs = refs[k:k + ng]
            dc_ref, dw_ref = refs[k + ng], refs[k + ng + 1]
            db_ref = refs[k + ng + 2] if has_bias else None
            buf = refs[-1]
            i = pl.program_id(0)
            c, taps = self._conv(i, x_ref, prev_ref, w_ref, b_ref, buf)
            _, vjp = jax.vjp(lambda *a: tuple(post(*a)), *[c[:, o:o + n] for o, n in c_pieces])
            dcs = vjp(tuple(r[...] for r in g_refs))
            for (o, n), d in zip(c_pieces, dcs):
                dc_ref[:, o:o + n] = d

            @pl.when(i == 0)
            def _():
                dw_ref[...] = jnp.zeros_like(dw_ref)
                if has_bias:
                    db_ref[...] = jnp.zeros_like(db_ref)

            dc = dc_ref[...]
            for j in range(width):
                dw_ref[j:j + 1, :] += jnp.sum(dc * taps[j], axis=0, keepdims=True)
            if has_bias:
                db_ref[...] += jnp.sum(dc, axis=0, keepdims=True)

        outs1 = _pcall(
            body1, name=self.name + "_bwd_act",
            out_shape=[jax.ShapeDtypeStruct((s, ch), F32), jax.ShapeDtypeStruct(w.shape, F32)]
            + ([jax.ShapeDtypeStruct((1, ch), F32)] if has_bias else []),
            grid=(nb,),
            in_specs=[pl.BlockSpec((tm, ch), lambda i: (i, 0)),
                      pl.BlockSpec((HALO, ch), lambda i: (jnp.maximum(i * r8 - 1, 0), 0)),
                      pl.BlockSpec(w.shape, lambda i: (0, 0))]
            + ([pl.BlockSpec((1, ch), lambda i: (0, 0))] if has_bias else [])
            + [pl.BlockSpec((tm, n), lambda i: (i, 0)) for n in self.out_widths],
            out_specs=[pl.BlockSpec((tm, ch), lambda i: (i, 0)), pl.BlockSpec(w.shape, lambda i: (0, 0))]
            + ([pl.BlockSpec((1, ch), lambda i: (0, 0))] if has_bias else []),
            scratch_shapes=[pltpu.VMEM((tm + HALO, ch), F32)],
        )(x, x, w, *bias, *gs)
        dc, dw = outs1[0], outs1[1]

        def body2(dc_ref, next_ref, w_ref, dx_ref, buf):
            i = pl.program_id(0)
            buf[0:tm, :] = dc_ref[...]
            buf[tm:tm + HALO, :] = jnp.where(i < nb - 1, next_ref[...], 0.0)
            dx = buf[pl.ds(width - 1, tm), :] * w_ref[0:1, :]
            for j in range(1, width):
                dx = dx + buf[pl.ds(width - 1 - j, tm), :] * w_ref[j:j + 1, :]
            dx_ref[...] = dx

        dx = _pcall(
            body2, name=self.name + "_bwd_in", out_shape=jax.ShapeDtypeStruct((s, ch), F32), grid=(nb,),
            in_specs=[pl.BlockSpec((tm, ch), lambda i: (i, 0)),
                      pl.BlockSpec((HALO, ch), lambda i: (jnp.minimum((i + 1) * r8, s // HALO - 1), 0)),
                      pl.BlockSpec(w.shape, lambda i: (0, 0))],
            out_specs=pl.BlockSpec((tm, ch), lambda i: (i, 0)),
            scratch_shapes=[pltpu.VMEM((tm + HALO, ch), F32)],
        )(dc, dc, w)
        return (dx, dw) + ((outs1[2],) if has_bias else ())


def _make_convops():
    return dict(
        gdn=ConvOp("gdnconv", 4, lambda q, k, v: (_silu(q), _silu(k), _silu(v)), _split(3 * HW, 3), [HW] * 3,
                   False, 256),
        ffn=ConvOp("ffnconv", 3, lambda a, b: (_silu(a) * b,), _split(2 * D_FF, 2), [D_FF], True, 256),
    )


ATT_Q = 512
ATT_K = 256
ATT_K_FOX = 512
SCALE = HEAD_DIM ** -0.5


ATT_Q_SB = 1024


def _att_tiles(s, fox=False):
    tk = _pick(s, ATT_K_FOX if fox else ATT_K, LANES)
    tq = _pick(s, ATT_Q if fox else ATT_Q_SB, tk)
    return tq, tk


def _att_specs(s, tq, tk):
    qspec = pl.BlockSpec((tq, HEAD_DIM), lambda h, i: (i, h))
    kspec = pl.BlockSpec((s, HEAD_DIM), lambda h, i: (0, h))
    colspec = pl.BlockSpec((None, tq, 1), lambda h, i: (h, i, 0))
    rowspec = pl.BlockSpec((None, s // tk, 1, tk), lambda h, i: (h, 0, 0, 0))
    return qspec, kspec, colspec, rowspec


def _krows(kb, tk):
    return pl.ds(pl.multiple_of(kb * tk, tk), tk)


def _stage_bf16(i, pairs):
    @pl.when(i == 0)
    def _():
        for src, dst in pairs:
            dst[...] = src[...].astype(BF16)


ATT_STRIP = 32


def _strips(tq):
    return [slice(r, r + ATT_STRIP) for r in range(0, tq, ATT_STRIP)]


def _visible(i, kb, rs, tq, tk, strict):
    rows = i * tq + rs.start + _iota2(ATT_STRIP, tk, 0)
    cols = kb * tk + _iota2(ATT_STRIP, tk, 1)
    return (cols < rows) if strict else (cols <= rows)


def _visible_block(i, kb, tq, tk):
    return kb * tk + _iota2(tq, tk, 1) <= i * tq + _iota2(tq, tk, 0)


def _blocks(i, ratio, blk, reverse=False):
    def full(n, carry):
        blk(i * ratio - 1 - n if reverse else n, False)
        return carry

    if reverse:
        for j in reversed(range(ratio)):
            blk(i * ratio + j, True)
    lax.fori_loop(0, i * ratio, full, 0)
    if not reverse:
        for j in range(ratio):
            blk(i * ratio + j, True)


def _vm(shape, dtype):
    return pltpu.VMEM(shape, dtype)


def _fox_fwd_call(q, k, v, ccol, crow):
    s = q.shape[0]
    tq, tk = _att_tiles(s, fox=True)
    qspec, kspec, colspec, rowspec = _att_specs(s, tq, tk)

    def body(q_ref, k_ref, v_ref, cq_ref, ck_ref, o_ref, lse_ref, k16, v16):
        i = pl.program_id(1)
        ratio = tq // tk
        _stage_bf16(i, [(k_ref, k16), (v_ref, v16)])
        qb = q_ref[...].astype(BF16)
        cq = cq_ref[...]

        def blk(kb, carry, masked):
            m, l, acc = carry
            sc = _dot(qb, k16[_krows(kb, tk), :], NT) * SCALE + (cq - ck_ref[kb])
            if masked:
                sc = jnp.where(_visible_block(i, kb, tq, tk), sc, NEG)
            m_new = jnp.maximum(m, jnp.max(sc, axis=-1, keepdims=True))
            alpha = jnp.exp(m - m_new)
            p = jnp.exp(sc - m_new)
            return (m_new, alpha * l + jnp.sum(p, axis=-1, keepdims=True),
                    alpha * acc + _dot(p, v16[_krows(kb, tk), :], NN))

        carry = (jnp.full((tq, 1), NEG, F32), jnp.zeros((tq, 1), F32), jnp.zeros((tq, HEAD_DIM), F32))
        carry = lax.fori_loop(0, i * ratio, lambda kb, c: blk(kb, c, False), carry)
        for j in range(ratio):
            carry = blk(i * ratio + j, carry, True)
        m, l, acc = carry
        o_ref[...] = acc / l
        lse_ref[...] = m + jnp.log(l)

    return _pcall(body, name="fox_fwd",
                  out_shape=[jax.ShapeDtypeStruct((s, HW), F32), jax.ShapeDtypeStruct((HEADS, s, 1), F32)],
                  grid=(HEADS, s // tq), in_specs=[qspec, kspec, kspec, colspec, rowspec],
                  out_specs=[qspec, colspec],
                  scratch_shapes=[_vm((s, HEAD_DIM), BF16), _vm((s, HEAD_DIM), BF16)])(q, k, v, ccol, crow)


def _fox_bwd_call(q, k, v, ccol, crow, o, lse, do):
    s = q.shape[0]
    tq, tk = _att_tiles(s, fox=True)
    ratio = tq // tk
    qspec, kspec, colspec, rowspec = _att_specs(s, tq, tk)

    def body(q_ref, k_ref, v_ref, cq_ref, ck_ref, o_ref, lse_ref, do_ref, dq_ref, dk_ref, dv_ref, dcq_ref, dck_ref,
             k16, v16):
        i = pl.program_id(1)
        _stage_bf16(i, [(k_ref, k16), (v_ref, v16)])

        @pl.when(i == 0)
        def _():
            dk_ref[...] = jnp.zeros_like(dk_ref)
            dv_ref[...] = jnp.zeros_like(dv_ref)
            dck_ref[...] = jnp.zeros_like(dck_ref)

        qb = q_ref[...].astype(BF16)
        dob = do_ref[...].astype(BF16)
        cq, lse = cq_ref[...], lse_ref[...]
        dl = jnp.sum(do_ref[...] * o_ref[...], axis=-1, keepdims=True)

        def blk(kb, carry, masked):
            dq, dcq = carry
            rows = _krows(kb, tk)
            kk, vv = k16[rows, :], v16[rows, :]
            sc = _dot(qb, kk, NT) * SCALE + (cq - ck_ref[kb])
            p = jnp.exp(sc - lse)
            if masked:
                p = jnp.where(_visible_block(i, kb, tq, tk), p, 0.0)
            dv_ref[rows, :] += _dot(p, dob, TN)
            ds = p * (_dot(dob, vv, NT) - dl)
            dk_ref[rows, :] += _dot(ds, qb, TN) * SCALE
            dck_ref[kb] += -jnp.sum(ds, axis=0, keepdims=True)
            return dq + _dot(ds, kk, NN) * SCALE, dcq + jnp.sum(ds, axis=-1, keepdims=True)

        carry = (jnp.zeros((tq, HEAD_DIM), F32), jnp.zeros((tq, 1), F32))
        carry = lax.fori_loop(0, i * ratio, lambda kb, c: blk(kb, c, False), carry)
        for j in range(ratio):
            carry = blk(i * ratio + j, carry, True)
        dq_ref[...] = carry[0]
        dcq_ref[...] = carry[1]

    return _pcall(body, name="fox_bwd",
                  out_shape=[jax.ShapeDtypeStruct((s, HW), F32)] * 3
                  + [jax.ShapeDtypeStruct((HEADS, s, 1), F32), jax.ShapeDtypeStruct((HEADS, s // tk, 1, tk), F32)],
                  grid=(HEADS, s // tq),
                  in_specs=[qspec, kspec, kspec, colspec, rowspec, qspec, colspec, qspec],
                  out_specs=[qspec, kspec, kspec, colspec, rowspec],
                  scratch_shapes=[_vm((s, HEAD_DIM), BF16), _vm((s, HEAD_DIM), BF16)],
                  )(q, k, v, ccol, crow, o, lse, do)


@jax.custom_vjp
def fox_attention(q, k, v, ccol, crow):
    return _fox_fwd_call(q, k, v, ccol, crow)[0]


def _fox_vjp_fwd(q, k, v, ccol, crow):
    o, lse = _fox_fwd_call(q, k, v, ccol, crow)
    return o, (q, k, v, ccol, crow, o, lse)


fox_attention.defvjp(_fox_vjp_fwd, lambda res, g: tuple(_fox_bwd_call(*res, g)))


def _sb_fwd_call(q, k, v):
    s = q.shape[0]
    tq, tk = _att_tiles(s)
    ratio = tq // tk
    qspec, kspec, colspec, _ = _att_specs(s, tq, tk)

    def body(q_ref, k_ref, v_ref, o_ref, tot_ref, k16, v16, q16, s_scr, w_scr, lk16, a16, run, acc):
        i = pl.program_id(1)
        _stage_bf16(i, [(k_ref, k16), (v_ref, v16)])
        q16[...] = q_ref[...].astype(BF16)
        run[...] = jnp.zeros_like(run)
        acc[...] = jnp.zeros_like(acc)
        suffix = jnp.where(_iota2(tk, tk, 0) >= _iota2(tk, tk, 1), 1.0, 0.0).astype(BF16)

        def blk(kb, masked):
            rows = _krows(kb, tk)
            s_scr[...] = _dot(q16[...], k16[rows, :], NT)
            for rs in _strips(tq):
                lk = -_softplus(s_scr[rs, :] * SCALE)
                if masked:
                    lk = jnp.where(_visible(i, kb, rs, tq, tk, True), lk, 0.0)
                lk16[rs, :] = lk.astype(BF16)
            w_scr[...] = _dot(lk16[...], suffix, NN)
            for rs in _strips(tq):
                a = jnp.exp(s_scr[rs, :] * SCALE + w_scr[rs, :] + run[rs, :])
                if masked:
                    a = jnp.where(_visible(i, kb, rs, tq, tk, True), a, 0.0)
                a16[rs, :] = a.astype(BF16)
                run[rs, :] += w_scr[rs, 0:1]
            acc[...] += _dot(a16[...], v16[rows, :], NN)

        _blocks(i, ratio, blk, reverse=True)
        o_ref[...] = acc[...]
        tot_ref[...] = run[...]

    return _pcall(body, name="sb_fwd",
                  out_shape=[jax.ShapeDtypeStruct((s, HW), F32), jax.ShapeDtypeStruct((HEADS, s, 1), F32)],
                  grid=(HEADS, s // tq), in_specs=[qspec, kspec, kspec], out_specs=[qspec, colspec],
                  scratch_shapes=[_vm((s, HEAD_DIM), BF16), _vm((s, HEAD_DIM), BF16), _vm((tq, HEAD_DIM), BF16),
                                  _vm((tq, tk), F32), _vm((tq, tk), F32), _vm((tq, tk), BF16), _vm((tq, tk), BF16),
                                  _vm((tq, 1), F32), _vm((tq, HEAD_DIM), F32)])(q, k, v)


def _sb_bwd_call(q, k, v, tot, do):
    s = q.shape[0]
    tq, tk = _att_tiles(s)
    ratio = tq // tk
    qspec, kspec, colspec, _ = _att_specs(s, tq, tk)

    def body(q_ref, k_ref, v_ref, tot_ref, do_ref, dq_ref, dk_ref, dv_ref, k16, v16, q16, do16, s_scr, e_scr, w_scr,
             lz16, a16, e16, left, esum):
        i = pl.program_id(1)
        _stage_bf16(i, [(k_ref, k16), (v_ref, v16)])

        @pl.when(i == 0)
        def _():
            dk_ref[...] = jnp.zeros_like(dk_ref)
            dv_ref[...] = jnp.zeros_like(dv_ref)

        q16[...] = q_ref[...].astype(BF16)
        do16[...] = do_ref[...].astype(BF16)
        left[...] = jnp.zeros_like(left)
        esum[...] = jnp.zeros_like(esum)
        dq_ref[...] = jnp.zeros_like(dq_ref)
        prefix = jnp.where(_iota2(tk, tk, 0) <= _iota2(tk, tk, 1), 1.0, 0.0).astype(BF16)

        def blk(kb, masked):
            rows = _krows(kb, tk)
            s_scr[...] = _dot(q16[...], k16[rows, :], NT)
            e_scr[...] = _dot(do16[...], v16[rows, :], NT)
            for rs in _strips(tq):
                lk = -_softplus(s_scr[rs, :] * SCALE)
                if masked:
                    lk = jnp.where(_visible(i, kb, rs, tq, tk, True), lk, 0.0)
                lz16[rs, :] = lk.astype(BF16)
            w_scr[...] = _dot(lz16[...], prefix, NN)
            for rs in _strips(tq):
                rc = (tot_ref[rs, :] - left[rs, :]) - (w_scr[rs, :] - lz16[rs, :].astype(F32))
                a = jnp.exp(s_scr[rs, :] * SCALE + rc)
                if masked:
                    a = jnp.where(_visible(i, kb, rs, tq, tk, True), a, 0.0)
                e = a * e_scr[rs, :]
                a16[rs, :] = a.astype(BF16)
                e16[rs, :] = e.astype(BF16)
                e_scr[rs, :] = e
                left[rs, :] += w_scr[rs, tk - 1:tk]
            w_scr[...] = _dot(e16[...], prefix, NN)
            for rs in _strips(tq):
                dz = e_scr[rs, :] - _sigmoid(s_scr[rs, :] * SCALE) * (esum[rs, :] + w_scr[rs, :])
                if masked:
                    dz = jnp.where(_visible(i, kb, rs, tq, tk, True), dz, 0.0)
                lz16[rs, :] = dz.astype(BF16)
                esum[rs, :] += w_scr[rs, tk - 1:tk]
            dv_ref[rows, :] += _dot(a16[...], do16[...], TN)
            dk_ref[rows, :] += _dot(lz16[...], q16[...], TN) * SCALE
            dq_ref[...] += _dot(lz16[...], k16[rows, :], NN) * SCALE

        _blocks(i, ratio, blk)

    return _pcall(body, name="sb_bwd", out_shape=[jax.ShapeDtypeStruct((s, HW), F32)] * 3, grid=(HEADS, s // tq),
                  in_specs=[qspec, kspec, kspec, colspec, qspec], out_specs=[qspec, kspec, kspec],
                  scratch_shapes=[_vm((s, HEAD_DIM), BF16), _vm((s, HEAD_DIM), BF16), _vm((tq, HEAD_DIM), BF16),
                                  _vm((tq, HEAD_DIM), BF16), _vm((tq, tk), F32), _vm((tq, tk), F32),
                                  _vm((tq, tk), F32), _vm((tq, tk), BF16), _vm((tq, tk), BF16), _vm((tq, tk), BF16),
                                  _vm((tq, 1), F32), _vm((tq, 1), F32)])(q, k, v, tot, do)


@jax.custom_vjp
def sb_attention(q, k, v):
    return _sb_fwd_call(q, k, v)[0]


def _sb_vjp_fwd(q, k, v):
    o, tot = _sb_fwd_call(q, k, v)
    return o, (q, k, v, tot)


sb_attention.defvjp(_sb_vjp_fwd, lambda res, g: tuple(_sb_bwd_call(*res, g)))


def _mem_specs(s, nk, t):
    return (pl.BlockSpec((t, HEAD_DIM), lambda h, i: (i, h)), pl.BlockSpec((nk, HEAD_DIM), lambda h, i: (0, h)))


def _mem_probs(qb, kk):
    sc = _dot(qb, kk, NT) * SCALE
    p = jnp.exp(sc - jnp.max(sc, axis=-1, keepdims=True))
    return p / jnp.sum(p, axis=-1, keepdims=True)


def _mem_fwd_call(q, k, v):
    s, nk = q.shape[0], k.shape[0]
    t = _pick(s, 512, SUBLANES)
    qspec, kspec = _mem_specs(s, nk, t)

    def body(q_ref, k_ref, v_ref, o_ref):
        o_ref[...] = _dot(_mem_probs(q_ref[...].astype(BF16), k_ref[...]), v_ref[...], NN)

    return _pcall(body, name="mem_fwd", out_shape=jax.ShapeDtypeStruct((s, HW), F32), grid=(HEADS, s // t),
                  in_specs=[qspec, kspec, kspec], out_specs=qspec)(q, k, v)


def _mem_bwd_call(q, k, v, do):
    s, nk = q.shape[0], k.shape[0]
    t = _pick(s, 512, SUBLANES)
    qspec, kspec = _mem_specs(s, nk, t)

    def body(q_ref, k_ref, v_ref, do_ref, dq_ref, dk_ref, dv_ref):
        @pl.when(pl.program_id(1) == 0)
        def _():
            dk_ref[...] = jnp.zeros_like(dk_ref)
            dv_ref[...] = jnp.zeros_like(dv_ref)

        qb = q_ref[...].astype(BF16)
        dob = do_ref[...].astype(BF16)
        p = _mem_probs(qb, k_ref[...])
        dv_ref[...] += _dot(p, dob, TN)
        dp = _dot(dob, v_ref[...], NT)
        ds = p * (dp - jnp.sum(p * dp, axis=-1, keepdims=True))
        dq_ref[...] = _dot(ds, k_ref[...], NN) * SCALE
        dk_ref[...] += _dot(ds, qb, TN) * SCALE

    return _pcall(body, name="mem_bwd",
                  out_shape=[jax.ShapeDtypeStruct((s, HW), F32)] + [jax.ShapeDtypeStruct((nk, HW), F32)] * 2,
                  grid=(HEADS, s // t), in_specs=[qspec, kspec, kspec, qspec],
                  out_specs=[qspec, kspec, kspec])(q, k, v, do)


@jax.custom_vjp
def mem_attention(q, k, v):
    return _mem_fwd_call(q, k, v)


mem_attention.defvjp(lambda q, k, v: (_mem_fwd_call(q, k, v), (q, k, v)),
                     lambda res, g: tuple(_mem_bwd_call(*res, g)))


BNN = (((2,), (1,)), ((0,), (0,)))
BNT = (((2,), (2,)), ((0,), (0,)))
BTN = (((1,), (1,)), ((0,), (0,)))


def _bdot(a, b, dn):
    return lax.dot_general(a.astype(BF16), b.astype(BF16), dn, preferred_element_type=F32)


def _bdotf(a, b, dn):
    return lax.dot_general(a, b, dn, precision=lax.Precision.HIGHEST, preferred_element_type=F32)


@jax.custom_vjp
def bmm(a, b):
    return _bdot(a, b, BNN)


bmm.defvjp(lambda a, b: (_bdot(a, b, BNN), (a, b)), lambda r, g: (_bdot(g, r[1], BNT), _bdot(r[0], g, BTN)))


@jax.custom_vjp
def bmm_nt(a, b):
    return _bdot(a, b, BNT)


bmm_nt.defvjp(lambda a, b: (_bdot(a, b, BNT), (a, b)), lambda r, g: (_bdot(g, r[1], BNN), _bdot(g, r[0], BTN)))


@jax.custom_vjp
def bmm_tn(a, b):
    return _bdot(a, b, BTN)


bmm_tn.defvjp(lambda a, b: (_bdot(a, b, BTN), (a, b)), lambda r, g: (_bdot(r[1], g, BNT), _bdot(r[0], g, BNN)))


def _unit_lower_inverse(nm):
    eye = jnp.where(_iota2(CHUNK, CHUNK, 0) == _iota2(CHUNK, CHUNK, 1), 1.0, 0.0).astype(F32)[None]
    p = eye - nm
    m = nm
    for _ in range(5):
        m = _bdotf(m, m, BNN)
        p = _bdotf(p, eye + m, BNN)
    return p


@jax.custom_vjp
def _solve2(nm, r1, r2):
    inv = _unit_lower_inverse(nm)
    return _bdotf(inv, r1, BNN), _bdotf(inv, r2, BNN)


def _solve2_fwd(nm, r1, r2):
    inv = _unit_lower_inverse(nm)
    u, w = _bdotf(inv, r1, BNN), _bdotf(inv, r2, BNN)
    return (u, w), (inv, u, w)


def _solve2_bwd(res, g):
    inv, u, w = res
    d1, d2 = _bdotf(inv, g[0], BTN), _bdotf(inv, g[1], BTN)
    return -(_bdotf(d1, u, BNT) + _bdotf(d2, w, BNT)), d1, d2


_solve2.defvjp(_solve2_fwd, _solve2_bwd)


def _gdn_chunk(q, k, v, gcc, gcr, b, gl, st):
    qn = q * lax.rsqrt(jnp.sum(q * q, axis=-1, keepdims=True) + EPS) * SCALE
    kn = k * lax.rsqrt(jnp.sum(k * k, axis=-1, keepdims=True) + EPS)
    r, c = _iota2(CHUNK, CHUNK, 0)[None], _iota2(CHUNK, CHUNK, 1)[None]
    decay = jnp.exp(jnp.where(r >= c, gcc - gcr, NEG))
    nm = jnp.where(r > c, b * bmm_nt(kn, kn) * decay, 0.0)
    eg = jnp.exp(gcc)
    u, w = _solve2(nm, v * b, kn * (b * eg))
    attn = bmm_nt(qn, kn) * decay
    v_new = u - bmm(w, st)
    o = bmm(qn * eg, st) + bmm(attn, v_new)
    st_new = st * jnp.exp(gl) + bmm_tn(kn * jnp.exp(gl - gcc), v_new)
    return o, st_new


def _heads_of(ref, rows):
    return jnp.stack([ref[rows, _head_cols(h)] for h in range(HEADS)])


def _head_cols(h):
    return slice(h * HEAD_DIM, (h + 1) * HEAD_DIM)


GDN_ROWS = 512


def _gdn_specs(s, tg, rev):
    nb = s // tg
    cpb = tg // CHUNK
    j_of = (lambda j: nb - 1 - j) if rev else (lambda j: j)
    qspec = pl.BlockSpec((tg, HW), lambda j: (j_of(j), 0))
    colspec = pl.BlockSpec((HEADS, tg, 1), lambda j: (0, j_of(j), 0))
    rowspec = pl.BlockSpec((HEADS, cpb, 1, CHUNK), lambda j: (0, j_of(j), 0, 0))
    onespec = pl.BlockSpec((HEADS, cpb, 1, 1), lambda j: (0, j_of(j), 0, 0))
    stspec = pl.BlockSpec((HEADS, cpb, HEAD_DIM, HEAD_DIM), lambda j: (0, j_of(j), 0, 0))
    return qspec, colspec, rowspec, onespec, stspec


def _gdn_fwd_call(q, k, v, gcc, gcr, bc, gl):
    s = q.shape[0]
    tg = _pick(s, GDN_ROWS, CHUNK)
    cpb = tg // CHUNK
    qspec, colspec, rowspec, onespec, stspec = _gdn_specs(s, tg, False)

    def body(q_ref, k_ref, v_ref, gcc_ref, gcr_ref, b_ref, gl_ref, o_ref, st_ref, st):
        @pl.when(pl.program_id(0) == 0)
        def _():
            st[...] = jnp.zeros_like(st)

        def chunk(ci, _):
            rows = pl.ds(pl.multiple_of(ci * CHUNK, CHUNK), CHUNK)
            s_in = st[...]
            st_ref[:, ci] = s_in
            o, s_new = _gdn_chunk(_heads_of(q_ref, rows), _heads_of(k_ref, rows), _heads_of(v_ref, rows),
                                  gcc_ref[:, rows, :], gcr_ref[:, ci], b_ref[:, rows, :], gl_ref[:, ci], s_in)
            for h in range(HEADS):
                o_ref[rows, _head_cols(h)] = o[h]
            st[...] = s_new
            return 0

        lax.fori_loop(0, cpb, chunk, 0)

    return _pcall(body, name="gdn_fwd",
                  out_shape=[jax.ShapeDtypeStruct((s, HW), F32),
                             jax.ShapeDtypeStruct((HEADS, s // CHUNK, HEAD_DIM, HEAD_DIM), F32)],
                  grid=(s // tg,), in_specs=[qspec, qspec, qspec, colspec, rowspec, colspec, onespec],
                  out_specs=[qspec, stspec], scratch_shapes=[pltpu.VMEM((HEADS, HEAD_DIM, HEAD_DIM), F32)],
                  )(q, k, v, gcc, gcr, bc, gl)


def _gdn_bwd_call(q, k, v, gcc, gcr, bc, gl, states, do):
    s = q.shape[0]
    tg = _pick(s, GDN_ROWS, CHUNK)
    cpb = tg // CHUNK
    qspec, colspec, rowspec, onespec, stspec = _gdn_specs(s, tg, True)

    def body(q_ref, k_ref, v_ref, gcc_ref, gcr_ref, b_ref, gl_ref, st_ref, do_ref,
             dq_ref, dk_ref, dv_ref, dgcc_ref, dgcr_ref, db_ref, dgl_ref, dst):
        @pl.when(pl.program_id(0) == 0)
        def _():
            dst[...] = jnp.zeros_like(dst)

        def chunk(n, _):
            ci = cpb - 1 - n
            rows = pl.ds(pl.multiple_of(ci * CHUNK, CHUNK), CHUNK)
            _, vjp = jax.vjp(_gdn_chunk, _heads_of(q_ref, rows), _heads_of(k_ref, rows), _heads_of(v_ref, rows),
                             gcc_ref[:, rows, :], gcr_ref[:, ci], b_ref[:, rows, :], gl_ref[:, ci], st_ref[:, ci])
            dq, dk, dv, dgcc, dgcr, db, dgl, ds_in = vjp((_heads_of(do_ref, rows), dst[...]))
            for h in range(HEADS):
                cols = _head_cols(h)
                dq_ref[rows, cols] = dq[h]
                dk_ref[rows, cols] = dk[h]
                dv_ref[rows, cols] = dv[h]
            dgcc_ref[:, rows, :] = dgcc
            dgcr_ref[:, ci] = dgcr
            db_ref[:, rows, :] = db
            dgl_ref[:, ci] = dgl
            dst[...] = ds_in
            return 0

        lax.fori_loop(0, cpb, chunk, 0)

    n = s // CHUNK
    return _pcall(body, name="gdn_bwd",
                  out_shape=[jax.ShapeDtypeStruct((s, HW), F32)] * 3
                  + [jax.ShapeDtypeStruct((HEADS, s, 1), F32), jax.ShapeDtypeStruct((HEADS, n, 1, CHUNK), F32),
                     jax.ShapeDtypeStruct((HEADS, s, 1), F32), jax.ShapeDtypeStruct((HEADS, n, 1, 1), F32)],
                  grid=(s // tg,),
                  in_specs=[qspec, qspec, qspec, colspec, rowspec, colspec, onespec, stspec, qspec],
                  out_specs=[qspec, qspec, qspec, colspec, rowspec, colspec, onespec],
                  scratch_shapes=[pltpu.VMEM((HEADS, HEAD_DIM, HEAD_DIM), F32)],
                  )(q, k, v, gcc, gcr, bc, gl, states, do)


@jax.custom_vjp
def gated_delta(q, k, v, gcc, gcr, bc, gl):
    return _gdn_fwd_call(q, k, v, gcc, gcr, bc, gl)[0]


def _gdn_vjp_fwd(q, k, v, gcc, gcr, bc, gl):
    o, states = _gdn_fwd_call(q, k, v, gcc, gcr, bc, gl)
    return o, (q, k, v, gcc, gcr, bc, gl, states)


gated_delta.defvjp(_gdn_vjp_fwd, lambda res, g: tuple(_gdn_bwd_call(*res, g)))


def _loss_call(y, target):
    s, d = y.shape
    tm = _pick(s, 512, SUBLANES)

    def body(y_ref, t_ref, dy_ref, loss_ref):
        @pl.when(pl.program_id(0) == 0)
        def _():
            loss_ref[...] = jnp.zeros_like(loss_ref)

        err = y_ref[...] - t_ref[...]
        dy_ref[...] = err * (1.0 / d)
        loss_ref[...] += 0.5 * jnp.sum(jnp.mean(err * err, axis=-1, keepdims=True), axis=0, keepdims=True)

    dy, part = _pcall(body, name="loss_head",
                      out_shape=[jax.ShapeDtypeStruct((s, d), F32), jax.ShapeDtypeStruct((1, 1), F32)],
                      grid=(s // tm,), in_specs=[pl.BlockSpec((tm, d), lambda i: (i, 0))] * 2,
                      out_specs=[pl.BlockSpec((tm, d), lambda i: (i, 0)), pl.BlockSpec((1, 1), lambda i: (0, 0))],
                      )(y, target)
    return part[0, 0], dy


def _cols_and_rows(a, lane0, t):
    s = a.shape[0]
    at = a[:, lane0:lane0 + HEADS].T
    return at, at[:, :, None], at.reshape(HEADS, s // t, 1, t)


def _pad_lanes(v, lane0):
    return jnp.pad(v, (lane0, LANES - lane0 - v.shape[0])).reshape(1, LANES)


def _layer(x, mem, w, w16, ops, convs):
    s = x.shape[0]
    row = lambda v: v.reshape(1, -1)
    mw = lambda n: (w[n], w16[n])
    h = ops["rms"](x, row(w["norm_mix"]))
    fq, fk, fv, gqkv, gz, sq, sk, sv, gt, gm = proj(h, *mw("w_in"))

    logf, beta, gc = ops["small"](gm, _pad_lanes(w["fox_fbias"], LANE_FF), _pad_lanes(w["gdn_a_log"], LANE_GA),
                                  _pad_lanes(w["gdn_dt_bias"], LANE_GA))
    _, ccol, crow = _cols_and_rows(seq_cumsum(logf), LANE_FF, _att_tiles(s, fox=True)[1])
    ya = fox_attention(ops["headnorm"](fq, row(w["fox_qnorm"])), ops["headnorm"](fk, row(w["fox_knorm"])), fv,
                       ccol, crow)
    cq, ck, cv = convs["gdn"](gqkv, w["gdn_conv"])
    gct, gcc, gcr = _cols_and_rows(gc, LANE_GA, CHUNK)
    gl = gct.reshape(HEADS, s // CHUNK, CHUNK)[:, :, CHUNK - 1].reshape(HEADS, s // CHUNK, 1, 1)
    bc = beta[:, LANE_GB:LANE_GB + HEADS].T[:, :, None]
    yb = ops["gdnpost"](gated_delta(cq, ck, cv, gcc, gcr, bc, gl), gz, row(w["gdn_onorm"]))
    yc = sb_attention(sq, sk, sv)
    gb = w["gate_bias"]
    mixed = ops["merge"](gt, matmul(ya, *mw("w_oa")), matmul(yb, *mw("w_ob")), matmul(yc, *mw("w_oc")),
                         row(gb[:D_MODEL]), row(gb[D_MODEL:2 * D_MODEL]), row(gb[2 * D_MODEL:]))
    x = matmul_add(x, mixed, *mw("w_out"))
    mq = ops["headnorm"](matmul(ops["rms"](x, row(w["norm_xq"])), *mw("w_mq")), row(w["mq_norm"]))
    kv = matmul(ops["rms"](mem, row(w["norm_mem"])), *mw("w_mkv"))
    mk = ops["headnorm"](kv[:, :HW], row(w["mk_norm"]))
    x = matmul_add(x, mem_attention(mq, mk, kv[:, HW:]), *mw("w_mo"))
    u = matmul(ops["rms"](x, row(w["norm_ffn"])), *mw("w_up"))
    act = convs["ffn"](u, w["ffn_conv"], row(w["ffn_conv_b"]))
    return matmul_add(x, act, *mw("w_down"))


def _forward(x, mem, layers, layers16):
    ops, convs = _make_rowops(), _make_convops()
    for w, w16 in zip(layers, layers16):
        x = _layer(x, mem, w, w16, ops, convs)
    return x


ANY = pl.BlockSpec(memory_space=pl.ANY)


N_PEERS = N_DEV - 1


def _ccall(body, *, name, out_shape, n_arrays):
    return pl.pallas_call(body, name=name, out_shape=out_shape, in_specs=[ANY] * n_arrays,
                          out_specs=[ANY] * n_arrays,
                          scratch_shapes=[pltpu.SemaphoreType.DMA((N_PEERS * n_arrays,)),
                                          pltpu.SemaphoreType.DMA((N_PEERS * n_arrays,)),
                                          pltpu.SemaphoreType.DMA((n_arrays,))],
                          interpret=False)


def _all_gather(name, shards):
    n = len(shards)

    def body(*refs):
        x_refs, out_refs = refs[:n], refs[n:2 * n]
        send_sems, recv_sems, local_sems = refs[2 * n:]
        x, y, c = lax.axis_index("x"), lax.axis_index("y"), lax.axis_index("c")
        me, sibling = (x, y, c), (x, y, 1 - c)
        chips = [(1 - x, y), (x, 1 - y), (1 - x, 1 - y)]

        def slot(a, px, py, pc):
            return out_refs[a].at[4 * px + 2 * py + pc]

        def copy(a, k, block, to, src=None):
            return pltpu.make_async_remote_copy(
                src_ref=slot(a, *block) if src is None else src, dst_ref=slot(a, *block),
                send_sem=send_sems.at[N_PEERS * a + k], recv_sem=recv_sems.at[N_PEERS * a + k], device_id=to,
                device_id_type=MESH)

        mine = [pltpu.make_async_copy(x_refs[a], slot(a, *me), local_sems.at[a]) for a in range(n)]
        first = []
        for a in range(n):
            first.append(copy(a, 0, me, sibling, src=x_refs[a]))
            first += [copy(a, 1 + j, me, (*chip, c), src=x_refs[a]) for j, chip in enumerate(chips)]
        for cp in mine + first:
            cp.start()
        passed = []
        for j, chip in enumerate(chips):
            for a in range(n):
                copy(a, 1 + j, (*chip, c), me).wait_recv()
                passed.append(copy(a, 4 + j, (*chip, c), sibling))
                passed[-1].start()
        for a in range(n):
            copy(a, 0, sibling, me).wait_recv()
            for j, chip in enumerate(chips):
                copy(a, 4 + j, (*chip, 1 - c), me).wait_recv()
        for cp in first + passed:
            cp.wait_send()
        for cp in mine:
            cp.wait()

    return _ccall(body, name=name, out_shape=[jax.ShapeDtypeStruct((N_DEV,) + s.shape, s.dtype) for s in shards],
                  n_arrays=n)(*shards)


SEM = pl.BlockSpec(memory_space=pltpu.SEMAPHORE)
HBM = pl.BlockSpec(memory_space=pltpu.HBM)
DATAFLOW = pltpu.SideEffectType.DATAFLOW_SIDE_EFFECTING


def _exchange_copies(p_refs, land_refs, send_sem, recv_sem):
    x, y, c = lax.axis_index("x"), lax.axis_index("y"), lax.axis_index("c")
    copies = []
    for p_ref, land_ref in zip(p_refs, land_refs):
        for k in range(1, N_DEV):
            px, py, pc = x ^ ((k >> 2) & 1), y ^ ((k >> 1) & 1), c ^ (k & 1)
            copies.append(pltpu.make_async_remote_copy(
                src_ref=p_ref.at[4 * px + 2 * py + pc], dst_ref=land_ref.at[k - 1], send_sem=send_sem,
                recv_sem=recv_sem, device_id=(px, py, pc), device_id_type=MESH))
    return copies


def _exchange_start(name, parts):
    n = len(parts)
    lands = [lax.empty((N_PEERS,) + p.shape[1:], p.dtype) for p in parts]

    def body(*refs):
        send_sem, recv_sem = refs[2 * n], refs[2 * n + 1]
        for cp in _exchange_copies(refs[:n], refs[n:2 * n], send_sem, recv_sem):
            cp.start()
        refs[-1][...] = jnp.zeros_like(refs[-1])

    outs = pl.pallas_call(
        body, name=name,
        out_shape=(pltpu.SemaphoreType.DMA(()), pltpu.SemaphoreType.DMA(()),
                   *[pltpu.HBM(a.shape, a.dtype) for a in parts + lands],
                   jax.ShapeDtypeStruct((SUBLANES, LANES), F32)),
        in_specs=[HBM] * (2 * n), out_specs=(SEM, SEM, *[HBM] * (2 * n), pl.BlockSpec(memory_space=pltpu.VMEM)),
        input_output_aliases={i: 2 + i for i in range(2 * n)},
        compiler_params=pltpu.CompilerParams(has_side_effects=DATAFLOW),
        interpret=False)(*[pltpu.with_memory_space_constraint(a, pltpu.HBM) for a in parts + lands])
    return outs[0], outs[1], list(outs[2:2 + n]), list(outs[2 + n:2 + 2 * n])


def _exchange_wait(name, send_sem, recv_sem, parts, lands, after):
    n = len(parts)

    def body(*refs):
        for cp in _exchange_copies(refs[:n], refs[n:2 * n], refs[2 * n], refs[2 * n + 1]):
            cp.wait_send()
            cp.wait_recv()

    outs = pl.pallas_call(
        body, name=name, out_shape=tuple(pltpu.HBM(a.shape, a.dtype) for a in parts + lands),
        in_specs=[HBM] * (2 * n) + [SEM, SEM, ANY], out_specs=[HBM] * (2 * n),
        input_output_aliases={i: i for i in range(2 * n)},
        compiler_params=pltpu.CompilerParams(has_side_effects=DATAFLOW),
        interpret=False)(*parts, *lands, send_sem, recv_sem, after)
    return list(outs[:n]), list(outs[n:])


ADAM_SLOT_BYTES = 4 * 1024 * 1024


def _adam_call(name, w, slots, m, v):
    r, n = w.shape
    rows_unit = 2 * SUBLANES
    tr = _pick(r, max(rows_unit, ADAM_SLOT_BYTES // (N_DEV * n * 4)), rows_unit)
    spec = pl.BlockSpec((tr, n), lambda i: (i, 0))

    def body(w_ref, s_ref, m_ref, v_ref, g_ref, d_ref, nm_ref, nv_ref):
        g = s_ref[0].astype(F32)
        for d in range(1, N_DEV):
            g = g + s_ref[d].astype(F32)
        nm = ADAM_B1 * m_ref[...] + (1.0 - ADAM_B1) * g
        nv = ADAM_B2 * v_ref[...] + (1.0 - ADAM_B2) * (g * g)
        m_hat = nm / (1.0 - ADAM_B1 ** ADAM_STEP)
        v_hat = nv / (1.0 - ADAM_B2 ** ADAM_STEP)
        g_ref[...] = g
        d_ref[...] = -ADAM_LR * (m_hat / (jnp.sqrt(v_hat) + ADAM_EPS) + ADAM_WD * w_ref[...])
        nm_ref[...] = nm
        nv_ref[...] = nv

    return _pcall(body, name=name, out_shape=[jax.ShapeDtypeStruct((r, n), F32)] * 4, grid=(r // tr,),
                  in_specs=[spec, pl.BlockSpec((N_DEV, tr, n), lambda i: (0, i, 0)), spec, spec],
                  out_specs=[spec] * 4)(w, slots, m, v)


def _pack_rows(flat, rows):
    return jnp.pad(flat, (0, rows * PACK_COLS - flat.shape[0])).reshape(rows, PACK_COLS)


def _regroup_in(w_in):
    cols = [w_in[:, a:b] for a, b in _IN_SRC]
    return jnp.concatenate(cols + [jnp.zeros((w_in.shape[0], N_IN_PAD - N_IN), w_in.dtype)], axis=1)


def _ungroup_in(d):
    starts = {}
    off = 0
    for a, b in _IN_SRC:
        starts[a] = (off, b - a)
        off += b - a
    return jnp.concatenate([d[:, starts[a][0]:starts[a][0] + starts[a][1]] for a in sorted(starts)], axis=1)


def _full_weights(gathered):
    out = {}
    for n, g in zip(SHARDED_ORDER, gathered):
        (r, c), axis = SHARDED[n]
        out[n] = g.reshape(r, c) if axis == 0 else g.transpose(1, 0, 2).reshape(r, c)
    out["w_in"] = _regroup_in(out["w_in"])
    return out


def _in_f32(full):
    return {n: v.astype(F32) for n, v in full.items()}


def _for_transport(name, shard):
    return shard if name in ("gdn_conv", "ffn_conv") else shard.astype(BF16)


def _grad_parts(grads):
    parts = []
    for n in SHARDED_ORDER:
        (r, c), axis = SHARDED[n]
        g = _ungroup_in(grads[n]) if n == "w_in" else grads[n]
        if axis == 0:
            parts.append(g.reshape(N_DEV, r // N_DEV, c).astype(BF16))
        else:
            parts.append(g.reshape(r, N_DEV, c // N_DEV).transpose(1, 0, 2).astype(BF16))
    return parts


def _pack_small(vals):
    return _pack_rows(jnp.concatenate([vals[n].reshape(-1) for n in SMALL_ORDER]), SMALL_ROWS)


def _unpack_small(packed):
    flat = packed.reshape(-1)
    out, off = {}, 0
    for n in SMALL_ORDER:
        size = DEPTH * SMALL_WIDTH[n]
        out[n] = flat[off:off + size].reshape(DEPTH, SMALL_WIDTH[n])
        off += size
    return out


def kernel(x, mem, norm_mix, w_in, fox_fbias, fox_qnorm, fox_knorm, gdn_conv, gdn_a_log, gdn_dt_bias, gdn_onorm, gate_bias, w_oa, w_ob, w_oc, w_out, norm_xq, norm_mem, w_mq, w_mkv, mq_norm, mk_norm, w_mo, norm_ffn, w_up, ffn_conv, ffn_conv_b, w_down, loss_target, m_norm_mix, m_w_in, m_fox_fbias, m_fox_qnorm, m_fox_knorm, m_gdn_conv, m_gdn_a_log, m_gdn_dt_bias, m_gdn_onorm, m_gate_bias, m_w_oa, m_w_ob, m_w_oc, m_w_out, m_norm_xq, m_norm_mem, m_w_mq, m_w_mkv, m_mq_norm, m_mk_norm, m_w_mo, m_norm_ffn, m_w_up, m_ffn_conv, m_ffn_conv_b, m_w_down, v_norm_mix, v_w_in, v_fox_fbias, v_fox_qnorm, v_fox_knorm, v_gdn_conv, v_gdn_a_log, v_gdn_dt_bias, v_gdn_onorm, v_gate_bias, v_w_oa, v_w_ob, v_w_oc, v_w_out, v_norm_xq, v_norm_mem, v_w_mq, v_w_mkv, v_mq_norm, v_mk_norm, v_w_mo, v_norm_ffn, v_w_up, v_ffn_conv, v_ffn_conv_b, v_w_down):
    given = dict(locals())
    wts = {n: given[n] for n in WEIGHTS}
    mom = {n: given["m_" + n] for n in WEIGHTS}
    var = {n: given["v_" + n] for n in WEIGHTS}

    layers, layers16 = [], []
    for l in range(DEPTH):
        full = _full_weights(_all_gather("gather_weights", [_for_transport(n, wts[n][l]) for n in SHARDED_ORDER]))
        layers16.append(full)
        layers.append({**_in_f32(full), **{n: wts[n][l] for n in SMALL_ORDER}})

    y, vjp = jax.vjp(lambda xx, ww: _forward(xx, mem[0], ww, layers16), x[0], layers)
    loss_part, dy = _loss_call(y, loss_target[0])
    dx, dlayers = vjp(dy)
    loss = lax.psum(loss_part, ("x", "y", "c"))

    out = {}
    started = [_exchange_start(f"exchange_start_{l}", _grad_parts(dlayers[l])) for l in range(DEPTH)]
    me = 4 * lax.axis_index("x") + 2 * lax.axis_index("y") + lax.axis_index("c")
    per_layer = []
    for l in range(DEPTH):
        parts, lands = _exchange_wait(f"exchange_wait_{l}", *started[l], dx)
        slots = [jnp.concatenate([lax.dynamic_index_in_dim(p, me, 0), ld]) for p, ld in zip(parts, lands)]
        per_layer.append({n: _adam_call("adam_shard", wts[n][l], sl, mom[n][l], var[n][l])
                          for n, sl in zip(SHARDED_ORDER, slots)})
    for n in SHARDED_ORDER:
        for k, kind in enumerate(("grad_", "delta_", "new_m_", "new_v_")):
            out[kind + n] = jnp.stack([per_layer[l][n][k] for l in range(DEPTH)])
    dsmall = {n: jnp.stack([dlayers[l][n] for l in range(DEPTH)]) for n in SMALL_ORDER}
    slots = _all_gather("gather_small_grads", [_pack_small(dsmall)])[0]
    res = _adam_call("adam_small", _pack_small(wts), slots, _pack_small(mom), _pack_small(var))
    for k, kind in enumerate(("grad_", "delta_", "new_m_", "new_v_")):
        un = _unpack_small(res[k])
        for n in SMALL_ORDER:
            out[kind + n] = un[n].reshape(wts[n].shape)

    return (loss, dx[None], *[out["grad_" + n] for n in WEIGHTS], *[out["delta_" + n] for n in WEIGHTS],
            *[out["new_m_" + n] for n in WEIGHTS], *[out["new_v_" + n] for n in WEIGHTS])
```

```python
import jax
import jax.numpy as jnp
from jax import lax
from jax.experimental import pallas as pl
from jax.experimental.pallas import tpu as pltpu

F32 = jnp.float32
BF16 = jnp.bfloat16

N_DEV = 8
D_MODEL = 1024
DEPTH = 4
CHUNK = 64
EPS = 1e-6
HEADS = 4
HEAD_DIM = 128
HW = HEADS * HEAD_DIM
D_FF = 2816
N_IN = 8204
LANES = 128
SUBLANES = 8
VMEM_LIMIT = 56 * 1024 * 1024

ADAM_LR = 0.001
ADAM_B1 = 0.9
ADAM_B2 = 0.999
ADAM_EPS = 1e-08
ADAM_WD = 0.01
ADAM_STEP = 10

NEG = -1e30
MESH = pl.DeviceIdType.MESH

WEIGHTS = ['norm_mix', 'w_in', 'fox_fbias', 'fox_qnorm', 'fox_knorm', 'gdn_conv', 'gdn_a_log', 'gdn_dt_bias',
           'gdn_onorm', 'gate_bias', 'w_oa', 'w_ob', 'w_oc', 'w_out', 'norm_xq', 'norm_mem', 'w_mq', 'w_mkv',
           'mq_norm', 'mk_norm', 'w_mo', 'norm_ffn', 'w_up', 'ffn_conv', 'ffn_conv_b', 'w_down']
SHARDED = {
    'w_in': ((D_MODEL, N_IN), 0), 'gdn_conv': ((4, 3 * HW), 1), 'w_oa': ((HW, D_MODEL), 1),
    'w_ob': ((HW, D_MODEL), 1), 'w_oc': ((HW, D_MODEL), 1), 'w_out': ((D_MODEL, D_MODEL), 0),
    'w_mq': ((D_MODEL, HW), 0), 'w_mkv': ((D_MODEL, 2 * HW), 0), 'w_mo': ((HW, D_MODEL), 1),
    'w_up': ((D_MODEL, 2 * D_FF), 1), 'ffn_conv': ((3, 2 * D_FF), 1), 'w_down': ((D_FF, D_MODEL), 0),
}
SHARDED_ORDER = [n for n in WEIGHTS if n in SHARDED]
SMALL_ORDER = [n for n in WEIGHTS if n not in SHARDED]
SMALL_WIDTH = {'norm_mix': D_MODEL, 'fox_fbias': HEADS, 'fox_qnorm': HEAD_DIM, 'fox_knorm': HEAD_DIM,
               'gdn_a_log': HEADS, 'gdn_dt_bias': HEADS, 'gdn_onorm': HEAD_DIM, 'gate_bias': 3 * D_MODEL,
               'norm_xq': D_MODEL, 'norm_mem': D_MODEL, 'mq_norm': HEAD_DIM, 'mk_norm': HEAD_DIM,
               'norm_ffn': D_MODEL, 'ffn_conv_b': 2 * D_FF}
PACK_COLS = 1024


def _round_up(n, m):
    return (n + m - 1) // m * m


SMALL_ROWS = _round_up(DEPTH * sum(SMALL_WIDTH.values()), SUBLANES * PACK_COLS) // PACK_COLS

_IN_SRC = [(0, 512), (512, 1024), (1024, 1536),
           (1540, 2052), (2052, 2564), (2564, 3076),
           (3084, 3596),
           (3596, 4108), (4108, 4620), (4620, 5132),
           (5132, 8204),
           (1536, 1540), (3076, 3080), (3080, 3084)]
N_IN_PAD = 8320
LANE_FF, LANE_GB, LANE_GA = 0, 4, 8


def _pick(dim, pref, unit):
    best = None
    t = unit
    while t <= min(dim, pref):
        if dim % t == 0:
            best = t
        t += unit
    return dim if best is None else best


def _params(n_grid):
    return pltpu.CompilerParams(dimension_semantics=("arbitrary",) * n_grid, vmem_limit_bytes=VMEM_LIMIT)


def _pcall(body, *, name, out_shape, grid, in_specs, out_specs, scratch_shapes=()):
    return pl.pallas_call(body, name=name, out_shape=out_shape, grid=grid, in_specs=in_specs, out_specs=out_specs,
                          scratch_shapes=scratch_shapes, compiler_params=_params(len(grid)),
                          interpret=False)


NN = ((1,), (0,))
NT = ((1,), (1,))
TN = ((0,), (0,))


def _dot(a, b, dn):
    return lax.dot_general(a.astype(BF16), b.astype(BF16), (dn, ((), ())), preferred_element_type=F32)


def _dotf(a, b, dn):
    return lax.dot_general(a, b, (dn, ((), ())), precision=lax.Precision.HIGHEST, preferred_element_type=F32)


@jax.custom_vjp
def mm(a, b):
    return _dot(a, b, NN)


mm.defvjp(lambda a, b: (_dot(a, b, NN), (a, b)), lambda r, g: (_dot(g, r[1], NT), _dot(r[0], g, TN)))


@jax.custom_vjp
def mm_nt(a, b):
    return _dot(a, b, NT)


mm_nt.defvjp(lambda a, b: (_dot(a, b, NT), (a, b)), lambda r, g: (_dot(g, r[1], NN), _dot(g, r[0], TN)))


@jax.custom_vjp
def mm_tn(a, b):
    return _dot(a, b, TN)


mm_tn.defvjp(lambda a, b: (_dot(a, b, TN), (a, b)), lambda r, g: (_dot(r[1], g, NT), _dot(r[0], g, NN)))


@jax.custom_vjp
def tri_apply(t, x):
    return _dotf(t, x, NN)


tri_apply.defvjp(lambda t, x: (_dotf(t, x, NN), t), lambda t, g: (jnp.zeros_like(t), _dotf(t, g, TN)))


def _sigmoid(x):
    return 1.0 / (1.0 + jnp.exp(-x))


@jax.custom_vjp
def _softplus(x):
    return jnp.maximum(x, 0.0) + jnp.log(1.0 + jnp.exp(-jnp.abs(x)))


_softplus.defvjp(lambda x: (_softplus(x), x), lambda x, g: (g * _sigmoid(x),))


@jax.custom_vjp
def _silu(x):
    return x * _sigmoid(x)


def _silu_fwd(x):
    s = _sigmoid(x)
    return x * s, (x, s)


_silu.defvjp(_silu_fwd, lambda r, g: (g * r[1] * (1.0 + r[0] * (1.0 - r[1])),))


def _rms(x, g):
    return x * lax.rsqrt(jnp.mean(x * x, axis=-1, keepdims=True) + EPS) * g


def _iota2(n, m, axis):
    return lax.broadcasted_iota(jnp.int32, (n, m), axis)


def _whole(width):
    return [(0, width)]


def _split(width, n):
    w = width // n
    return [(k * w, w) for k in range(n)]


class RowOp:
    def __init__(self, name, f, in_pieces, out_pieces, tm=256):
        self.name, self.f, self.in_pieces, self.out_pieces, self.tm = name, f, in_pieces, out_pieces, tm
        op = jax.custom_vjp(self._fwd_call)
        op.defvjp(lambda *a: (self._fwd_call(*a), a), lambda res, g: self._bwd_call(res, g))
        self.op = op

    def __call__(self, *args):
        return self.op(*args)

    def _width(self, pieces):
        return max(o + w for o, w in pieces)

    def _row_specs(self, pieces_list, tm):
        return [pl.BlockSpec((tm, self._width(p)), lambda i: (i, 0)) for p in pieces_list]

    def _fwd_call(self, *args):
        nr = len(self.in_pieces)
        rows, params = args[:nr], args[nr:]
        m = rows[0].shape[0]
        tm = _pick(m, self.tm, SUBLANES)
        f, in_pieces, out_pieces = self.f, self.in_pieces, self.out_pieces
        no = len(out_pieces)

        def body(*refs):
            rin, pr, ro = refs[:nr], refs[nr:nr + len(params)], refs[nr + len(params):]
            xs = [r[:, o:o + w] for r, ps in zip(rin, in_pieces) for (o, w) in ps]
            ys = f(*xs, *[p[...] for p in pr])
            k = 0
            for r, ps in zip(ro, out_pieces):
                for (o, w) in ps:
                    r[:, o:o + w] = ys[k]
                    k += 1

        outs = _pcall(
            body, name=self.name + "_fwd",
            out_shape=[jax.ShapeDtypeStruct((m, self._width(p)), F32) for p in out_pieces],
            grid=(m // tm,),
            in_specs=self._row_specs(in_pieces, tm) + [pl.BlockSpec(p.shape, lambda i: (0, 0)) for p in params],
            out_specs=self._row_specs(out_pieces, tm),
        )(*rows, *params)
        return tuple(outs) if no > 1 else outs[0]

    def _bwd_call(self, res, g):
        nr = len(self.in_pieces)
        rows, params = res[:nr], res[nr:]
        no = len(self.out_pieces)
        gs = tuple(g) if no > 1 else (g,)
        m = rows[0].shape[0]
        tm = _pick(m, self.tm, SUBLANES)
        f, in_pieces, out_pieces = self.f, self.in_pieces, self.out_pieces
        npar = len(params)

        def body(*refs):
            rin, pr, dro = refs[:nr], refs[nr:nr + npar], refs[nr + npar:nr + npar + no]
            drin, dpr = refs[nr + npar + no:nr + npar + no + nr], refs[nr + npar + no + nr:]
            xs = [r[:, o:o + w] for r, ps in zip(rin, in_pieces) for (o, w) in ps]
            dys = [r[:, o:o + w] for r, ps in zip(dro, out_pieces) for (o, w) in ps]
            _, vjp = jax.vjp(lambda *a: tuple(f(*a)), *xs, *[p[...] for p in pr])
            grads = vjp(tuple(dys))
            k = 0
            for r, ps in zip(drin, in_pieces):
                for (o, w) in ps:
                    r[:, o:o + w] = grads[k]
                    k += 1

            @pl.when(pl.program_id(0) == 0)
            def _():
                for r in dpr:
                    r[...] = jnp.zeros_like(r)

            for j, r in enumerate(dpr):
                r[...] += grads[k + j]

        outs = _pcall(
            body, name=self.name + "_bwd",
            out_shape=[jax.ShapeDtypeStruct(r.shape, F32) for r in rows]
            + [jax.ShapeDtypeStruct(p.shape, F32) for p in params],
            grid=(m // tm,),
            in_specs=self._row_specs(in_pieces, tm) + [pl.BlockSpec(p.shape, lambda i: (0, 0)) for p in params]
            + self._row_specs(out_pieces, tm),
            out_specs=self._row_specs(in_pieces, tm) + [pl.BlockSpec(p.shape, lambda i: (0, 0)) for p in params],
        )(*rows, *params, *gs)
        return tuple(outs)


def _f_rms(x, g):
    return (_rms(x, g),)


def _f_headnorm(x0, x1, x2, x3, g):
    return tuple(_rms(x, g) for x in (x0, x1, x2, x3))


def _f_small(sm, fb, al, db):
    tm = sm.shape[0]
    logf = -_softplus(-(sm + fb))
    beta = _sigmoid(sm)
    glog = -jnp.exp(al) * _softplus(sm + db)
    r, c = _iota2(tm, tm, 0), _iota2(tm, tm, 1)
    bd = jnp.where((r >= c) & (jnp.bitwise_xor(r, c) < CHUNK), 1.0, 0.0).astype(F32)
    return logf, beta, tri_apply(bd, glog)


def _f_gdnpost(o0, o1, o2, o3, z0, z1, z2, z3, g):
    return tuple(_rms(o, g) * _silu(z) for o, z in zip((o0, o1, o2, o3), (z0, z1, z2, z3)))


def _f_merge(t0, t1, t2, a, b, c, b0, b1, b2):
    return (_sigmoid(t0 + b0) * a + _sigmoid(t1 + b1) * b + _sigmoid(t2 + b2) * c,)


def _make_rowops():
    return dict(
        rms=RowOp("rms", _f_rms, [_whole(D_MODEL)], [_whole(D_MODEL)], tm=512),
        headnorm=RowOp("headnorm", _f_headnorm, [_split(HW, HEADS)], [_split(HW, HEADS)], tm=1024),
        small=RowOp("smallprep", _f_small, [_whole(LANES)], [_whole(LANES)] * 3),
        gdnpost=RowOp("gdnpost", _f_gdnpost, [_split(HW, HEADS)] * 2, [_split(HW, HEADS)], tm=512),
        merge=RowOp("merge", _f_merge, [_split(3 * D_MODEL, 3)] + [_whole(D_MODEL)] * 3, [_whole(D_MODEL)]),
    )


ROUND_ONCE_READS = 3


def _b16(x):
    return x.astype(BF16)


def _mm_call(name, a, b, mode, c=None):
    if mode == "nn":
        (m, kc), n = a.shape, b.shape[1]
    elif mode == "nt":
        (m, kc), n = a.shape, b.shape[0]
    else:
        (kc, m), n = a.shape, b.shape[1]
    tm = _pick(m, 1408, LANES) if mode == "tn" else _pick(m, 1024, SUBLANES)
    tn = _pick(n, 1408, LANES)
    tk = _pick(kc, 1024, SUBLANES) if mode == "tn" else _pick(kc, 1536, LANES)
    if a.dtype == F32 and n // tn >= ROUND_ONCE_READS:
        a = _b16(a)
    if b.dtype == F32 and m // tm >= ROUND_ONCE_READS:
        b = _b16(b)
    dn = {"nn": NN, "nt": NT, "tn": TN}[mode]
    a_spec = {"nn": pl.BlockSpec((tm, tk), lambda i, j, k: (i, k)),
              "nt": pl.BlockSpec((tm, tk), lambda i, j, k: (i, k)),
              "tn": pl.BlockSpec((tk, tm), lambda i, j, k: (k, i))}[mode]
    b_spec = {"nn": pl.BlockSpec((tk, tn), lambda i, j, k: (k, j)),
              "nt": pl.BlockSpec((tn, tk), lambda i, j, k: (j, k)),
              "tn": pl.BlockSpec((tk, tn), lambda i, j, k: (k, j))}[mode]
    o_spec = pl.BlockSpec((tm, tn), lambda i, j, k: (i, j))
    has_c = c is not None

    def body(*refs):
        a_ref, b_ref = refs[0], refs[1]
        o_ref = refs[-1]

        @pl.when(pl.program_id(2) == 0)
        def _():
            o_ref[...] = refs[2][...] if has_c else jnp.zeros_like(o_ref)

        o_ref[...] += _dot(a_ref[...], b_ref[...], dn)

    return _pcall(body, name=name, out_shape=jax.ShapeDtypeStruct((m, n), F32), grid=(m // tm, n // tn, kc // tk),
                  in_specs=[a_spec, b_spec] + ([o_spec] if has_c else []), out_specs=o_spec,
                  )(*((a, b, c) if has_c else (a, b)))


@jax.custom_vjp
def matmul(a, w, w16):
    return _mm_call("mm_nn", a, w16, "nn")


def _matmul_bwd(res, g):
    a, w16 = res
    return _mm_call("mm_nt", g, w16, "nt"), _mm_call("mm_tn", a, g, "tn"), jnp.zeros_like(w16)


matmul.defvjp(lambda a, w, w16: (_mm_call("mm_nn", a, w16, "nn"), (a, w16)), _matmul_bwd)


@jax.custom_vjp
def matmul_add(c, a, w, w16):
    return _mm_call("mm_nn_add", a, w16, "nn", c)


matmul_add.defvjp(lambda c, a, w, w16: (_mm_call("mm_nn_add", a, w16, "nn", c), (a, w16)),
                  lambda res, g: (g,) + _matmul_bwd(res, g))

_PROJ_GROUPS = [(0, 512), (512, 512), (1024, 512), (1536, 1536), (3072, 512), (3584, 512), (4096, 512), (4608, 512),
                (5120, 3072), (8192, 128)]


def _proj_fwd(h, w, w16):
    h16 = _b16(h)
    return tuple(_mm_call("proj_nn", h16, w16[:, s:s + n], "nn") for s, n in _PROJ_GROUPS), (h16, w16)


proj = jax.custom_vjp(lambda h, w, w16: _proj_fwd(h, w, w16)[0])


def _proj_bwd(res, gs):
    h16, w16 = res
    dh = None
    dws = []
    for (s, n), g in zip(_PROJ_GROUPS, gs):
        dh = _mm_call("proj_nt", g, w16[:, s:s + n], "nt", dh)
        dws.append(_mm_call("proj_tn", h16, g, "tn"))
    return dh, jnp.concatenate(dws, axis=1), jnp.zeros_like(w16)


proj.defvjp(_proj_fwd, _proj_bwd)


def _cumsum_call(x, reverse):
    s, w = x.shape
    tm = _pick(s, 256, SUBLANES)
    nb = s // tm

    def body(x_ref, o_ref, carry):
        @pl.when(pl.program_id(0) == 0)
        def _():
            carry[...] = jnp.zeros_like(carry)

        blk = x_ref[...]
        r, c = _iota2(tm, tm, 0), _iota2(tm, tm, 1)
        tri = jnp.where((r <= c) if reverse else (r >= c), 1.0, 0.0).astype(F32)
        o_ref[...] = _dotf(tri, blk, NN) + carry[...]
        carry[...] += jnp.sum(blk, axis=0, keepdims=True)

    idx = (lambda i: (nb - 1 - i, 0)) if reverse else (lambda i: (i, 0))
    return _pcall(body, name="cumsum_rev" if reverse else "cumsum", out_shape=jax.ShapeDtypeStruct((s, w), F32),
                  grid=(nb,), in_specs=[pl.BlockSpec((tm, w), idx)], out_specs=pl.BlockSpec((tm, w), idx),
                  scratch_shapes=[pltpu.VMEM((1, w), F32)])(x)


@jax.custom_vjp
def seq_cumsum(x):
    return _cumsum_call(x, False)


seq_cumsum.defvjp(lambda x: (_cumsum_call(x, False), None), lambda _, g: (_cumsum_call(g, True),))


HALO = SUBLANES


class ConvOp:
    def __init__(self, name, width, post, c_pieces, out_widths, has_bias, tm):
        self.name, self.width, self.post, self.c_pieces = name, width, post, c_pieces
        self.out_widths, self.has_bias, self.tm = out_widths, has_bias, tm
        op = jax.custom_vjp(self._fwd_call)
        op.defvjp(lambda *a: (self._fwd_call(*a), a), lambda res, g: self._bwd_call(res, g))
        self.op = op

    def __call__(self, *args):
        return self.op(*args)

    def _conv(self, i, x_ref, prev_ref, w_ref, b_ref, buf):
        tm = x_ref.shape[0]
        buf[0:HALO, :] = jnp.where(i > 0, prev_ref[...], 0.0)
        buf[HALO:HALO + tm, :] = x_ref[...]
        taps = [buf[pl.ds(HALO - (self.width - 1) + j, tm), :] for j in range(self.width)]
        c = taps[0] * w_ref[0:1, :]
        for j in range(1, self.width):
            c = c + taps[j] * w_ref[j:j + 1, :]
        if self.has_bias:
            c = c + b_ref[...]
        return c, taps

    def _fwd_call(self, x, w, *bias):
        s, ch = x.shape
        tm = _pick(s, self.tm, SUBLANES)
        r8 = tm // HALO
        has_bias, post, c_pieces = self.has_bias, self.post, self.c_pieces

        def body(*refs):
            x_ref, prev_ref, w_ref = refs[:3]
            b_ref = refs[3] if has_bias else None
            outs, buf = refs[3 + has_bias:-1], refs[-1]
            c, _ = self._conv(pl.program_id(0), x_ref, prev_ref, w_ref, b_ref, buf)
            ys = post(*[c[:, o:o + n] for o, n in c_pieces])
            for r, y in zip(outs, ys):
                r[...] = y

        outs = _pcall(
            body, name=self.name + "_fwd", out_shape=[jax.ShapeDtypeStruct((s, n), F32) for n in self.out_widths],
            grid=(s // tm,),
            in_specs=[pl.BlockSpec((tm, ch), lambda i: (i, 0)),
                      pl.BlockSpec((HALO, ch), lambda i: (jnp.maximum(i * r8 - 1, 0), 0)),
                      pl.BlockSpec(w.shape, lambda i: (0, 0))]
            + ([pl.BlockSpec((1, ch), lambda i: (0, 0))] if has_bias else []),
            out_specs=[pl.BlockSpec((tm, n), lambda i: (i, 0)) for n in self.out_widths],
            scratch_shapes=[pltpu.VMEM((tm + HALO, ch), F32)],
        )(x, x, w, *bias)
        return tuple(outs) if len(outs) > 1 else outs[0]

    def _bwd_call(self, res, g):
        x, w = res[0], res[1]
        bias = res[2:]
        gs = tuple(g) if len(self.out_widths) > 1 else (g,)
        s, ch = x.shape
        tm = _pick(s, self.tm, SUBLANES)
        r8 = tm // HALO
        nb = s // tm
        has_bias, post, c_pieces, width = self.has_bias, self.post, self.c_pieces, self.width
        ng = len(gs)

        def body1(*refs):
            x_ref, prev_ref, w_ref = refs[:3]
            b_ref = refs[3] if has_bias else None
            k = 3 + has_bias
            g_refs = refs[k:k + ng]
            dc_ref, dw_ref = refs[k + ng], refs[k + ng + 1]
            db_ref = refs[k + ng + 2] if has_bias else None
            buf = refs[-1]
            i = pl.program_id(0)
            c, taps = self._conv(i, x_ref, prev_ref, w_ref, b_ref, buf)
            _, vjp = jax.vjp(lambda *a: tuple(post(*a)), *[c[:, o:o + n] for o, n in c_pieces])
            dcs = vjp(tuple(r[...] for r in g_refs))
            for (o, n), d in zip(c_pieces, dcs):
                dc_ref[:, o:o + n] = d

            @pl.when(i == 0)
            def _():
                dw_ref[...] = jnp.zeros_like(dw_ref)
                if has_bias:
                    db_ref[...] = jnp.zeros_like(db_ref)

            dc = dc_ref[...]
            for j in range(width):
                dw_ref[j:j + 1, :] += jnp.sum(dc * taps[j], axis=0, keepdims=True)
            if has_bias:
                db_ref[...] += jnp.sum(dc, axis=0, keepdims=True)

        outs1 = _pcall(
            body1, name=self.name + "_bwd_act",
            out_shape=[jax.ShapeDtypeStruct((s, ch), F32), jax.ShapeDtypeStruct(w.shape, F32)]
            + ([jax.ShapeDtypeStruct((1, ch), F32)] if has_bias else []),
            grid=(nb,),
            in_specs=[pl.BlockSpec((tm, ch), lambda i: (i, 0)),
                      pl.BlockSpec((HALO, ch), lambda i: (jnp.maximum(i * r8 - 1, 0), 0)),
                      pl.BlockSpec(w.shape, lambda i: (0, 0))]
            + ([pl.BlockSpec((1, ch), lambda i: (0, 0))] if has_bias else [])
            + [pl.BlockSpec((tm, n), lambda i: (i, 0)) for n in self.out_widths],
            out_specs=[pl.BlockSpec((tm, ch), lambda i: (i, 0)), pl.BlockSpec(w.shape, lambda i: (0, 0))]
            + ([pl.BlockSpec((1, ch), lambda i: (0, 0))] if has_bias else []),
            scratch_shapes=[pltpu.VMEM((tm + HALO, ch), F32)],
        )(x, x, w, *bias, *gs)
        dc, dw = outs1[0], outs1[1]

        def body2(dc_ref, next_ref, w_ref, dx_ref, buf):
            i = pl.program_id(0)
            buf[0:tm, :] = dc_ref[...]
            buf[tm:tm + HALO, :] = jnp.where(i < nb - 1, next_ref[...], 0.0)
            dx = buf[pl.ds(width - 1, tm), :] * w_ref[0:1, :]
            for j in range(1, width):
                dx = dx + buf[pl.ds(width - 1 - j, tm), :] * w_ref[j:j + 1, :]
            dx_ref[...] = dx

        dx = _pcall(
            body2, name=self.name + "_bwd_in", out_shape=jax.ShapeDtypeStruct((s, ch), F32), grid=(nb,),
            in_specs=[pl.BlockSpec((tm, ch), lambda i: (i, 0)),
                      pl.BlockSpec((HALO, ch), lambda i: (jnp.minimum((i + 1) * r8, s // HALO - 1), 0)),
                      pl.BlockSpec(w.shape, lambda i: (0, 0))],
            out_specs=pl.BlockSpec((tm, ch), lambda i: (i, 0)),
            scratch_shapes=[pltpu.VMEM((tm + HALO, ch), F32)],
        )(dc, dc, w)
        return (dx, dw) + ((outs1[2],) if has_bias else ())


def _make_convops():
    return dict(
        gdn=ConvOp("gdnconv", 4, lambda q, k, v: (_silu(q), _silu(k), _silu(v)), _split(3 * HW, 3), [HW] * 3,
                   False, 256),
        ffn=ConvOp("ffnconv", 3, lambda a, b: (_silu(a) * b,), _split(2 * D_FF, 2), [D_FF], True, 256),
    )


ATT_Q = 512
ATT_K = 256
ATT_K_FOX = 512
SCALE = HEAD_DIM ** -0.5


ATT_Q_SB = 1024


def _att_tiles(s, fox=False):
    tk = _pick(s, ATT_K_FOX if fox else ATT_K, LANES)
    tq = _pick(s, ATT_Q if fox else ATT_Q_SB, tk)
    return tq, tk


def _att_specs(s, tq, tk):
    qspec = pl.BlockSpec((tq, HEAD_DIM), lambda h, i: (i, h))
    kspec = pl.BlockSpec((s, HEAD_DIM), lambda h, i: (0, h))
    colspec = pl.BlockSpec((None, tq, 1), lambda h, i: (h, i, 0))
    rowspec = pl.BlockSpec((None, s // tk, 1, tk), lambda h, i: (h, 0, 0, 0))
    return qspec, kspec, colspec, rowspec


def _krows(kb, tk):
    return pl.ds(pl.multiple_of(kb * tk, tk), tk)


def _stage_bf16(i, pairs):
    @pl.when(i == 0)
    def _():
        for src, dst in pairs:
            dst[...] = src[...].astype(BF16)


ATT_STRIP = 32


def _strips(tq):
    return [slice(r, r + ATT_STRIP) for r in range(0, tq, ATT_STRIP)]


def _visible(i, kb, rs, tq, tk, strict):
    rows = i * tq + rs.start + _iota2(ATT_STRIP, tk, 0)
    cols = kb * tk + _iota2(ATT_STRIP, tk, 1)
    return (cols < rows) if strict else (cols <= rows)


def _visible_block(i, kb, tq, tk):
    return kb * tk + _iota2(tq, tk, 1) <= i * tq + _iota2(tq, tk, 0)


def _blocks(i, ratio, blk, reverse=False):
    def full(n, carry):
        blk(i * ratio - 1 - n if reverse else n, False)
        return carry

    if reverse:
        for j in reversed(range(ratio)):
            blk(i * ratio + j, True)
    lax.fori_loop(0, i * ratio, full, 0)
    if not reverse:
        for j in range(ratio):
            blk(i * ratio + j, True)


def _vm(shape, dtype):
    return pltpu.VMEM(shape, dtype)


def _fox_fwd_call(q, k, v, ccol, crow):
    s = q.shape[0]
    tq, tk = _att_tiles(s, fox=True)
    qspec, kspec, colspec, rowspec = _att_specs(s, tq, tk)

    def body(q_ref, k_ref, v_ref, cq_ref, ck_ref, o_ref, lse_ref, k16, v16):
        i = pl.program_id(1)
        ratio = tq // tk
        _stage_bf16(i, [(k_ref, k16), (v_ref, v16)])
        qb = q_ref[...].astype(BF16)
        cq = cq_ref[...]

        def blk(kb, carry, masked):
            m, l, acc = carry
            sc = _dot(qb, k16[_krows(kb, tk), :], NT) * SCALE + (cq - ck_ref[kb])
            if masked:
                sc = jnp.where(_visible_block(i, kb, tq, tk), sc, NEG)
            m_new = jnp.maximum(m, jnp.max(sc, axis=-1, keepdims=True))
            alpha = jnp.exp(m - m_new)
            p = jnp.exp(sc - m_new)
            return (m_new, alpha * l + jnp.sum(p, axis=-1, keepdims=True),
                    alpha * acc + _dot(p, v16[_krows(kb, tk), :], NN))

        carry = (jnp.full((tq, 1), NEG, F32), jnp.zeros((tq, 1), F32), jnp.zeros((tq, HEAD_DIM), F32))
        carry = lax.fori_loop(0, i * ratio, lambda kb, c: blk(kb, c, False), carry)
        for j in range(ratio):
            carry = blk(i * ratio + j, carry, True)
        m, l, acc = carry
        o_ref[...] = acc / l
        lse_ref[...] = m + jnp.log(l)

    return _pcall(body, name="fox_fwd",
                  out_shape=[jax.ShapeDtypeStruct((s, HW), F32), jax.ShapeDtypeStruct((HEADS, s, 1), F32)],
                  grid=(HEADS, s // tq), in_specs=[qspec, kspec, kspec, colspec, rowspec],
                  out_specs=[qspec, colspec],
                  scratch_shapes=[_vm((s, HEAD_DIM), BF16), _vm((s, HEAD_DIM), BF16)])(q, k, v, ccol, crow)


def _fox_bwd_call(q, k, v, ccol, crow, o, lse, do):
    s = q.shape[0]
    tq, tk = _att_tiles(s, fox=True)
    ratio = tq // tk
    qspec, kspec, colspec, rowspec = _att_specs(s, tq, tk)

    def body(q_ref, k_ref, v_ref, cq_ref, ck_ref, o_ref, lse_ref, do_ref, dq_ref, dk_ref, dv_ref, dcq_ref, dck_ref,
             k16, v16):
        i = pl.program_id(1)
        _stage_bf16(i, [(k_ref, k16), (v_ref, v16)])

        @pl.when(i == 0)
        def _():
            dk_ref[...] = jnp.zeros_like(dk_ref)
            dv_ref[...] = jnp.zeros_like(dv_ref)
            dck_ref[...] = jnp.zeros_like(dck_ref)

        qb = q_ref[...].astype(BF16)
        dob = do_ref[...].astype(BF16)
        cq, lse = cq_ref[...], lse_ref[...]
        dl = jnp.sum(do_ref[...] * o_ref[...], axis=-1, keepdims=True)

        def blk(kb, carry, masked):
            dq, dcq = carry
            rows = _krows(kb, tk)
            kk, vv = k16[rows, :], v16[rows, :]
            sc = _dot(qb, kk, NT) * SCALE + (cq - ck_ref[kb])
            p = jnp.exp(sc - lse)
            if masked:
                p = jnp.where(_visible_block(i, kb, tq, tk), p, 0.0)
            dv_ref[rows, :] += _dot(p, dob, TN)
            ds = p * (_dot(dob, vv, NT) - dl)
            dk_ref[rows, :] += _dot(ds, qb, TN) * SCALE
            dck_ref[kb] += -jnp.sum(ds, axis=0, keepdims=True)
            return dq + _dot(ds, kk, NN) * SCALE, dcq + jnp.sum(ds, axis=-1, keepdims=True)

        carry = (jnp.zeros((tq, HEAD_DIM), F32), jnp.zeros((tq, 1), F32))
        carry = lax.fori_loop(0, i * ratio, lambda kb, c: blk(kb, c, False), carry)
        for j in range(ratio):
            carry = blk(i * ratio + j, carry, True)
        dq_ref[...] = carry[0]
        dcq_ref[...] = carry[1]

    return _pcall(body, name="fox_bwd",
                  out_shape=[jax.ShapeDtypeStruct((s, HW), F32)] * 3
                  + [jax.ShapeDtypeStruct((HEADS, s, 1), F32), jax.ShapeDtypeStruct((HEADS, s // tk, 1, tk), F32)],
                  grid=(HEADS, s // tq),
                  in_specs=[qspec, kspec, kspec, colspec, rowspec, qspec, colspec, qspec],
                  out_specs=[qspec, kspec, kspec, colspec, rowspec],
                  scratch_shapes=[_vm((s, HEAD_DIM), BF16), _vm((s, HEAD_DIM), BF16)],
                  )(q, k, v, ccol, crow, o, lse, do)


@jax.custom_vjp
def fox_attention(q, k, v, ccol, crow):
    return _fox_fwd_call(q, k, v, ccol, crow)[0]


def _fox_vjp_fwd(q, k, v, ccol, crow):
    o, lse = _fox_fwd_call(q, k, v, ccol, crow)
    return o, (q, k, v, ccol, crow, o, lse)


fox_attention.defvjp(_fox_vjp_fwd, lambda res, g: tuple(_fox_bwd_call(*res, g)))


def _sb_fwd_call(q, k, v):
    s = q.shape[0]
    tq, tk = _att_tiles(s)
    ratio = tq // tk
    qspec, kspec, colspec, _ = _att_specs(s, tq, tk)

    def body(q_ref, k_ref, v_ref, o_ref, tot_ref, k16, v16, q16, s_scr, w_scr, lk16, a16, run, acc):
        i = pl.program_id(1)
        _stage_bf16(i, [(k_ref, k16), (v_ref, v16)])
        q16[...] = q_ref[...].astype(BF16)
        run[...] = jnp.zeros_like(run)
        acc[...] = jnp.zeros_like(acc)
        suffix = jnp.where(_iota2(tk, tk, 0) >= _iota2(tk, tk, 1), 1.0, 0.0).astype(BF16)

        def blk(kb, masked):
            rows = _krows(kb, tk)
            s_scr[...] = _dot(q16[...], k16[rows, :], NT)
            for rs in _strips(tq):
                lk = -_softplus(s_scr[rs, :] * SCALE)
                if masked:
                    lk = jnp.where(_visible(i, kb, rs, tq, tk, True), lk, 0.0)
                lk16[rs, :] = lk.astype(BF16)
            w_scr[...] = _dot(lk16[...], suffix, NN)
            for rs in _strips(tq):
                a = jnp.exp(s_scr[rs, :] * SCALE + w_scr[rs, :] + run[rs, :])
                if masked:
                    a = jnp.where(_visible(i, kb, rs, tq, tk, True), a, 0.0)
                a16[rs, :] = a.astype(BF16)
                run[rs, :] += w_scr[rs, 0:1]
            acc[...] += _dot(a16[...], v16[rows, :], NN)

        _blocks(i, ratio, blk, reverse=True)
        o_ref[...] = acc[...]
        tot_ref[...] = run[...]

    return _pcall(body, name="sb_fwd",
                  out_shape=[jax.ShapeDtypeStruct((s, HW), F32), jax.ShapeDtypeStruct((HEADS, s, 1), F32)],
                  grid=(HEADS, s // tq), in_specs=[qspec, kspec, kspec], out_specs=[qspec, colspec],
                  scratch_shapes=[_vm((s, HEAD_DIM), BF16), _vm((s, HEAD_DIM), BF16), _vm((tq, HEAD_DIM), BF16),
                                  _vm((tq, tk), F32), _vm((tq, tk), F32), _vm((tq, tk), BF16), _vm((tq, tk), BF16),
                                  _vm((tq, 1), F32), _vm((tq, HEAD_DIM), F32)])(q, k, v)


def _sb_bwd_call(q, k, v, tot, do):
    s = q.shape[0]
    tq, tk = _att_tiles(s)
    ratio = tq // tk
    qspec, kspec, colspec, _ = _att_specs(s, tq, tk)

    def body(q_ref, k_ref, v_ref, tot_ref, do_ref, dq_ref, dk_ref, dv_ref, k16, v16, q16, do16, s_scr, e_scr, w_scr,
             lz16, a16, e16, left, esum):
        i = pl.program_id(1)
        _stage_bf16(i, [(k_ref, k16), (v_ref, v16)])

        @pl.when(i == 0)
        def _():
            dk_ref[...] = jnp.zeros_like(dk_ref)
            dv_ref[...] = jnp.zeros_like(dv_ref)

        q16[...] = q_ref[...].astype(BF16)
        do16[...] = do_ref[...].astype(BF16)
        left[...] = jnp.zeros_like(left)
        esum[...] = jnp.zeros_like(esum)
        dq_ref[...] = jnp.zeros_like(dq_ref)
        prefix = jnp.where(_iota2(tk, tk, 0) <= _iota2(tk, tk, 1), 1.0, 0.0).astype(BF16)

        def blk(kb, masked):
            rows = _krows(kb, tk)
            s_scr[...] = _dot(q16[...], k16[rows, :], NT)
            e_scr[...] = _dot(do16[...], v16[rows, :], NT)
            for rs in _strips(tq):
                lk = -_softplus(s_scr[rs, :] * SCALE)
                if masked:
                    lk = jnp.where(_visible(i, kb, rs, tq, tk, True), lk, 0.0)
                lz16[rs, :] = lk.astype(BF16)
            w_scr[...] = _dot(lz16[...], prefix, NN)
            for rs in _strips(tq):
                rc = (tot_ref[rs, :] - left[rs, :]) - (w_scr[rs, :] - lz16[rs, :].astype(F32))
                a = jnp.exp(s_scr[rs, :] * SCALE + rc)
                if masked:
                    a = jnp.where(_visible(i, kb, rs, tq, tk, True), a, 0.0)
                e = a * e_scr[rs, :]
                a16[rs, :] = a.astype(BF16)
                e16[rs, :] = e.astype(BF16)
                e_scr[rs, :] = e
                left[rs, :] += w_scr[rs, tk - 1:tk]
            w_scr[...] = _dot(e16[...], prefix, NN)
            for rs in _strips(tq):
                dz = e_scr[rs, :] - _sigmoid(s_scr[rs, :] * SCALE) * (esum[rs, :] + w_scr[rs, :])
                if masked:
                    dz = jnp.where(_visible(i, kb, rs, tq, tk, True), dz, 0.0)
                lz16[rs, :] = dz.astype(BF16)
                esum[rs, :] += w_scr[rs, tk - 1:tk]
            dv_ref[rows, :] += _dot(a16[...], do16[...], TN)
            dk_ref[rows, :] += _dot(lz16[...], q16[...], TN) * SCALE
            dq_ref[...] += _dot(lz16[...], k16[rows, :], NN) * SCALE

        _blocks(i, ratio, blk)

    return _pcall(body, name="sb_bwd", out_shape=[jax.ShapeDtypeStruct((s, HW), F32)] * 3, grid=(HEADS, s // tq),
                  in_specs=[qspec, kspec, kspec, colspec, qspec], out_specs=[qspec, kspec, kspec],
                  scratch_shapes=[_vm((s, HEAD_DIM), BF16), _vm((s, HEAD_DIM), BF16), _vm((tq, HEAD_DIM), BF16),
                                  _vm((tq, HEAD_DIM), BF16), _vm((tq, tk), F32), _vm((tq, tk), F32),
                                  _vm((tq, tk), F32), _vm((tq, tk), BF16), _vm((tq, tk), BF16), _vm((tq, tk), BF16),
                                  _vm((tq, 1), F32), _vm((tq, 1), F32)])(q, k, v, tot, do)


@jax.custom_vjp
def sb_attention(q, k, v):
    return _sb_fwd_call(q, k, v)[0]


def _sb_vjp_fwd(q, k, v):
    o, tot = _sb_fwd_call(q, k, v)
    return o, (q, k, v, tot)


sb_attention.defvjp(_sb_vjp_fwd, lambda res, g: tuple(_sb_bwd_call(*res, g)))


def _mem_specs(s, nk, t):
    return (pl.BlockSpec((t, HEAD_DIM), lambda h, i: (i, h)), pl.BlockSpec((nk, HEAD_DIM), lambda h, i: (0, h)))


def _mem_probs(qb, kk):
    sc = _dot(qb, kk, NT) * SCALE
    p = jnp.exp(sc - jnp.max(sc, axis=-1, keepdims=True))
    return p / jnp.sum(p, axis=-1, keepdims=True)


def _mem_fwd_call(q, k, v):
    s, nk = q.shape[0], k.shape[0]
    t = _pick(s, 512, SUBLANES)
    qspec, kspec = _mem_specs(s, nk, t)

    def body(q_ref, k_ref, v_ref, o_ref):
        o_ref[...] = _dot(_mem_probs(q_ref[...].astype(BF16), k_ref[...]), v_ref[...], NN)

    return _pcall(body, name="mem_fwd", out_shape=jax.ShapeDtypeStruct((s, HW), F32), grid=(HEADS, s // t),
                  in_specs=[qspec, kspec, kspec], out_specs=qspec)(q, k, v)


def _mem_bwd_call(q, k, v, do):
    s, nk = q.shape[0], k.shape[0]
    t = _pick(s, 512, SUBLANES)
    qspec, kspec = _mem_specs(s, nk, t)

    def body(q_ref, k_ref, v_ref, do_ref, dq_ref, dk_ref, dv_ref):
        @pl.when(pl.program_id(1) == 0)
        def _():
            dk_ref[...] = jnp.zeros_like(dk_ref)
            dv_ref[...] = jnp.zeros_like(dv_ref)

        qb = q_ref[...].astype(BF16)
        dob = do_ref[...].astype(BF16)
        p = _mem_probs(qb, k_ref[...])
        dv_ref[...] += _dot(p, dob, TN)
        dp = _dot(dob, v_ref[...], NT)
        ds = p * (dp - jnp.sum(p * dp, axis=-1, keepdims=True))
        dq_ref[...] = _dot(ds, k_ref[...], NN) * SCALE
        dk_ref[...] += _dot(ds, qb, TN) * SCALE

    return _pcall(body, name="mem_bwd",
                  out_shape=[jax.ShapeDtypeStruct((s, HW), F32)] + [jax.ShapeDtypeStruct((nk, HW), F32)] * 2,
                  grid=(HEADS, s // t), in_specs=[qspec, kspec, kspec, qspec],
                  out_specs=[qspec, kspec, kspec])(q, k, v, do)


@jax.custom_vjp
def mem_attention(q, k, v):
    return _mem_fwd_call(q, k, v)


mem_attention.defvjp(lambda q, k, v: (_mem_fwd_call(q, k, v), (q, k, v)),
                     lambda res, g: tuple(_mem_bwd_call(*res, g)))


BNN = (((2,), (1,)), ((0,), (0,)))
BNT = (((2,), (2,)), ((0,), (0,)))
BTN = (((1,), (1,)), ((0,), (0,)))


def _bdot(a, b, dn):
    return lax.dot_general(a.astype(BF16), b.astype(BF16), dn, preferred_element_type=F32)


def _bdotf(a, b, dn):
    return lax.dot_general(a, b, dn, precision=lax.Precision.HIGHEST, preferred_element_type=F32)


@jax.custom_vjp
def bmm(a, b):
    return _bdot(a, b, BNN)


bmm.defvjp(lambda a, b: (_bdot(a, b, BNN), (a, b)), lambda r, g: (_bdot(g, r[1], BNT), _bdot(r[0], g, BTN)))


@jax.custom_vjp
def bmm_nt(a, b):
    return _bdot(a, b, BNT)


bmm_nt.defvjp(lambda a, b: (_bdot(a, b, BNT), (a, b)), lambda r, g: (_bdot(g, r[1], BNN), _bdot(g, r[0], BTN)))


@jax.custom_vjp
def bmm_tn(a, b):
    return _bdot(a, b, BTN)


bmm_tn.defvjp(lambda a, b: (_bdot(a, b, BTN), (a, b)), lambda r, g: (_bdot(r[1], g, BNT), _bdot(r[0], g, BNN)))


def _unit_lower_inverse(nm):
    eye = jnp.where(_iota2(CHUNK, CHUNK, 0) == _iota2(CHUNK, CHUNK, 1), 1.0, 0.0).astype(F32)[None]
    p = eye - nm
    m = nm
    for _ in range(5):
        m = _bdotf(m, m, BNN)
        p = _bdotf(p, eye + m, BNN)
    return p


@jax.custom_vjp
def _solve2(nm, r1, r2):
    inv = _unit_lower_inverse(nm)
    return _bdotf(inv, r1, BNN), _bdotf(inv, r2, BNN)


def _solve2_fwd(nm, r1, r2):
    inv = _unit_lower_inverse(nm)
    u, w = _bdotf(inv, r1, BNN), _bdotf(inv, r2, BNN)
    return (u, w), (inv, u, w)


def _solve2_bwd(res, g):
    inv, u, w = res
    d1, d2 = _bdotf(inv, g[0], BTN), _bdotf(inv, g[1], BTN)
    return -(_bdotf(d1, u, BNT) + _bdotf(d2, w, BNT)), d1, d2


_solve2.defvjp(_solve2_fwd, _solve2_bwd)


def _gdn_chunk(q, k, v, gcc, gcr, b, gl, st):
    qn = q * lax.rsqrt(jnp.sum(q * q, axis=-1, keepdims=True) + EPS) * SCALE
    kn = k * lax.rsqrt(jnp.sum(k * k, axis=-1, keepdims=True) + EPS)
    r, c = _iota2(CHUNK, CHUNK, 0)[None], _iota2(CHUNK, CHUNK, 1)[None]
    decay = jnp.exp(jnp.where(r >= c, gcc - gcr, NEG))
    nm = jnp.where(r > c, b * bmm_nt(kn, kn) * decay, 0.0)
    eg = jnp.exp(gcc)
    u, w = _solve2(nm, v * b, kn * (b * eg))
    attn = bmm_nt(qn, kn) * decay
    v_new = u - bmm(w, st)
    o = bmm(qn * eg, st) + bmm(attn, v_new)
    st_new = st * jnp.exp(gl) + bmm_tn(kn * jnp.exp(gl - gcc), v_new)
    return o, st_new


def _heads_of(ref, rows):
    return jnp.stack([ref[rows, _head_cols(h)] for h in range(HEADS)])


def _head_cols(h):
    return slice(h * HEAD_DIM, (h + 1) * HEAD_DIM)


GDN_ROWS = 512


def _gdn_specs(s, tg, rev):
    nb = s // tg
    cpb = tg // CHUNK
    j_of = (lambda j: nb - 1 - j) if rev else (lambda j: j)
    qspec = pl.BlockSpec((tg, HW), lambda j: (j_of(j), 0))
    colspec = pl.BlockSpec((HEADS, tg, 1), lambda j: (0, j_of(j), 0))
    rowspec = pl.BlockSpec((HEADS, cpb, 1, CHUNK), lambda j: (0, j_of(j), 0, 0))
    onespec = pl.BlockSpec((HEADS, cpb, 1, 1), lambda j: (0, j_of(j), 0, 0))
    stspec = pl.BlockSpec((HEADS, cpb, HEAD_DIM, HEAD_DIM), lambda j: (0, j_of(j), 0, 0))
    return qspec, colspec, rowspec, onespec, stspec


def _gdn_fwd_call(q, k, v, gcc, gcr, bc, gl):
    s = q.shape[0]
    tg = _pick(s, GDN_ROWS, CHUNK)
    cpb = tg // CHUNK
    qspec, colspec, rowspec, onespec, stspec = _gdn_specs(s, tg, False)

    def body(q_ref, k_ref, v_ref, gcc_ref, gcr_ref, b_ref, gl_ref, o_ref, st_ref, st):
        @pl.when(pl.program_id(0) == 0)
        def _():
            st[...] = jnp.zeros_like(st)

        def chunk(ci, _):
            rows = pl.ds(pl.multiple_of(ci * CHUNK, CHUNK), CHUNK)
            s_in = st[...]
            st_ref[:, ci] = s_in
            o, s_new = _gdn_chunk(_heads_of(q_ref, rows), _heads_of(k_ref, rows), _heads_of(v_ref, rows),
                                  gcc_ref[:, rows, :], gcr_ref[:, ci], b_ref[:, rows, :], gl_ref[:, ci], s_in)
            for h in range(HEADS):
                o_ref[rows, _head_cols(h)] = o[h]
            st[...] = s_new
            return 0

        lax.fori_loop(0, cpb, chunk, 0)

    return _pcall(body, name="gdn_fwd",
                  out_shape=[jax.ShapeDtypeStruct((s, HW), F32),
                             jax.ShapeDtypeStruct((HEADS, s // CHUNK, HEAD_DIM, HEAD_DIM), F32)],
                  grid=(s // tg,), in_specs=[qspec, qspec, qspec, colspec, rowspec, colspec, onespec],
                  out_specs=[qspec, stspec], scratch_shapes=[pltpu.VMEM((HEADS, HEAD_DIM, HEAD_DIM), F32)],
                  )(q, k, v, gcc, gcr, bc, gl)


def _gdn_bwd_call(q, k, v, gcc, gcr, bc, gl, states, do):
    s = q.shape[0]
    tg = _pick(s, GDN_ROWS, CHUNK)
    cpb = tg // CHUNK
    qspec, colspec, rowspec, onespec, stspec = _gdn_specs(s, tg, True)

    def body(q_ref, k_ref, v_ref, gcc_ref, gcr_ref, b_ref, gl_ref, st_ref, do_ref,
             dq_ref, dk_ref, dv_ref, dgcc_ref, dgcr_ref, db_ref, dgl_ref, dst):
        @pl.when(pl.program_id(0) == 0)
        def _():
            dst[...] = jnp.zeros_like(dst)

        def chunk(n, _):
            ci = cpb - 1 - n
            rows = pl.ds(pl.multiple_of(ci * CHUNK, CHUNK), CHUNK)
            _, vjp = jax.vjp(_gdn_chunk, _heads_of(q_ref, rows), _heads_of(k_ref, rows), _heads_of(v_ref, rows),
                             gcc_ref[:, rows, :], gcr_ref[:, ci], b_ref[:, rows, :], gl_ref[:, ci], st_ref[:, ci])
            dq, dk, dv, dgcc, dgcr, db, dgl, ds_in = vjp((_heads_of(do_ref, rows), dst[...]))
            for h in range(HEADS):
                cols = _head_cols(h)
                dq_ref[rows, cols] = dq[h]
                dk_ref[rows, cols] = dk[h]
                dv_ref[rows, cols] = dv[h]
            dgcc_ref[:, rows, :] = dgcc
            dgcr_ref[:, ci] = dgcr
            db_ref[:, rows, :] = db
            dgl_ref[:, ci] = dgl
            dst[...] = ds_in
            return 0

        lax.fori_loop(0, cpb, chunk, 0)

    n = s // CHUNK
    return _pcall(body, name="gdn_bwd",
                  out_shape=[jax.ShapeDtypeStruct((s, HW), F32)] * 3
                  + [jax.ShapeDtypeStruct((HEADS, s, 1), F32), jax.ShapeDtypeStruct((HEADS, n, 1, CHUNK), F32),
                     jax.ShapeDtypeStruct((HEADS, s, 1), F32), jax.ShapeDtypeStruct((HEADS, n, 1, 1), F32)],
                  grid=(s // tg,),
                  in_specs=[qspec, qspec, qspec, colspec, rowspec, colspec, onespec, stspec, qspec],
                  out_specs=[qspec, qspec, qspec, colspec, rowspec, colspec, onespec],
                  scratch_shapes=[pltpu.VMEM((HEADS, HEAD_DIM, HEAD_DIM), F32)],
                  )(q, k, v, gcc, gcr, bc, gl, states, do)


@jax.custom_vjp
def gated_delta(q, k, v, gcc, gcr, bc, gl):
    return _gdn_fwd_call(q, k, v, gcc, gcr, bc, gl)[0]


def _gdn_vjp_fwd(q, k, v, gcc, gcr, bc, gl):
    o, states = _gdn_fwd_call(q, k, v, gcc, gcr, bc, gl)
    return o, (q, k, v, gcc, gcr, bc, gl, states)


gated_delta.defvjp(_gdn_vjp_fwd, lambda res, g: tuple(_gdn_bwd_call(*res, g)))


def _loss_call(y, target):
    s, d = y.shape
    tm = _pick(s, 512, SUBLANES)

    def body(y_ref, t_ref, dy_ref, loss_ref):
        @pl.when(pl.program_id(0) == 0)
        def _():
            loss_ref[...] = jnp.zeros_like(loss_ref)

        err = y_ref[...] - t_ref[...]
        dy_ref[...] = err * (1.0 / d)
        loss_ref[...] += 0.5 * jnp.sum(jnp.mean(err * err, axis=-1, keepdims=True), axis=0, keepdims=True)

    dy, part = _pcall(body, name="loss_head",
                      out_shape=[jax.ShapeDtypeStruct((s, d), F32), jax.ShapeDtypeStruct((1, 1), F32)],
                      grid=(s // tm,), in_specs=[pl.BlockSpec((tm, d), lambda i: (i, 0))] * 2,
                      out_specs=[pl.BlockSpec((tm, d), lambda i: (i, 0)), pl.BlockSpec((1, 1), lambda i: (0, 0))],
                      )(y, target)
    return part[0, 0], dy


def _cols_and_rows(a, lane0, t):
    s = a.shape[0]
    at = a[:, lane0:lane0 + HEADS].T
    return at, at[:, :, None], at.reshape(HEADS, s // t, 1, t)


def _pad_lanes(v, lane0):
    return jnp.pad(v, (lane0, LANES - lane0 - v.shape[0])).reshape(1, LANES)


def _layer(x, mem, w, w16, ops, convs):
    s = x.shape[0]
    row = lambda v: v.reshape(1, -1)
    mw = lambda n: (w[n], w16[n])
    h = ops["rms"](x, row(w["norm_mix"]))
    fq, fk, fv, gqkv, gz, sq, sk, sv, gt, gm = proj(h, *mw("w_in"))

    logf, beta, gc = ops["small"](gm, _pad_lanes(w["fox_fbias"], LANE_FF), _pad_lanes(w["gdn_a_log"], LANE_GA),
                                  _pad_lanes(w["gdn_dt_bias"], LANE_GA))
    _, ccol, crow = _cols_and_rows(seq_cumsum(logf), LANE_FF, _att_tiles(s, fox=True)[1])
    ya = fox_attention(ops["headnorm"](fq, row(w["fox_qnorm"])), ops["headnorm"](fk, row(w["fox_knorm"])), fv,
                       ccol, crow)
    cq, ck, cv = convs["gdn"](gqkv, w["gdn_conv"])
    gct, gcc, gcr = _cols_and_rows(gc, LANE_GA, CHUNK)
    gl = gct.reshape(HEADS, s // CHUNK, CHUNK)[:, :, CHUNK - 1].reshape(HEADS, s // CHUNK, 1, 1)
    bc = beta[:, LANE_GB:LANE_GB + HEADS].T[:, :, None]
    yb = ops["gdnpost"](gated_delta(cq, ck, cv, gcc, gcr, bc, gl), gz, row(w["gdn_onorm"]))
    yc = sb_attention(sq, sk, sv)
    gb = w["gate_bias"]
    mixed = ops["merge"](gt, matmul(ya, *mw("w_oa")), matmul(yb, *mw("w_ob")), matmul(yc, *mw("w_oc")),
                         row(gb[:D_MODEL]), row(gb[D_MODEL:2 * D_MODEL]), row(gb[2 * D_MODEL:]))
    x = matmul_add(x, mixed, *mw("w_out"))
    mq = ops["headnorm"](matmul(ops["rms"](x, row(w["norm_xq"])), *mw("w_mq")), row(w["mq_norm"]))
    kv = matmul(ops["rms"](mem, row(w["norm_mem"])), *mw("w_mkv"))
    mk = ops["headnorm"](kv[:, :HW], row(w["mk_norm"]))
    x = matmul_add(x, mem_attention(mq, mk, kv[:, HW:]), *mw("w_mo"))
    u = matmul(ops["rms"](x, row(w["norm_ffn"])), *mw("w_up"))
    act = convs["ffn"](u, w["ffn_conv"], row(w["ffn_conv_b"]))
    return matmul_add(x, act, *mw("w_down"))


def _forward(x, mem, layers, layers16):
    ops, convs = _make_rowops(), _make_convops()
    for w, w16 in zip(layers, layers16):
        x = _layer(x, mem, w, w16, ops, convs)
    return x


ANY = pl.BlockSpec(memory_space=pl.ANY)


N_PEERS = N_DEV - 1


def _ccall(body, *, name, out_shape, n_arrays):
    return pl.pallas_call(body, name=name, out_shape=out_shape, in_specs=[ANY] * n_arrays,
                          out_specs=[ANY] * n_arrays,
                          scratch_shapes=[pltpu.SemaphoreType.DMA((N_PEERS * n_arrays,)),
                                          pltpu.SemaphoreType.DMA((N_PEERS * n_arrays,)),
                                          pltpu.SemaphoreType.DMA((n_arrays,))],
                          interpret=False)


def _all_gather(name, shards):
    n = len(shards)

    def body(*refs):
        x_refs, out_refs = refs[:n], refs[n:2 * n]
        send_sems, recv_sems, local_sems = refs[2 * n:]
        x, y, c = lax.axis_index("x"), lax.axis_index("y"), lax.axis_index("c")
        me, sibling = (x, y, c), (x, y, 1 - c)
        chips = [(1 - x, y), (x, 1 - y), (1 - x, 1 - y)]

        def slot(a, px, py, pc):
            return out_refs[a].at[4 * px + 2 * py + pc]

        def copy(a, k, block, to, src=None):
            return pltpu.make_async_remote_copy(
                src_ref=slot(a, *block) if src is None else src, dst_ref=slot(a, *block),
                send_sem=send_sems.at[N_PEERS * a + k], recv_sem=recv_sems.at[N_PEERS * a + k], device_id=to,
                device_id_type=MESH)

        mine = [pltpu.make_async_copy(x_refs[a], slot(a, *me), local_sems.at[a]) for a in range(n)]
        first = []
        for a in range(n):
            first.append(copy(a, 0, me, sibling, src=x_refs[a]))
            first += [copy(a, 1 + j, me, (*chip, c), src=x_refs[a]) for j, chip in enumerate(chips)]
        for cp in mine + first:
            cp.start()
        passed = []
        for j, chip in enumerate(chips):
            for a in range(n):
                copy(a, 1 + j, (*chip, c), me).wait_recv()
                passed.append(copy(a, 4 + j, (*chip, c), sibling))
                passed[-1].start()
        for a in range(n):
            copy(a, 0, sibling, me).wait_recv()
            for j, chip in enumerate(chips):
                copy(a, 4 + j, (*chip, 1 - c), me).wait_recv()
        for cp in first + passed:
            cp.wait_send()
        for cp in mine:
            cp.wait()

    return _ccall(body, name=name, out_shape=[jax.ShapeDtypeStruct((N_DEV,) + s.shape, s.dtype) for s in shards],
                  n_arrays=n)(*shards)


SEM = pl.BlockSpec(memory_space=pltpu.SEMAPHORE)
HBM = pl.BlockSpec(memory_space=pltpu.HBM)
DATAFLOW = pltpu.SideEffectType.DATAFLOW_SIDE_EFFECTING


def _exchange_copies(p_refs, land_refs, send_sem, recv_sem):
    x, y, c = lax.axis_index("x"), lax.axis_index("y"), lax.axis_index("c")
    copies = []
    for p_ref, land_ref in zip(p_refs, land_refs):
        for k in range(1, N_DEV):
            px, py, pc = x ^ ((k >> 2) & 1), y ^ ((k >> 1) & 1), c ^ (k & 1)
            copies.append(pltpu.make_async_remote_copy(
                src_ref=p_ref.at[4 * px + 2 * py + pc], dst_ref=land_ref.at[k - 1], send_sem=send_sem,
                recv_sem=recv_sem, device_id=(px, py, pc), device_id_type=MESH))
    return copies


def _exchange_start(name, parts):
    n = len(parts)
    lands = [lax.empty((N_PEERS,) + p.shape[1:], p.dtype) for p in parts]

    def body(*refs):
        send_sem, recv_sem = refs[2 * n], refs[2 * n + 1]
        for cp in _exchange_copies(refs[:n], refs[n:2 * n], send_sem, recv_sem):
            cp.start()
        refs[-1][...] = jnp.zeros_like(refs[-1])

    outs = pl.pallas_call(
        body, name=name,
        out_shape=(pltpu.SemaphoreType.DMA(()), pltpu.SemaphoreType.DMA(()),
                   *[pltpu.HBM(a.shape, a.dtype) for a in parts + lands],
                   jax.ShapeDtypeStruct((SUBLANES, LANES), F32)),
        in_specs=[HBM] * (2 * n), out_specs=(SEM, SEM, *[HBM] * (2 * n), pl.BlockSpec(memory_space=pltpu.VMEM)),
        input_output_aliases={i: 2 + i for i in range(2 * n)},
        compiler_params=pltpu.CompilerParams(has_side_effects=DATAFLOW),
        interpret=False)(*[pltpu.with_memory_space_constraint(a, pltpu.HBM) for a in parts + lands])
    return (outs[0], outs[1], list(outs[2:2 + n]), list(outs[2 + n:2 + 2 * n])), outs[-1]


def _exchange_wait(name, send_sem, recv_sem, parts, lands, after):
    n = len(parts)

    def body(*refs):
        for cp in _exchange_copies(refs[:n], refs[n:2 * n], refs[2 * n], refs[2 * n + 1]):
            cp.wait_send()
            cp.wait_recv()

    outs = pl.pallas_call(
        body, name=name, out_shape=tuple(pltpu.HBM(a.shape, a.dtype) for a in parts + lands),
        in_specs=[HBM] * (2 * n) + [SEM, SEM, ANY], out_specs=[HBM] * (2 * n),
        input_output_aliases={i: i for i in range(2 * n)},
        compiler_params=pltpu.CompilerParams(has_side_effects=DATAFLOW),
        interpret=False)(*parts, *lands, send_sem, recv_sem, after)
    return list(outs[:n]), list(outs[n:])


ADAM_SLOT_BYTES = 4 * 1024 * 1024


def _adam_call(name, w, slots, m, v):
    r, n = w.shape
    rows_unit = 2 * SUBLANES
    tr = _pick(r, max(rows_unit, ADAM_SLOT_BYTES // (N_DEV * n * 4)), rows_unit)
    spec = pl.BlockSpec((tr, n), lambda i: (i, 0))

    def body(w_ref, s_ref, m_ref, v_ref, g_ref, d_ref, nm_ref, nv_ref):
        g = s_ref[0].astype(F32)
        for d in range(1, N_DEV):
            g = g + s_ref[d].astype(F32)
        nm = ADAM_B1 * m_ref[...] + (1.0 - ADAM_B1) * g
        nv = ADAM_B2 * v_ref[...] + (1.0 - ADAM_B2) * (g * g)
        m_hat = nm / (1.0 - ADAM_B1 ** ADAM_STEP)
        v_hat = nv / (1.0 - ADAM_B2 ** ADAM_STEP)
        g_ref[...] = g
        d_ref[...] = -ADAM_LR * (m_hat / (jnp.sqrt(v_hat) + ADAM_EPS) + ADAM_WD * w_ref[...])
        nm_ref[...] = nm
        nv_ref[...] = nv

    return _pcall(body, name=name, out_shape=[jax.ShapeDtypeStruct((r, n), F32)] * 4, grid=(r // tr,),
                  in_specs=[spec, pl.BlockSpec((N_DEV, tr, n), lambda i: (0, i, 0)), spec, spec],
                  out_specs=[spec] * 4)(w, slots, m, v)


def _pack_rows(flat, rows):
    return jnp.pad(flat, (0, rows * PACK_COLS - flat.shape[0])).reshape(rows, PACK_COLS)


def _regroup_in(w_in):
    cols = [w_in[:, a:b] for a, b in _IN_SRC]
    return jnp.concatenate(cols + [jnp.zeros((w_in.shape[0], N_IN_PAD - N_IN), w_in.dtype)], axis=1)


def _ungroup_in(d):
    starts = {}
    off = 0
    for a, b in _IN_SRC:
        starts[a] = (off, b - a)
        off += b - a
    return jnp.concatenate([d[:, starts[a][0]:starts[a][0] + starts[a][1]] for a in sorted(starts)], axis=1)


def _full_weights(gathered):
    out = {}
    for n, g in zip(SHARDED_ORDER, gathered):
        (r, c), axis = SHARDED[n]
        out[n] = g.reshape(r, c) if axis == 0 else g.transpose(1, 0, 2).reshape(r, c)
    out["w_in"] = _regroup_in(out["w_in"])
    return out


def _in_f32(full):
    return {n: v.astype(F32) for n, v in full.items()}


def _for_transport(name, shard):
    return shard if name in ("gdn_conv", "ffn_conv") else shard.astype(BF16)


def _grad_parts(grads):
    parts = []
    for n in SHARDED_ORDER:
        (r, c), axis = SHARDED[n]
        g = _ungroup_in(grads[n]) if n == "w_in" else grads[n]
        if axis == 0:
            parts.append(g.reshape(N_DEV, r // N_DEV, c).astype(BF16))
        else:
            parts.append(g.reshape(r, N_DEV, c // N_DEV).transpose(1, 0, 2).astype(BF16))
    return parts


def _pack_small(vals):
    return _pack_rows(jnp.concatenate([vals[n].reshape(-1) for n in SMALL_ORDER]), SMALL_ROWS)


def _unpack_small(packed):
    flat = packed.reshape(-1)
    out, off = {}, 0
    for n in SMALL_ORDER:
        size = DEPTH * SMALL_WIDTH[n]
        out[n] = flat[off:off + size].reshape(DEPTH, SMALL_WIDTH[n])
        off += size
    return out


def kernel(x, mem, norm_mix, w_in, fox_fbias, fox_qnorm, fox_knorm, gdn_conv, gdn_a_log, gdn_dt_bias, gdn_onorm, gate_bias, w_oa, w_ob, w_oc, w_out, norm_xq, norm_mem, w_mq, w_mkv, mq_norm, mk_norm, w_mo, norm_ffn, w_up, ffn_conv, ffn_conv_b, w_down, loss_target, m_norm_mix, m_w_in, m_fox_fbias, m_fox_qnorm, m_fox_knorm, m_gdn_conv, m_gdn_a_log, m_gdn_dt_bias, m_gdn_onorm, m_gate_bias, m_w_oa, m_w_ob, m_w_oc, m_w_out, m_norm_xq, m_norm_mem, m_w_mq, m_w_mkv, m_mq_norm, m_mk_norm, m_w_mo, m_norm_ffn, m_w_up, m_ffn_conv, m_ffn_conv_b, m_w_down, v_norm_mix, v_w_in, v_fox_fbias, v_fox_qnorm, v_fox_knorm, v_gdn_conv, v_gdn_a_log, v_gdn_dt_bias, v_gdn_onorm, v_gate_bias, v_w_oa, v_w_ob, v_w_oc, v_w_out, v_norm_xq, v_norm_mem, v_w_mq, v_w_mkv, v_mq_norm, v_mk_norm, v_w_mo, v_norm_ffn, v_w_up, v_ffn_conv, v_ffn_conv_b, v_w_down):
    given = dict(locals())
    wts = {n: given[n] for n in WEIGHTS}
    mom = {n: given["m_" + n] for n in WEIGHTS}
    var = {n: given["v_" + n] for n in WEIGHTS}

    layers, layers16 = [], []
    for l in range(DEPTH):
        full = _full_weights(_all_gather("gather_weights", [_for_transport(n, wts[n][l]) for n in SHARDED_ORDER]))
        layers16.append(full)
        layers.append({**_in_f32(full), **{n: wts[n][l] for n in SMALL_ORDER}})

    ops, convs = _make_rowops(), _make_convops()
    y, vjps = x[0], []
    for l in range(DEPTH):
        y, vjp = jax.vjp(lambda xx, ww, l=l: _layer(xx, mem[0], ww, layers16[l], ops, convs), y, layers[l])
        vjps.append(vjp)
    loss_part, dx = _loss_call(y, loss_target[0])
    loss = lax.psum(loss_part, ("x", "y", "c"))

    out = {}
    dlayers, started = [None] * DEPTH, [None] * DEPTH
    for l in reversed(range(DEPTH)):
        dx, dlayers[l] = vjps[l](dx)
        started[l], token = _exchange_start(f"exchange_start_{l}", _grad_parts(dlayers[l]))
        dx, _ = lax.optimization_barrier((dx, token))
    me = 4 * lax.axis_index("x") + 2 * lax.axis_index("y") + lax.axis_index("c")
    per_layer = []
    for l in range(DEPTH):
        parts, lands = _exchange_wait(f"exchange_wait_{l}", *started[l], dx)
        slots = [jnp.concatenate([lax.dynamic_index_in_dim(p, me, 0), ld]) for p, ld in zip(parts, lands)]
        per_layer.append({n: _adam_call("adam_shard", wts[n][l], sl, mom[n][l], var[n][l])
                          for n, sl in zip(SHARDED_ORDER, slots)})
    for n in SHARDED_ORDER:
        for k, kind in enumerate(("grad_", "delta_", "new_m_", "new_v_")):
            out[kind + n] = jnp.stack([per_layer[l][n][k] for l in range(DEPTH)])
    dsmall = {n: jnp.stack([dlayers[l][n] for l in range(DEPTH)]) for n in SMALL_ORDER}
    slots = _all_gather("gather_small_grads", [_pack_small(dsmall)])[0]
    res = _adam_call("adam_small", _pack_small(wts), slots, _pack_small(mom), _pack_small(var))
    for k, kind in enumerate(("grad_", "delta_", "new_m_", "new_v_")):
        un = _unpack_small(res[k])
        for n in SMALL_ORDER:
            out[kind + n] = un[n].reshape(wts[n].shape)

    return (loss, dx[None], *[out["grad_" + n] for n in WEIGHTS], *[out["delta_" + n] for n in WEIGHTS],
            *[out["new_m_" + n] for n in WEIGHTS], *[out["new_v_" + n] for n in WEIGHTS])
```

```python
import jax
import jax.numpy as jnp
from jax import lax
from jax.experimental import pallas as pl
from jax.experimental.pallas import tpu as pltpu

F32 = jnp.float32
BF16 = jnp.bfloat16

N_DEV = 8
D_MODEL = 1024
DEPTH = 4
CHUNK = 64
EPS = 1e-6
HEADS = 4
HEAD_DIM = 128
HW = HEADS * HEAD_DIM
D_FF = 2816
N_IN = 8204
LANES = 128
SUBLANES = 8
VMEM_LIMIT = 56 * 1024 * 1024

ADAM_LR = 0.001
ADAM_B1 = 0.9
ADAM_B2 = 0.999
ADAM_EPS = 1e-08
ADAM_WD = 0.01
ADAM_STEP = 10

NEG = -1e30
MESH = pl.DeviceIdType.MESH

WEIGHTS = ['norm_mix', 'w_in', 'fox_fbias', 'fox_qnorm', 'fox_knorm', 'gdn_conv', 'gdn_a_log', 'gdn_dt_bias',
           'gdn_onorm', 'gate_bias', 'w_oa', 'w_ob', 'w_oc', 'w_out', 'norm_xq', 'norm_mem', 'w_mq', 'w_mkv',
           'mq_norm', 'mk_norm', 'w_mo', 'norm_ffn', 'w_up', 'ffn_conv', 'ffn_conv_b', 'w_down']
SHARDED = {
    'w_in': ((D_MODEL, N_IN), 0), 'gdn_conv': ((4, 3 * HW), 1), 'w_oa': ((HW, D_MODEL), 1),
    'w_ob': ((HW, D_MODEL), 1), 'w_oc': ((HW, D_MODEL), 1), 'w_out': ((D_MODEL, D_MODEL), 0),
    'w_mq': ((D_MODEL, HW), 0), 'w_mkv': ((D_MODEL, 2 * HW), 0), 'w_mo': ((HW, D_MODEL), 1),
    'w_up': ((D_MODEL, 2 * D_FF), 1), 'ffn_conv': ((3, 2 * D_FF), 1), 'w_down': ((D_FF, D_MODEL), 0),
}
SHARDED_ORDER = [n for n in WEIGHTS if n in SHARDED]
SMALL_ORDER = [n for n in WEIGHTS if n not in SHARDED]
SMALL_WIDTH = {'norm_mix': D_MODEL, 'fox_fbias': HEADS, 'fox_qnorm': HEAD_DIM, 'fox_knorm': HEAD_DIM,
               'gdn_a_log': HEADS, 'gdn_dt_bias': HEADS, 'gdn_onorm': HEAD_DIM, 'gate_bias': 3 * D_MODEL,
               'norm_xq': D_MODEL, 'norm_mem': D_MODEL, 'mq_norm': HEAD_DIM, 'mk_norm': HEAD_DIM,
               'norm_ffn': D_MODEL, 'ffn_conv_b': 2 * D_FF}
PACK_COLS = 1024


def _round_up(n, m):
    return (n + m - 1) // m * m


SMALL_ROWS = _round_up(DEPTH * sum(SMALL_WIDTH.values()), SUBLANES * PACK_COLS) // PACK_COLS

_IN_SRC = [(0, 512), (512, 1024), (1024, 1536),
           (1540, 2052), (2052, 2564), (2564, 3076),
           (3084, 3596),
           (3596, 4108), (4108, 4620), (4620, 5132),
           (5132, 8204),
           (1536, 1540), (3076, 3080), (3080, 3084)]
N_IN_PAD = 8320
LANE_FF, LANE_GB, LANE_GA = 0, 4, 8


def _pick(dim, pref, unit):
    best = None
    t = unit
    while t <= min(dim, pref):
        if dim % t == 0:
            best = t
        t += unit
    return dim if best is None else best


def _params(n_grid):
    return pltpu.CompilerParams(dimension_semantics=("arbitrary",) * n_grid, vmem_limit_bytes=VMEM_LIMIT)


def _pcall(body, *, name, out_shape, grid, in_specs, out_specs, scratch_shapes=()):
    return pl.pallas_call(body, name=name, out_shape=out_shape, grid=grid, in_specs=in_specs, out_specs=out_specs,
                          scratch_shapes=scratch_shapes, compiler_params=_params(len(grid)),
                          interpret=False)


NN = ((1,), (0,))
NT = ((1,), (1,))
TN = ((0,), (0,))


def _dot(a, b, dn):
    return lax.dot_general(a.astype(BF16), b.astype(BF16), (dn, ((), ())), preferred_element_type=F32)


def _dotf(a, b, dn):
    return lax.dot_general(a, b, (dn, ((), ())), precision=lax.Precision.HIGHEST, preferred_element_type=F32)


@jax.custom_vjp
def mm(a, b):
    return _dot(a, b, NN)


mm.defvjp(lambda a, b: (_dot(a, b, NN), (a, b)), lambda r, g: (_dot(g, r[1], NT), _dot(r[0], g, TN)))


@jax.custom_vjp
def mm_nt(a, b):
    return _dot(a, b, NT)


mm_nt.defvjp(lambda a, b: (_dot(a, b, NT), (a, b)), lambda r, g: (_dot(g, r[1], NN), _dot(g, r[0], TN)))


@jax.custom_vjp
def mm_tn(a, b):
    return _dot(a, b, TN)


mm_tn.defvjp(lambda a, b: (_dot(a, b, TN), (a, b)), lambda r, g: (_dot(r[1], g, NT), _dot(r[0], g, NN)))


@jax.custom_vjp
def tri_apply(t, x):
    return _dotf(t, x, NN)


tri_apply.defvjp(lambda t, x: (_dotf(t, x, NN), t), lambda t, g: (jnp.zeros_like(t), _dotf(t, g, TN)))


def _sigmoid(x):
    return 1.0 / (1.0 + jnp.exp(-x))


@jax.custom_vjp
def _softplus(x):
    return jnp.maximum(x, 0.0) + jnp.log(1.0 + jnp.exp(-jnp.abs(x)))


_softplus.defvjp(lambda x: (_softplus(x), x), lambda x, g: (g * _sigmoid(x),))


@jax.custom_vjp
def _silu(x):
    return x * _sigmoid(x)


def _silu_fwd(x):
    s = _sigmoid(x)
    return x * s, (x, s)


_silu.defvjp(_silu_fwd, lambda r, g: (g * r[1] * (1.0 + r[0] * (1.0 - r[1])),))


def _rms(x, g):
    return x * lax.rsqrt(jnp.mean(x * x, axis=-1, keepdims=True) + EPS) * g


def _iota2(n, m, axis):
    return lax.broadcasted_iota(jnp.int32, (n, m), axis)


def _whole(width):
    return [(0, width)]


def _split(width, n):
    w = width // n
    return [(k * w, w) for k in range(n)]


class RowOp:
    def __init__(self, name, f, in_pieces, out_pieces, tm=256):
        self.name, self.f, self.in_pieces, self.out_pieces, self.tm = name, f, in_pieces, out_pieces, tm
        op = jax.custom_vjp(self._fwd_call)
        op.defvjp(lambda *a: (self._fwd_call(*a), a), lambda res, g: self._bwd_call(res, g))
        self.op = op

    def __call__(self, *args):
        return self.op(*args)

    def _width(self, pieces):
        return max(o + w for o, w in pieces)

    def _row_specs(self, pieces_list, tm):
        return [pl.BlockSpec((tm, self._width(p)), lambda i: (i, 0)) for p in pieces_list]

    def _fwd_call(self, *args):
        nr = len(self.in_pieces)
        rows, params = args[:nr], args[nr:]
        m = rows[0].shape[0]
        tm = _pick(m, self.tm, SUBLANES)
        f, in_pieces, out_pieces = self.f, self.in_pieces, self.out_pieces
        no = len(out_pieces)

        def body(*refs):
            rin, pr, ro = refs[:nr], refs[nr:nr + len(params)], refs[nr + len(params):]
            xs = [r[:, o:o + w] for r, ps in zip(rin, in_pieces) for (o, w) in ps]
            ys = f(*xs, *[p[...] for p in pr])
            k = 0
            for r, ps in zip(ro, out_pieces):
                for (o, w) in ps:
                    r[:, o:o + w] = ys[k]
                    k += 1

        outs = _pcall(
            body, name=self.name + "_fwd",
            out_shape=[jax.ShapeDtypeStruct((m, self._width(p)), F32) for p in out_pieces],
            grid=(m // tm,),
            in_specs=self._row_specs(in_pieces, tm) + [pl.BlockSpec(p.shape, lambda i: (0, 0)) for p in params],
            out_specs=self._row_specs(out_pieces, tm),
        )(*rows, *params)
        return tuple(outs) if no > 1 else outs[0]

    def _bwd_call(self, res, g):
        nr = len(self.in_pieces)
        rows, params = res[:nr], res[nr:]
        no = len(self.out_pieces)
        gs = tuple(g) if no > 1 else (g,)
        m = rows[0].shape[0]
        tm = _pick(m, self.tm, SUBLANES)
        f, in_pieces, out_pieces = self.f, self.in_pieces, self.out_pieces
        npar = len(params)

        def body(*refs):
            rin, pr, dro = refs[:nr], refs[nr:nr + npar], refs[nr + npar:nr + npar + no]
            drin, dpr = refs[nr + npar + no:nr + npar + no + nr], refs[nr + npar + no + nr:]
            xs = [r[:, o:o + w] for r, ps in zip(rin, in_pieces) for (o, w) in ps]
            dys = [r[:, o:o + w] for r, ps in zip(dro, out_pieces) for (o, w) in ps]
            _, vjp = jax.vjp(lambda *a: tuple(f(*a)), *xs, *[p[...] for p in pr])
            grads = vjp(tuple(dys))
            k = 0
            for r, ps in zip(drin, in_pieces):
                for (o, w) in ps:
                    r[:, o:o + w] = grads[k]
                    k += 1

            @pl.when(pl.program_id(0) == 0)
            def _():
                for r in dpr:
                    r[...] = jnp.zeros_like(r)

            for j, r in enumerate(dpr):
                r[...] += grads[k + j]

        outs = _pcall(
            body, name=self.name + "_bwd",
            out_shape=[jax.ShapeDtypeStruct(r.shape, F32) for r in rows]
            + [jax.ShapeDtypeStruct(p.shape, F32) for p in params],
            grid=(m // tm,),
            in_specs=self._row_specs(in_pieces, tm) + [pl.BlockSpec(p.shape, lambda i: (0, 0)) for p in params]
            + self._row_specs(out_pieces, tm),
            out_specs=self._row_specs(in_pieces, tm) + [pl.BlockSpec(p.shape, lambda i: (0, 0)) for p in params],
        )(*rows, *params, *gs)
        return tuple(outs)


def _f_rms(x, g):
    return (_rms(x, g),)


def _f_headnorm(x0, x1, x2, x3, g):
    return tuple(_rms(x, g) for x in (x0, x1, x2, x3))


def _f_small(sm, fb, al, db):
    tm = sm.shape[0]
    logf = -_softplus(-(sm + fb))
    beta = _sigmoid(sm)
    glog = -jnp.exp(al) * _softplus(sm + db)
    r, c = _iota2(tm, tm, 0), _iota2(tm, tm, 1)
    bd = jnp.where((r >= c) & (jnp.bitwise_xor(r, c) < CHUNK), 1.0, 0.0).astype(F32)
    return logf, beta, tri_apply(bd, glog)


def _f_gdnpost(o0, o1, o2, o3, z0, z1, z2, z3, g):
    return tuple(_rms(o, g) * _silu(z) for o, z in zip((o0, o1, o2, o3), (z0, z1, z2, z3)))


def _f_merge(t0, t1, t2, a, b, c, b0, b1, b2):
    return (_sigmoid(t0 + b0) * a + _sigmoid(t1 + b1) * b + _sigmoid(t2 + b2) * c,)


def _make_rowops():
    return dict(
        rms=RowOp("rms", _f_rms, [_whole(D_MODEL)], [_whole(D_MODEL)], tm=512),
        headnorm=RowOp("headnorm", _f_headnorm, [_split(HW, HEADS)], [_split(HW, HEADS)], tm=1024),
        small=RowOp("smallprep", _f_small, [_whole(LANES)], [_whole(LANES)] * 3),
        gdnpost=RowOp("gdnpost", _f_gdnpost, [_split(HW, HEADS)] * 2, [_split(HW, HEADS)], tm=512),
        merge=RowOp("merge", _f_merge, [_split(3 * D_MODEL, 3)] + [_whole(D_MODEL)] * 3, [_whole(D_MODEL)]),
    )


ROUND_ONCE_READS = 3


def _b16(x):
    return x.astype(BF16)


def _mm_call(name, a, b, mode, c=None):
    if mode == "nn":
        (m, kc), n = a.shape, b.shape[1]
    elif mode == "nt":
        (m, kc), n = a.shape, b.shape[0]
    else:
        (kc, m), n = a.shape, b.shape[1]
    tm = _pick(m, 1408, LANES) if mode == "tn" else _pick(m, 1024, SUBLANES)
    tn = _pick(n, 1408, LANES)
    tk = _pick(kc, 1024, SUBLANES) if mode == "tn" else _pick(kc, 1536, LANES)
    if a.dtype == F32 and n // tn >= ROUND_ONCE_READS:
        a = _b16(a)
    if b.dtype == F32 and m // tm >= ROUND_ONCE_READS:
        b = _b16(b)
    dn = {"nn": NN, "nt": NT, "tn": TN}[mode]
    a_spec = {"nn": pl.BlockSpec((tm, tk), lambda i, j, k: (i, k)),
              "nt": pl.BlockSpec((tm, tk), lambda i, j, k: (i, k)),
              "tn": pl.BlockSpec((tk, tm), lambda i, j, k: (k, i))}[mode]
    b_spec = {"nn": pl.BlockSpec((tk, tn), lambda i, j, k: (k, j)),
              "nt": pl.BlockSpec((tn, tk), lambda i, j, k: (j, k)),
              "tn": pl.BlockSpec((tk, tn), lambda i, j, k: (k, j))}[mode]
    o_spec = pl.BlockSpec((tm, tn), lambda i, j, k: (i, j))
    has_c = c is not None

    def body(*refs):
        a_ref, b_ref = refs[0], refs[1]
        o_ref = refs[-1]

        @pl.when(pl.program_id(2) == 0)
        def _():
            o_ref[...] = refs[2][...] if has_c else jnp.zeros_like(o_ref)

        o_ref[...] += _dot(a_ref[...], b_ref[...], dn)

    return _pcall(body, name=name, out_shape=jax.ShapeDtypeStruct((m, n), F32), grid=(m // tm, n // tn, kc // tk),
                  in_specs=[a_spec, b_spec] + ([o_spec] if has_c else []), out_specs=o_spec,
                  )(*((a, b, c) if has_c else (a, b)))


@jax.custom_vjp
def matmul(a, w, w16):
    return _mm_call("mm_nn", a, w16, "nn")


def _matmul_bwd(res, g):
    a, w16 = res
    return _mm_call("mm_nt", g, w16, "nt"), _mm_call("mm_tn", a, g, "tn"), jnp.zeros_like(w16)


matmul.defvjp(lambda a, w, w16: (_mm_call("mm_nn", a, w16, "nn"), (a, w16)), _matmul_bwd)


@jax.custom_vjp
def matmul_add(c, a, w, w16):
    return _mm_call("mm_nn_add", a, w16, "nn", c)


matmul_add.defvjp(lambda c, a, w, w16: (_mm_call("mm_nn_add", a, w16, "nn", c), (a, w16)),
                  lambda res, g: (g,) + _matmul_bwd(res, g))

_PROJ_GROUPS = [(0, 512), (512, 512), (1024, 512), (1536, 1536), (3072, 512), (3584, 512), (4096, 512), (4608, 512),
                (5120, 3072), (8192, 128)]


def _proj_fwd(h, w, w16):
    h16 = _b16(h)
    return tuple(_mm_call("proj_nn", h16, w16[:, s:s + n], "nn") for s, n in _PROJ_GROUPS), (h16, w16)


proj = jax.custom_vjp(lambda h, w, w16: _proj_fwd(h, w, w16)[0])


def _proj_bwd(res, gs):
    h16, w16 = res
    dh = None
    dws = []
    for (s, n), g in zip(_PROJ_GROUPS, gs):
        dh = _mm_call("proj_nt", g, w16[:, s:s + n], "nt", dh)
        dws.append(_mm_call("proj_tn", h16, g, "tn"))
    return dh, jnp.concatenate(dws, axis=1), jnp.zeros_like(w16)


proj.defvjp(_proj_fwd, _proj_bwd)


def _cumsum_call(x, reverse):
    s, w = x.shape
    tm = _pick(s, 256, SUBLANES)
    nb = s // tm

    def body(x_ref, o_ref, carry):
        @pl.when(pl.program_id(0) == 0)
        def _():
            carry[...] = jnp.zeros_like(carry)

        blk = x_ref[...]
        r, c = _iota2(tm, tm, 0), _iota2(tm, tm, 1)
        tri = jnp.where((r <= c) if reverse else (r >= c), 1.0, 0.0).astype(F32)
        o_ref[...] = _dotf(tri, blk, NN) + carry[...]
        carry[...] += jnp.sum(blk, axis=0, keepdims=True)

    idx = (lambda i: (nb - 1 - i, 0)) if reverse else (lambda i: (i, 0))
    return _pcall(body, name="cumsum_rev" if reverse else "cumsum", out_shape=jax.ShapeDtypeStruct((s, w), F32),
                  grid=(nb,), in_specs=[pl.BlockSpec((tm, w), idx)], out_specs=pl.BlockSpec((tm, w), idx),
                  scratch_shapes=[pltpu.VMEM((1, w), F32)])(x)


@jax.custom_vjp
def seq_cumsum(x):
    return _cumsum_call(x, False)


seq_cumsum.defvjp(lambda x: (_cumsum_call(x, False), None), lambda _, g: (_cumsum_call(g, True),))


HALO = SUBLANES


class ConvOp:
    def __init__(self, name, width, post, c_pieces, out_widths, has_bias, tm):
        self.name, self.width, self.post, self.c_pieces = name, width, post, c_pieces
        self.out_widths, self.has_bias, self.tm = out_widths, has_bias, tm
        op = jax.custom_vjp(self._fwd_call)
        op.defvjp(lambda *a: (self._fwd_call(*a), a), lambda res, g: self._bwd_call(res, g))
        self.op = op

    def __call__(self, *args):
        return self.op(*args)

    def _conv(self, i, x_ref, prev_ref, w_ref, b_ref, buf):
        tm = x_ref.shape[0]
        buf[0:HALO, :] = jnp.where(i > 0, prev_ref[...], 0.0)
        buf[HALO:HALO + tm, :] = x_ref[...]
        taps = [buf[pl.ds(HALO - (self.width - 1) + j, tm), :] for j in range(self.width)]
        c = taps[0] * w_ref[0:1, :]
        for j in range(1, self.width):
            c = c + taps[j] * w_ref[j:j + 1, :]
        if self.has_bias:
            c = c + b_ref[...]
        return c, taps

    def _fwd_call(self, x, w, *bias):
        s, ch = x.shape
        tm = _pick(s, self.tm, SUBLANES)
        r8 = tm // HALO
        has_bias, post, c_pieces = self.has_bias, self.post, self.c_pieces

        def body(*refs):
            x_ref, prev_ref, w_ref = refs[:3]
            b_ref = refs[3] if has_bias else None
            outs, buf = refs[3 + has_bias:-1], refs[-1]
            c, _ = self._conv(pl.program_id(0), x_ref, prev_ref, w_ref, b_ref, buf)
            ys = post(*[c[:, o:o + n] for o, n in c_pieces])
            for r, y in zip(outs, ys):
                r[...] = y

        outs = _pcall(
            body, name=self.name + "_fwd", out_shape=[jax.ShapeDtypeStruct((s, n), F32) for n in self.out_widths],
            grid=(s // tm,),
            in_specs=[pl.BlockSpec((tm, ch), lambda i: (i, 0)),
                      pl.BlockSpec((HALO, ch), lambda i: (jnp.maximum(i * r8 - 1, 0), 0)),
                      pl.BlockSpec(w.shape, lambda i: (0, 0))]
            + ([pl.BlockSpec((1, ch), lambda i: (0, 0))] if has_bias else []),
            out_specs=[pl.BlockSpec((tm, n), lambda i: (i, 0)) for n in self.out_widths],
            scratch_shapes=[pltpu.VMEM((tm + HALO, ch), F32)],
        )(x, x, w, *bias)
        return tuple(outs) if len(outs) > 1 else outs[0]

    def _bwd_call(self, res, g):
        x, w = res[0], res[1]
        bias = res[2:]
        gs = tuple(g) if len(self.out_widths) > 1 else (g,)
        s, ch = x.shape
        tm = _pick(s, self.tm, SUBLANES)
        r8 = tm // HALO
        nb = s // tm
        has_bias, post, c_pieces, width = self.has_bias, self.post, self.c_pieces, self.width
        ng = len(gs)

        def body1(*refs):
            x_ref, prev_ref, w_ref = refs[:3]
            b_ref = refs[3] if has_bias else None
            k = 3 + has_bias
            g_refs = refs[k:k + ng]
            dc_ref, dw_ref = refs[k + ng], refs[k + ng + 1]
            db_ref = refs[k + ng + 2] if has_bias else None
            buf = refs[-1]
            i = pl.program_id(0)
            c, taps = self._conv(i, x_ref, prev_ref, w_ref, b_ref, buf)
            _, vjp = jax.vjp(lambda *a: tuple(post(*a)), *[c[:, o:o + n] for o, n in c_pieces])
            dcs = vjp(tuple(r[...] for r in g_refs))
            for (o, n), d in zip(c_pieces, dcs):
                dc_ref[:, o:o + n] = d

            @pl.when(i == 0)
            def _():
                dw_ref[...] = jnp.zeros_like(dw_ref)
                if has_bias:
                    db_ref[...] = jnp.zeros_like(db_ref)

            dc = dc_ref[...]
            for j in range(width):
                dw_ref[j:j + 1, :] += jnp.sum(dc * taps[j], axis=0, keepdims=True)
            if has_bias:
                db_ref[...] += jnp.sum(dc, axis=0, keepdims=True)

        outs1 = _pcall(
            body1, name=self.name + "_bwd_act",
            out_shape=[jax.ShapeDtypeStruct((s, ch), F32), jax.ShapeDtypeStruct(w.shape, F32)]
            + ([jax.ShapeDtypeStruct((1, ch), F32)] if has_bias else []),
            grid=(nb,),
            in_specs=[pl.BlockSpec((tm, ch), lambda i: (i, 0)),
                      pl.BlockSpec((HALO, ch), lambda i: (jnp.maximum(i * r8 - 1, 0), 0)),
                      pl.BlockSpec(w.shape, lambda i: (0, 0))]
            + ([pl.BlockSpec((1, ch), lambda i: (0, 0))] if has_bias else [])
            + [pl.BlockSpec((tm, n), lambda i: (i, 0)) for n in self.out_widths],
            out_specs=[pl.BlockSpec((tm, ch), lambda i: (i, 0)), pl.BlockSpec(w.shape, lambda i: (0, 0))]
            + ([pl.BlockSpec((1, ch), lambda i: (0, 0))] if has_bias else []),
            scratch_shapes=[pltpu.VMEM((tm + HALO, ch), F32)],
        )(x, x, w, *bias, *gs)
        dc, dw = outs1[0], outs1[1]

        def body2(dc_ref, next_ref, w_ref, dx_ref, buf):
            i = pl.program_id(0)
            buf[0:tm, :] = dc_ref[...]
            buf[tm:tm + HALO, :] = jnp.where(i < nb - 1, next_ref[...], 0.0)
            dx = buf[pl.ds(width - 1, tm), :] * w_ref[0:1, :]
            for j in range(1, width):
                dx = dx + buf[pl.ds(width - 1 - j, tm), :] * w_ref[j:j + 1, :]
            dx_ref[...] = dx

        dx = _pcall(
            body2, name=self.name + "_bwd_in", out_shape=jax.ShapeDtypeStruct((s, ch), F32), grid=(nb,),
            in_specs=[pl.BlockSpec((tm, ch), lambda i: (i, 0)),
                      pl.BlockSpec((HALO, ch), lambda i: (jnp.minimum((i + 1) * r8, s // HALO - 1), 0)),
                      pl.BlockSpec(w.shape, lambda i: (0, 0))],
            out_specs=pl.BlockSpec((tm, ch), lambda i: (i, 0)),
            scratch_shapes=[pltpu.VMEM((tm + HALO, ch), F32)],
        )(dc, dc, w)
        return (dx, dw) + ((outs1[2],) if has_bias else ())


def _make_convops():
    return dict(
        gdn=ConvOp("gdnconv", 4, lambda q, k, v: (_silu(q), _silu(k), _silu(v)), _split(3 * HW, 3), [HW] * 3,
                   False, 256),
        ffn=ConvOp("ffnconv", 3, lambda a, b: (_silu(a) * b,), _split(2 * D_FF, 2), [D_FF], True, 256),
    )


ATT_Q = 512
ATT_K = 256
ATT_K_FOX = 512
SCALE = HEAD_DIM ** -0.5


ATT_Q_SB = 1024


def _att_tiles(s, fox=False):
    tk = _pick(s, ATT_K_FOX if fox else ATT_K, LANES)
    tq = _pick(s, ATT_Q if fox else ATT_Q_SB, tk)
    return tq, tk


def _att_specs(s, tq, tk):
    qspec = pl.BlockSpec((tq, HEAD_DIM), lambda h, i: (i, h))
    kspec = pl.BlockSpec((s, HEAD_DIM), lambda h, i: (0, h))
    colspec = pl.BlockSpec((None, tq, 1), lambda h, i: (h, i, 0))
    rowspec = pl.BlockSpec((None, s // tk, 1, tk), lambda h, i: (h, 0, 0, 0))
    return qspec, kspec, colspec, rowspec


def _krows(kb, tk):
    return pl.ds(pl.multiple_of(kb * tk, tk), tk)


def _stage_bf16(i, pairs):
    @pl.when(i == 0)
    def _():
        for src, dst in pairs:
            dst[...] = src[...].astype(BF16)


ATT_STRIP = 32


def _strips(tq):
    return [slice(r, r + ATT_STRIP) for r in range(0, tq, ATT_STRIP)]


def _visible(i, kb, rs, tq, tk, strict):
    rows = i * tq + rs.start + _iota2(ATT_STRIP, tk, 0)
    cols = kb * tk + _iota2(ATT_STRIP, tk, 1)
    return (cols < rows) if strict else (cols <= rows)


def _visible_block(i, kb, tq, tk):
    return kb * tk + _iota2(tq, tk, 1) <= i * tq + _iota2(tq, tk, 0)


def _blocks(i, ratio, blk, reverse=False):
    def full(n, carry):
        blk(i * ratio - 1 - n if reverse else n, False)
        return carry

    if reverse:
        for j in reversed(range(ratio)):
            blk(i * ratio + j, True)
    lax.fori_loop(0, i * ratio, full, 0)
    if not reverse:
        for j in range(ratio):
            blk(i * ratio + j, True)


def _vm(shape, dtype):
    return pltpu.VMEM(shape, dtype)


def _fox_fwd_call(q, k, v, ccol, crow):
    s = q.shape[0]
    tq, tk = _att_tiles(s, fox=True)
    qspec, kspec, colspec, rowspec = _att_specs(s, tq, tk)

    def body(q_ref, k_ref, v_ref, cq_ref, ck_ref, o_ref, lse_ref, k16, v16):
        i = pl.program_id(1)
        ratio = tq // tk
        _stage_bf16(i, [(k_ref, k16), (v_ref, v16)])
        qb = q_ref[...].astype(BF16)
        cq = cq_ref[...]

        def blk(kb, carry, masked):
            m, l, acc = carry
            sc = _dot(qb, k16[_krows(kb, tk), :], NT) * SCALE + (cq - ck_ref[kb])
            if masked:
                sc = jnp.where(_visible_block(i, kb, tq, tk), sc, NEG)
            m_new = jnp.maximum(m, jnp.max(sc, axis=-1, keepdims=True))
            alpha = jnp.exp(m - m_new)
            p = jnp.exp(sc - m_new)
            return (m_new, alpha * l + jnp.sum(p, axis=-1, keepdims=True),
                    alpha * acc + _dot(p, v16[_krows(kb, tk), :], NN))

        carry = (jnp.full((tq, 1), NEG, F32), jnp.zeros((tq, 1), F32), jnp.zeros((tq, HEAD_DIM), F32))
        carry = lax.fori_loop(0, i * ratio, lambda kb, c: blk(kb, c, False), carry)
        for j in range(ratio):
            carry = blk(i * ratio + j, carry, True)
        m, l, acc = carry
        o_ref[...] = acc / l
        lse_ref[...] = m + jnp.log(l)

    return _pcall(body, name="fox_fwd",
                  out_shape=[jax.ShapeDtypeStruct((s, HW), F32), jax.ShapeDtypeStruct((HEADS, s, 1), F32)],
                  grid=(HEADS, s // tq), in_specs=[qspec, kspec, kspec, colspec, rowspec],
                  out_specs=[qspec, colspec],
                  scratch_shapes=[_vm((s, HEAD_DIM), BF16), _vm((s, HEAD_DIM), BF16)])(q, k, v, ccol, crow)


def _fox_bwd_call(q, k, v, ccol, crow, o, lse, do):
    s = q.shape[0]
    tq, tk = _att_tiles(s, fox=True)
    ratio = tq // tk
    qspec, kspec, colspec, rowspec = _att_specs(s, tq, tk)

    def body(q_ref, k_ref, v_ref, cq_ref, ck_ref, o_ref, lse_ref, do_ref, dq_ref, dk_ref, dv_ref, dcq_ref, dck_ref,
             k16, v16):
        i = pl.program_id(1)
        _stage_bf16(i, [(k_ref, k16), (v_ref, v16)])

        @pl.when(i == 0)
        def _():
            dk_ref[...] = jnp.zeros_like(dk_ref)
            dv_ref[...] = jnp.zeros_like(dv_ref)
            dck_ref[...] = jnp.zeros_like(dck_ref)

        qb = q_ref[...].astype(BF16)
        dob = do_ref[...].astype(BF16)
        cq, lse = cq_ref[...], lse_ref[...]
        dl = jnp.sum(do_ref[...] * o_ref[...], axis=-1, keepdims=True)

        def blk(kb, carry, masked):
            dq, dcq = carry
            rows = _krows(kb, tk)
            kk, vv = k16[rows, :], v16[rows, :]
            sc = _dot(qb, kk, NT) * SCALE + (cq - ck_ref[kb])
            p = jnp.exp(sc - lse)
            if masked:
                p = jnp.where(_visible_block(i, kb, tq, tk), p, 0.0)
            dv_ref[rows, :] += _dot(p, dob, TN)
            ds = p * (_dot(dob, vv, NT) - dl)
            dk_ref[rows, :] += _dot(ds, qb, TN) * SCALE
            dck_ref[kb] += -jnp.sum(ds, axis=0, keepdims=True)
            return dq + _dot(ds, kk, NN) * SCALE, dcq + jnp.sum(ds, axis=-1, keepdims=True)

        carry = (jnp.zeros((tq, HEAD_DIM), F32), jnp.zeros((tq, 1), F32))
        carry = lax.fori_loop(0, i * ratio, lambda kb, c: blk(kb, c, False), carry)
        for j in range(ratio):
            carry = blk(i * ratio + j, carry, True)
        dq_ref[...] = carry[0]
        dcq_ref[...] = carry[1]

    return _pcall(body, name="fox_bwd",
                  out_shape=[jax.ShapeDtypeStruct((s, HW), F32)] * 3
                  + [jax.ShapeDtypeStruct((HEADS, s, 1), F32), jax.ShapeDtypeStruct((HEADS, s // tk, 1, tk), F32)],
                  grid=(HEADS, s // tq),
                  in_specs=[qspec, kspec, kspec, colspec, rowspec, qspec, colspec, qspec],
                  out_specs=[qspec, kspec, kspec, colspec, rowspec],
                  scratch_shapes=[_vm((s, HEAD_DIM), BF16), _vm((s, HEAD_DIM), BF16)],
                  )(q, k, v, ccol, crow, o, lse, do)


@jax.custom_vjp
def fox_attention(q, k, v, ccol, crow):
    return _fox_fwd_call(q, k, v, ccol, crow)[0]


def _fox_vjp_fwd(q, k, v, ccol, crow):
    o, lse = _fox_fwd_call(q, k, v, ccol, crow)
    return o, (q, k, v, ccol, crow, o, lse)


fox_attention.defvjp(_fox_vjp_fwd, lambda res, g: tuple(_fox_bwd_call(*res, g)))


def _sb_fwd_call(q, k, v):
    s = q.shape[0]
    tq, tk = _att_tiles(s)
    ratio = tq // tk
    qspec, kspec, colspec, _ = _att_specs(s, tq, tk)

    def body(q_ref, k_ref, v_ref, o_ref, tot_ref, k16, v16, q16, s_scr, w_scr, lk16, a16, run, acc):
        i = pl.program_id(1)
        _stage_bf16(i, [(k_ref, k16), (v_ref, v16)])
        q16[...] = q_ref[...].astype(BF16)
        run[...] = jnp.zeros_like(run)
        acc[...] = jnp.zeros_like(acc)
        suffix = jnp.where(_iota2(tk, tk, 0) >= _iota2(tk, tk, 1), 1.0, 0.0).astype(BF16)

        def blk(kb, masked):
            rows = _krows(kb, tk)
            s_scr[...] = _dot(q16[...], k16[rows, :], NT)
            for rs in _strips(tq):
                lk = -_softplus(s_scr[rs, :] * SCALE)
                if masked:
                    lk = jnp.where(_visible(i, kb, rs, tq, tk, True), lk, 0.0)
                lk16[rs, :] = lk.astype(BF16)
            w_scr[...] = _dot(lk16[...], suffix, NN)
            for rs in _strips(tq):
                a = jnp.exp(s_scr[rs, :] * SCALE + w_scr[rs, :] + run[rs, :])
                if masked:
                    a = jnp.where(_visible(i, kb, rs, tq, tk, True), a, 0.0)
                a16[rs, :] = a.astype(BF16)
                run[rs, :] += w_scr[rs, 0:1]
            acc[...] += _dot(a16[...], v16[rows, :], NN)

        _blocks(i, ratio, blk, reverse=True)
        o_ref[...] = acc[...]
        tot_ref[...] = run[...]

    return _pcall(body, name="sb_fwd",
                  out_shape=[jax.ShapeDtypeStruct((s, HW), F32), jax.ShapeDtypeStruct((HEADS, s, 1), F32)],
                  grid=(HEADS, s // tq), in_specs=[qspec, kspec, kspec], out_specs=[qspec, colspec],
                  scratch_shapes=[_vm((s, HEAD_DIM), BF16), _vm((s, HEAD_DIM), BF16), _vm((tq, HEAD_DIM), BF16),
                                  _vm((tq, tk), F32), _vm((tq, tk), F32), _vm((tq, tk), BF16), _vm((tq, tk), BF16),
                                  _vm((tq, 1), F32), _vm((tq, HEAD_DIM), F32)])(q, k, v)


def _sb_bwd_call(q, k, v, tot, do):
    s = q.shape[0]
    tq, tk = _att_tiles(s)
    ratio = tq // tk
    qspec, kspec, colspec, _ = _att_specs(s, tq, tk)

    def body(q_ref, k_ref, v_ref, tot_ref, do_ref, dq_ref, dk_ref, dv_ref, k16, v16, q16, do16, s_scr, e_scr, w_scr,
             lz16, a16, e16, left, esum):
        i = pl.program_id(1)
        _stage_bf16(i, [(k_ref, k16), (v_ref, v16)])

        @pl.when(i == 0)
        def _():
            dk_ref[...] = jnp.zeros_like(dk_ref)
            dv_ref[...] = jnp.zeros_like(dv_ref)

        q16[...] = q_ref[...].astype(BF16)
        do16[...] = do_ref[...].astype(BF16)
        left[...] = jnp.zeros_like(left)
        esum[...] = jnp.zeros_like(esum)
        dq_ref[...] = jnp.zeros_like(dq_ref)
        prefix = jnp.where(_iota2(tk, tk, 0) <= _iota2(tk, tk, 1), 1.0, 0.0).astype(BF16)

        def blk(kb, masked):
            rows = _krows(kb, tk)
            s_scr[...] = _dot(q16[...], k16[rows, :], NT)
            e_scr[...] = _dot(do16[...], v16[rows, :], NT)
            for rs in _strips(tq):
                lk = -_softplus(s_scr[rs, :] * SCALE)
                if masked:
                    lk = jnp.where(_visible(i, kb, rs, tq, tk, True), lk, 0.0)
                lz16[rs, :] = lk.astype(BF16)
            w_scr[...] = _dot(lz16[...], prefix, NN)
            for rs in _strips(tq):
                rc = (tot_ref[rs, :] - left[rs, :]) - (w_scr[rs, :] - lz16[rs, :].astype(F32))
                a = jnp.exp(s_scr[rs, :] * SCALE + rc)
                if masked:
                    a = jnp.where(_visible(i, kb, rs, tq, tk, True), a, 0.0)
                e = a * e_scr[rs, :]
                a16[rs, :] = a.astype(BF16)
                e16[rs, :] = e.astype(BF16)
                e_scr[rs, :] = e
                left[rs, :] += w_scr[rs, tk - 1:tk]
            w_scr[...] = _dot(e16[...], prefix, NN)
            for rs in _strips(tq):
                dz = e_scr[rs, :] - _sigmoid(s_scr[rs, :] * SCALE) * (esum[rs, :] + w_scr[rs, :])
                if masked:
                    dz = jnp.where(_visible(i, kb, rs, tq, tk, True), dz, 0.0)
                lz16[rs, :] = dz.astype(BF16)
                esum[rs, :] += w_scr[rs, tk - 1:tk]
            dv_ref[rows, :] += _dot(a16[...], do16[...], TN)
            dk_ref[rows, :] += _dot(lz16[...], q16[...], TN) * SCALE
            dq_ref[...] += _dot(lz16[...], k16[rows, :], NN) * SCALE

        _blocks(i, ratio, blk)

    return _pcall(body, name="sb_bwd", out_shape=[jax.ShapeDtypeStruct((s, HW), F32)] * 3, grid=(HEADS, s // tq),
                  in_specs=[qspec, kspec, kspec, colspec, qspec], out_specs=[qspec, kspec, kspec],
                  scratch_shapes=[_vm((s, HEAD_DIM), BF16), _vm((s, HEAD_DIM), BF16), _vm((tq, HEAD_DIM), BF16),
                                  _vm((tq, HEAD_DIM), BF16), _vm((tq, tk), F32), _vm((tq, tk), F32),
                                  _vm((tq, tk), F32), _vm((tq, tk), BF16), _vm((tq, tk), BF16), _vm((tq, tk), BF16),
                                  _vm((tq, 1), F32), _vm((tq, 1), F32)])(q, k, v, tot, do)


@jax.custom_vjp
def sb_attention(q, k, v):
    return _sb_fwd_call(q, k, v)[0]


def _sb_vjp_fwd(q, k, v):
    o, tot = _sb_fwd_call(q, k, v)
    return o, (q, k, v, tot)


sb_attention.defvjp(_sb_vjp_fwd, lambda res, g: tuple(_sb_bwd_call(*res, g)))


def _mem_specs(s, nk, t):
    return (pl.BlockSpec((t, HEAD_DIM), lambda h, i: (i, h)), pl.BlockSpec((nk, HEAD_DIM), lambda h, i: (0, h)))


def _mem_probs(qb, kk):
    sc = _dot(qb, kk, NT) * SCALE
    p = jnp.exp(sc - jnp.max(sc, axis=-1, keepdims=True))
    return p / jnp.sum(p, axis=-1, keepdims=True)


def _mem_fwd_call(q, k, v):
    s, nk = q.shape[0], k.shape[0]
    t = _pick(s, 512, SUBLANES)
    qspec, kspec = _mem_specs(s, nk, t)

    def body(q_ref, k_ref, v_ref, o_ref):
        o_ref[...] = _dot(_mem_probs(q_ref[...].astype(BF16), k_ref[...]), v_ref[...], NN)

    return _pcall(body, name="mem_fwd", out_shape=jax.ShapeDtypeStruct((s, HW), F32), grid=(HEADS, s // t),
                  in_specs=[qspec, kspec, kspec], out_specs=qspec)(q, k, v)


def _mem_bwd_call(q, k, v, do):
    s, nk = q.shape[0], k.shape[0]
    t = _pick(s, 512, SUBLANES)
    qspec, kspec = _mem_specs(s, nk, t)

    def body(q_ref, k_ref, v_ref, do_ref, dq_ref, dk_ref, dv_ref):
        @pl.when(pl.program_id(1) == 0)
        def _():
            dk_ref[...] = jnp.zeros_like(dk_ref)
            dv_ref[...] = jnp.zeros_like(dv_ref)

        qb = q_ref[...].astype(BF16)
        dob = do_ref[...].astype(BF16)
        p = _mem_probs(qb, k_ref[...])
        dv_ref[...] += _dot(p, dob, TN)
        dp = _dot(dob, v_ref[...], NT)
        ds = p * (dp - jnp.sum(p * dp, axis=-1, keepdims=True))
        dq_ref[...] = _dot(ds, k_ref[...], NN) * SCALE
        dk_ref[...] += _dot(ds, qb, TN) * SCALE

    return _pcall(body, name="mem_bwd",
                  out_shape=[jax.ShapeDtypeStruct((s, HW), F32)] + [jax.ShapeDtypeStruct((nk, HW), F32)] * 2,
                  grid=(HEADS, s // t), in_specs=[qspec, kspec, kspec, qspec],
                  out_specs=[qspec, kspec, kspec])(q, k, v, do)


@jax.custom_vjp
def mem_attention(q, k, v):
    return _mem_fwd_call(q, k, v)


mem_attention.defvjp(lambda q, k, v: (_mem_fwd_call(q, k, v), (q, k, v)),
                     lambda res, g: tuple(_mem_bwd_call(*res, g)))


BNN = (((2,), (1,)), ((0,), (0,)))
BNT = (((2,), (2,)), ((0,), (0,)))
BTN = (((1,), (1,)), ((0,), (0,)))


def _bdot(a, b, dn):
    return lax.dot_general(a.astype(BF16), b.astype(BF16), dn, preferred_element_type=F32)


def _bdotf(a, b, dn):
    return lax.dot_general(a, b, dn, precision=lax.Precision.HIGHEST, preferred_element_type=F32)


@jax.custom_vjp
def bmm(a, b):
    return _bdot(a, b, BNN)


bmm.defvjp(lambda a, b: (_bdot(a, b, BNN), (a, b)), lambda r, g: (_bdot(g, r[1], BNT), _bdot(r[0], g, BTN)))


@jax.custom_vjp
def bmm_nt(a, b):
    return _bdot(a, b, BNT)


bmm_nt.defvjp(lambda a, b: (_bdot(a, b, BNT), (a, b)), lambda r, g: (_bdot(g, r[1], BNN), _bdot(g, r[0], BTN)))


@jax.custom_vjp
def bmm_tn(a, b):
    return _bdot(a, b, BTN)


bmm_tn.defvjp(lambda a, b: (_bdot(a, b, BTN), (a, b)), lambda r, g: (_bdot(r[1], g, BNT), _bdot(r[0], g, BNN)))


def _unit_lower_inverse(nm):
    eye = jnp.where(_iota2(CHUNK, CHUNK, 0) == _iota2(CHUNK, CHUNK, 1), 1.0, 0.0).astype(F32)[None]
    p = eye - nm
    m = nm
    for _ in range(5):
        m = _bdotf(m, m, BNN)
        p = _bdotf(p, eye + m, BNN)
    return p


@jax.custom_vjp
def _solve2(nm, r1, r2):
    inv = _unit_lower_inverse(nm)
    return _bdotf(inv, r1, BNN), _bdotf(inv, r2, BNN)


def _solve2_fwd(nm, r1, r2):
    inv = _unit_lower_inverse(nm)
    u, w = _bdotf(inv, r1, BNN), _bdotf(inv, r2, BNN)
    return (u, w), (inv, u, w)


def _solve2_bwd(res, g):
    inv, u, w = res
    d1, d2 = _bdotf(inv, g[0], BTN), _bdotf(inv, g[1], BTN)
    return -(_bdotf(d1, u, BNT) + _bdotf(d2, w, BNT)), d1, d2


_solve2.defvjp(_solve2_fwd, _solve2_bwd)


def _gdn_chunk(q, k, v, gcc, gcr, b, gl, st):
    qn = q * lax.rsqrt(jnp.sum(q * q, axis=-1, keepdims=True) + EPS) * SCALE
    kn = k * lax.rsqrt(jnp.sum(k * k, axis=-1, keepdims=True) + EPS)
    r, c = _iota2(CHUNK, CHUNK, 0)[None], _iota2(CHUNK, CHUNK, 1)[None]
    decay = jnp.exp(jnp.where(r >= c, gcc - gcr, NEG))
    nm = jnp.where(r > c, b * bmm_nt(kn, kn) * decay, 0.0)
    eg = jnp.exp(gcc)
    u, w = _solve2(nm, v * b, kn * (b * eg))
    attn = bmm_nt(qn, kn) * decay
    v_new = u - bmm(w, st)
    o = bmm(qn * eg, st) + bmm(attn, v_new)
    st_new = st * jnp.exp(gl) + bmm_tn(kn * jnp.exp(gl - gcc), v_new)
    return o, st_new


def _heads_of(ref, rows):
    return jnp.stack([ref[rows, _head_cols(h)] for h in range(HEADS)])


def _head_cols(h):
    return slice(h * HEAD_DIM, (h + 1) * HEAD_DIM)


GDN_ROWS = 512


def _gdn_specs(s, tg, rev):
    nb = s // tg
    cpb = tg // CHUNK
    j_of = (lambda j: nb - 1 - j) if rev else (lambda j: j)
    qspec = pl.BlockSpec((tg, HW), lambda j: (j_of(j), 0))
    colspec = pl.BlockSpec((HEADS, tg, 1), lambda j: (0, j_of(j), 0))
    rowspec = pl.BlockSpec((HEADS, cpb, 1, CHUNK), lambda j: (0, j_of(j), 0, 0))
    onespec = pl.BlockSpec((HEADS, cpb, 1, 1), lambda j: (0, j_of(j), 0, 0))
    stspec = pl.BlockSpec((HEADS, cpb, HEAD_DIM, HEAD_DIM), lambda j: (0, j_of(j), 0, 0))
    return qspec, colspec, rowspec, onespec, stspec


def _gdn_fwd_call(q, k, v, gcc, gcr, bc, gl):
    s = q.shape[0]
    tg = _pick(s, GDN_ROWS, CHUNK)
    cpb = tg // CHUNK
    qspec, colspec, rowspec, onespec, stspec = _gdn_specs(s, tg, False)

    def body(q_ref, k_ref, v_ref, gcc_ref, gcr_ref, b_ref, gl_ref, o_ref, st_ref, st):
        @pl.when(pl.program_id(0) == 0)
        def _():
            st[...] = jnp.zeros_like(st)

        def chunk(ci, _):
            rows = pl.ds(pl.multiple_of(ci * CHUNK, CHUNK), CHUNK)
            s_in = st[...]
            st_ref[:, ci] = s_in
            o, s_new = _gdn_chunk(_heads_of(q_ref, rows), _heads_of(k_ref, rows), _heads_of(v_ref, rows),
                                  gcc_ref[:, rows, :], gcr_ref[:, ci], b_ref[:, rows, :], gl_ref[:, ci], s_in)
            for h in range(HEADS):
                o_ref[rows, _head_cols(h)] = o[h]
            st[...] = s_new
            return 0

        lax.fori_loop(0, cpb, chunk, 0)

    return _pcall(body, name="gdn_fwd",
                  out_shape=[jax.ShapeDtypeStruct((s, HW), F32),
                             jax.ShapeDtypeStruct((HEADS, s // CHUNK, HEAD_DIM, HEAD_DIM), F32)],
                  grid=(s // tg,), in_specs=[qspec, qspec, qspec, colspec, rowspec, colspec, onespec],
                  out_specs=[qspec, stspec], scratch_shapes=[pltpu.VMEM((HEADS, HEAD_DIM, HEAD_DIM), F32)],
                  )(q, k, v, gcc, gcr, bc, gl)


def _gdn_bwd_call(q, k, v, gcc, gcr, bc, gl, states, do):
    s = q.shape[0]
    tg = _pick(s, GDN_ROWS, CHUNK)
    cpb = tg // CHUNK
    qspec, colspec, rowspec, onespec, stspec = _gdn_specs(s, tg, True)

    def body(q_ref, k_ref, v_ref, gcc_ref, gcr_ref, b_ref, gl_ref, st_ref, do_ref,
             dq_ref, dk_ref, dv_ref, dgcc_ref, dgcr_ref, db_ref, dgl_ref, dst):
        @pl.when(pl.program_id(0) == 0)
        def _():
            dst[...] = jnp.zeros_like(dst)

        def chunk(n, _):
            ci = cpb - 1 - n
            rows = pl.ds(pl.multiple_of(ci * CHUNK, CHUNK), CHUNK)
            _, vjp = jax.vjp(_gdn_chunk, _heads_of(q_ref, rows), _heads_of(k_ref, rows), _heads_of(v_ref, rows),
                             gcc_ref[:, rows, :], gcr_ref[:, ci], b_ref[:, rows, :], gl_ref[:, ci], st_ref[:, ci])
            dq, dk, dv, dgcc, dgcr, db, dgl, ds_in = vjp((_heads_of(do_ref, rows), dst[...]))
            for h in range(HEADS):
                cols = _head_cols(h)
                dq_ref[rows, cols] = dq[h]
                dk_ref[rows, cols] = dk[h]
                dv_ref[rows, cols] = dv[h]
            dgcc_ref[:, rows, :] = dgcc
            dgcr_ref[:, ci] = dgcr
            db_ref[:, rows, :] = db
            dgl_ref[:, ci] = dgl
            dst[...] = ds_in
            return 0

        lax.fori_loop(0, cpb, chunk, 0)

    n = s // CHUNK
    return _pcall(body, name="gdn_bwd",
                  out_shape=[jax.ShapeDtypeStruct((s, HW), F32)] * 3
                  + [jax.ShapeDtypeStruct((HEADS, s, 1), F32), jax.ShapeDtypeStruct((HEADS, n, 1, CHUNK), F32),
                     jax.ShapeDtypeStruct((HEADS, s, 1), F32), jax.ShapeDtypeStruct((HEADS, n, 1, 1), F32)],
                  grid=(s // tg,),
                  in_specs=[qspec, qspec, qspec, colspec, rowspec, colspec, onespec, stspec, qspec],
                  out_specs=[qspec, qspec, qspec, colspec, rowspec, colspec, onespec],
                  scratch_shapes=[pltpu.VMEM((HEADS, HEAD_DIM, HEAD_DIM), F32)],
                  )(q, k, v, gcc, gcr, bc, gl, states, do)


@jax.custom_vjp
def gated_delta(q, k, v, gcc, gcr, bc, gl):
    return _gdn_fwd_call(q, k, v, gcc, gcr, bc, gl)[0]


def _gdn_vjp_fwd(q, k, v, gcc, gcr, bc, gl):
    o, states = _gdn_fwd_call(q, k, v, gcc, gcr, bc, gl)
    return o, (q, k, v, gcc, gcr, bc, gl, states)


gated_delta.defvjp(_gdn_vjp_fwd, lambda res, g: tuple(_gdn_bwd_call(*res, g)))


def _loss_call(y, target):
    s, d = y.shape
    tm = _pick(s, 512, SUBLANES)

    def body(y_ref, t_ref, dy_ref, loss_ref):
        @pl.when(pl.program_id(0) == 0)
        def _():
            loss_ref[...] = jnp.zeros_like(loss_ref)

        err = y_ref[...] - t_ref[...]
        dy_ref[...] = err * (1.0 / d)
        loss_ref[...] += 0.5 * jnp.sum(jnp.mean(err * err, axis=-1, keepdims=True), axis=0, keepdims=True)

    dy, part = _pcall(body, name="loss_head",
                      out_shape=[jax.ShapeDtypeStruct((s, d), F32), jax.ShapeDtypeStruct((1, 1), F32)],
                      grid=(s // tm,), in_specs=[pl.BlockSpec((tm, d), lambda i: (i, 0))] * 2,
                      out_specs=[pl.BlockSpec((tm, d), lambda i: (i, 0)), pl.BlockSpec((1, 1), lambda i: (0, 0))],
                      )(y, target)
    return part[0, 0], dy


def _cols_and_rows(a, lane0, t):
    s = a.shape[0]
    at = a[:, lane0:lane0 + HEADS].T
    return at, at[:, :, None], at.reshape(HEADS, s // t, 1, t)


def _pad_lanes(v, lane0):
    return jnp.pad(v, (lane0, LANES - lane0 - v.shape[0])).reshape(1, LANES)


def _layer(x, mem, w, w16, ops, convs):
    s = x.shape[0]
    row = lambda v: v.reshape(1, -1)
    mw = lambda n: (w[n], w16[n])
    h = ops["rms"](x, row(w["norm_mix"]))
    fq, fk, fv, gqkv, gz, sq, sk, sv, gt, gm = proj(h, *mw("w_in"))

    logf, beta, gc = ops["small"](gm, _pad_lanes(w["fox_fbias"], LANE_FF), _pad_lanes(w["gdn_a_log"], LANE_GA),
                                  _pad_lanes(w["gdn_dt_bias"], LANE_GA))
    _, ccol, crow = _cols_and_rows(seq_cumsum(logf), LANE_FF, _att_tiles(s, fox=True)[1])
    ya = fox_attention(ops["headnorm"](fq, row(w["fox_qnorm"])), ops["headnorm"](fk, row(w["fox_knorm"])), fv,
                       ccol, crow)
    cq, ck, cv = convs["gdn"](gqkv, w["gdn_conv"])
    gct, gcc, gcr = _cols_and_rows(gc, LANE_GA, CHUNK)
    gl = gct.reshape(HEADS, s // CHUNK, CHUNK)[:, :, CHUNK - 1].reshape(HEADS, s // CHUNK, 1, 1)
    bc = beta[:, LANE_GB:LANE_GB + HEADS].T[:, :, None]
    yb = ops["gdnpost"](gated_delta(cq, ck, cv, gcc, gcr, bc, gl), gz, row(w["gdn_onorm"]))
    yc = sb_attention(sq, sk, sv)
    gb = w["gate_bias"]
    mixed = ops["merge"](gt, matmul(ya, *mw("w_oa")), matmul(yb, *mw("w_ob")), matmul(yc, *mw("w_oc")),
                         row(gb[:D_MODEL]), row(gb[D_MODEL:2 * D_MODEL]), row(gb[2 * D_MODEL:]))
    x = matmul_add(x, mixed, *mw("w_out"))
    mq = ops["headnorm"](matmul(ops["rms"](x, row(w["norm_xq"])), *mw("w_mq")), row(w["mq_norm"]))
    kv = matmul(ops["rms"](mem, row(w["norm_mem"])), *mw("w_mkv"))
    mk = ops["headnorm"](kv[:, :HW], row(w["mk_norm"]))
    x = matmul_add(x, mem_attention(mq, mk, kv[:, HW:]), *mw("w_mo"))
    u = matmul(ops["rms"](x, row(w["norm_ffn"])), *mw("w_up"))
    act = convs["ffn"](u, w["ffn_conv"], row(w["ffn_conv_b"]))
    return matmul_add(x, act, *mw("w_down"))


def _forward(x, mem, layers, layers16):
    ops, convs = _make_rowops(), _make_convops()
    for w, w16 in zip(layers, layers16):
        x = _layer(x, mem, w, w16, ops, convs)
    return x


ANY = pl.BlockSpec(memory_space=pl.ANY)


N_PEERS = N_DEV - 1


def _ccall(body, *, name, out_shape, n_arrays):
    return pl.pallas_call(body, name=name, out_shape=out_shape, in_specs=[ANY] * n_arrays,
                          out_specs=[ANY] * n_arrays,
                          scratch_shapes=[pltpu.SemaphoreType.DMA((N_PEERS * n_arrays,)),
                                          pltpu.SemaphoreType.DMA((N_PEERS * n_arrays,)),
                                          pltpu.SemaphoreType.DMA((n_arrays,))],
                          interpret=False)


def _all_gather(name, shards):
    n = len(shards)

    def body(*refs):
        x_refs, out_refs = refs[:n], refs[n:2 * n]
        send_sems, recv_sems, local_sems = refs[2 * n:]
        x, y, c = lax.axis_index("x"), lax.axis_index("y"), lax.axis_index("c")
        me, sibling = (x, y, c), (x, y, 1 - c)
        chips = [(1 - x, y), (x, 1 - y), (1 - x, 1 - y)]

        def slot(a, px, py, pc):
            return out_refs[a].at[4 * px + 2 * py + pc]

        def copy(a, k, block, to, src=None):
            return pltpu.make_async_remote_copy(
                src_ref=slot(a, *block) if src is None else src, dst_ref=slot(a, *block),
                send_sem=send_sems.at[N_PEERS * a + k], recv_sem=recv_sems.at[N_PEERS * a + k], device_id=to,
                device_id_type=MESH)

        mine = [pltpu.make_async_copy(x_refs[a], slot(a, *me), local_sems.at[a]) for a in range(n)]
        first = []
        for a in range(n):
            first.append(copy(a, 0, me, sibling, src=x_refs[a]))
            first += [copy(a, 1 + j, me, (*chip, c), src=x_refs[a]) for j, chip in enumerate(chips)]
        for cp in mine + first:
            cp.start()
        passed = []
        for j, chip in enumerate(chips):
            for a in range(n):
                copy(a, 1 + j, (*chip, c), me).wait_recv()
                passed.append(copy(a, 4 + j, (*chip, c), sibling))
                passed[-1].start()
        for a in range(n):
            copy(a, 0, sibling, me).wait_recv()
            for j, chip in enumerate(chips):
                copy(a, 4 + j, (*chip, 1 - c), me).wait_recv()
        for cp in first + passed:
            cp.wait_send()
        for cp in mine:
            cp.wait()

    return _ccall(body, name=name, out_shape=[jax.ShapeDtypeStruct((N_DEV,) + s.shape, s.dtype) for s in shards],
                  n_arrays=n)(*shards)


SEM = pl.BlockSpec(memory_space=pltpu.SEMAPHORE)
HBM = pl.BlockSpec(memory_space=pltpu.HBM)
DATAFLOW = pltpu.SideEffectType.DATAFLOW_SIDE_EFFECTING


def _exchange_copies(p_refs, land_refs, send_sem, recv_sem):
    x, y, c = lax.axis_index("x"), lax.axis_index("y"), lax.axis_index("c")
    copies = []
    for p_ref, land_ref in zip(p_refs, land_refs):
        for k in range(1, N_DEV):
            px, py, pc = x ^ ((k >> 2) & 1), y ^ ((k >> 1) & 1), c ^ (k & 1)
            copies.append(pltpu.make_async_remote_copy(
                src_ref=p_ref.at[4 * px + 2 * py + pc], dst_ref=land_ref.at[k - 1], send_sem=send_sem,
                recv_sem=recv_sem, device_id=(px, py, pc), device_id_type=MESH))
    return copies


def _exchange_start(name, parts):
    n = len(parts)
    lands = [lax.empty((N_PEERS,) + p.shape[1:], p.dtype) for p in parts]

    def body(*refs):
        send_sem, recv_sem = refs[2 * n], refs[2 * n + 1]
        for cp in _exchange_copies(refs[:n], refs[n:2 * n], send_sem, recv_sem):
            cp.start()
        refs[-1][...] = jnp.zeros_like(refs[-1])

    outs = pl.pallas_call(
        body, name=name,
        out_shape=(pltpu.SemaphoreType.DMA(()), pltpu.SemaphoreType.DMA(()),
                   *[pltpu.HBM(a.shape, a.dtype) for a in parts + lands],
                   jax.ShapeDtypeStruct((SUBLANES, LANES), F32)),
        in_specs=[HBM] * (2 * n), out_specs=(SEM, SEM, *[HBM] * (2 * n), pl.BlockSpec(memory_space=pltpu.VMEM)),
        input_output_aliases={i: 2 + i for i in range(2 * n)},
        compiler_params=pltpu.CompilerParams(has_side_effects=DATAFLOW),
        interpret=False)(*[pltpu.with_memory_space_constraint(a, pltpu.HBM) for a in parts + lands])
    return (outs[0], outs[1], list(outs[2:2 + n]), list(outs[2 + n:2 + 2 * n])), outs[-1]


def _exchange_wait(name, send_sem, recv_sem, parts, lands, after):
    n = len(parts)

    def body(*refs):
        for cp in _exchange_copies(refs[:n], refs[n:2 * n], refs[2 * n], refs[2 * n + 1]):
            cp.wait_send()
            cp.wait_recv()

    outs = pl.pallas_call(
        body, name=name, out_shape=tuple(pltpu.HBM(a.shape, a.dtype) for a in parts + lands),
        in_specs=[HBM] * (2 * n) + [SEM, SEM, ANY], out_specs=[HBM] * (2 * n),
        input_output_aliases={i: i for i in range(2 * n)},
        compiler_params=pltpu.CompilerParams(has_side_effects=DATAFLOW),
        interpret=False)(*parts, *lands, send_sem, recv_sem, after)
    return list(outs[:n]), list(outs[n:])


ADAM_SLOT_BYTES = 4 * 1024 * 1024


def _adam_call(name, w, slots, m, v):
    r, n = w.shape
    rows_unit = 2 * SUBLANES
    tr = _pick(r, max(rows_unit, ADAM_SLOT_BYTES // (N_DEV * n * 4)), rows_unit)
    spec = pl.BlockSpec((tr, n), lambda i: (i, 0))

    def body(w_ref, s_ref, m_ref, v_ref, g_ref, d_ref, nm_ref, nv_ref):
        g = s_ref[0].astype(F32)
        for d in range(1, N_DEV):
            g = g + s_ref[d].astype(F32)
        nm = ADAM_B1 * m_ref[...] + (1.0 - ADAM_B1) * g
        nv = ADAM_B2 * v_ref[...] + (1.0 - ADAM_B2) * (g * g)
        m_hat = nm / (1.0 - ADAM_B1 ** ADAM_STEP)
        v_hat = nv / (1.0 - ADAM_B2 ** ADAM_STEP)
        g_ref[...] = g
        d_ref[...] = -ADAM_LR * (m_hat / (jnp.sqrt(v_hat) + ADAM_EPS) + ADAM_WD * w_ref[...])
        nm_ref[...] = nm
        nv_ref[...] = nv

    return _pcall(body, name=name, out_shape=[jax.ShapeDtypeStruct((r, n), F32)] * 4, grid=(r // tr,),
                  in_specs=[spec, pl.BlockSpec((N_DEV, tr, n), lambda i: (0, i, 0)), spec, spec],
                  out_specs=[spec] * 4)(w, slots, m, v)


def _pack_rows(flat, rows):
    return jnp.pad(flat, (0, rows * PACK_COLS - flat.shape[0])).reshape(rows, PACK_COLS)


def _regroup_in(w_in):
    cols = [w_in[:, a:b] for a, b in _IN_SRC]
    return jnp.concatenate(cols + [jnp.zeros((w_in.shape[0], N_IN_PAD - N_IN), w_in.dtype)], axis=1)


def _ungroup_in(d):
    starts = {}
    off = 0
    for a, b in _IN_SRC:
        starts[a] = (off, b - a)
        off += b - a
    return jnp.concatenate([d[:, starts[a][0]:starts[a][0] + starts[a][1]] for a in sorted(starts)], axis=1)


def _full_weights(gathered):
    out = {}
    for n, g in zip(SHARDED_ORDER, gathered):
        (r, c), axis = SHARDED[n]
        out[n] = g.reshape(r, c) if axis == 0 else g.transpose(1, 0, 2).reshape(r, c)
    out["w_in"] = _regroup_in(out["w_in"])
    return out


def _in_f32(full):
    return {n: v.astype(F32) for n, v in full.items()}


def _for_transport(name, shard):
    return shard if name in ("gdn_conv", "ffn_conv") else shard.astype(BF16)


def _grad_parts(grads):
    parts = []
    for n in SHARDED_ORDER:
        (r, c), axis = SHARDED[n]
        g = _ungroup_in(grads[n]) if n == "w_in" else grads[n]
        if axis == 0:
            parts.append(g.reshape(N_DEV, r // N_DEV, c).astype(BF16))
        else:
            parts.append(g.reshape(r, N_DEV, c // N_DEV).transpose(1, 0, 2).astype(BF16))
    return parts


def _pack_small(vals):
    return _pack_rows(jnp.concatenate([vals[n].reshape(-1) for n in SMALL_ORDER]), SMALL_ROWS)


def _unpack_small(packed):
    flat = packed.reshape(-1)
    out, off = {}, 0
    for n in SMALL_ORDER:
        size = DEPTH * SMALL_WIDTH[n]
        out[n] = flat[off:off + size].reshape(DEPTH, SMALL_WIDTH[n])
        off += size
    return out


def kernel(x, mem, norm_mix, w_in, fox_fbias, fox_qnorm, fox_knorm, gdn_conv, gdn_a_log, gdn_dt_bias, gdn_onorm, gate_bias, w_oa, w_ob, w_oc, w_out, norm_xq, norm_mem, w_mq, w_mkv, mq_norm, mk_norm, w_mo, norm_ffn, w_up, ffn_conv, ffn_conv_b, w_down, loss_target, m_norm_mix, m_w_in, m_fox_fbias, m_fox_qnorm, m_fox_knorm, m_gdn_conv, m_gdn_a_log, m_gdn_dt_bias, m_gdn_onorm, m_gate_bias, m_w_oa, m_w_ob, m_w_oc, m_w_out, m_norm_xq, m_norm_mem, m_w_mq, m_w_mkv, m_mq_norm, m_mk_norm, m_w_mo, m_norm_ffn, m_w_up, m_ffn_conv, m_ffn_conv_b, m_w_down, v_norm_mix, v_w_in, v_fox_fbias, v_fox_qnorm, v_fox_knorm, v_gdn_conv, v_gdn_a_log, v_gdn_dt_bias, v_gdn_onorm, v_gate_bias, v_w_oa, v_w_ob, v_w_oc, v_w_out, v_norm_xq, v_norm_mem, v_w_mq, v_w_mkv, v_mq_norm, v_mk_norm, v_w_mo, v_norm_ffn, v_w_up, v_ffn_conv, v_ffn_conv_b, v_w_down):
    given = dict(locals())
    wts = {n: given[n] for n in WEIGHTS}
    mom = {n: given["m_" + n] for n in WEIGHTS}
    var = {n: given["v_" + n] for n in WEIGHTS}

    layers, layers16 = [], []
    for l in range(DEPTH):
        full = _full_weights(_all_gather("gather_weights", [_for_transport(n, wts[n][l]) for n in SHARDED_ORDER]))
        layers16.append(full)
        layers.append({**_in_f32(full), **{n: wts[n][l] for n in SMALL_ORDER}})

    ops, convs = _make_rowops(), _make_convops()
    y, vjps = x[0], []
    for l in range(DEPTH):
        y, vjp = jax.vjp(lambda xx, ww, l=l: _layer(xx, mem[0], ww, layers16[l], ops, convs), y, layers[l])
        vjps.append(vjp)
    loss_part, dx = _loss_call(y, loss_target[0])
    loss = lax.psum(loss_part, ("x", "y", "c"))

    out = {}
    dlayers, started = [None] * DEPTH, [None] * DEPTH
    for l in reversed(range(DEPTH)):
        dx, dlayers[l] = vjps[l](dx)
        started[l], token = _exchange_start(f"exchange_start_{l}", _grad_parts(dlayers[l]))
        dx = dx + token[0, 0]
    me = 4 * lax.axis_index("x") + 2 * lax.axis_index("y") + lax.axis_index("c")
    per_layer = []
    for l in range(DEPTH):
        parts, lands = _exchange_wait(f"exchange_wait_{l}", *started[l], dx)
        slots = [jnp.concatenate([lax.dynamic_index_in_dim(p, me, 0), ld]) for p, ld in zip(parts, lands)]
        per_layer.append({n: _adam_call("adam_shard", wts[n][l], sl, mom[n][l], var[n][l])
                          for n, sl in zip(SHARDED_ORDER, slots)})
    for n in SHARDED_ORDER:
        for k, kind in enumerate(("grad_", "delta_", "new_m_", "new_v_")):
            out[kind + n] = jnp.stack([per_layer[l][n][k] for l in range(DEPTH)])
    dsmall = {n: jnp.stack([dlayers[l][n] for l in range(DEPTH)]) for n in SMALL_ORDER}
    slots = _all_gather("gather_small_grads", [_pack_small(dsmall)])[0]
    res = _adam_call("adam_small", _pack_small(wts), slots, _pack_small(mom), _pack_small(var))
    for k, kind in enumerate(("grad_", "delta_", "new_m_", "new_v_")):
        un = _unpack_small(res[k])
        for n in SMALL_ORDER:
            out[kind + n] = un[n].reshape(wts[n].shape)

    return (loss, dx[None], *[out["grad_" + n] for n in WEIGHTS], *[out["delta_" + n] for n in WEIGHTS],
            *[out["new_m_" + n] for n in WEIGHTS], *[out["new_v_" + n] for n in WEIGHTS])
```

```python
import jax
import jax.numpy as jnp
from jax import lax
from jax.experimental import pallas as pl
from jax.experimental.pallas import tpu as pltpu

F32 = jnp.float32
BF16 = jnp.bfloat16

N_DEV = 8
D_MODEL = 1024
DEPTH = 4
CHUNK = 64
EPS = 1e-6
HEADS = 4
HEAD_DIM = 128
HW = HEADS * HEAD_DIM
D_FF = 2816
N_IN = 8204
LANES = 128
SUBLANES = 8
VMEM_LIMIT = 56 * 1024 * 1024

ADAM_LR = 0.001
ADAM_B1 = 0.9
ADAM_B2 = 0.999
ADAM_EPS = 1e-08
ADAM_WD = 0.01
ADAM_STEP = 10

NEG = -1e30
MESH = pl.DeviceIdType.MESH

WEIGHTS = ['norm_mix', 'w_in', 'fox_fbias', 'fox_qnorm', 'fox_knorm', 'gdn_conv', 'gdn_a_log', 'gdn_dt_bias',
           'gdn_onorm', 'gate_bias', 'w_oa', 'w_ob', 'w_oc', 'w_out', 'norm_xq', 'norm_mem', 'w_mq', 'w_mkv',
           'mq_norm', 'mk_norm', 'w_mo', 'norm_ffn', 'w_up', 'ffn_conv', 'ffn_conv_b', 'w_down']
SHARDED = {
    'w_in': ((D_MODEL, N_IN), 0), 'gdn_conv': ((4, 3 * HW), 1), 'w_oa': ((HW, D_MODEL), 1),
    'w_ob': ((HW, D_MODEL), 1), 'w_oc': ((HW, D_MODEL), 1), 'w_out': ((D_MODEL, D_MODEL), 0),
    'w_mq': ((D_MODEL, HW), 0), 'w_mkv': ((D_MODEL, 2 * HW), 0), 'w_mo': ((HW, D_MODEL), 1),
    'w_up': ((D_MODEL, 2 * D_FF), 1), 'ffn_conv': ((3, 2 * D_FF), 1), 'w_down': ((D_FF, D_MODEL), 0),
}
SHARDED_ORDER = [n for n in WEIGHTS if n in SHARDED]
SMALL_ORDER = [n for n in WEIGHTS if n not in SHARDED]
SMALL_WIDTH = {'norm_mix': D_MODEL, 'fox_fbias': HEADS, 'fox_qnorm': HEAD_DIM, 'fox_knorm': HEAD_DIM,
               'gdn_a_log': HEADS, 'gdn_dt_bias': HEADS, 'gdn_onorm': HEAD_DIM, 'gate_bias': 3 * D_MODEL,
               'norm_xq': D_MODEL, 'norm_mem': D_MODEL, 'mq_norm': HEAD_DIM, 'mk_norm': HEAD_DIM,
               'norm_ffn': D_MODEL, 'ffn_conv_b': 2 * D_FF}
PACK_COLS = 1024


def _round_up(n, m):
    return (n + m - 1) // m * m


SMALL_ROWS = _round_up(DEPTH * sum(SMALL_WIDTH.values()), SUBLANES * PACK_COLS) // PACK_COLS

_IN_SRC = [(0, 512), (512, 1024), (1024, 1536),
           (1540, 2052), (2052, 2564), (2564, 3076),
           (3084, 3596),
           (3596, 4108), (4108, 4620), (4620, 5132),
           (5132, 8204),
           (1536, 1540), (3076, 3080), (3080, 3084)]
N_IN_PAD = 8320
LANE_FF, LANE_GB, LANE_GA = 0, 4, 8


def _pick(dim, pref, unit):
    best = None
    t = unit
    while t <= min(dim, pref):
        if dim % t == 0:
            best = t
        t += unit
    return dim if best is None else best


def _params(n_grid):
    return pltpu.CompilerParams(dimension_semantics=("arbitrary",) * n_grid, vmem_limit_bytes=VMEM_LIMIT)


def _pcall(body, *, name, out_shape, grid, in_specs, out_specs, scratch_shapes=()):
    return pl.pallas_call(body, name=name, out_shape=out_shape, grid=grid, in_specs=in_specs, out_specs=out_specs,
                          scratch_shapes=scratch_shapes, compiler_params=_params(len(grid)),
                          interpret=False)


NN = ((1,), (0,))
NT = ((1,), (1,))
TN = ((0,), (0,))


def _dot(a, b, dn):
    return lax.dot_general(a.astype(BF16), b.astype(BF16), (dn, ((), ())), preferred_element_type=F32)


def _dotf(a, b, dn):
    return lax.dot_general(a, b, (dn, ((), ())), precision=lax.Precision.HIGHEST, preferred_element_type=F32)


@jax.custom_vjp
def mm(a, b):
    return _dot(a, b, NN)


mm.defvjp(lambda a, b: (_dot(a, b, NN), (a, b)), lambda r, g: (_dot(g, r[1], NT), _dot(r[0], g, TN)))


@jax.custom_vjp
def mm_nt(a, b):
    return _dot(a, b, NT)


mm_nt.defvjp(lambda a, b: (_dot(a, b, NT), (a, b)), lambda r, g: (_dot(g, r[1], NN), _dot(g, r[0], TN)))


@jax.custom_vjp
def mm_tn(a, b):
    return _dot(a, b, TN)


mm_tn.defvjp(lambda a, b: (_dot(a, b, TN), (a, b)), lambda r, g: (_dot(r[1], g, NT), _dot(r[0], g, NN)))


@jax.custom_vjp
def tri_apply(t, x):
    return _dotf(t, x, NN)


tri_apply.defvjp(lambda t, x: (_dotf(t, x, NN), t), lambda t, g: (jnp.zeros_like(t), _dotf(t, g, TN)))


def _sigmoid(x):
    return 1.0 / (1.0 + jnp.exp(-x))


@jax.custom_vjp
def _softplus(x):
    return jnp.maximum(x, 0.0) + jnp.log(1.0 + jnp.exp(-jnp.abs(x)))


_softplus.defvjp(lambda x: (_softplus(x), x), lambda x, g: (g * _sigmoid(x),))


@jax.custom_vjp
def _silu(x):
    return x * _sigmoid(x)


def _silu_fwd(x):
    s = _sigmoid(x)
    return x * s, (x, s)


_silu.defvjp(_silu_fwd, lambda r, g: (g * r[1] * (1.0 + r[0] * (1.0 - r[1])),))


def _rms(x, g):
    return x * lax.rsqrt(jnp.mean(x * x, axis=-1, keepdims=True) + EPS) * g


def _iota2(n, m, axis):
    return lax.broadcasted_iota(jnp.int32, (n, m), axis)


def _whole(width):
    return [(0, width)]


def _split(width, n):
    w = width // n
    return [(k * w, w) for k in range(n)]


class RowOp:
    def __init__(self, name, f, in_pieces, out_pieces, tm=256):
        self.name, self.f, self.in_pieces, self.out_pieces, self.tm = name, f, in_pieces, out_pieces, tm
        op = jax.custom_vjp(self._fwd_call)
        op.defvjp(lambda *a: (self._fwd_call(*a), a), lambda res, g: self._bwd_call(res, g))
        self.op = op

    def __call__(self, *args):
        return self.op(*args)

    def _width(self, pieces):
        return max(o + w for o, w in pieces)

    def _row_specs(self, pieces_list, tm):
        return [pl.BlockSpec((tm, self._width(p)), lambda i: (i, 0)) for p in pieces_list]

    def _fwd_call(self, *args):
        nr = len(self.in_pieces)
        rows, params = args[:nr], args[nr:]
        m = rows[0].shape[0]
        tm = _pick(m, self.tm, SUBLANES)
        f, in_pieces, out_pieces = self.f, self.in_pieces, self.out_pieces
        no = len(out_pieces)

        def body(*refs):
            rin, pr, ro = refs[:nr], refs[nr:nr + len(params)], refs[nr + len(params):]
            xs = [r[:, o:o + w] for r, ps in zip(rin, in_pieces) for (o, w) in ps]
            ys = f(*xs, *[p[...] for p in pr])
            k = 0
            for r, ps in zip(ro, out_pieces):
                for (o, w) in ps:
                    r[:, o:o + w] = ys[k]
                    k += 1

        outs = _pcall(
            body, name=self.name + "_fwd",
            out_shape=[jax.ShapeDtypeStruct((m, self._width(p)), F32) for p in out_pieces],
            grid=(m // tm,),
            in_specs=self._row_specs(in_pieces, tm) + [pl.BlockSpec(p.shape, lambda i: (0, 0)) for p in params],
            out_specs=self._row_specs(out_pieces, tm),
        )(*rows, *params)
        return tuple(outs) if no > 1 else outs[0]

    def _bwd_call(self, res, g):
        nr = len(self.in_pieces)
        rows, params = res[:nr], res[nr:]
        no = len(self.out_pieces)
        gs = tuple(g) if no > 1 else (g,)
        m = rows[0].shape[0]
        tm = _pick(m, self.tm, SUBLANES)
        f, in_pieces, out_pieces = self.f, self.in_pieces, self.out_pieces
        npar = len(params)

        def body(*refs):
            rin, pr, dro = refs[:nr], refs[nr:nr + npar], refs[nr + npar:nr + npar + no]
            drin, dpr = refs[nr + npar + no:nr + npar + no + nr], refs[nr + npar + no + nr:]
            xs = [r[:, o:o + w] for r, ps in zip(rin, in_pieces) for (o, w) in ps]
            dys = [r[:, o:o + w] for r, ps in zip(dro, out_pieces) for (o, w) in ps]
            _, vjp = jax.vjp(lambda *a: tuple(f(*a)), *xs, *[p[...] for p in pr])
            grads = vjp(tuple(dys))
            k = 0
            for r, ps in zip(drin, in_pieces):
                for (o, w) in ps:
                    r[:, o:o + w] = grads[k]
                    k += 1

            @pl.when(pl.program_id(0) == 0)
            def _():
                for r in dpr:
                    r[...] = jnp.zeros_like(r)

            for j, r in enumerate(dpr):
                r[...] += grads[k + j]

        outs = _pcall(
            body, name=self.name + "_bwd",
            out_shape=[jax.ShapeDtypeStruct(r.shape, F32) for r in rows]
            + [jax.ShapeDtypeStruct(p.shape, F32) for p in params],
            grid=(m // tm,),
            in_specs=self._row_specs(in_pieces, tm) + [pl.BlockSpec(p.shape, lambda i: (0, 0)) for p in params]
            + self._row_specs(out_pieces, tm),
            out_specs=self._row_specs(in_pieces, tm) + [pl.BlockSpec(p.shape, lambda i: (0, 0)) for p in params],
        )(*rows, *params, *gs)
        return tuple(outs)


def _f_rms(x, g):
    return (_rms(x, g),)


def _f_headnorm(x0, x1, x2, x3, g):
    return tuple(_rms(x, g) for x in (x0, x1, x2, x3))


def _f_small(sm, fb, al, db):
    tm = sm.shape[0]
    logf = -_softplus(-(sm + fb))
    beta = _sigmoid(sm)
    glog = -jnp.exp(al) * _softplus(sm + db)
    r, c = _iota2(tm, tm, 0), _iota2(tm, tm, 1)
    bd = jnp.where((r >= c) & (jnp.bitwise_xor(r, c) < CHUNK), 1.0, 0.0).astype(F32)
    return logf, beta, tri_apply(bd, glog)


def _f_gdnpost(o0, o1, o2, o3, z0, z1, z2, z3, g):
    return tuple(_rms(o, g) * _silu(z) for o, z in zip((o0, o1, o2, o3), (z0, z1, z2, z3)))


def _f_merge(t0, t1, t2, a, b, c, b0, b1, b2):
    return (_sigmoid(t0 + b0) * a + _sigmoid(t1 + b1) * b + _sigmoid(t2 + b2) * c,)


def _make_rowops():
    return dict(
        rms=RowOp("rms", _f_rms, [_whole(D_MODEL)], [_whole(D_MODEL)], tm=512),
        headnorm=RowOp("headnorm", _f_headnorm, [_split(HW, HEADS)], [_split(HW, HEADS)], tm=1024),
        small=RowOp("smallprep", _f_small, [_whole(LANES)], [_whole(LANES)] * 3),
        gdnpost=RowOp("gdnpost", _f_gdnpost, [_split(HW, HEADS)] * 2, [_split(HW, HEADS)], tm=512),
        merge=RowOp("merge", _f_merge, [_split(3 * D_MODEL, 3)] + [_whole(D_MODEL)] * 3, [_whole(D_MODEL)]),
    )


ROUND_ONCE_READS = 3


def _b16(x):
    return x.astype(BF16)


def _mm_call(name, a, b, mode, c=None):
    if mode == "nn":
        (m, kc), n = a.shape, b.shape[1]
    elif mode == "nt":
        (m, kc), n = a.shape, b.shape[0]
    else:
        (kc, m), n = a.shape, b.shape[1]
    tm = _pick(m, 1408, LANES) if mode == "tn" else _pick(m, 1024, SUBLANES)
    tn = _pick(n, 1408, LANES)
    tk = _pick(kc, 1024, SUBLANES) if mode == "tn" else _pick(kc, 1536, LANES)
    if a.dtype == F32 and n // tn >= ROUND_ONCE_READS:
        a = _b16(a)
    if b.dtype == F32 and m // tm >= ROUND_ONCE_READS:
        b = _b16(b)
    dn = {"nn": NN, "nt": NT, "tn": TN}[mode]
    a_spec = {"nn": pl.BlockSpec((tm, tk), lambda i, j, k: (i, k)),
              "nt": pl.BlockSpec((tm, tk), lambda i, j, k: (i, k)),
              "tn": pl.BlockSpec((tk, tm), lambda i, j, k: (k, i))}[mode]
    b_spec = {"nn": pl.BlockSpec((tk, tn), lambda i, j, k: (k, j)),
              "nt": pl.BlockSpec((tn, tk), lambda i, j, k: (j, k)),
              "tn": pl.BlockSpec((tk, tn), lambda i, j, k: (k, j))}[mode]
    o_spec = pl.BlockSpec((tm, tn), lambda i, j, k: (i, j))
    has_c = c is not None

    def body(*refs):
        a_ref, b_ref = refs[0], refs[1]
        o_ref = refs[-1]

        @pl.when(pl.program_id(2) == 0)
        def _():
            o_ref[...] = refs[2][...] if has_c else jnp.zeros_like(o_ref)

        o_ref[...] += _dot(a_ref[...], b_ref[...], dn)

    return _pcall(body, name=name, out_shape=jax.ShapeDtypeStruct((m, n), F32), grid=(m // tm, n // tn, kc // tk),
                  in_specs=[a_spec, b_spec] + ([o_spec] if has_c else []), out_specs=o_spec,
                  )(*((a, b, c) if has_c else (a, b)))


@jax.custom_vjp
def matmul(a, w, w16):
    return _mm_call("mm_nn", a, w16, "nn")


def _matmul_bwd(res, g):
    a, w16 = res
    return _mm_call("mm_nt", g, w16, "nt"), _mm_call("mm_tn", a, g, "tn"), jnp.zeros_like(w16)


matmul.defvjp(lambda a, w, w16: (_mm_call("mm_nn", a, w16, "nn"), (a, w16)), _matmul_bwd)


@jax.custom_vjp
def matmul_add(c, a, w, w16):
    return _mm_call("mm_nn_add", a, w16, "nn", c)


matmul_add.defvjp(lambda c, a, w, w16: (_mm_call("mm_nn_add", a, w16, "nn", c), (a, w16)),
                  lambda res, g: (g,) + _matmul_bwd(res, g))

_PROJ_GROUPS = [(0, 512), (512, 512), (1024, 512), (1536, 1536), (3072, 512), (3584, 512), (4096, 512), (4608, 512),
                (5120, 3072), (8192, 128)]


def _proj_fwd(h, w, w16):
    h16 = _b16(h)
    return tuple(_mm_call("proj_nn", h16, w16[:, s:s + n], "nn") for s, n in _PROJ_GROUPS), (h16, w16)


proj = jax.custom_vjp(lambda h, w, w16: _proj_fwd(h, w, w16)[0])


def _proj_bwd(res, gs):
    h16, w16 = res
    dh = None
    dws = []
    for (s, n), g in zip(_PROJ_GROUPS, gs):
        dh = _mm_call("proj_nt", g, w16[:, s:s + n], "nt", dh)
        dws.append(_mm_call("proj_tn", h16, g, "tn"))
    return dh, jnp.concatenate(dws, axis=1), jnp.zeros_like(w16)


proj.defvjp(_proj_fwd, _proj_bwd)


def _cumsum_call(x, reverse):
    s, w = x.shape
    tm = _pick(s, 256, SUBLANES)
    nb = s // tm

    def body(x_ref, o_ref, carry):
        @pl.when(pl.program_id(0) == 0)
        def _():
            carry[...] = jnp.zeros_like(carry)

        blk = x_ref[...]
        r, c = _iota2(tm, tm, 0), _iota2(tm, tm, 1)
        tri = jnp.where((r <= c) if reverse else (r >= c), 1.0, 0.0).astype(F32)
        o_ref[...] = _dotf(tri, blk, NN) + carry[...]
        carry[...] += jnp.sum(blk, axis=0, keepdims=True)

    idx = (lambda i: (nb - 1 - i, 0)) if reverse else (lambda i: (i, 0))
    return _pcall(body, name="cumsum_rev" if reverse else "cumsum", out_shape=jax.ShapeDtypeStruct((s, w), F32),
                  grid=(nb,), in_specs=[pl.BlockSpec((tm, w), idx)], out_specs=pl.BlockSpec((tm, w), idx),
                  scratch_shapes=[pltpu.VMEM((1, w), F32)])(x)


@jax.custom_vjp
def seq_cumsum(x):
    return _cumsum_call(x, False)


seq_cumsum.defvjp(lambda x: (_cumsum_call(x, False), None), lambda _, g: (_cumsum_call(g, True),))


HALO = SUBLANES


class ConvOp:
    def __init__(self, name, width, post, c_pieces, out_widths, has_bias, tm):
        self.name, self.width, self.post, self.c_pieces = name, width, post, c_pieces
        self.out_widths, self.has_bias, self.tm = out_widths, has_bias, tm
        op = jax.custom_vjp(self._fwd_call)
        op.defvjp(lambda *a: (self._fwd_call(*a), a), lambda res, g: self._bwd_call(res, g))
        self.op = op

    def __call__(self, *args):
        return self.op(*args)

    def _conv(self, i, x_ref, prev_ref, w_ref, b_ref, buf):
        tm = x_ref.shape[0]
        buf[0:HALO, :] = jnp.where(i > 0, prev_ref[...], 0.0)
        buf[HALO:HALO + tm, :] = x_ref[...]
        taps = [buf[pl.ds(HALO - (self.width - 1) + j, tm), :] for j in range(self.width)]
        c = taps[0] * w_ref[0:1, :]
        for j in range(1, self.width):
            c = c + taps[j] * w_ref[j:j + 1, :]
        if self.has_bias:
            c = c + b_ref[...]
        return c, taps

    def _fwd_call(self, x, w, *bias):
        s, ch = x.shape
        tm = _pick(s, self.tm, SUBLANES)
        r8 = tm // HALO
        has_bias, post, c_pieces = self.has_bias, self.post, self.c_pieces

        def body(*refs):
            x_ref, prev_ref, w_ref = refs[:3]
            b_ref = refs[3] if has_bias else None
            outs, buf = refs[3 + has_bias:-1], refs[-1]
            c, _ = self._conv(pl.program_id(0), x_ref, prev_ref, w_ref, b_ref, buf)
            ys = post(*[c[:, o:o + n] for o, n in c_pieces])
            for r, y in zip(outs, ys):
                r[...] = y

        outs = _pcall(
            body, name=self.name + "_fwd", out_shape=[jax.ShapeDtypeStruct((s, n), F32) for n in self.out_widths],
            grid=(s // tm,),
            in_specs=[pl.BlockSpec((tm, ch), lambda i: (i, 0)),
                      pl.BlockSpec((HALO, ch), lambda i: (jnp.maximum(i * r8 - 1, 0), 0)),
                      pl.BlockSpec(w.shape, lambda i: (0, 0))]
            + ([pl.BlockSpec((1, ch), lambda i: (0, 0))] if has_bias else []),
            out_specs=[pl.BlockSpec((tm, n), lambda i: (i, 0)) for n in self.out_widths],
            scratch_shapes=[pltpu.VMEM((tm + HALO, ch), F32)],
        )(x, x, w, *bias)
        return tuple(outs) if len(outs) > 1 else outs[0]

    def _bwd_call(self, res, g):
        x, w = res[0], res[1]
        bias = res[2:]
        gs = tuple(g) if len(self.out_widths) > 1 else (g,)
        s, ch = x.shape
        tm = _pick(s, self.tm, SUBLANES)
        r8 = tm // HALO
        nb = s // tm
        has_bias, post, c_pieces, width = self.has_bias, self.post, self.c_pieces, self.width
        ng = len(gs)

        def body1(*refs):
            x_ref, prev_ref, w_ref = refs[:3]
            b_ref = refs[3] if has_bias else None
            k = 3 + has_bias
            g_refs = refs[k:k + ng]
            dc_ref, dw_ref = refs[k + ng], refs[k + ng + 1]
            db_ref = refs[k + ng + 2] if has_bias else None
            buf = refs[-1]
            i = pl.program_id(0)
            c, taps = self._conv(i, x_ref, prev_ref, w_ref, b_ref, buf)
            _, vjp = jax.vjp(lambda *a: tuple(post(*a)), *[c[:, o:o + n] for o, n in c_pieces])
            dcs = vjp(tuple(r[...] for r in g_refs))
            for (o, n), d in zip(c_pieces, dcs):
                dc_ref[:, o:o + n] = d

            @pl.when(i == 0)
            def _():
                dw_ref[...] = jnp.zeros_like(dw_ref)
                if has_bias:
                    db_ref[...] = jnp.zeros_like(db_ref)

            dc = dc_ref[...]
            for j in range(width):
                dw_ref[j:j + 1, :] += jnp.sum(dc * taps[j], axis=0, keepdims=True)
            if has_bias:
                db_ref[...] += jnp.sum(dc, axis=0, keepdims=True)

        outs1 = _pcall(
            body1, name=self.name + "_bwd_act",
            out_shape=[jax.ShapeDtypeStruct((s, ch), F32), jax.ShapeDtypeStruct(w.shape, F32)]
            + ([jax.ShapeDtypeStruct((1, ch), F32)] if has_bias else []),
            grid=(nb,),
            in_specs=[pl.BlockSpec((tm, ch), lambda i: (i, 0)),
                      pl.BlockSpec((HALO, ch), lambda i: (jnp.maximum(i * r8 - 1, 0), 0)),
                      pl.BlockSpec(w.shape, lambda i: (0, 0))]
            + ([pl.BlockSpec((1, ch), lambda i: (0, 0))] if has_bias else [])
            + [pl.BlockSpec((tm, n), lambda i: (i, 0)) for n in self.out_widths],
            out_specs=[pl.BlockSpec((tm, ch), lambda i: (i, 0)), pl.BlockSpec(w.shape, lambda i: (0, 0))]
            + ([pl.BlockSpec((1, ch), lambda i: (0, 0))] if has_bias else []),
            scratch_shapes=[pltpu.VMEM((tm + HALO, ch), F32)],
        )(x, x, w, *bias, *gs)
        dc, dw = outs1[0], outs1[1]

        def body2(dc_ref, next_ref, w_ref, dx_ref, buf):
            i = pl.program_id(0)
            buf[0:tm, :] = dc_ref[...]
            buf[tm:tm + HALO, :] = jnp.where(i < nb - 1, next_ref[...], 0.0)
            dx = buf[pl.ds(width - 1, tm), :] * w_ref[0:1, :]
            for j in range(1, width):
                dx = dx + buf[pl.ds(width - 1 - j, tm), :] * w_ref[j:j + 1, :]
            dx_ref[...] = dx

        dx = _pcall(
            body2, name=self.name + "_bwd_in", out_shape=jax.ShapeDtypeStruct((s, ch), F32), grid=(nb,),
            in_specs=[pl.BlockSpec((tm, ch), lambda i: (i, 0)),
                      pl.BlockSpec((HALO, ch), lambda i: (jnp.minimum((i + 1) * r8, s // HALO - 1), 0)),
                      pl.BlockSpec(w.shape, lambda i: (0, 0))],
            out_specs=pl.BlockSpec((tm, ch), lambda i: (i, 0)),
            scratch_shapes=[pltpu.VMEM((tm + HALO, ch), F32)],
        )(dc, dc, w)
        return (dx, dw) + ((outs1[2],) if has_bias else ())


def _make_convops():
    return dict(
        gdn=ConvOp("gdnconv", 4, lambda q, k, v: (_silu(q), _silu(k), _silu(v)), _split(3 * HW, 3), [HW] * 3,
                   False, 256),
        ffn=ConvOp("ffnconv", 3, lambda a, b: (_silu(a) * b,), _split(2 * D_FF, 2), [D_FF], True, 256),
    )


ATT_Q = 512
ATT_K = 256
ATT_K_FOX = 512
SCALE = HEAD_DIM ** -0.5


ATT_Q_SB = 1024


def _att_tiles(s, fox=False):
    tk = _pick(s, ATT_K_FOX if fox else ATT_K, LANES)
    tq = _pick(s, ATT_Q if fox else ATT_Q_SB, tk)
    return tq, tk


def _att_specs(s, tq, tk):
    qspec = pl.BlockSpec((tq, HEAD_DIM), lambda h, i: (i, h))
    kspec = pl.BlockSpec((s, HEAD_DIM), lambda h, i: (0, h))
    colspec = pl.BlockSpec((None, tq, 1), lambda h, i: (h, i, 0))
    rowspec = pl.BlockSpec((None, s // tk, 1, tk), lambda h, i: (h, 0, 0, 0))
    return qspec, kspec, colspec, rowspec


def _krows(kb, tk):
    return pl.ds(pl.multiple_of(kb * tk, tk), tk)


def _stage_bf16(i, pairs):
    @pl.when(i == 0)
    def _():
        for src, dst in pairs:
            dst[...] = src[...].astype(BF16)


ATT_STRIP = 32


def _strips(tq):
    return [slice(r, r + ATT_STRIP) for r in range(0, tq, ATT_STRIP)]


def _visible(i, kb, rs, tq, tk, strict):
    rows = i * tq + rs.start + _iota2(ATT_STRIP, tk, 0)
    cols = kb * tk + _iota2(ATT_STRIP, tk, 1)
    return (cols < rows) if strict else (cols <= rows)


def _visible_block(i, kb, tq, tk):
    return kb * tk + _iota2(tq, tk, 1) <= i * tq + _iota2(tq, tk, 0)


def _blocks(i, ratio, blk, reverse=False):
    def full(n, carry):
        blk(i * ratio - 1 - n if reverse else n, False)
        return carry

    if reverse:
        for j in reversed(range(ratio)):
            blk(i * ratio + j, True)
    lax.fori_loop(0, i * ratio, full, 0)
    if not reverse:
        for j in range(ratio):
            blk(i * ratio + j, True)


def _vm(shape, dtype):
    return pltpu.VMEM(shape, dtype)


def _fox_fwd_call(q, k, v, ccol, crow):
    s = q.shape[0]
    tq, tk = _att_tiles(s, fox=True)
    qspec, kspec, colspec, rowspec = _att_specs(s, tq, tk)

    def body(q_ref, k_ref, v_ref, cq_ref, ck_ref, o_ref, lse_ref, k16, v16):
        i = pl.program_id(1)
        ratio = tq // tk
        _stage_bf16(i, [(k_ref, k16), (v_ref, v16)])
        qb = q_ref[...].astype(BF16)
        cq = cq_ref[...]

        def blk(kb, carry, masked):
            m, l, acc = carry
            sc = _dot(qb, k16[_krows(kb, tk), :], NT) * SCALE + (cq - ck_ref[kb])
            if masked:
                sc = jnp.where(_visible_block(i, kb, tq, tk), sc, NEG)
            m_new = jnp.maximum(m, jnp.max(sc, axis=-1, keepdims=True))
            alpha = jnp.exp(m - m_new)
            p = jnp.exp(sc - m_new)
            return (m_new, alpha * l + jnp.sum(p, axis=-1, keepdims=True),
                    alpha * acc + _dot(p, v16[_krows(kb, tk), :], NN))

        carry = (jnp.full((tq, 1), NEG, F32), jnp.zeros((tq, 1), F32), jnp.zeros((tq, HEAD_DIM), F32))
        carry = lax.fori_loop(0, i * ratio, lambda kb, c: blk(kb, c, False), carry)
        for j in range(ratio):
            carry = blk(i * ratio + j, carry, True)
        m, l, acc = carry
        o_ref[...] = acc / l
        lse_ref[...] = m + jnp.log(l)

    return _pcall(body, name="fox_fwd",
                  out_shape=[jax.ShapeDtypeStruct((s, HW), F32), jax.ShapeDtypeStruct((HEADS, s, 1), F32)],
                  grid=(HEADS, s // tq), in_specs=[qspec, kspec, kspec, colspec, rowspec],
                  out_specs=[qspec, colspec],
                  scratch_shapes=[_vm((s, HEAD_DIM), BF16), _vm((s, HEAD_DIM), BF16)])(q, k, v, ccol, crow)


def _fox_bwd_call(q, k, v, ccol, crow, o, lse, do):
    s = q.shape[0]
    tq, tk = _att_tiles(s, fox=True)
    ratio = tq // tk
    qspec, kspec, colspec, rowspec = _att_specs(s, tq, tk)

    def body(q_ref, k_ref, v_ref, cq_ref, ck_ref, o_ref, lse_ref, do_ref, dq_ref, dk_ref, dv_ref, dcq_ref, dck_ref,
             k16, v16):
        i = pl.program_id(1)
        _stage_bf16(i, [(k_ref, k16), (v_ref, v16)])

        @pl.when(i == 0)
        def _():
            dk_ref[...] = jnp.zeros_like(dk_ref)
            dv_ref[...] = jnp.zeros_like(dv_ref)
            dck_ref[...] = jnp.zeros_like(dck_ref)

        qb = q_ref[...].astype(BF16)
        dob = do_ref[...].astype(BF16)
        cq, lse = cq_ref[...], lse_ref[...]
        dl = jnp.sum(do_ref[...] * o_ref[...], axis=-1, keepdims=True)

        def blk(kb, carry, masked):
            dq, dcq = carry
            rows = _krows(kb, tk)
            kk, vv = k16[rows, :], v16[rows, :]
            sc = _dot(qb, kk, NT) * SCALE + (cq - ck_ref[kb])
            p = jnp.exp(sc - lse)
            if masked:
                p = jnp.where(_visible_block(i, kb, tq, tk), p, 0.0)
            dv_ref[rows, :] += _dot(p, dob, TN)
            ds = p * (_dot(dob, vv, NT) - dl)
            dk_ref[rows, :] += _dot(ds, qb, TN) * SCALE
            dck_ref[kb] += -jnp.sum(ds, axis=0, keepdims=True)
            return dq + _dot(ds, kk, NN) * SCALE, dcq + jnp.sum(ds, axis=-1, keepdims=True)

        carry = (jnp.zeros((tq, HEAD_DIM), F32), jnp.zeros((tq, 1), F32))
        carry = lax.fori_loop(0, i * ratio, lambda kb, c: blk(kb, c, False), carry)
        for j in range(ratio):
            carry = blk(i * ratio + j, carry, True)
        dq_ref[...] = carry[0]
        dcq_ref[...] = carry[1]

    return _pcall(body, name="fox_bwd",
                  out_shape=[jax.ShapeDtypeStruct((s, HW), F32)] * 3
                  + [jax.ShapeDtypeStruct((HEADS, s, 1), F32), jax.ShapeDtypeStruct((HEADS, s // tk, 1, tk), F32)],
                  grid=(HEADS, s // tq),
                  in_specs=[qspec, kspec, kspec, colspec, rowspec, qspec, colspec, qspec],
                  out_specs=[qspec, kspec, kspec, colspec, rowspec],
                  scratch_shapes=[_vm((s, HEAD_DIM), BF16), _vm((s, HEAD_DIM), BF16)],
                  )(q, k, v, ccol, crow, o, lse, do)


@jax.custom_vjp
def fox_attention(q, k, v, ccol, crow):
    return _fox_fwd_call(q, k, v, ccol, crow)[0]


def _fox_vjp_fwd(q, k, v, ccol, crow):
    o, lse = _fox_fwd_call(q, k, v, ccol, crow)
    return o, (q, k, v, ccol, crow, o, lse)


fox_attention.defvjp(_fox_vjp_fwd, lambda res, g: tuple(_fox_bwd_call(*res, g)))


def _sb_fwd_call(q, k, v):
    s = q.shape[0]
    tq, tk = _att_tiles(s)
    ratio = tq // tk
    qspec, kspec, colspec, _ = _att_specs(s, tq, tk)

    def body(q_ref, k_ref, v_ref, o_ref, tot_ref, k16, v16, q16, s_scr, w_scr, lk16, a16, run, acc):
        i = pl.program_id(1)
        _stage_bf16(i, [(k_ref, k16), (v_ref, v16)])
        q16[...] = q_ref[...].astype(BF16)
        run[...] = jnp.zeros_like(run)
        acc[...] = jnp.zeros_like(acc)
        suffix = jnp.where(_iota2(tk, tk, 0) >= _iota2(tk, tk, 1), 1.0, 0.0).astype(BF16)

        def blk(kb, masked):
            rows = _krows(kb, tk)
            s_scr[...] = _dot(q16[...], k16[rows, :], NT)
            for rs in _strips(tq):
                lk = -_softplus(s_scr[rs, :] * SCALE)
                if masked:
                    lk = jnp.where(_visible(i, kb, rs, tq, tk, True), lk, 0.0)
                lk16[rs, :] = lk.astype(BF16)
            w_scr[...] = _dot(lk16[...], suffix, NN)
            for rs in _strips(tq):
                a = jnp.exp(s_scr[rs, :] * SCALE + w_scr[rs, :] + run[rs, :])
                if masked:
                    a = jnp.where(_visible(i, kb, rs, tq, tk, True), a, 0.0)
                a16[rs, :] = a.astype(BF16)
                run[rs, :] += w_scr[rs, 0:1]
            acc[...] += _dot(a16[...], v16[rows, :], NN)

        _blocks(i, ratio, blk, reverse=True)
        o_ref[...] = acc[...]
        tot_ref[...] = run[...]

    return _pcall(body, name="sb_fwd",
                  out_shape=[jax.ShapeDtypeStruct((s, HW), F32), jax.ShapeDtypeStruct((HEADS, s, 1), F32)],
                  grid=(HEADS, s // tq), in_specs=[qspec, kspec, kspec], out_specs=[qspec, colspec],
                  scratch_shapes=[_vm((s, HEAD_DIM), BF16), _vm((s, HEAD_DIM), BF16), _vm((tq, HEAD_DIM), BF16),
                                  _vm((tq, tk), F32), _vm((tq, tk), F32), _vm((tq, tk), BF16), _vm((tq, tk), BF16),
                                  _vm((tq, 1), F32), _vm((tq, HEAD_DIM), F32)])(q, k, v)


def _sb_bwd_call(q, k, v, tot, do):
    s = q.shape[0]
    tq, tk = _att_tiles(s)
    ratio = tq // tk
    qspec, kspec, colspec, _ = _att_specs(s, tq, tk)

    def body(q_ref, k_ref, v_ref, tot_ref, do_ref, dq_ref, dk_ref, dv_ref, k16, v16, q16, do16, s_scr, e_scr, w_scr,
             lz16, a16, e16, left, esum):
        i = pl.program_id(1)
        _stage_bf16(i, [(k_ref, k16), (v_ref, v16)])

        @pl.when(i == 0)
        def _():
            dk_ref[...] = jnp.zeros_like(dk_ref)
            dv_ref[...] = jnp.zeros_like(dv_ref)

        q16[...] = q_ref[...].astype(BF16)
        do16[...] = do_ref[...].astype(BF16)
        left[...] = jnp.zeros_like(left)
        esum[...] = jnp.zeros_like(esum)
        dq_ref[...] = jnp.zeros_like(dq_ref)
        prefix = jnp.where(_iota2(tk, tk, 0) <= _iota2(tk, tk, 1), 1.0, 0.0).astype(BF16)

        def blk(kb, masked):
            rows = _krows(kb, tk)
            s_scr[...] = _dot(q16[...], k16[rows, :], NT)
            e_scr[...] = _dot(do16[...], v16[rows, :], NT)
            for rs in _strips(tq):
                lk = -_softplus(s_scr[rs, :] * SCALE)
                if masked:
                    lk = jnp.where(_visible(i, kb, rs, tq, tk, True), lk, 0.0)
                lz16[rs, :] = lk.astype(BF16)
            w_scr[...] = _dot(lz16[...], prefix, NN)
            for rs in _strips(tq):
                rc = (tot_ref[rs, :] - left[rs, :]) - (w_scr[rs, :] - lz16[rs, :].astype(F32))
                a = jnp.exp(s_scr[rs, :] * SCALE + rc)
                if masked:
                    a = jnp.where(_visible(i, kb, rs, tq, tk, True), a, 0.0)
                e = a * e_scr[rs, :]
                a16[rs, :] = a.astype(BF16)
                e16[rs, :] = e.astype(BF16)
                e_scr[rs, :] = e
                left[rs, :] += w_scr[rs, tk - 1:tk]
            w_scr[...] = _dot(e16[...], prefix, NN)
            for rs in _strips(tq):
                dz = e_scr[rs, :] - _sigmoid(s_scr[rs, :] * SCALE) * (esum[rs, :] + w_scr[rs, :])
                if masked:
                    dz = jnp.where(_visible(i, kb, rs, tq, tk, True), dz, 0.0)
                lz16[rs, :] = dz.astype(BF16)
                esum[rs, :] += w_scr[rs, tk - 1:tk]
            dv_ref[rows, :] += _dot(a16[...], do16[...], TN)
            dk_ref[rows, :] += _dot(lz16[...], q16[...], TN) * SCALE
            dq_ref[...] += _dot(lz16[...], k16[rows, :], NN) * SCALE

        _blocks(i, ratio, blk)

    return _pcall(body, name="sb_bwd", out_shape=[jax.ShapeDtypeStruct((s, HW), F32)] * 3, grid=(HEADS, s // tq),
                  in_specs=[qspec, kspec, kspec, colspec, qspec], out_specs=[qspec, kspec, kspec],
                  scratch_shapes=[_vm((s, HEAD_DIM), BF16), _vm((s, HEAD_DIM), BF16), _vm((tq, HEAD_DIM), BF16),
                                  _vm((tq, HEAD_DIM), BF16), _vm((tq, tk), F32), _vm((tq, tk), F32),
                                  _vm((tq, tk), F32), _vm((tq, tk), BF16), _vm((tq, tk), BF16), _vm((tq, tk), BF16),
                                  _vm((tq, 1), F32), _vm((tq, 1), F32)])(q, k, v, tot, do)


@jax.custom_vjp
def sb_attention(q, k, v):
    return _sb_fwd_call(q, k, v)[0]


def _sb_vjp_fwd(q, k, v):
    o, tot = _sb_fwd_call(q, k, v)
    return o, (q, k, v, tot)


sb_attention.defvjp(_sb_vjp_fwd, lambda res, g: tuple(_sb_bwd_call(*res, g)))


def _mem_specs(s, nk, t):
    return (pl.BlockSpec((t, HEAD_DIM), lambda h, i: (i, h)), pl.BlockSpec((nk, HEAD_DIM), lambda h, i: (0, h)))


def _mem_probs(qb, kk):
    sc = _dot(qb, kk, NT) * SCALE
    p = jnp.exp(sc - jnp.max(sc, axis=-1, keepdims=True))
    return p / jnp.sum(p, axis=-1, keepdims=True)


def _mem_fwd_call(q, k, v):
    s, nk = q.shape[0], k.shape[0]
    t = _pick(s, 512, SUBLANES)
    qspec, kspec = _mem_specs(s, nk, t)

    def body(q_ref, k_ref, v_ref, o_ref):
        o_ref[...] = _dot(_mem_probs(q_ref[...].astype(BF16), k_ref[...]), v_ref[...], NN)

    return _pcall(body, name="mem_fwd", out_shape=jax.ShapeDtypeStruct((s, HW), F32), grid=(HEADS, s // t),
                  in_specs=[qspec, kspec, kspec], out_specs=qspec)(q, k, v)


def _mem_bwd_call(q, k, v, do):
    s, nk = q.shape[0], k.shape[0]
    t = _pick(s, 512, SUBLANES)
    qspec, kspec = _mem_specs(s, nk, t)

    def body(q_ref, k_ref, v_ref, do_ref, dq_ref, dk_ref, dv_ref):
        @pl.when(pl.program_id(1) == 0)
        def _():
            dk_ref[...] = jnp.zeros_like(dk_ref)
            dv_ref[...] = jnp.zeros_like(dv_ref)

        qb = q_ref[...].astype(BF16)
        dob = do_ref[...].astype(BF16)
        p = _mem_probs(qb, k_ref[...])
        dv_ref[...] += _dot(p, dob, TN)
        dp = _dot(dob, v_ref[...], NT)
        ds = p * (dp - jnp.sum(p * dp, axis=-1, keepdims=True))
        dq_ref[...] = _dot(ds, k_ref[...], NN) * SCALE
        dk_ref[...] += _dot(ds, qb, TN) * SCALE

    return _pcall(body, name="mem_bwd",
                  out_shape=[jax.ShapeDtypeStruct((s, HW), F32)] + [jax.ShapeDtypeStruct((nk, HW), F32)] * 2,
                  grid=(HEADS, s // t), in_specs=[qspec, kspec, kspec, qspec],
                  out_specs=[qspec, kspec, kspec])(q, k, v, do)


@jax.custom_vjp
def mem_attention(q, k, v):
    return _mem_fwd_call(q, k, v)


mem_attention.defvjp(lambda q, k, v: (_mem_fwd_call(q, k, v), (q, k, v)),
                     lambda res, g: tuple(_mem_bwd_call(*res, g)))


BNN = (((2,), (1,)), ((0,), (0,)))
BNT = (((2,), (2,)), ((0,), (0,)))
BTN = (((1,), (1,)), ((0,), (0,)))


def _bdot(a, b, dn):
    return lax.dot_general(a.astype(BF16), b.astype(BF16), dn, preferred_element_type=F32)


def _bdotf(a, b, dn):
    return lax.dot_general(a, b, dn, precision=lax.Precision.HIGHEST, preferred_element_type=F32)


@jax.custom_vjp
def bmm(a, b):
    return _bdot(a, b, BNN)


bmm.defvjp(lambda a, b: (_bdot(a, b, BNN), (a, b)), lambda r, g: (_bdot(g, r[1], BNT), _bdot(r[0], g, BTN)))


@jax.custom_vjp
def bmm_nt(a, b):
    return _bdot(a, b, BNT)


bmm_nt.defvjp(lambda a, b: (_bdot(a, b, BNT), (a, b)), lambda r, g: (_bdot(g, r[1], BNN), _bdot(g, r[0], BTN)))


@jax.custom_vjp
def bmm_tn(a, b):
    return _bdot(a, b, BTN)


bmm_tn.defvjp(lambda a, b: (_bdot(a, b, BTN), (a, b)), lambda r, g: (_bdot(r[1], g, BNT), _bdot(r[0], g, BNN)))


def _unit_lower_inverse(nm):
    eye = jnp.where(_iota2(CHUNK, CHUNK, 0) == _iota2(CHUNK, CHUNK, 1), 1.0, 0.0).astype(F32)[None]
    p = eye - nm
    m = nm
    for _ in range(5):
        m = _bdotf(m, m, BNN)
        p = _bdotf(p, eye + m, BNN)
    return p


@jax.custom_vjp
def _solve2(nm, r1, r2):
    inv = _unit_lower_inverse(nm)
    return _bdotf(inv, r1, BNN), _bdotf(inv, r2, BNN)


def _solve2_fwd(nm, r1, r2):
    inv = _unit_lower_inverse(nm)
    u, w = _bdotf(inv, r1, BNN), _bdotf(inv, r2, BNN)
    return (u, w), (inv, u, w)


def _solve2_bwd(res, g):
    inv, u, w = res
    d1, d2 = _bdotf(inv, g[0], BTN), _bdotf(inv, g[1], BTN)
    return -(_bdotf(d1, u, BNT) + _bdotf(d2, w, BNT)), d1, d2


_solve2.defvjp(_solve2_fwd, _solve2_bwd)


def _gdn_chunk(q, k, v, gcc, gcr, b, gl, st):
    qn = q * lax.rsqrt(jnp.sum(q * q, axis=-1, keepdims=True) + EPS) * SCALE
    kn = k * lax.rsqrt(jnp.sum(k * k, axis=-1, keepdims=True) + EPS)
    r, c = _iota2(CHUNK, CHUNK, 0)[None], _iota2(CHUNK, CHUNK, 1)[None]
    decay = jnp.exp(jnp.where(r >= c, gcc - gcr, NEG))
    nm = jnp.where(r > c, b * bmm_nt(kn, kn) * decay, 0.0)
    eg = jnp.exp(gcc)
    u, w = _solve2(nm, v * b, kn * (b * eg))
    attn = bmm_nt(qn, kn) * decay
    v_new = u - bmm(w, st)
    o = bmm(qn * eg, st) + bmm(attn, v_new)
    st_new = st * jnp.exp(gl) + bmm_tn(kn * jnp.exp(gl - gcc), v_new)
    return o, st_new


def _heads_of(ref, rows):
    return jnp.stack([ref[rows, _head_cols(h)] for h in range(HEADS)])


def _head_cols(h):
    return slice(h * HEAD_DIM, (h + 1) * HEAD_DIM)


GDN_ROWS = 512


def _gdn_specs(s, tg, rev):
    nb = s // tg
    cpb = tg // CHUNK
    j_of = (lambda j: nb - 1 - j) if rev else (lambda j: j)
    qspec = pl.BlockSpec((tg, HW), lambda j: (j_of(j), 0))
    colspec = pl.BlockSpec((HEADS, tg, 1), lambda j: (0, j_of(j), 0))
    rowspec = pl.BlockSpec((HEADS, cpb, 1, CHUNK), lambda j: (0, j_of(j), 0, 0))
    onespec = pl.BlockSpec((HEADS, cpb, 1, 1), lambda j: (0, j_of(j), 0, 0))
    stspec = pl.BlockSpec((HEADS, cpb, HEAD_DIM, HEAD_DIM), lambda j: (0, j_of(j), 0, 0))
    return qspec, colspec, rowspec, onespec, stspec


def _gdn_fwd_call(q, k, v, gcc, gcr, bc, gl):
    s = q.shape[0]
    tg = _pick(s, GDN_ROWS, CHUNK)
    cpb = tg // CHUNK
    qspec, colspec, rowspec, onespec, stspec = _gdn_specs(s, tg, False)

    def body(q_ref, k_ref, v_ref, gcc_ref, gcr_ref, b_ref, gl_ref, o_ref, st_ref, st):
        @pl.when(pl.program_id(0) == 0)
        def _():
            st[...] = jnp.zeros_like(st)

        def chunk(ci, _):
            rows = pl.ds(pl.multiple_of(ci * CHUNK, CHUNK), CHUNK)
            s_in = st[...]
            st_ref[:, ci] = s_in
            o, s_new = _gdn_chunk(_heads_of(q_ref, rows), _heads_of(k_ref, rows), _heads_of(v_ref, rows),
                                  gcc_ref[:, rows, :], gcr_ref[:, ci], b_ref[:, rows, :], gl_ref[:, ci], s_in)
            for h in range(HEADS):
                o_ref[rows, _head_cols(h)] = o[h]
            st[...] = s_new
            return 0

        lax.fori_loop(0, cpb, chunk, 0)

    return _pcall(body, name="gdn_fwd",
                  out_shape=[jax.ShapeDtypeStruct((s, HW), F32),
                             jax.ShapeDtypeStruct((HEADS, s // CHUNK, HEAD_DIM, HEAD_DIM), F32)],
                  grid=(s // tg,), in_specs=[qspec, qspec, qspec, colspec, rowspec, colspec, onespec],
                  out_specs=[qspec, stspec], scratch_shapes=[pltpu.VMEM((HEADS, HEAD_DIM, HEAD_DIM), F32)],
                  )(q, k, v, gcc, gcr, bc, gl)


def _gdn_bwd_call(q, k, v, gcc, gcr, bc, gl, states, do):
    s = q.shape[0]
    tg = _pick(s, GDN_ROWS, CHUNK)
    cpb = tg // CHUNK
    qspec, colspec, rowspec, onespec, stspec = _gdn_specs(s, tg, True)

    def body(q_ref, k_ref, v_ref, gcc_ref, gcr_ref, b_ref, gl_ref, st_ref, do_ref,
             dq_ref, dk_ref, dv_ref, dgcc_ref, dgcr_ref, db_ref, dgl_ref, dst):
        @pl.when(pl.program_id(0) == 0)
        def _():
            dst[...] = jnp.zeros_like(dst)

        def chunk(n, _):
            ci = cpb - 1 - n
            rows = pl.ds(pl.multiple_of(ci * CHUNK, CHUNK), CHUNK)
            _, vjp = jax.vjp(_gdn_chunk, _heads_of(q_ref, rows), _heads_of(k_ref, rows), _heads_of(v_ref, rows),
                             gcc_ref[:, rows, :], gcr_ref[:, ci], b_ref[:, rows, :], gl_ref[:, ci], st_ref[:, ci])
            dq, dk, dv, dgcc, dgcr, db, dgl, ds_in = vjp((_heads_of(do_ref, rows), dst[...]))
            for h in range(HEADS):
                cols = _head_cols(h)
                dq_ref[rows, cols] = dq[h]
                dk_ref[rows, cols] = dk[h]
                dv_ref[rows, cols] = dv[h]
            dgcc_ref[:, rows, :] = dgcc
            dgcr_ref[:, ci] = dgcr
            db_ref[:, rows, :] = db
            dgl_ref[:, ci] = dgl
            dst[...] = ds_in
            return 0

        lax.fori_loop(0, cpb, chunk, 0)

    n = s // CHUNK
    return _pcall(body, name="gdn_bwd",
                  out_shape=[jax.ShapeDtypeStruct((s, HW), F32)] * 3
                  + [jax.ShapeDtypeStruct((HEADS, s, 1), F32), jax.ShapeDtypeStruct((HEADS, n, 1, CHUNK), F32),
                     jax.ShapeDtypeStruct((HEADS, s, 1), F32), jax.ShapeDtypeStruct((HEADS, n, 1, 1), F32)],
                  grid=(s // tg,),
                  in_specs=[qspec, qspec, qspec, colspec, rowspec, colspec, onespec, stspec, qspec],
                  out_specs=[qspec, qspec, qspec, colspec, rowspec, colspec, onespec],
                  scratch_shapes=[pltpu.VMEM((HEADS, HEAD_DIM, HEAD_DIM), F32)],
                  )(q, k, v, gcc, gcr, bc, gl, states, do)


@jax.custom_vjp
def gated_delta(q, k, v, gcc, gcr, bc, gl):
    return _gdn_fwd_call(q, k, v, gcc, gcr, bc, gl)[0]


def _gdn_vjp_fwd(q, k, v, gcc, gcr, bc, gl):
    o, states = _gdn_fwd_call(q, k, v, gcc, gcr, bc, gl)
    return o, (q, k, v, gcc, gcr, bc, gl, states)


gated_delta.defvjp(_gdn_vjp_fwd, lambda res, g: tuple(_gdn_bwd_call(*res, g)))


def _loss_call(y, target):
    s, d = y.shape
    tm = _pick(s, 512, SUBLANES)

    def body(y_ref, t_ref, dy_ref, loss_ref):
        @pl.when(pl.program_id(0) == 0)
        def _():
            loss_ref[...] = jnp.zeros_like(loss_ref)

        err = y_ref[...] - t_ref[...]
        dy_ref[...] = err * (1.0 / d)
        loss_ref[...] += 0.5 * jnp.sum(jnp.mean(err * err, axis=-1, keepdims=True), axis=0, keepdims=True)

    dy, part = _pcall(body, name="loss_head",
                      out_shape=[jax.ShapeDtypeStruct((s, d), F32), jax.ShapeDtypeStruct((1, 1), F32)],
                      grid=(s // tm,), in_specs=[pl.BlockSpec((tm, d), lambda i: (i, 0))] * 2,
                      out_specs=[pl.BlockSpec((tm, d), lambda i: (i, 0)), pl.BlockSpec((1, 1), lambda i: (0, 0))],
                      )(y, target)
    return part[0, 0], dy


def _cols_and_rows(a, lane0, t):
    s = a.shape[0]
    at = a[:, lane0:lane0 + HEADS].T
    return at, at[:, :, None], at.reshape(HEADS, s // t, 1, t)


def _pad_lanes(v, lane0):
    return jnp.pad(v, (lane0, LANES - lane0 - v.shape[0])).reshape(1, LANES)


def _layer(x, mem, w, w16, ops, convs):
    s = x.shape[0]
    row = lambda v: v.reshape(1, -1)
    mw = lambda n: (w[n], w16[n])
    h = ops["rms"](x, row(w["norm_mix"]))
    fq, fk, fv, gqkv, gz, sq, sk, sv, gt, gm = proj(h, *mw("w_in"))

    logf, beta, gc = ops["small"](gm, _pad_lanes(w["fox_fbias"], LANE_FF), _pad_lanes(w["gdn_a_log"], LANE_GA),
                                  _pad_lanes(w["gdn_dt_bias"], LANE_GA))
    _, ccol, crow = _cols_and_rows(seq_cumsum(logf), LANE_FF, _att_tiles(s, fox=True)[1])
    ya = fox_attention(ops["headnorm"](fq, row(w["fox_qnorm"])), ops["headnorm"](fk, row(w["fox_knorm"])), fv,
                       ccol, crow)
    cq, ck, cv = convs["gdn"](gqkv, w["gdn_conv"])
    gct, gcc, gcr = _cols_and_rows(gc, LANE_GA, CHUNK)
    gl = gct.reshape(HEADS, s // CHUNK, CHUNK)[:, :, CHUNK - 1].reshape(HEADS, s // CHUNK, 1, 1)
    bc = beta[:, LANE_GB:LANE_GB + HEADS].T[:, :, None]
    yb = ops["gdnpost"](gated_delta(cq, ck, cv, gcc, gcr, bc, gl), gz, row(w["gdn_onorm"]))
    yc = sb_attention(sq, sk, sv)
    gb = w["gate_bias"]
    mixed = ops["merge"](gt, matmul(ya, *mw("w_oa")), matmul(yb, *mw("w_ob")), matmul(yc, *mw("w_oc")),
                         row(gb[:D_MODEL]), row(gb[D_MODEL:2 * D_MODEL]), row(gb[2 * D_MODEL:]))
    x = matmul_add(x, mixed, *mw("w_out"))
    mq = ops["headnorm"](matmul(ops["rms"](x, row(w["norm_xq"])), *mw("w_mq")), row(w["mq_norm"]))
    kv = matmul(ops["rms"](mem, row(w["norm_mem"])), *mw("w_mkv"))
    mk = ops["headnorm"](kv[:, :HW], row(w["mk_norm"]))
    x = matmul_add(x, mem_attention(mq, mk, kv[:, HW:]), *mw("w_mo"))
    u = matmul(ops["rms"](x, row(w["norm_ffn"])), *mw("w_up"))
    act = convs["ffn"](u, w["ffn_conv"], row(w["ffn_conv_b"]))
    return matmul_add(x, act, *mw("w_down"))


def _forward(x, mem, layers, layers16):
    ops, convs = _make_rowops(), _make_convops()
    for w, w16 in zip(layers, layers16):
        x = _layer(x, mem, w, w16, ops, convs)
    return x


ANY = pl.BlockSpec(memory_space=pl.ANY)


N_PEERS = N_DEV - 1


def _ccall(body, *, name, out_shape, n_arrays):
    return pl.pallas_call(body, name=name, out_shape=out_shape, in_specs=[ANY] * n_arrays,
                          out_specs=[ANY] * n_arrays,
                          scratch_shapes=[pltpu.SemaphoreType.DMA((N_PEERS * n_arrays,)),
                                          pltpu.SemaphoreType.DMA((N_PEERS * n_arrays,)),
                                          pltpu.SemaphoreType.DMA((n_arrays,))],
                          interpret=False)


def _all_gather(name, shards):
    n = len(shards)

    def body(*refs):
        x_refs, out_refs = refs[:n], refs[n:2 * n]
        send_sems, recv_sems, local_sems = refs[2 * n:]
        x, y, c = lax.axis_index("x"), lax.axis_index("y"), lax.axis_index("c")
        me, sibling = (x, y, c), (x, y, 1 - c)
        chips = [(1 - x, y), (x, 1 - y), (1 - x, 1 - y)]

        def slot(a, px, py, pc):
            return out_refs[a].at[4 * px + 2 * py + pc]

        def copy(a, k, block, to, src=None):
            return pltpu.make_async_remote_copy(
                src_ref=slot(a, *block) if src is None else src, dst_ref=slot(a, *block),
                send_sem=send_sems.at[N_PEERS * a + k], recv_sem=recv_sems.at[N_PEERS * a + k], device_id=to,
                device_id_type=MESH)

        mine = [pltpu.make_async_copy(x_refs[a], slot(a, *me), local_sems.at[a]) for a in range(n)]
        first = []
        for a in range(n):
            first.append(copy(a, 0, me, sibling, src=x_refs[a]))
            first += [copy(a, 1 + j, me, (*chip, c), src=x_refs[a]) for j, chip in enumerate(chips)]
        for cp in mine + first:
            cp.start()
        passed = []
        for j, chip in enumerate(chips):
            for a in range(n):
                copy(a, 1 + j, (*chip, c), me).wait_recv()
                passed.append(copy(a, 4 + j, (*chip, c), sibling))
                passed[-1].start()
        for a in range(n):
            copy(a, 0, sibling, me).wait_recv()
            for j, chip in enumerate(chips):
                copy(a, 4 + j, (*chip, 1 - c), me).wait_recv()
        for cp in first + passed:
            cp.wait_send()
        for cp in mine:
            cp.wait()

    return _ccall(body, name=name, out_shape=[jax.ShapeDtypeStruct((N_DEV,) + s.shape, s.dtype) for s in shards],
                  n_arrays=n)(*shards)


SEM = pl.BlockSpec(memory_space=pltpu.SEMAPHORE)
HBM = pl.BlockSpec(memory_space=pltpu.HBM)
DATAFLOW = pltpu.SideEffectType.DATAFLOW_SIDE_EFFECTING


def _exchange_copies(p_refs, land_refs, send_sem, recv_sem):
    x, y, c = lax.axis_index("x"), lax.axis_index("y"), lax.axis_index("c")
    copies = []
    for p_ref, land_ref in zip(p_refs, land_refs):
        for k in range(1, N_DEV):
            px, py, pc = x ^ ((k >> 2) & 1), y ^ ((k >> 1) & 1), c ^ (k & 1)
            copies.append(pltpu.make_async_remote_copy(
                src_ref=p_ref.at[4 * px + 2 * py + pc], dst_ref=land_ref.at[k - 1], send_sem=send_sem,
                recv_sem=recv_sem, device_id=(px, py, pc), device_id_type=MESH))
    return copies


def _exchange_start(name, parts):
    n = len(parts)
    lands = [lax.empty((N_PEERS,) + p.shape[1:], p.dtype) for p in parts]

    def body(*refs):
        send_sem, recv_sem = refs[2 * n], refs[2 * n + 1]
        for cp in _exchange_copies(refs[:n], refs[n:2 * n], send_sem, recv_sem):
            cp.start()
        refs[-1][...] = jnp.zeros_like(refs[-1])

    outs = pl.pallas_call(
        body, name=name,
        out_shape=(pltpu.SemaphoreType.DMA(()), pltpu.SemaphoreType.DMA(()),
                   *[pltpu.HBM(a.shape, a.dtype) for a in parts + lands],
                   jax.ShapeDtypeStruct((SUBLANES, LANES), F32)),
        in_specs=[HBM] * (2 * n), out_specs=(SEM, SEM, *[HBM] * (2 * n), pl.BlockSpec(memory_space=pltpu.VMEM)),
        input_output_aliases={i: 2 + i for i in range(2 * n)},
        compiler_params=pltpu.CompilerParams(has_side_effects=DATAFLOW),
        interpret=False)(*[pltpu.with_memory_space_constraint(a, pltpu.HBM) for a in parts + lands])
    return (outs[0], outs[1], list(outs[2:2 + n]), list(outs[2 + n:2 + 2 * n])), outs[-1]


def _exchange_wait(name, send_sem, recv_sem, parts, lands, after):
    n = len(parts)

    def body(*refs):
        for cp in _exchange_copies(refs[:n], refs[n:2 * n], refs[2 * n], refs[2 * n + 1]):
            cp.wait_send()
            cp.wait_recv()

    outs = pl.pallas_call(
        body, name=name, out_shape=tuple(pltpu.HBM(a.shape, a.dtype) for a in parts + lands),
        in_specs=[HBM] * (2 * n) + [SEM, SEM, ANY], out_specs=[HBM] * (2 * n),
        input_output_aliases={i: i for i in range(2 * n)},
        compiler_params=pltpu.CompilerParams(has_side_effects=DATAFLOW),
        interpret=False)(*parts, *lands, send_sem, recv_sem, after)
    return list(outs[:n]), list(outs[n:])


ADAM_SLOT_BYTES = 4 * 1024 * 1024


def _adam_call(name, w, slots, m, v):
    r, n = w.shape
    rows_unit = 2 * SUBLANES
    tr = _pick(r, max(rows_unit, ADAM_SLOT_BYTES // (N_DEV * n * 4)), rows_unit)
    spec = pl.BlockSpec((tr, n), lambda i: (i, 0))

    def body(w_ref, s_ref, m_ref, v_ref, g_ref, d_ref, nm_ref, nv_ref):
        g = s_ref[0].astype(F32)
        for d in range(1, N_DEV):
            g = g + s_ref[d].astype(F32)
        nm = ADAM_B1 * m_ref[...] + (1.0 - ADAM_B1) * g
        nv = ADAM_B2 * v_ref[...] + (1.0 - ADAM_B2) * (g * g)
        m_hat = nm / (1.0 - ADAM_B1 ** ADAM_STEP)
        v_hat = nv / (1.0 - ADAM_B2 ** ADAM_STEP)
        g_ref[...] = g
        d_ref[...] = -ADAM_LR * (m_hat / (jnp.sqrt(v_hat) + ADAM_EPS) + ADAM_WD * w_ref[...])
        nm_ref[...] = nm
        nv_ref[...] = nv

    return _pcall(body, name=name, out_shape=[jax.ShapeDtypeStruct((r, n), F32)] * 4, grid=(r // tr,),
                  in_specs=[spec, pl.BlockSpec((N_DEV, tr, n), lambda i: (0, i, 0)), spec, spec],
                  out_specs=[spec] * 4)(w, slots, m, v)


def _pack_rows(flat, rows):
    return jnp.pad(flat, (0, rows * PACK_COLS - flat.shape[0])).reshape(rows, PACK_COLS)


def _regroup_in(w_in):
    cols = [w_in[:, a:b] for a, b in _IN_SRC]
    return jnp.concatenate(cols + [jnp.zeros((w_in.shape[0], N_IN_PAD - N_IN), w_in.dtype)], axis=1)


def _ungroup_in(d):
    starts = {}
    off = 0
    for a, b in _IN_SRC:
        starts[a] = (off, b - a)
        off += b - a
    return jnp.concatenate([d[:, starts[a][0]:starts[a][0] + starts[a][1]] for a in sorted(starts)], axis=1)


def _full_weights(gathered):
    out = {}
    for n, g in zip(SHARDED_ORDER, gathered):
        (r, c), axis = SHARDED[n]
        out[n] = g.reshape(r, c) if axis == 0 else g.transpose(1, 0, 2).reshape(r, c)
    out["w_in"] = _regroup_in(out["w_in"])
    return out


def _in_f32(full):
    return {n: v.astype(F32) for n, v in full.items()}


def _for_transport(name, shard):
    return shard if name in ("gdn_conv", "ffn_conv") else shard.astype(BF16)


def _grad_parts(grads):
    parts = []
    for n in SHARDED_ORDER:
        (r, c), axis = SHARDED[n]
        g = _ungroup_in(grads[n]) if n == "w_in" else grads[n]
        if axis == 0:
            parts.append(g.reshape(N_DEV, r // N_DEV, c).astype(BF16))
        else:
            parts.append(g.reshape(r, N_DEV, c // N_DEV).transpose(1, 0, 2).astype(BF16))
    return parts


def _pack_small(vals):
    return _pack_rows(jnp.concatenate([vals[n].reshape(-1) for n in SMALL_ORDER]), SMALL_ROWS)


def _unpack_small(packed):
    flat = packed.reshape(-1)
    out, off = {}, 0
    for n in SMALL_ORDER:
        size = DEPTH * SMALL_WIDTH[n]
        out[n] = flat[off:off + size].reshape(DEPTH, SMALL_WIDTH[n])
        off += size
    return out


def kernel(x, mem, norm_mix, w_in, fox_fbias, fox_qnorm, fox_knorm, gdn_conv, gdn_a_log, gdn_dt_bias, gdn_onorm, gate_bias, w_oa, w_ob, w_oc, w_out, norm_xq, norm_mem, w_mq, w_mkv, mq_norm, mk_norm, w_mo, norm_ffn, w_up, ffn_conv, ffn_conv_b, w_down, loss_target, m_norm_mix, m_w_in, m_fox_fbias, m_fox_qnorm, m_fox_knorm, m_gdn_conv, m_gdn_a_log, m_gdn_dt_bias, m_gdn_onorm, m_gate_bias, m_w_oa, m_w_ob, m_w_oc, m_w_out, m_norm_xq, m_norm_mem, m_w_mq, m_w_mkv, m_mq_norm, m_mk_norm, m_w_mo, m_norm_ffn, m_w_up, m_ffn_conv, m_ffn_conv_b, m_w_down, v_norm_mix, v_w_in, v_fox_fbias, v_fox_qnorm, v_fox_knorm, v_gdn_conv, v_gdn_a_log, v_gdn_dt_bias, v_gdn_onorm, v_gate_bias, v_w_oa, v_w_ob, v_w_oc, v_w_out, v_norm_xq, v_norm_mem, v_w_mq, v_w_mkv, v_mq_norm, v_mk_norm, v_w_mo, v_norm_ffn, v_w_up, v_ffn_conv, v_ffn_conv_b, v_w_down):
    given = dict(locals())
    wts = {n: given[n] for n in WEIGHTS}
    mom = {n: given["m_" + n] for n in WEIGHTS}
    var = {n: given["v_" + n] for n in WEIGHTS}

    layers, layers16 = [], []
    for l in range(DEPTH):
        full = _full_weights(_all_gather("gather_weights", [_for_transport(n, wts[n][l]) for n in SHARDED_ORDER]))
        layers16.append(full)
        layers.append({**_in_f32(full), **{n: wts[n][l] for n in SMALL_ORDER}})

    ops, convs = _make_rowops(), _make_convops()
    y, vjps = x[0], []
    for l in range(DEPTH):
        y, vjp = jax.vjp(lambda xx, ww, l=l: _layer(xx, mem[0], ww, layers16[l], ops, convs), y, layers[l])
        vjps.append(vjp)
    loss_part, dx = _loss_call(y, loss_target[0])
    loss = lax.psum(loss_part, ("x", "y", "c"))

    out = {}
    dlayers, started = [None] * DEPTH, [None] * DEPTH
    for l in reversed(range(DEPTH)):
        dx, dlayers[l] = vjps[l](dx)
        started[l], token = _exchange_start(f"exchange_start_{l}", _grad_parts(dlayers[l]))
        dx = dx + token[0, 0]
    me = 4 * lax.axis_index("x") + 2 * lax.axis_index("y") + lax.axis_index("c")
    per_layer, after = [None] * DEPTH, dx
    for l in reversed(range(DEPTH)):
        parts, lands = _exchange_wait(f"exchange_wait_{l}", *started[l], after)
        slots = [jnp.concatenate([lax.dynamic_index_in_dim(p, me, 0), ld]) for p, ld in zip(parts, lands)]
        per_layer[l] = {n: _adam_call("adam_shard", wts[n][l], sl, mom[n][l], var[n][l])
                        for n, sl in zip(SHARDED_ORDER, slots)}
        after = per_layer[l]["w_in"][0]
    for n in SHARDED_ORDER:
        for k, kind in enumerate(("grad_", "delta_", "new_m_", "new_v_")):
            out[kind + n] = jnp.stack([per_layer[l][n][k] for l in range(DEPTH)])
    dsmall = {n: jnp.stack([dlayers[l][n] for l in range(DEPTH)]) for n in SMALL_ORDER}
    slots = _all_gather("gather_small_grads", [_pack_small(dsmall)])[0]
    res = _adam_call("adam_small", _pack_small(wts), slots, _pack_small(mom), _pack_small(var))
    for k, kind in enumerate(("grad_", "delta_", "new_m_", "new_v_")):
        un = _unpack_small(res[k])
        for n in SMALL_ORDER:
            out[kind + n] = un[n].reshape(wts[n].shape)

    return (loss, dx[None], *[out["grad_" + n] for n in WEIGHTS], *[out["delta_" + n] for n in WEIGHTS],
            *[out["new_m_" + n] for n in WEIGHTS], *[out["new_v_" + n] for n in WEIGHTS])
```
